```python
import math
import jax, jax.numpy as jnp
from jax import lax
import numpy as np

D_MODEL = 1024
BATCH = 32
SEQ = 256
DEPTH = 4
DEC_BATCH = 8
DEC_SEQ = 2048
PAST_LEN = 512

GRID_W = 64
CHUNK = 128
N_MIXERS = 2
N_A = (DEPTH + 1) // 2
N_B = DEPTH // 2
A_HEADS = 4
A_DV = D_MODEL // A_HEADS
A_DK = A_DV // 2
A_IN = 2 * A_HEADS * A_DK + 2 * A_HEADS * A_DV + 4 * A_HEADS
A_FBIAS_LO = 3.0
A_FBIAS_HI = 6.0
B_HEADS = 8
B_DK = D_MODEL // B_HEADS
B_DV = 2 * D_MODEL // B_HEADS
B_IN = 2 * B_HEADS * B_DK + 2 * B_HEADS * B_DV
B_DECAY_BASE = 5.0
ROPE_BASE = 10000.0
D_FF = ((8 * D_MODEL // 3 + 127) // 128) * 128
CONV_W = 3
EPS = 1e-6

kernel_name = 'hybrid_mlstm_retention_diffusion_step'

F32 = jnp.float32


def rmsnorm(x, g):
    xf = x.astype(F32)
    y = xf * lax.rsqrt(jnp.mean(xf * xf, axis=-1, keepdims=True) + EPS)
    return (y * g.astype(F32)).astype(x.dtype)


def head_norm(h, g):
    mu = jnp.mean(h, axis=-1, keepdims=True)
    var = jnp.mean(jnp.square(h - mu), axis=-1, keepdims=True)
    return (h - mu) * lax.rsqrt(var + EPS) * g.astype(F32)


def modulation(cond, w, b):
    mod = jax.nn.silu(cond) @ w + b
    return [m[:, None, :] for m in jnp.split(mod, 6, axis=-1)]


def flip_t(a):
    return jnp.flip(a, axis=2)


def to_chunks(a):
    nc = a.shape[2] // CHUNK
    a = a.reshape(a.shape[:2] + (nc, CHUNK) + a.shape[3:])
    return jnp.moveaxis(a, 2, 0)


def from_chunks(a):
    a = jnp.moveaxis(a, 0, 2)
    return a.reshape(a.shape[:2] + (a.shape[2] * a.shape[3],) + a.shape[4:])


def mlstm_scan(q, k, v, ig, lf, C0, n0, m0):
    causal = jnp.tril(jnp.ones((CHUNK, CHUNK), bool))

    def step(carry, xs):
        C, n, m = carry
        qj, kj, vj, ij, fj = xs
        a = jnp.cumsum(fj, axis=-1)
        g = a[..., -1]
        dmat = jnp.where(causal, a[..., :, None] - a[..., None, :] + ij[..., None, :], -jnp.inf)
        inter = m[..., None] + a
        m_q = jnp.maximum(inter, jnp.max(dmat, axis=-1))
        w_intra = jnp.exp(dmat - m_q[..., None])
        w_inter = jnp.exp(inter - m_q)
        s = jnp.einsum('bhld,bhsd->bhls', qj, kj) * w_intra
        num = jnp.einsum('bhls,bhsv->bhlv', s, vj) + w_inter[..., None] * jnp.einsum('bhld,bhdv->bhlv', qj, C)
        den = jnp.sum(s, axis=-1) + w_inter * jnp.einsum('bhld,bhd->bhl', qj, n)
        h = num / jnp.maximum(jnp.abs(den), jnp.exp(-m_q))[..., None]
        w_s = g[..., None] - a + ij
        m_new = jnp.maximum(m + g, jnp.max(w_s, axis=-1))
        ws = jnp.exp(w_s - m_new[..., None])
        dec = jnp.exp(m + g - m_new)
        kw = kj * ws[..., None]
        C_new = dec[..., None, None] * C + jnp.einsum('bhsd,bhsv->bhdv', kw, vj)
        n_new = dec[..., None] * n + jnp.sum(kw, axis=2)
        return (C_new, n_new, m_new), h

    xs = (to_chunks(q), to_chunks(k), to_chunks(v), to_chunks(ig), to_chunks(lf))
    (C, n, m), hs = lax.scan(step, (C0, n0, m0), xs)
    return from_chunks(hs), C, n, m


def mlstm_mixer(u, w_in, b_gate, g_norm, w_out, C0, n0, m0):
    B, T, _ = u.shape
    z = jnp.einsum('btd,de->bte', u, w_in).astype(F32)
    hk, hv = A_HEADS * A_DK, A_HEADS * A_DV
    q, k, v, o, gates = jnp.split(z, [hk, 2 * hk, 2 * hk + hv, 2 * hk + 2 * hv], axis=-1)
    q = q.reshape(B, T, A_HEADS, A_DK).transpose(0, 2, 1, 3)
    k = (k * (A_DK ** -0.5)).reshape(B, T, A_HEADS, A_DK).transpose(0, 2, 1, 3)
    v = v.reshape(B, T, A_HEADS, A_DV).transpose(0, 2, 1, 3)
    gates = (gates.reshape(B, T, 4, A_HEADS) + b_gate.astype(F32)).transpose(2, 0, 3, 1)
    ig_f, lf_f = gates[0], jax.nn.log_sigmoid(gates[1])
    ig_b, lf_b = gates[2], jax.nn.log_sigmoid(gates[3])
    C0, n0, m0 = C0.astype(F32), n0.astype(F32), m0.astype(F32)
    h_f, Cf, nf, mf = mlstm_scan(q, k, v, ig_f, lf_f, C0[:, 0], n0[:, 0], m0[:, 0])
    h_b, Cb, nb, mb = mlstm_scan(flip_t(q), flip_t(k), flip_t(v), flip_t(ig_b), flip_t(lf_b),
                                 C0[:, 1], n0[:, 1], m0[:, 1])
    h = head_norm((h_f + flip_t(h_b)).transpose(0, 2, 1, 3), g_norm)
    h = h.reshape(B, T, hv) * jax.nn.sigmoid(o)
    y = h.astype(u.dtype) @ w_out
    return y, jnp.stack([Cf, Cb], 1), jnp.stack([nf, nb], 1), jnp.stack([mf, mb], 1)


def axial_rope(x, rows, cols):
    half = x.shape[-1] // 2
    quarter = half // 2
    inv = ROPE_BASE ** (-jnp.arange(quarter, dtype=F32) / quarter)

    def rot(xp, pos):
        ang = pos.astype(F32)[:, None] * inv
        cos, sin = jnp.cos(ang)[:, None, :], jnp.sin(ang)[:, None, :]
        x1, x2 = xp[..., :quarter], xp[..., quarter:]
        return jnp.concatenate([x1 * cos - x2 * sin, x1 * sin + x2 * cos], axis=-1)

    return jnp.concatenate([rot(x[..., :half], rows), rot(x[..., half:], cols)], axis=-1)


def retention_scan(q, k, v, log_gamma, S0):
    idx = jnp.arange(CHUNK, dtype=F32)
    lg = log_gamma[:, None]
    tri = jnp.tril(jnp.ones((CHUNK, CHUNK), bool))
    diff = jnp.where(tri, idx[:, None] - idx[None, :], 0.0)
    decay_mat = jnp.where(tri, jnp.exp(diff[None] * lg[..., None]), 0.0)
    xi = jnp.exp((idx + 1.0) * lg)
    zeta = jnp.exp((CHUNK - 1.0 - idx) * lg)
    chunk_decay = jnp.exp(CHUNK * log_gamma)

    def step(S, xs):
        qj, kj, vj = xs
        s = jnp.einsum('bhld,bhsd->bhls', qj, kj) * decay_mat
        o = jnp.einsum('bhls,bhsv->bhlv', s, vj) + xi[..., None] * jnp.einsum('bhld,bhdv->bhlv', qj, S)
        S_new = chunk_decay[:, None, None] * S + jnp.einsum('bhsd,bhsv->bhdv', kj * zeta[..., None], vj)
        return S_new, o

    S, hs = lax.scan(step, S0, (to_chunks(q), to_chunks(k), to_chunks(v)))
    return from_chunks(hs), S


def retention_mixer(u, w_in, decay_exp, g_norm, w_out, S0, grid):
    B, T, _ = u.shape
    z = jnp.einsum('btd,de->bte', u, w_in).astype(F32)
    hk, hv = B_HEADS * B_DK, B_HEADS * B_DV
    q, k, v, gate = jnp.split(z, [hk, 2 * hk, 2 * hk + hv], axis=-1)
    q = q.reshape(B, T, B_HEADS, B_DK)
    k = (k * (B_DK ** -0.5)).reshape(B, T, B_HEADS, B_DK)
    if grid is not None:
        q = axial_rope(q, grid[0], grid[1])
        k = axial_rope(k, grid[0], grid[1])
    q, k = q.transpose(0, 2, 1, 3), k.transpose(0, 2, 1, 3)
    v = v.reshape(B, T, B_HEADS, B_DV).transpose(0, 2, 1, 3)
    log_gamma = jnp.log1p(-jnp.exp2(-decay_exp.astype(F32)))
    S0 = S0.astype(F32)
    h_f, S_f = retention_scan(q, k, v, log_gamma[0], S0[:, 0])
    h_b, S_b = retention_scan(flip_t(q), flip_t(k), flip_t(v), log_gamma[1], S0[:, 1])
    h = head_norm((h_f + flip_t(h_b)).transpose(0, 2, 1, 3), g_norm).reshape(B, T, hv)
    h = h * jax.nn.silu(gate)
    return h.astype(u.dtype) @ w_out, jnp.stack([S_f, S_b], 1)


def dwconv3(h, w, b):
    L = h.shape[-2]
    hp = jnp.pad(h, [(0, 0)] * (h.ndim - 2) + [(1, 1), (0, 0)])
    return hp[..., 0:L, :] * w[0] + hp[..., 1:L + 1, :] * w[1] + hp[..., 2:L + 2, :] * w[2] + b


def conv_ffn(u, w_up, conv_w, conv_b, w_down, grid):
    B, T, _ = u.shape
    h = u @ w_up
    if grid is None:
        h = dwconv3(h, conv_w, conv_b)
    else:
        n_rows = grid[2]
        h = dwconv3(h.reshape(B, n_rows, GRID_W, 2 * D_FF), conv_w, conv_b).reshape(B, T, 2 * D_FF)
    g, up = jnp.split(h, 2, axis=-1)
    return (jax.nn.gelu(g, approximate=True) * up) @ w_down


def trunk(x, cond, grid, st_C, st_n, st_m, st_S, norm_gain, ada_w, ada_b,
          ml_w_in, ml_b_gate, ml_norm, ml_w_out, ret_w_in, ret_decay, ret_norm, ret_w_out,
          ffn_w_up, ffn_conv, ffn_conv_b, ffn_w_down):
    Cs, ns, ms, Ss = [], [], [], []
    for i in range(DEPTH):
        sh1, sc1, g1, sh2, sc2, g2 = modulation(cond, ada_w[i], ada_b[i])
        u = rmsnorm(x, norm_gain[i, 0]) * (1 + sc1) + sh1
        j = i // N_MIXERS
        if i % N_MIXERS == 0:
            y, C, n, m = mlstm_mixer(u, ml_w_in[j], ml_b_gate[j], ml_norm[j], ml_w_out[j],
                                     st_C[:, j], st_n[:, j], st_m[:, j])
            Cs.append(C)
            ns.append(n)
            ms.append(m)
        else:
            y, S = retention_mixer(u, ret_w_in[j], ret_decay[j], ret_norm[j], ret_w_out[j], st_S[:, j], grid)
            Ss.append(S)
        x = x + g1 * rmsnorm(y, norm_gain[i, 1])
        u = rmsnorm(x, norm_gain[i, 2]) * (1 + sc2) + sh2
        f = conv_ffn(u, ffn_w_up[i], ffn_conv[i], ffn_conv_b[i], ffn_w_down[i], grid)
        x = x + g2 * rmsnorm(f, norm_gain[i, 3])
    return x, Cs, ns, ms, Ss


def setup_inputs(seed: int = 0) -> dict:
    key = jax.random.key(seed)
    ks = jax.random.split(key, 32)
    nrm = lambda k, shape, s: jax.random.normal(k, shape, F32) * s
    f_bias = jnp.linspace(A_FBIAS_LO, A_FBIAS_HI, A_HEADS, dtype=F32)
    b_gate = nrm(ks[10], (N_A, 4, A_HEADS), 0.1)
    b_gate = b_gate.at[:, 1].add(f_bias).at[:, 3].add(f_bias)
    decay = B_DECAY_BASE + jnp.arange(B_HEADS, dtype=F32) + nrm(ks[14], (N_B, 2, B_HEADS), 0.1)
    return {
        'x_prompt': nrm(ks[0], (BATCH, SEQ, D_MODEL), 1.0),
        'x_sample': nrm(ks[1], (DEC_BATCH, DEC_SEQ, D_MODEL), 1.0),
        'state_mlstm_C': nrm(ks[2], (DEC_BATCH, N_A, 2, A_HEADS, A_DK, A_DV), 0.5),
        'state_mlstm_n': nrm(ks[3], (DEC_BATCH, N_A, 2, A_HEADS, A_DK), 0.5),
        'state_mlstm_m': nrm(ks[4], (DEC_BATCH, N_A, 2, A_HEADS), 1.0),
        'state_ret_S': nrm(ks[5], (DEC_BATCH, N_B, 2, B_HEADS, B_DK, B_DV), 1.0),
        'c': nrm(ks[6], (DEC_BATCH, D_MODEL), 1.0),
        'c_ctx': nrm(ks[7], (D_MODEL,), 1.0),
        'norm_gain': 1.0 + nrm(ks[8], (DEPTH, 4, D_MODEL), 0.02),
        'ada_w': nrm(ks[9], (DEPTH, D_MODEL, 6 * D_MODEL), 0.5 * D_MODEL ** -0.5),
        'ada_b': nrm(ks[11], (DEPTH, 6 * D_MODEL), 0.1),
        'ml_w_in': nrm(ks[12], (N_A, D_MODEL, A_IN), D_MODEL ** -0.5),
        'ml_b_gate': b_gate,
        'ml_norm': 1.0 + nrm(ks[13], (N_A, A_HEADS, A_DV), 0.02),
        'ml_w_out': nrm(ks[15], (N_A, A_HEADS * A_DV, D_MODEL), (A_HEADS * A_DV) ** -0.5),
        'ret_w_in': nrm(ks[16], (N_B, D_MODEL, B_IN), D_MODEL ** -0.5),
        'ret_decay': decay,
        'ret_norm': 1.0 + nrm(ks[17], (N_B, B_HEADS, B_DV), 0.02),
        'ret_w_out': nrm(ks[18], (N_B, B_HEADS * B_DV, D_MODEL), (B_HEADS * B_DV) ** -0.5),
        'ffn_w_up': nrm(ks[19], (DEPTH, D_MODEL, 2 * D_FF), D_MODEL ** -0.5),
        'ffn_conv': nrm(ks[20], (DEPTH, CONV_W, 2 * D_FF), 0.5),
        'ffn_conv_b': nrm(ks[21], (DEPTH, 2 * D_FF), 0.02),
        'ffn_w_down': nrm(ks[22], (DEPTH, D_FF, D_MODEL), D_FF ** -0.5),
    }


def reference(x_prompt, x_sample, state_mlstm_C, state_mlstm_n, state_mlstm_m, state_ret_S, c, c_ctx,
              norm_gain, ada_w, ada_b, ml_w_in, ml_b_gate, ml_norm, ml_w_out,
              ret_w_in, ret_decay, ret_norm, ret_w_out, ffn_w_up, ffn_conv, ffn_conv_b, ffn_w_down):
    weights = (norm_gain, ada_w, ada_b, ml_w_in, ml_b_gate, ml_norm, ml_w_out,
               ret_w_in, ret_decay, ret_norm, ret_w_out, ffn_w_up, ffn_conv, ffn_conv_b, ffn_w_down)
    Bp = x_prompt.shape[0]
    z_C = jnp.zeros((Bp, N_A, 2, A_HEADS, A_DK, A_DV), F32)
    z_n = jnp.zeros((Bp, N_A, 2, A_HEADS, A_DK), F32)
    z_m = jnp.zeros((Bp, N_A, 2, A_HEADS), F32)
    z_S = jnp.zeros((Bp, N_B, 2, B_HEADS, B_DK, B_DV), F32)
    y_prompt, Cs, ns, ms, Ss = trunk(x_prompt, c_ctx[None, :], None, z_C, z_n, z_m, z_S, *weights)
    ROWS = x_sample.shape[1] // GRID_W
    t = jnp.arange(ROWS * GRID_W)
    grid = (t // GRID_W, t % GRID_W, ROWS)
    y_sample, _, _, _, _ = trunk(x_sample, c, grid, state_mlstm_C, state_mlstm_n, state_mlstm_m,
                                 state_ret_S, *weights)
    dt = x_prompt.dtype
    new_mlstm_C = jnp.stack(Cs, 1).astype(dt)
    new_mlstm_n = jnp.stack(ns, 1).astype(dt)
    new_mlstm_m = jnp.stack(ms, 1).astype(dt)
    new_ret_S = jnp.stack(Ss, 1).astype(dt)
    return (y_prompt, y_sample, new_mlstm_C, new_mlstm_n, new_mlstm_m, new_ret_S)
```

```python
import functools
import math

import jax
import jax.numpy as jnp
from jax import lax
from jax.experimental import pallas as pl
from jax.experimental.pallas import tpu as pltpu

D_MODEL = 1024
DEPTH = 4
GRID_W = 64
CHUNK = 128
N_MIXERS = 2
A_HEADS = 4
A_DV = D_MODEL // A_HEADS
A_DK = A_DV // 2
A_MAIN = 2 * A_HEADS * A_DK + 2 * A_HEADS * A_DV
B_HEADS = 8
B_DK = D_MODEL // B_HEADS
B_DV = 2 * D_MODEL // B_HEADS
ROPE_BASE = 10000.0
D_FF = ((8 * D_MODEL // 3 + 127) // 128) * 128
EPS = 1e-6
LN2 = math.log(2.0)

F32 = jnp.float32
BF16 = jnp.bfloat16

LANES = 128
MOD_ROWS = 16
VMEM_LIMIT = 48 * 1024 * 1024

TM_PROJ = 1024
TN_PROJ = 512
TM_OUT = 512
TM_FFN = 512
TF_FFN = 256


def _dot(a, b):
    return jnp.dot(a, b, preferred_element_type=F32)


def _dot_nt(a, b):
    return lax.dot_general(a, b, (((1,), (1,)), ((), ())), preferred_element_type=F32)


def _rms(x):
    return x * lax.rsqrt(jnp.mean(x * x, axis=-1, keepdims=True) + EPS)


def _params(*sem):
    return pltpu.CompilerParams(dimension_semantics=sem, vmem_limit_bytes=VMEM_LIMIT)


def _mod_kernel(cond_ref, w_ref, b_ref, o_ref):
    cnd = cond_ref[...]
    s = cnd * jax.nn.sigmoid(cnd)
    o_ref[...] = _dot(s.astype(BF16), w_ref[...].astype(BF16)) + b_ref[...]


def _modulation(cond, ada_w, ada_b):
    tn = 1024
    n_out = ada_w.shape[-1]
    return pl.pallas_call(
        _mod_kernel,
        grid=(DEPTH, n_out // tn),
        in_specs=[
            pl.BlockSpec((MOD_ROWS, D_MODEL), lambda l, j: (0, 0)),
            pl.BlockSpec((None, D_MODEL, tn), lambda l, j: (l, 0, j)),
            pl.BlockSpec((None, 1, tn), lambda l, j: (l, 0, j)),
        ],
        out_specs=pl.BlockSpec((None, MOD_ROWS, tn), lambda l, j: (l, 0, j)),
        out_shape=jax.ShapeDtypeStruct((DEPTH, MOD_ROWS, n_out), F32),
        compiler_params=_params("parallel", "parallel"),
        name="modulation",
    )(cond, ada_w, ada_b.reshape(DEPTH, 1, n_out))


def _mod_row(sample, seq_len, tm):
    if not sample:
        return lambda i: 0
    tiles_per_seq = seq_len // tm
    return lambda i: 1 + i // tiles_per_seq


def _rope_slab(x, cos, sin, low_half):
    rot = jnp.where(low_half, pltpu.roll(x, 96, axis=1), pltpu.roll(x, 32, axis=1))
    return x * cos + rot * sin


def _inproj_kernel(*refs, n_q, n_k, k_scale, rope, gates):
    x_ref, g_ref, sh_ref, sc_ref, w_ref = refs[:5]
    pos = 5
    if gates:
        wgi_ref, wgf_ref, bgi_ref, bgf_ref = refs[pos:pos + 4]
        pos += 4
    if rope:
        cos_ref, sin_ref = refs[pos:pos + 2]
        pos += 2
    z_ref = refs[pos]
    pos += 1
    if gates:
        gi_ref, gf_ref = refs[pos:pos + 2]
        pos += 2
    u_sc = refs[pos]

    j = pl.program_id(1)

    @pl.when(j == 0)
    def _():
        u = _rms(x_ref[...]) * g_ref[...] * (1.0 + sc_ref[...]) + sh_ref[...]
        ub = u.astype(BF16)
        u_sc[...] = ub
        if gates:
            gi_ref[...] = _dot(ub, wgi_ref[...]) + bgi_ref[...]
            gf_ref[...] = _dot(ub, wgf_ref[...]) + bgf_ref[...]

    z = _dot(u_sc[...], w_ref[...])
    n_slab = z.shape[1] // LANES

    def store(scale):
        if rope:
            lane = lax.broadcasted_iota(jnp.int32, (1, LANES), 1)
            low_half = (lane & 63) < 32
            cos = cos_ref[...]
            sin = sin_ref[...]
            for s in range(n_slab):
                sl = slice(s * LANES, (s + 1) * LANES)
                r = _rope_slab(z[:, sl], cos, sin, low_half)
                if scale != 1.0:
                    r = r * scale
                z_ref[:, sl] = r.astype(BF16)
        elif scale != 1.0:
            z_ref[...] = (z * scale).astype(BF16)
        else:
            z_ref[...] = z.astype(BF16)

    is_q = j < n_q
    is_k = (j >= n_q) & (j < n_q + n_k)
    if rope:
        pl.when(is_q)(lambda: store(1.0))
        is_plain = j >= n_q + n_k
    else:
        is_plain = jnp.logical_not(is_k)
    pl.when(is_k)(lambda: store(k_scale))

    @pl.when(is_plain)
    def _():
        z_ref[...] = z.astype(BF16)


def _inproj(x, ng4, mod5, layer, w, *, seq_len, sample, n_q, n_k, k_scale, rope=None, gates=None):
    n_tok = x.shape[0]
    n_col = w.shape[1]
    tm, tn = TM_PROJ, TN_PROJ
    row = _mod_row(sample, seq_len, tm)
    in_specs = [
        pl.BlockSpec((tm, D_MODEL), lambda i, j: (i, 0)),
        pl.BlockSpec((None, None, 1, D_MODEL), lambda i, j: (layer, 0, 0, 0)),
        pl.BlockSpec((None, None, None, 1, D_MODEL), lambda i, j: (layer, row(i), 0, 0, 0)),
        pl.BlockSpec((None, None, None, 1, D_MODEL), lambda i, j: (layer, row(i), 1, 0, 0)),
        pl.BlockSpec((D_MODEL, tn), lambda i, j: (0, j)),
    ]
    args = [x, ng4, mod5, mod5, w]
    out_specs = [pl.BlockSpec((tm, tn), lambda i, j: (i, j))]
    out_shape = [jax.ShapeDtypeStruct((n_tok, n_col), BF16)]
    if gates is not None:
        in_specs += [pl.BlockSpec((D_MODEL, LANES), lambda i, j: (0, 0))] * 2
        in_specs += [pl.BlockSpec((1, LANES), lambda i, j: (0, 0))] * 2
        args += list(gates)
        out_specs += [pl.BlockSpec((tm, LANES), lambda i, j: (i, 0))] * 2
        out_shape += [jax.ShapeDtypeStruct((n_tok, LANES), F32)] * 2
    if rope is not None:
        tiles_per_seq = seq_len // tm
        in_specs += [pl.BlockSpec((tm, LANES), lambda i, j: (i % tiles_per_seq, 0))] * 2
        args += list(rope)
    kern = functools.partial(_inproj_kernel, n_q=n_q, n_k=n_k, k_scale=k_scale,
                             rope=rope is not None, gates=gates is not None)
    outs = pl.pallas_call(
        kern,
        grid=(n_tok // tm, n_col // tn),
        in_specs=in_specs,
        out_specs=out_specs,
        out_shape=out_shape,
        scratch_shapes=[pltpu.VMEM((tm, D_MODEL), BF16)],
        compiler_params=_params("parallel", "arbitrary"),
        name="inproj",
    )(*args)
    return outs


def _tri_masks():
    li = lax.broadcasted_iota(jnp.int32, (CHUNK, CHUNK), 0)
    si = lax.broadcasted_iota(jnp.int32, (CHUNK, CHUNK), 1)
    return si <= li, si >= li


def _head_norm_store(acc_ref, gate_ref, gn_ref, out_ref, n_chunks, act):
    def body(c, carry):
        t0 = pl.multiple_of(c * CHUNK, CHUNK)
        h = acc_ref[pl.ds(t0, CHUNK), :]
        mu = jnp.mean(h, axis=-1, keepdims=True)
        d = h - mu
        var = jnp.mean(d * d, axis=-1, keepdims=True)
        hn = d * lax.rsqrt(var + EPS) * gn_ref[...]
        g = gate_ref[pl.ds(t0, CHUNK), :].astype(F32)
        sg = jax.nn.sigmoid(g)
        if act == "silu":
            sg = g * sg
        out_ref[pl.ds(t0, CHUNK), :] = (hn * sg).astype(BF16)
        return carry
    lax.fori_loop(0, n_chunks, body, 0)


def _split_dot(mask_b, x):
    hi = x.astype(BF16)
    r1 = x - hi.astype(F32)
    mid = r1.astype(BF16)
    lo = (r1 - mid.astype(F32)).astype(BF16)
    return _dot(mask_b, hi) + _dot(mask_b, mid) + _dot(mask_b, lo)


def _mlstm_kernel(*refs, n_chunks, carry_in, carry_out):
    q_ref, k_ref, v_ref, o_ref, gi_ref, gf_ref, gn_ref = refs[:7]
    pos = 7
    if carry_in:
        c0_ref, n0_ref, m0_ref = refs[pos:pos + 3]
        pos += 3
    h_ref = refs[pos]
    pos += 1
    if carry_out:
        cout_ref, nout_ref, mout_ref = refs[pos:pos + 3]
        pos += 3
    acc_ref, c_sc, n_sc = refs[pos:pos + 3]

    head = pl.program_id(1)
    masks = _tri_masks()
    masks_b = [m.astype(BF16) for m in masks]
    lane = lax.broadcasted_iota(jnp.int32, (CHUNK, LANES), 1)
    sub = lax.broadcasted_iota(jnp.int32, (LANES, CHUNK), 0)

    acc_ref[...] = jnp.zeros_like(acc_ref)
    if carry_in:
        c_sc[...] = c0_ref[...]
        n_sc[...] = n0_ref[...]
        m_init = (m0_ref[0], m0_ref[1])
    else:
        c_sc[...] = jnp.zeros_like(c_sc)
        n_sc[...] = jnp.zeros_like(n_sc)
        m_init = (jnp.zeros((1, 1), F32), jnp.zeros((1, 1), F32))

    def chunk(dirn, c, m_prev):
        t0 = pl.multiple_of(c * CHUNK, CHUNK)
        q = q_ref[pl.ds(t0, CHUNK), :]
        k = k_ref[pl.ds(t0, CHUNK), :]
        v = v_ref[pl.ds(t0, CHUNK), :]
        gi = gi_ref[pl.ds(t0, CHUNK), :]
        gf = gf_ref[pl.ds(t0, CHUNK), :]
        lf = jnp.minimum(gf, 0.0) - jnp.log1p(jnp.exp(-jnp.abs(gf)))
        a_all = _split_dot(masks_b[dirn], lf)
        b_all = gi - a_all
        col = head + A_HEADS * dirn
        a_col = jnp.sum(jnp.where(lane == col, a_all, 0.0), axis=1, keepdims=True)
        b_col = jnp.sum(jnp.where(lane == col, b_all, 0.0), axis=1, keepdims=True)
        b_row = jnp.sum(jnp.where(sub == col, b_all.T, 0.0), axis=0, keepdims=True)
        g = a_col[CHUNK - 1:CHUNK, :] if dirn == 0 else a_col[0:1, :]

        dmat = jnp.where(masks[dirn], a_col + b_row, -jnp.inf)
        inter = m_prev + a_col
        m_q = jnp.maximum(inter, jnp.max(dmat, axis=1, keepdims=True))
        w_intra = jnp.exp(dmat - m_q)
        w_inter = jnp.exp(inter - m_q)
        s = _dot_nt(q, k) * w_intra
        c_old = c_sc[dirn]
        n_old = n_sc[dirn]
        num = _dot(s.astype(BF16), v) + w_inter * _dot(q, c_old.astype(BF16))
        qn = jnp.sum(q.astype(F32) * n_old, axis=1, keepdims=True)
        den = jnp.sum(s, axis=1, keepdims=True) + w_inter * qn
        hcur = num / jnp.maximum(jnp.abs(den), jnp.exp(-m_q))
        acc_ref[pl.ds(t0, CHUNK), :] += hcur

        w_s = g + b_col
        m_new = jnp.maximum(m_prev + g, jnp.max(w_s, axis=0, keepdims=True))
        ws = jnp.exp(w_s - m_new)
        dec = jnp.exp(m_prev + g - m_new)
        kw = k.astype(F32) * ws
        c_sc[dirn] = dec * c_old + _dot(kw.T.astype(BF16), v)
        n_sc[dirn] = dec * n_old + jnp.sum(kw, axis=0, keepdims=True)
        return m_new

    def body(c, carry):
        m_f, m_b = carry
        m_f = chunk(0, c, m_f)
        m_b = chunk(1, n_chunks - 1 - c, m_b)
        return m_f, m_b

    m_f, m_b = lax.fori_loop(0, n_chunks, body, m_init)
    if carry_out:
        cout_ref[...] = c_sc[...]
        nout_ref[...] = n_sc[...]
        mout_ref[0] = m_f
        mout_ref[1] = m_b
    _head_norm_store(acc_ref, o_ref, gn_ref, h_ref, n_chunks, "sigmoid")


def _mlstm_scan(z, gi, gf, gn4, j, *, batch, seq_len, state=None):
    n_tok = z.shape[0]
    n_chunks = seq_len // CHUNK
    t = seq_len
    carry_in = state is not None
    carry_out = not carry_in
    in_specs = [
        pl.BlockSpec((t, A_DK), lambda b, h: (b, h)),
        pl.BlockSpec((t, A_DK), lambda b, h: (b, A_HEADS + h)),
        pl.BlockSpec((t, A_DV), lambda b, h: (b, A_HEADS + h)),
        pl.BlockSpec((t, A_DV), lambda b, h: (b, 2 * A_HEADS + h)),
        pl.BlockSpec((t, LANES), lambda b, h: (b, 0)),
        pl.BlockSpec((t, LANES), lambda b, h: (b, 0)),
        pl.BlockSpec((None, None, 1, A_DV), lambda b, h: (j, h, 0, 0)),
    ]
    args = [z, z, z, z, gi, gf, gn4]
    if carry_in:
        c0, n0, m0 = state
        in_specs += [
            pl.BlockSpec((None, None, 2, None, A_DK, A_DV), lambda b, h: (b, j, 0, h, 0, 0)),
            pl.BlockSpec((None, None, 2, None, 1, A_DK), lambda b, h: (b, j, 0, h, 0, 0)),
            pl.BlockSpec((None, None, 2, None, 1, 1), lambda b, h: (b, j, 0, h, 0, 0)),
        ]
        args += [c0, n0, m0]
    out_specs = [pl.BlockSpec((t, A_DV), lambda b, h: (b, h))]
    out_shape = [jax.ShapeDtypeStruct((n_tok, A_HEADS * A_DV), BF16)]
    if carry_out:
        out_specs += [
            pl.BlockSpec((None, 2, None, A_DK, A_DV), lambda b, h: (b, 0, h, 0, 0)),
            pl.BlockSpec((None, 2, None, 1, A_DK), lambda b, h: (b, 0, h, 0, 0)),
            pl.BlockSpec((None, 2, None, 1, 1), lambda b, h: (b, 0, h, 0, 0)),
        ]
        out_shape += [
            jax.ShapeDtypeStruct((batch, 2, A_HEADS, A_DK, A_DV), F32),
            jax.ShapeDtypeStruct((batch, 2, A_HEADS, 1, A_DK), F32),
            jax.ShapeDtypeStruct((batch, 2, A_HEADS, 1, 1), F32),
        ]
    kern = functools.partial(_mlstm_kernel, n_chunks=n_chunks, carry_in=carry_in, carry_out=carry_out)
    return pl.pallas_call(
        kern,
        grid=(batch, A_HEADS),
        in_specs=in_specs,
        out_specs=out_specs,
        out_shape=out_shape,
        scratch_shapes=[
            pltpu.VMEM((t, A_DV), F32),
            pltpu.VMEM((2, A_DK, A_DV), F32),
            pltpu.VMEM((2, 1, A_DK), F32),
        ],
        compiler_params=_params("parallel", "parallel"),
        name="mlstm_scan",
    )(*args)


def _ret_kernel(*refs, n_chunks, carry_in, carry_out):
    q_ref, k_ref, v_ref, gate_ref, dec_ref, gn_ref = refs[:6]
    pos = 6
    if carry_in:
        s0_ref = refs[pos]
        pos += 1
    h_ref = refs[pos]
    pos += 1
    if carry_out:
        sout_ref = refs[pos]
        pos += 1
    acc_ref, s_sc = refs[pos:pos + 2]

    masks = _tri_masks()
    lg = jnp.log1p(-jnp.exp(-dec_ref[...] * LN2))
    li = lax.broadcasted_iota(jnp.int32, (CHUNK, B_DV), 0).astype(F32)
    si = lax.broadcasted_iota(jnp.int32, (CHUNK, CHUNK), 1).astype(F32)
    lg_f = lg[0:1, :]
    lg_b = lg[1:2, :]
    lq = li[:, :CHUNK]
    decay = (
        jnp.where(masks[0], jnp.exp(jnp.where(masks[0], lq - si, 0.0) * lg_f[:, :CHUNK]), 0.0),
        jnp.where(masks[1], jnp.exp(jnp.where(masks[1], si - lq, 0.0) * lg_b[:, :CHUNK]), 0.0),
    )
    xi = (jnp.exp((li + 1.0) * lg_f), jnp.exp((CHUNK - li) * lg_b))
    zeta = (jnp.exp((CHUNK - 1.0 - lq) * lg_f[:, :CHUNK]), jnp.exp(lq * lg_b[:, :CHUNK]))
    cdec = (jnp.exp(CHUNK * lg_f), jnp.exp(CHUNK * lg_b))

    acc_ref[...] = jnp.zeros_like(acc_ref)
    if carry_in:
        s_sc[...] = s0_ref[...]
    else:
        s_sc[...] = jnp.zeros_like(s_sc)

    def chunk(dirn, c):
        t0 = pl.multiple_of(c * CHUNK, CHUNK)
        q = q_ref[pl.ds(t0, CHUNK), :]
        k = k_ref[pl.ds(t0, CHUNK), :]
        v = v_ref[pl.ds(t0, CHUNK), :]
        s = _dot_nt(q, k) * decay[dirn]
        s_old = s_sc[dirn]
        o = _dot(s.astype(BF16), v) + xi[dirn] * _dot(q, s_old.astype(BF16))
        acc_ref[pl.ds(t0, CHUNK), :] += o
        kz = k.astype(F32) * zeta[dirn]
        s_sc[dirn] = cdec[dirn] * s_old + _dot(kz.T.astype(BF16), v)

    def body(c, carry):
        chunk(0, c)
        chunk(1, n_chunks - 1 - c)
        return carry

    lax.fori_loop(0, n_chunks, body, 0)
    if carry_out:
        sout_ref[...] = s_sc[...]
    _head_norm_store(acc_ref, gate_ref, gn_ref, h_ref, n_chunks, "silu")


def _ret_scan(z, dec_rep, gn4, j, *, batch, seq_len, state=None):
    n_tok = z.shape[0]
    n_chunks = seq_len // CHUNK
    t = seq_len
    carry_in = state is not None
    carry_out = not carry_in
    in_specs = [
        pl.BlockSpec((t, B_DK), lambda b, h: (b, h)),
        pl.BlockSpec((t, B_DK), lambda b, h: (b, B_HEADS + h)),
        pl.BlockSpec((t, B_DV), lambda b, h: (b, B_HEADS + h)),
        pl.BlockSpec((t, B_DV), lambda b, h: (b, 2 * B_HEADS + h)),
        pl.BlockSpec((None, None, 2, B_DV), lambda b, h: (j, h, 0, 0)),
        pl.BlockSpec((None, None, 1, B_DV), lambda b, h: (j, h, 0, 0)),
    ]
    args = [z, z, z, z, dec_rep, gn4]
    if carry_in:
        in_specs += [pl.BlockSpec((None, None, 2, None, B_DK, B_DV), lambda b, h: (b, j, 0, h, 0, 0))]
        args += [state]
    out_specs = [pl.BlockSpec((t, B_DV), lambda b, h: (b, h))]
    out_shape = [jax.ShapeDtypeStruct((n_tok, B_HEADS * B_DV), BF16)]
    if carry_out:
        out_specs += [pl.BlockSpec((None, 2, None, B_DK, B_DV), lambda b, h: (b, 0, h, 0, 0))]
        out_shape += [jax.ShapeDtypeStruct((batch, 2, B_HEADS, B_DK, B_DV), F32)]
    kern = functools.partial(_ret_kernel, n_chunks=n_chunks, carry_in=carry_in, carry_out=carry_out)
    return pl.pallas_call(
        kern,
        grid=(batch, B_HEADS),
        in_specs=in_specs,
        out_specs=out_specs,
        out_shape=out_shape,
        scratch_shapes=[
            pltpu.VMEM((t, B_DV), F32),
            pltpu.VMEM((2, B_DK, B_DV), F32),
        ],
        compiler_params=_params("parallel", "parallel"),
        name="ret_scan",
    )(*args)


def _outproj_kernel(h_ref, w_ref, x_ref, g_ref, gate_ref, o_ref):
    y = _dot(h_ref[...], w_ref[...])
    o_ref[...] = x_ref[...] + gate_ref[...] * (_rms(y) * g_ref[...])


def _outproj(h, w, x, ng4, mod5, layer, *, seq_len, sample):
    n_tok, hv = h.shape
    tm = TM_OUT
    row = _mod_row(sample, seq_len, tm)
    return pl.pallas_call(
        _outproj_kernel,
        grid=(n_tok // tm,),
        in_specs=[
            pl.BlockSpec((tm, hv), lambda i: (i, 0)),
            pl.BlockSpec((hv, D_MODEL), lambda i: (0, 0)),
            pl.BlockSpec((tm, D_MODEL), lambda i: (i, 0)),
            pl.BlockSpec((None, None, 1, D_MODEL), lambda i: (layer, 1, 0, 0)),
            pl.BlockSpec((None, None, None, 1, D_MODEL), lambda i: (layer, row(i), 2, 0, 0)),
        ],
        out_specs=pl.BlockSpec((tm, D_MODEL), lambda i: (i, 0)),
        out_shape=jax.ShapeDtypeStruct((n_tok, D_MODEL), F32),
        compiler_params=_params("parallel"),
        name="outproj",
    )(h, w, x, ng4, mod5)


def _conv3(h, cw_ref, cb_ref, first, last):
    n = h.shape[0]
    h_prev = jnp.where(first, 0.0, pltpu.roll(h, 1, axis=0))
    h_next = jnp.where(last, 0.0, pltpu.roll(h, n - 1, axis=0))
    return h_prev * cw_ref[0:1, :] + h * cw_ref[1:2, :] + h_next * cw_ref[2:3, :] + cb_ref[...]


def _ffn_kernel(x_ref, g2_ref, sh_ref, sc_ref, wg_ref, wu_ref, cwg_ref, cwu_ref, cbg_ref, cbu_ref,
                wd_ref, g3_ref, gate_ref, o_ref, u_sc, acc_sc, *, seg):
    cidx = pl.program_id(1)

    @pl.when(cidx == 0)
    def _():
        u = _rms(x_ref[...]) * g2_ref[...] * (1.0 + sc_ref[...]) + sh_ref[...]
        u_sc[...] = u.astype(BF16)
        acc_sc[...] = jnp.zeros_like(acc_sc)

    tm = x_ref.shape[0]
    pos = lax.broadcasted_iota(jnp.int32, (tm, 1), 0) % seg
    first = pos == 0
    last = pos == seg - 1
    ub = u_sc[...]
    hg = _conv3(_dot(ub, wg_ref[...]), cwg_ref, cbg_ref, first, last)
    hu = _conv3(_dot(ub, wu_ref[...]), cwu_ref, cbu_ref, first, last)
    act = jax.nn.gelu(hg, approximate=True) * hu
    acc_sc[...] += _dot(act.astype(BF16), wd_ref[...])

    @pl.when(cidx == pl.num_programs(1) - 1)
    def _():
        o_ref[...] = x_ref[...] + gate_ref[...] * (_rms(acc_sc[...]) * g3_ref[...])


def _ffn(x, ng4, mod5, layer, w_up, conv_w, conv_b, w_down, *, seq_len, sample):
    n_tok = x.shape[0]
    tm, tf = TM_FFN, TF_FFN
    n_f = D_FF // tf
    row = _mod_row(sample, seq_len, tm)
    seg = GRID_W if sample else seq_len
    kern = functools.partial(_ffn_kernel, seg=seg)
    mod_spec = lambda k: pl.BlockSpec((None, None, None, 1, D_MODEL), lambda i, c: (layer, row(i), k, 0, 0))
    gain_spec = lambda k: pl.BlockSpec((None, None, 1, D_MODEL), lambda i, c: (layer, k, 0, 0))
    return pl.pallas_call(
        kern,
        grid=(n_tok // tm, n_f),
        in_specs=[
            pl.BlockSpec((tm, D_MODEL), lambda i, c: (i, 0)),
            gain_spec(2),
            mod_spec(3),
            mod_spec(4),
            pl.BlockSpec((D_MODEL, tf), lambda i, c: (0, c)),
            pl.BlockSpec((D_MODEL, tf), lambda i, c: (0, n_f + c)),
            pl.BlockSpec((3, tf), lambda i, c: (0, c)),
            pl.BlockSpec((3, tf), lambda i, c: (0, n_f + c)),
            pl.BlockSpec((1, tf), lambda i, c: (0, c)),
            pl.BlockSpec((1, tf), lambda i, c: (0, n_f + c)),
            pl.BlockSpec((tf, D_MODEL), lambda i, c: (c, 0)),
            gain_spec(3),
            mod_spec(5),
        ],
        out_specs=pl.BlockSpec((tm, D_MODEL), lambda i, c: (i, 0)),
        out_shape=jax.ShapeDtypeStruct((n_tok, D_MODEL), F32),
        scratch_shapes=[pltpu.VMEM((tm, D_MODEL), BF16), pltpu.VMEM((tm, D_MODEL), F32)],
        compiler_params=_params("parallel", "arbitrary"),
        name="convffn",
    )(x, ng4, mod5, mod5, w_up, w_up, conv_w, conv_w, conv_b, conv_b, w_down, ng4, mod5)


def _rope_tables(seq_len):
    quarter = B_DK // 4
    inv = ROPE_BASE ** (-jnp.arange(quarter, dtype=F32) / quarter)
    t = jnp.arange(seq_len)
    rows = (t // GRID_W).astype(F32)[:, None] * inv
    cols = (t % GRID_W).astype(F32)[:, None] * inv
    cos = jnp.concatenate([jnp.cos(rows)] * 2 + [jnp.cos(cols)] * 2, axis=-1)
    sin = jnp.concatenate([-jnp.sin(rows), jnp.sin(rows), -jnp.sin(cols), jnp.sin(cols)], axis=-1)
    return cos, sin


def _gate_weights(w_in_j, b_gate_j):
    wg = w_in_j[:, A_MAIN:].reshape(D_MODEL, 4, A_HEADS)
    pad_w = jnp.zeros((D_MODEL, LANES - 2 * A_HEADS), F32)
    pad_b = jnp.zeros((LANES - 2 * A_HEADS,), F32)
    wgi = jnp.concatenate([wg[:, 0], wg[:, 2], pad_w], axis=1).astype(BF16)
    wgf = jnp.concatenate([wg[:, 1], wg[:, 3], pad_w], axis=1).astype(BF16)
    bgi = jnp.concatenate([b_gate_j[0], b_gate_j[2], pad_b])[None, :]
    bgf = jnp.concatenate([b_gate_j[1], b_gate_j[3], pad_b])[None, :]
    return wgi, wgf, bgi, bgf


def kernel(x_prompt, x_sample, state_mlstm_C, state_mlstm_n, state_mlstm_m, state_ret_S, c, c_ctx,
           norm_gain, ada_w, ada_b, ml_w_in, ml_b_gate, ml_norm, ml_w_out,
           ret_w_in, ret_decay, ret_norm, ret_w_out, ffn_w_up, ffn_conv, ffn_conv_b, ffn_w_down):
    bp, tp, _ = x_prompt.shape
    bs, ts, _ = x_sample.shape
    n_a = ml_w_in.shape[0]
    n_b = ret_w_in.shape[0]

    cond = jnp.concatenate([c_ctx[None, :], c, jnp.zeros((MOD_ROWS - 1 - bs, D_MODEL), F32)], axis=0)
    mod5 = _modulation(cond, ada_w, ada_b).reshape(DEPTH, MOD_ROWS, 6, 1, D_MODEL)
    ng4 = norm_gain.reshape(DEPTH, 4, 1, D_MODEL)
    rope = _rope_tables(ts)

    ml_w_main = ml_w_in[:, :, :A_MAIN].astype(BF16)
    ml_w_out_b = ml_w_out.astype(BF16)
    ret_w_in_b = ret_w_in.astype(BF16)
    ret_w_out_b = ret_w_out.astype(BF16)
    ffn_w_up_b = ffn_w_up.astype(BF16)
    ffn_w_down_b = ffn_w_down.astype(BF16)
    ml_gn4 = ml_norm.reshape(n_a, A_HEADS, 1, A_DV)
    ret_gn4 = ret_norm.reshape(n_b, B_HEADS, 1, B_DV)
    dec_rep = jnp.broadcast_to(jnp.swapaxes(ret_decay, 1, 2)[..., None], (n_b, B_HEADS, 2, B_DV))
    st_c = state_mlstm_C
    st_n = state_mlstm_n.reshape(bs, n_a, 2, A_HEADS, 1, A_DK)
    st_m = state_mlstm_m.reshape(bs, n_a, 2, A_HEADS, 1, 1)

    groups = [
        dict(x=x_prompt.reshape(bp * tp, D_MODEL), batch=bp, seq_len=tp, sample=False),
        dict(x=x_sample.reshape(bs * ts, D_MODEL), batch=bs, seq_len=ts, sample=True),
    ]
    new_c, new_n, new_m, new_s = [], [], [], []
    for i in range(DEPTH):
        j = i // N_MIXERS
        for grp in groups:
            x = grp["x"]
            geo = dict(seq_len=grp["seq_len"], sample=grp["sample"])
            bt = dict(batch=grp["batch"], seq_len=grp["seq_len"])
            if i % N_MIXERS == 0:
                z, gi, gf = _inproj(x, ng4, mod5, i, ml_w_main[j], n_q=1, n_k=1, k_scale=A_DK ** -0.5,
                                    gates=_gate_weights(ml_w_in[j], ml_b_gate[j]), **geo)
                if grp["sample"]:
                    (h,) = _mlstm_scan(z, gi, gf, ml_gn4, j, state=(st_c, st_n, st_m), **bt)
                else:
                    h, c_new, n_new, m_new = _mlstm_scan(z, gi, gf, ml_gn4, j, **bt)
                    new_c.append(c_new)
                    new_n.append(n_new.reshape(bp, 2, A_HEADS, A_DK))
                    new_m.append(m_new.reshape(bp, 2, A_HEADS))
                x = _outproj(h, ml_w_out_b[j], x, ng4, mod5, i, **geo)
            else:
                (z,) = _inproj(x, ng4, mod5, i, ret_w_in_b[j], n_q=2, n_k=2, k_scale=B_DK ** -0.5,
                               rope=rope if grp["sample"] else None, **geo)
                if grp["sample"]:
                    (h,) = _ret_scan(z, dec_rep, ret_gn4, j, state=state_ret_S, **bt)
                else:
                    h, s_new = _ret_scan(z, dec_rep, ret_gn4, j, **bt)
                    new_s.append(s_new)
                x = _outproj(h, ret_w_out_b[j], x, ng4, mod5, i, **geo)
            grp["x"] = _ffn(x, ng4, mod5, i, ffn_w_up_b[i], ffn_conv[i], ffn_conv_b[i][None, :],
                            ffn_w_down_b[i], **geo)

    y_prompt = groups[0]["x"].reshape(bp, tp, D_MODEL)
    y_sample = groups[1]["x"].reshape(bs, ts, D_MODEL)
    return (y_prompt, y_sample, jnp.stack(new_c, 1), jnp.stack(new_n, 1), jnp.stack(new_m, 1),
            jnp.stack(new_s, 1))
```

```python
import functools
import math

import jax
import jax.numpy as jnp
from jax import lax
from jax.experimental import pallas as pl
from jax.experimental.pallas import tpu as pltpu

D_MODEL = 1024
DEPTH = 4
GRID_W = 64
CHUNK = 128
N_MIXERS = 2
A_HEADS = 4
A_DV = D_MODEL // A_HEADS
A_DK = A_DV // 2
A_MAIN = 2 * A_HEADS * A_DK + 2 * A_HEADS * A_DV
B_HEADS = 8
B_DK = D_MODEL // B_HEADS
B_DV = 2 * D_MODEL // B_HEADS
ROPE_BASE = 10000.0
D_FF = ((8 * D_MODEL // 3 + 127) // 128) * 128
EPS = 1e-6
LN2 = math.log(2.0)

F32 = jnp.float32
BF16 = jnp.bfloat16

LANES = 128
MOD_ROWS = 16
VMEM_LIMIT = 48 * 1024 * 1024

TM_PROJ = 1024
TN_PROJ = 512
TM_OUT = 512
TM_FFN = 512
TF_FFN = 256


def _dot(a, b):
    return jnp.dot(a, b, preferred_element_type=F32)


def _dot_nt(a, b):
    return lax.dot_general(a, b, (((1,), (1,)), ((), ())), preferred_element_type=F32)


def _rms(x):
    return x * lax.rsqrt(jnp.mean(x * x, axis=-1, keepdims=True) + EPS)


def _params(*sem):
    return pltpu.CompilerParams(dimension_semantics=sem, vmem_limit_bytes=VMEM_LIMIT)


def _mod_kernel(cond_ref, w_ref, b_ref, o_ref):
    cnd = cond_ref[...]
    s = cnd * jax.nn.sigmoid(cnd)
    o_ref[...] = _dot(s.astype(BF16), w_ref[...].astype(BF16)) + b_ref[...]


def _modulation(cond, ada_w, ada_b):
    tn = 1024
    n_out = ada_w.shape[-1]
    return pl.pallas_call(
        _mod_kernel,
        grid=(DEPTH, n_out // tn),
        in_specs=[
            pl.BlockSpec((MOD_ROWS, D_MODEL), lambda l, j: (0, 0)),
            pl.BlockSpec((None, D_MODEL, tn), lambda l, j: (l, 0, j)),
            pl.BlockSpec((None, 1, tn), lambda l, j: (l, 0, j)),
        ],
        out_specs=pl.BlockSpec((None, MOD_ROWS, tn), lambda l, j: (l, 0, j)),
        out_shape=jax.ShapeDtypeStruct((DEPTH, MOD_ROWS, n_out), F32),
        compiler_params=_params("parallel", "parallel"),
        name="modulation",
    )(cond, ada_w, ada_b.reshape(DEPTH, 1, n_out))


def _mod_row(sample, seq_len, tm):
    if not sample:
        return lambda i: 0
    tiles_per_seq = seq_len // tm
    return lambda i: 1 + i // tiles_per_seq


def _rope_slab(x, cos, sin, low_half):
    rot = jnp.where(low_half, pltpu.roll(x, 96, axis=1), pltpu.roll(x, 32, axis=1))
    return x * cos + rot * sin


def _inproj_kernel(*refs, n_q, n_k, k_scale, rope, gates):
    x_ref, g_ref, sh_ref, sc_ref, w_ref = refs[:5]
    pos = 5
    if gates:
        wgi_ref, wgf_ref, bgi_ref, bgf_ref = refs[pos:pos + 4]
        pos += 4
    if rope:
        cos_ref, sin_ref = refs[pos:pos + 2]
        pos += 2
    z_ref = refs[pos]
    pos += 1
    if gates:
        gi_ref, gf_ref = refs[pos:pos + 2]
        pos += 2
    u_sc = refs[pos]

    j = pl.program_id(1)

    @pl.when(j == 0)
    def _():
        u = _rms(x_ref[...]) * g_ref[...] * (1.0 + sc_ref[...]) + sh_ref[...]
        ub = u.astype(BF16)
        u_sc[...] = ub
        if gates:
            gi_ref[...] = _dot(ub, wgi_ref[...]) + bgi_ref[...]
            gf_ref[...] = _dot(ub, wgf_ref[...]) + bgf_ref[...]

    z = _dot(u_sc[...], w_ref[...])
    n_slab = z.shape[1] // LANES

    def store(scale):
        if rope:
            lane = lax.broadcasted_iota(jnp.int32, (1, LANES), 1)
            low_half = (lane & 63) < 32
            cos = cos_ref[...]
            sin = sin_ref[...]
            for s in range(n_slab):
                sl = slice(s * LANES, (s + 1) * LANES)
                r = _rope_slab(z[:, sl], cos, sin, low_half)
                if scale != 1.0:
                    r = r * scale
                z_ref[:, sl] = r.astype(BF16)
        elif scale != 1.0:
            z_ref[...] = (z * scale).astype(BF16)
        else:
            z_ref[...] = z.astype(BF16)

    is_q = j < n_q
    is_k = (j >= n_q) & (j < n_q + n_k)
    if rope:
        pl.when(is_q)(lambda: store(1.0))
        is_plain = j >= n_q + n_k
    else:
        is_plain = jnp.logical_not(is_k)
    pl.when(is_k)(lambda: store(k_scale))

    @pl.when(is_plain)
    def _():
        z_ref[...] = z.astype(BF16)


def _inproj(x, ng4, mod5, layer, w, *, seq_len, sample, n_q, n_k, k_scale, rope=None, gates=None):
    n_tok = x.shape[0]
    n_col = w.shape[1]
    tm, tn = TM_PROJ, TN_PROJ
    row = _mod_row(sample, seq_len, tm)
    in_specs = [
        pl.BlockSpec((tm, D_MODEL), lambda i, j: (i, 0)),
        pl.BlockSpec((None, None, 1, D_MODEL), lambda i, j: (layer, 0, 0, 0)),
        pl.BlockSpec((None, None, None, 1, D_MODEL), lambda i, j: (layer, row(i), 0, 0, 0)),
        pl.BlockSpec((None, None, None, 1, D_MODEL), lambda i, j: (layer, row(i), 1, 0, 0)),
        pl.BlockSpec((D_MODEL, tn), lambda i, j: (0, j)),
    ]
    args = [x, ng4, mod5, mod5, w]
    out_specs = [pl.BlockSpec((tm, tn), lambda i, j: (i, j))]
    out_shape = [jax.ShapeDtypeStruct((n_tok, n_col), BF16)]
    if gates is not None:
        in_specs += [pl.BlockSpec((D_MODEL, LANES), lambda i, j: (0, 0))] * 2
        in_specs += [pl.BlockSpec((1, LANES), lambda i, j: (0, 0))] * 2
        args += list(gates)
        out_specs += [pl.BlockSpec((tm, LANES), lambda i, j: (i, 0))] * 2
        out_shape += [jax.ShapeDtypeStruct((n_tok, LANES), F32)] * 2
    if rope is not None:
        tiles_per_seq = seq_len // tm
        in_specs += [pl.BlockSpec((tm, LANES), lambda i, j: (i % tiles_per_seq, 0))] * 2
        args += list(rope)
    kern = functools.partial(_inproj_kernel, n_q=n_q, n_k=n_k, k_scale=k_scale,
                             rope=rope is not None, gates=gates is not None)
    outs = pl.pallas_call(
        kern,
        grid=(n_tok // tm, n_col // tn),
        in_specs=in_specs,
        out_specs=out_specs,
        out_shape=out_shape,
        scratch_shapes=[pltpu.VMEM((tm, D_MODEL), BF16)],
        compiler_params=_params("parallel", "arbitrary"),
        name="inproj",
    )(*args)
    return outs


def _tri_masks():
    li = lax.broadcasted_iota(jnp.int32, (CHUNK, CHUNK), 0)
    si = lax.broadcasted_iota(jnp.int32, (CHUNK, CHUNK), 1)
    return si <= li, si >= li


def _head_norm_store(acc_ref, gate_ref, gn_ref, out_ref, n_chunks, act):
    def body(c, carry):
        t0 = pl.multiple_of(c * CHUNK, CHUNK)
        h = acc_ref[pl.ds(t0, CHUNK), :]
        mu = jnp.mean(h, axis=-1, keepdims=True)
        d = h - mu
        var = jnp.mean(d * d, axis=-1, keepdims=True)
        hn = d * lax.rsqrt(var + EPS) * gn_ref[...]
        g = gate_ref[pl.ds(t0, CHUNK), :].astype(F32)
        sg = jax.nn.sigmoid(g)
        if act == "silu":
            sg = g * sg
        out_ref[pl.ds(t0, CHUNK), :] = (hn * sg).astype(BF16)
        return carry
    lax.fori_loop(0, n_chunks, body, 0)


def _split_dot(mask_b, x):
    hi = x.astype(BF16)
    r1 = x - hi.astype(F32)
    mid = r1.astype(BF16)
    lo = (r1 - mid.astype(F32)).astype(BF16)
    return _dot(mask_b, hi) + _dot(mask_b, mid) + _dot(mask_b, lo)


def _mlstm_kernel(*refs, n_chunks, carry_in, carry_out):
    q_ref, k_ref, v_ref, o_ref, gi_ref, gf_ref, gn_ref = refs[:7]
    pos = 7
    if carry_in:
        c0_ref, n0_ref, m0_ref = refs[pos:pos + 3]
        pos += 3
    h_ref = refs[pos]
    pos += 1
    if carry_out:
        cout_ref, nout_ref, mout_ref = refs[pos:pos + 3]
        pos += 3
    acc_ref, c_sc, n_sc = refs[pos:pos + 3]

    head = pl.program_id(1)
    masks = _tri_masks()
    masks_b = [m.astype(BF16) for m in masks]
    lane = lax.broadcasted_iota(jnp.int32, (CHUNK, LANES), 1)
    sub = lax.broadcasted_iota(jnp.int32, (LANES, CHUNK), 0)

    acc_ref[...] = jnp.zeros_like(acc_ref)
    if carry_in:
        c_sc[...] = c0_ref[...]
        n_sc[...] = n0_ref[...]
        m_init = (m0_ref[0], m0_ref[1])
    else:
        c_sc[...] = jnp.zeros_like(c_sc)
        n_sc[...] = jnp.zeros_like(n_sc)
        m_init = (jnp.zeros((1, 1), F32), jnp.zeros((1, 1), F32))

    def chunk(dirn, c, m_prev):
        t0 = pl.multiple_of(c * CHUNK, CHUNK)
        q = q_ref[pl.ds(t0, CHUNK), :]
        k = k_ref[pl.ds(t0, CHUNK), :]
        v = v_ref[pl.ds(t0, CHUNK), :]
        gi = gi_ref[pl.ds(t0, CHUNK), :]
        gf = gf_ref[pl.ds(t0, CHUNK), :]
        lf = jnp.minimum(gf, 0.0) - jnp.log1p(jnp.exp(-jnp.abs(gf)))
        a_all = _split_dot(masks_b[dirn], lf)
        b_all = gi - a_all
        col = head + A_HEADS * dirn
        a_col = jnp.sum(jnp.where(lane == col, a_all, 0.0), axis=1, keepdims=True)
        b_col = jnp.sum(jnp.where(lane == col, b_all, 0.0), axis=1, keepdims=True)
        b_row = jnp.sum(jnp.where(sub == col, b_all.T, 0.0), axis=0, keepdims=True)
        g = a_col[CHUNK - 1:CHUNK, :] if dirn == 0 else a_col[0:1, :]

        dmat = jnp.where(masks[dirn], a_col + b_row, -jnp.inf)
        inter = m_prev + a_col
        m_q = jnp.maximum(inter, jnp.max(dmat, axis=1, keepdims=True))
        w_intra = jnp.exp(dmat - m_q)
        w_inter = jnp.exp(inter - m_q)
        s = _dot_nt(q, k) * w_intra
        c_old = c_sc[dirn]
        n_old = n_sc[dirn]
        num = _dot(s.astype(BF16), v) + w_inter * _dot(q, c_old.astype(BF16))
        qn = jnp.sum(q.astype(F32) * n_old, axis=1, keepdims=True)
        den = jnp.sum(s, axis=1, keepdims=True) + w_inter * qn
        hcur = num / jnp.maximum(jnp.abs(den), jnp.exp(-m_q))
        acc_ref[pl.ds(t0, CHUNK), :] += hcur

        w_s = g + b_col
        m_new = jnp.maximum(m_prev + g, jnp.max(w_s, axis=0, keepdims=True))
        ws = jnp.exp(w_s - m_new)
        dec = jnp.exp(m_prev + g - m_new)
        kw = k.astype(F32) * ws
        c_sc[dirn] = dec * c_old + _dot(kw.T.astype(BF16), v)
        n_sc[dirn] = dec * n_old + jnp.sum(kw, axis=0, keepdims=True)
        return m_new

    def body(c, carry):
        m_f, m_b = carry
        m_f = chunk(0, c, m_f)
        m_b = chunk(1, n_chunks - 1 - c, m_b)
        return m_f, m_b

    m_f, m_b = lax.fori_loop(0, n_chunks, body, m_init)
    if carry_out:
        cout_ref[...] = c_sc[...]
        nout_ref[...] = n_sc[...]
        mout_ref[0] = m_f
        mout_ref[1] = m_b
    _head_norm_store(acc_ref, o_ref, gn_ref, h_ref, n_chunks, "sigmoid")


def _mlstm_scan(z, gi, gf, gn4, j, *, batch, seq_len, state=None):
    n_tok = z.shape[0]
    n_chunks = seq_len // CHUNK
    t = seq_len
    carry_in = state is not None
    carry_out = not carry_in
    in_specs = [
        pl.BlockSpec((t, A_DK), lambda b, h: (b, h)),
        pl.BlockSpec((t, A_DK), lambda b, h: (b, A_HEADS + h)),
        pl.BlockSpec((t, A_DV), lambda b, h: (b, A_HEADS + h)),
        pl.BlockSpec((t, A_DV), lambda b, h: (b, 2 * A_HEADS + h)),
        pl.BlockSpec((t, LANES), lambda b, h: (b, 0)),
        pl.BlockSpec((t, LANES), lambda b, h: (b, 0)),
        pl.BlockSpec((None, None, 1, A_DV), lambda b, h: (j, h, 0, 0)),
    ]
    args = [z, z, z, z, gi, gf, gn4]
    if carry_in:
        c0, n0, m0 = state
        in_specs += [
            pl.BlockSpec((None, None, 2, None, A_DK, A_DV), lambda b, h: (b, j, 0, h, 0, 0)),
            pl.BlockSpec((None, None, 2, None, 1, A_DK), lambda b, h: (b, j, 0, h, 0, 0)),
            pl.BlockSpec((None, None, 2, None, 1, 1), lambda b, h: (b, j, 0, h, 0, 0)),
        ]
        args += [c0, n0, m0]
    out_specs = [pl.BlockSpec((t, A_DV), lambda b, h: (b, h))]
    out_shape = [jax.ShapeDtypeStruct((n_tok, A_HEADS * A_DV), BF16)]
    if carry_out:
        out_specs += [
            pl.BlockSpec((None, 2, None, A_DK, A_DV), lambda b, h: (b, 0, h, 0, 0)),
            pl.BlockSpec((None, 2, None, 1, A_DK), lambda b, h: (b, 0, h, 0, 0)),
            pl.BlockSpec((None, 2, None, 1, 1), lambda b, h: (b, 0, h, 0, 0)),
        ]
        out_shape += [
            jax.ShapeDtypeStruct((batch, 2, A_HEADS, A_DK, A_DV), F32),
            jax.ShapeDtypeStruct((batch, 2, A_HEADS, 1, A_DK), F32),
            jax.ShapeDtypeStruct((batch, 2, A_HEADS, 1, 1), F32),
        ]
    kern = functools.partial(_mlstm_kernel, n_chunks=n_chunks, carry_in=carry_in, carry_out=carry_out)
    return pl.pallas_call(
        kern,
        grid=(batch, A_HEADS),
        in_specs=in_specs,
        out_specs=out_specs,
        out_shape=out_shape,
        scratch_shapes=[
            pltpu.VMEM((t, A_DV), F32),
            pltpu.VMEM((2, A_DK, A_DV), F32),
            pltpu.VMEM((2, 1, A_DK), F32),
        ],
        compiler_params=_params("parallel", "parallel"),
        name="mlstm_scan",
    )(*args)


def _ret_kernel(*refs, n_chunks, carry_in, carry_out):
    q_ref, k_ref, v_ref, gate_ref, dec_ref, gn_ref = refs[:6]
    pos = 6
    if carry_in:
        s0_ref = refs[pos]
        pos += 1
    h_ref = refs[pos]
    pos += 1
    if carry_out:
        sout_ref = refs[pos]
        pos += 1
    acc_ref, s_sc = refs[pos:pos + 2]

    masks = _tri_masks()
    lg = jnp.log1p(-jnp.exp(-dec_ref[...] * LN2))
    li = lax.broadcasted_iota(jnp.int32, (CHUNK, B_DV), 0).astype(F32)
    si = lax.broadcasted_iota(jnp.int32, (CHUNK, CHUNK), 1).astype(F32)
    lg_f = lg[0:1, :]
    lg_b = lg[1:2, :]
    lq = li[:, :CHUNK]
    decay = (
        jnp.where(masks[0], jnp.exp(jnp.where(masks[0], lq - si, 0.0) * lg_f[:, :CHUNK]), 0.0),
        jnp.where(masks[1], jnp.exp(jnp.where(masks[1], si - lq, 0.0) * lg_b[:, :CHUNK]), 0.0),
    )
    xi = (jnp.exp((li + 1.0) * lg_f), jnp.exp((CHUNK - li) * lg_b))
    zeta = (jnp.exp((CHUNK - 1.0 - lq) * lg_f[:, :CHUNK]), jnp.exp(lq * lg_b[:, :CHUNK]))
    cdec = (jnp.exp(CHUNK * lg_f), jnp.exp(CHUNK * lg_b))

    acc_ref[...] = jnp.zeros_like(acc_ref)
    if carry_in:
        s_sc[...] = s0_ref[...]
    else:
        s_sc[...] = jnp.zeros_like(s_sc)

    def chunk(dirn, c):
        t0 = pl.multiple_of(c * CHUNK, CHUNK)
        q = q_ref[pl.ds(t0, CHUNK), :]
        k = k_ref[pl.ds(t0, CHUNK), :]
        v = v_ref[pl.ds(t0, CHUNK), :]
        s = _dot_nt(q, k) * decay[dirn]
        s_old = s_sc[dirn]
        o = _dot(s.astype(BF16), v) + xi[dirn] * _dot(q, s_old.astype(BF16))
        acc_ref[pl.ds(t0, CHUNK), :] += o
        kz = k.astype(F32) * zeta[dirn]
        s_sc[dirn] = cdec[dirn] * s_old + _dot(kz.T.astype(BF16), v)

    def body(c, carry):
        chunk(0, c)
        chunk(1, n_chunks - 1 - c)
        return carry

    lax.fori_loop(0, n_chunks, body, 0)
    if carry_out:
        sout_ref[...] = s_sc[...]
    _head_norm_store(acc_ref, gate_ref, gn_ref, h_ref, n_chunks, "silu")


def _ret_scan(z, dec_rep, gn4, j, *, batch, seq_len, state=None):
    n_tok = z.shape[0]
    n_chunks = seq_len // CHUNK
    t = seq_len
    carry_in = state is not None
    carry_out = not carry_in
    in_specs = [
        pl.BlockSpec((t, B_DK), lambda b, h: (b, h)),
        pl.BlockSpec((t, B_DK), lambda b, h: (b, B_HEADS + h)),
        pl.BlockSpec((t, B_DV), lambda b, h: (b, B_HEADS + h)),
        pl.BlockSpec((t, B_DV), lambda b, h: (b, 2 * B_HEADS + h)),
        pl.BlockSpec((None, None, 2, B_DV), lambda b, h: (j, h, 0, 0)),
        pl.BlockSpec((None, None, 1, B_DV), lambda b, h: (j, h, 0, 0)),
    ]
    args = [z, z, z, z, dec_rep, gn4]
    if carry_in:
        in_specs += [pl.BlockSpec((None, None, 2, None, B_DK, B_DV), lambda b, h: (b, j, 0, h, 0, 0))]
        args += [state]
    out_specs = [pl.BlockSpec((t, B_DV), lambda b, h: (b, h))]
    out_shape = [jax.ShapeDtypeStruct((n_tok, B_HEADS * B_DV), BF16)]
    if carry_out:
        out_specs += [pl.BlockSpec((None, 2, None, B_DK, B_DV), lambda b, h: (b, 0, h, 0, 0))]
        out_shape += [jax.ShapeDtypeStruct((batch, 2, B_HEADS, B_DK, B_DV), F32)]
    kern = functools.partial(_ret_kernel, n_chunks=n_chunks, carry_in=carry_in, carry_out=carry_out)
    return pl.pallas_call(
        kern,
        grid=(batch, B_HEADS),
        in_specs=in_specs,
        out_specs=out_specs,
        out_shape=out_shape,
        scratch_shapes=[
            pltpu.VMEM((t, B_DV), F32),
            pltpu.VMEM((2, B_DK, B_DV), F32),
        ],
        compiler_params=_params("parallel", "parallel"),
        name="ret_scan",
    )(*args)


def _outproj_kernel(h_ref, w_ref, x_ref, g_ref, gate_ref, o_ref):
    y = _dot(h_ref[...], w_ref[...])
    o_ref[...] = x_ref[...] + gate_ref[...] * (_rms(y) * g_ref[...])


def _outproj(h, w, x, ng4, mod5, layer, *, seq_len, sample):
    n_tok, hv = h.shape
    tm = TM_OUT
    row = _mod_row(sample, seq_len, tm)
    return pl.pallas_call(
        _outproj_kernel,
        grid=(n_tok // tm,),
        in_specs=[
            pl.BlockSpec((tm, hv), lambda i: (i, 0)),
            pl.BlockSpec((hv, D_MODEL), lambda i: (0, 0)),
            pl.BlockSpec((tm, D_MODEL), lambda i: (i, 0)),
            pl.BlockSpec((None, None, 1, D_MODEL), lambda i: (layer, 1, 0, 0)),
            pl.BlockSpec((None, None, None, 1, D_MODEL), lambda i: (layer, row(i), 2, 0, 0)),
        ],
        out_specs=pl.BlockSpec((tm, D_MODEL), lambda i: (i, 0)),
        out_shape=jax.ShapeDtypeStruct((n_tok, D_MODEL), F32),
        compiler_params=_params("parallel"),
        name="outproj",
    )(h, w, x, ng4, mod5)


CONV_PAD = 8


def _conv3_act(hs_ref, half, h, cw, cb, seg, n_seg):
    for s in range(n_seg):
        base = CONV_PAD + s * (seg + CONV_PAD)
        h_seg = h[s * seg:(s + 1) * seg, :]
        hs_ref[2 * half, base + 1:base + 1 + seg, :] = h_seg
        hs_ref[2 * half + 1, base - 1:base - 1 + seg, :] = h_seg
    parts = []
    for s in range(n_seg):
        base = CONV_PAD + s * (seg + CONV_PAD)
        h_prev = hs_ref[2 * half, base:base + seg, :]
        h_next = hs_ref[2 * half + 1, base:base + seg, :]
        h_mid = h[s * seg:(s + 1) * seg, :]
        parts.append(h_prev * cw[0:1, :] + h_mid * cw[1:2, :] + h_next * cw[2:3, :] + cb)
    return parts


def _ffn_kernel(x_ref, g2_ref, sh_ref, sc_ref, wup_ref, cw_ref, cb_ref, wd_ref, g3_ref, gate_ref,
                o_ref, u_sc, act_sc, hs_sc, *, seg, n_seg, n_f, tf):
    u = _rms(x_ref[...]) * g2_ref[...] * (1.0 + sc_ref[...]) + sh_ref[...]
    u_sc[...] = u.astype(BF16)
    zero_rows = jnp.zeros((CONV_PAD, tf), F32)
    for s in range(n_seg):
        base = CONV_PAD + s * (seg + CONV_PAD)
        for half in range(2):
            hs_sc[2 * half, base:base + CONV_PAD, :] = zero_rows
            hs_sc[2 * half + 1, base + seg - CONV_PAD:base + seg, :] = zero_rows

    for cidx in range(n_f):
        ub = u_sc[...]
        hg = _conv3_act(hs_sc, 0, _dot(ub, wup_ref[cidx]), cw_ref[cidx], cb_ref[cidx], seg, n_seg)
        hu = _conv3_act(hs_sc, 1, _dot(ub, wup_ref[n_f + cidx]), cw_ref[n_f + cidx], cb_ref[n_f + cidx],
                        seg, n_seg)
        for s in range(n_seg):
            act = jax.nn.gelu(hg[s], approximate=True) * hu[s]
            act_sc[s * seg:(s + 1) * seg, cidx * tf:(cidx + 1) * tf] = act.astype(BF16)

    f = _dot(act_sc[...], wd_ref[...])
    o_ref[...] = x_ref[...] + gate_ref[...] * (_rms(f) * g3_ref[...])


def _ffn(x, ng4, mod5, layer, w_up3, conv_w3, conv_b3, w_down, *, seq_len, sample):
    n_tok = x.shape[0]
    tm, tf = TM_FFN, TF_FFN
    n_f = D_FF // tf
    row = _mod_row(sample, seq_len, tm)
    seg = GRID_W if sample else seq_len
    n_seg = tm // seg
    kern = functools.partial(_ffn_kernel, seg=seg, n_seg=n_seg, n_f=n_f, tf=tf)
    mod_spec = lambda k: pl.BlockSpec((None, None, None, 1, D_MODEL), lambda i: (layer, row(i), k, 0, 0))
    gain_spec = lambda k: pl.BlockSpec((None, None, 1, D_MODEL), lambda i: (layer, k, 0, 0))
    resident = lambda shape: pl.BlockSpec(shape, lambda i: (0,) * len(shape), pipeline_mode=pl.Buffered(1))
    return pl.pallas_call(
        kern,
        grid=(n_tok // tm,),
        in_specs=[
            pl.BlockSpec((tm, D_MODEL), lambda i: (i, 0)),
            gain_spec(2),
            mod_spec(3),
            mod_spec(4),
            resident((2 * n_f, D_MODEL, tf)),
            resident((2 * n_f, 3, tf)),
            resident((2 * n_f, 1, tf)),
            resident((D_FF, D_MODEL)),
            gain_spec(3),
            mod_spec(5),
        ],
        out_specs=pl.BlockSpec((tm, D_MODEL), lambda i: (i, 0)),
        out_shape=jax.ShapeDtypeStruct((n_tok, D_MODEL), F32),
        scratch_shapes=[
            pltpu.VMEM((tm, D_MODEL), BF16),
            pltpu.VMEM((tm, D_FF), BF16),
            pltpu.VMEM((4, CONV_PAD + n_seg * (seg + CONV_PAD), tf), F32),
        ],
        compiler_params=_params("parallel"),
        name="convffn",
    )(x, ng4, mod5, mod5, w_up3, conv_w3, conv_b3, w_down, ng4, mod5)


def _rope_tables(seq_len):
    quarter = B_DK // 4
    inv = ROPE_BASE ** (-jnp.arange(quarter, dtype=F32) / quarter)
    t = jnp.arange(seq_len)
    rows = (t // GRID_W).astype(F32)[:, None] * inv
    cols = (t % GRID_W).astype(F32)[:, None] * inv
    cos = jnp.concatenate([jnp.cos(rows)] * 2 + [jnp.cos(cols)] * 2, axis=-1)
    sin = jnp.concatenate([-jnp.sin(rows), jnp.sin(rows), -jnp.sin(cols), jnp.sin(cols)], axis=-1)
    return cos, sin


def _gate_weights(w_in_j, b_gate_j):
    wg = w_in_j[:, A_MAIN:].reshape(D_MODEL, 4, A_HEADS)
    pad_w = jnp.zeros((D_MODEL, LANES - 2 * A_HEADS), F32)
    pad_b = jnp.zeros((LANES - 2 * A_HEADS,), F32)
    wgi = jnp.concatenate([wg[:, 0], wg[:, 2], pad_w], axis=1).astype(BF16)
    wgf = jnp.concatenate([wg[:, 1], wg[:, 3], pad_w], axis=1).astype(BF16)
    bgi = jnp.concatenate([b_gate_j[0], b_gate_j[2], pad_b])[None, :]
    bgf = jnp.concatenate([b_gate_j[1], b_gate_j[3], pad_b])[None, :]
    return wgi, wgf, bgi, bgf


def kernel(x_prompt, x_sample, state_mlstm_C, state_mlstm_n, state_mlstm_m, state_ret_S, c, c_ctx,
           norm_gain, ada_w, ada_b, ml_w_in, ml_b_gate, ml_norm, ml_w_out,
           ret_w_in, ret_decay, ret_norm, ret_w_out, ffn_w_up, ffn_conv, ffn_conv_b, ffn_w_down):
    bp, tp, _ = x_prompt.shape
    bs, ts, _ = x_sample.shape
    n_a = ml_w_in.shape[0]
    n_b = ret_w_in.shape[0]

    cond = jnp.concatenate([c_ctx[None, :], c, jnp.zeros((MOD_ROWS - 1 - bs, D_MODEL), F32)], axis=0)
    mod5 = _modulation(cond, ada_w, ada_b).reshape(DEPTH, MOD_ROWS, 6, 1, D_MODEL)
    ng4 = norm_gain.reshape(DEPTH, 4, 1, D_MODEL)
    rope = _rope_tables(ts)

    ml_w_main = ml_w_in[:, :, :A_MAIN].astype(BF16)
    ml_w_out_b = ml_w_out.astype(BF16)
    ret_w_in_b = ret_w_in.astype(BF16)
    ret_w_out_b = ret_w_out.astype(BF16)
    n_f2 = 2 * D_FF // TF_FFN
    ffn_w_up_b = jnp.swapaxes(ffn_w_up.astype(BF16).reshape(DEPTH, D_MODEL, n_f2, TF_FFN), 1, 2)
    ffn_conv3 = jnp.swapaxes(ffn_conv.reshape(DEPTH, 3, n_f2, TF_FFN), 1, 2)
    ffn_conv_b3 = ffn_conv_b.reshape(DEPTH, n_f2, 1, TF_FFN)
    ffn_w_down_b = ffn_w_down.astype(BF16)
    ml_gn4 = ml_norm.reshape(n_a, A_HEADS, 1, A_DV)
    ret_gn4 = ret_norm.reshape(n_b, B_HEADS, 1, B_DV)
    dec_rep = jnp.broadcast_to(jnp.swapaxes(ret_decay, 1, 2)[..., None], (n_b, B_HEADS, 2, B_DV))
    st_c = state_mlstm_C
    st_n = state_mlstm_n.reshape(bs, n_a, 2, A_HEADS, 1, A_DK)
    st_m = state_mlstm_m.reshape(bs, n_a, 2, A_HEADS, 1, 1)

    groups = [
        dict(x=x_prompt.reshape(bp * tp, D_MODEL), batch=bp, seq_len=tp, sample=False),
        dict(x=x_sample.reshape(bs * ts, D_MODEL), batch=bs, seq_len=ts, sample=True),
    ]
    new_c, new_n, new_m, new_s = [], [], [], []
    for i in range(DEPTH):
        j = i // N_MIXERS
        for grp in groups:
            x = grp["x"]
            geo = dict(seq_len=grp["seq_len"], sample=grp["sample"])
            bt = dict(batch=grp["batch"], seq_len=grp["seq_len"])
            if i % N_MIXERS == 0:
                z, gi, gf = _inproj(x, ng4, mod5, i, ml_w_main[j], n_q=1, n_k=1, k_scale=A_DK ** -0.5,
                                    gates=_gate_weights(ml_w_in[j], ml_b_gate[j]), **geo)
                if grp["sample"]:
                    (h,) = _mlstm_scan(z, gi, gf, ml_gn4, j, state=(st_c, st_n, st_m), **bt)
                else:
                    h, c_new, n_new, m_new = _mlstm_scan(z, gi, gf, ml_gn4, j, **bt)
                    new_c.append(c_new)
                    new_n.append(n_new.reshape(bp, 2, A_HEADS, A_DK))
                    new_m.append(m_new.reshape(bp, 2, A_HEADS))
                x = _outproj(h, ml_w_out_b[j], x, ng4, mod5, i, **geo)
            else:
                (z,) = _inproj(x, ng4, mod5, i, ret_w_in_b[j], n_q=2, n_k=2, k_scale=B_DK ** -0.5,
                               rope=rope if grp["sample"] else None, **geo)
                if grp["sample"]:
                    (h,) = _ret_scan(z, dec_rep, ret_gn4, j, state=state_ret_S, **bt)
                else:
                    h, s_new = _ret_scan(z, dec_rep, ret_gn4, j, **bt)
                    new_s.append(s_new)
                x = _outproj(h, ret_w_out_b[j], x, ng4, mod5, i, **geo)
            grp["x"] = _ffn(x, ng4, mod5, i, ffn_w_up_b[i], ffn_conv3[i], ffn_conv_b3[i],
                            ffn_w_down_b[i], **geo)

    y_prompt = groups[0]["x"].reshape(bp, tp, D_MODEL)
    y_sample = groups[1]["x"].reshape(bs, ts, D_MODEL)
    return (y_prompt, y_sample, jnp.stack(new_c, 1), jnp.stack(new_n, 1), jnp.stack(new_m, 1),
            jnp.stack(new_s, 1))
```

```python
import functools
import math

import jax
import jax.numpy as jnp
from jax import lax
from jax.experimental import pallas as pl
from jax.experimental.pallas import tpu as pltpu

D_MODEL = 1024
DEPTH = 4
GRID_W = 64
CHUNK = 128
N_MIXERS = 2
A_HEADS = 4
A_DV = D_MODEL // A_HEADS
A_DK = A_DV // 2
A_MAIN = 2 * A_HEADS * A_DK + 2 * A_HEADS * A_DV
B_HEADS = 8
B_DK = D_MODEL // B_HEADS
B_DV = 2 * D_MODEL // B_HEADS
ROPE_BASE = 10000.0
D_FF = ((8 * D_MODEL // 3 + 127) // 128) * 128
EPS = 1e-6
LN2 = math.log(2.0)

F32 = jnp.float32
BF16 = jnp.bfloat16

LANES = 128
MOD_ROWS = 16
VMEM_LIMIT = 48 * 1024 * 1024

TM_PROJ = 1024
TN_PROJ = 512
TM_OUT = 512
TM_FFN = 512
TF_FFN = 256
STATE_UNROLL = 4
OUT_UNROLL = 8
MLSTM_OUT_UNROLL = 4


def _dot(a, b):
    return jnp.dot(a, b, preferred_element_type=F32)


def _dot_nt(a, b):
    return lax.dot_general(a, b, (((1,), (1,)), ((), ())), preferred_element_type=F32)


def _dot_tn(a, b):
    return lax.dot_general(a, b, (((0,), (0,)), ((), ())), preferred_element_type=F32)


def _rms(x):
    return x * lax.rsqrt(jnp.mean(x * x, axis=-1, keepdims=True) + EPS)


def _layer_norm(h):
    d = h - jnp.mean(h, axis=-1, keepdims=True)
    return d * lax.rsqrt(jnp.mean(d * d, axis=-1, keepdims=True) + EPS)


def _params(*sem):
    return pltpu.CompilerParams(dimension_semantics=sem, vmem_limit_bytes=VMEM_LIMIT)


def _mod_kernel(cond_ref, w_ref, b_ref, o_ref):
    cnd = cond_ref[...]
    s = cnd * jax.nn.sigmoid(cnd)
    o_ref[...] = _dot(s.astype(BF16), w_ref[...].astype(BF16)) + b_ref[...]


def _modulation(cond, ada_w, ada_b):
    tn = 1024
    n_out = ada_w.shape[-1]
    return pl.pallas_call(
        _mod_kernel,
        grid=(DEPTH, n_out // tn),
        in_specs=[
            pl.BlockSpec((MOD_ROWS, D_MODEL), lambda l, j: (0, 0)),
            pl.BlockSpec((None, D_MODEL, tn), lambda l, j: (l, 0, j)),
            pl.BlockSpec((None, 1, tn), lambda l, j: (l, 0, j)),
        ],
        out_specs=pl.BlockSpec((None, MOD_ROWS, tn), lambda l, j: (l, 0, j)),
        out_shape=jax.ShapeDtypeStruct((DEPTH, MOD_ROWS, n_out), F32),
        compiler_params=_params("parallel", "parallel"),
        name="modulation",
    )(cond, ada_w, ada_b.reshape(DEPTH, 1, n_out))


def _mod_row(sample, seq_len, tm):
    if not sample:
        return lambda i: 0
    tiles_per_seq = seq_len // tm
    return lambda i: 1 + i // tiles_per_seq


def _rope_slab(x, cos, sin, low_half):
    rot = jnp.where(low_half, pltpu.roll(x, 96, axis=1), pltpu.roll(x, 32, axis=1))
    return x * cos + rot * sin


def _inproj_kernel(*refs, n_q, n_k, k_scale, rope, gates):
    x_ref, g_ref, sh_ref, sc_ref, w_ref = refs[:5]
    pos = 5
    if gates:
        wgi_ref, wgf_ref, bgi_ref, bgf_ref = refs[pos:pos + 4]
        pos += 4
    if rope:
        cos_ref, sin_ref = refs[pos:pos + 2]
        pos += 2
    z_ref = refs[pos]
    pos += 1
    if gates:
        gi_ref, gf_ref = refs[pos:pos + 2]
        pos += 2
    u_sc = refs[pos]

    j = pl.program_id(1)

    @pl.when(j == 0)
    def _():
        u = _rms(x_ref[...]) * g_ref[...] * (1.0 + sc_ref[...]) + sh_ref[...]
        ub = u.astype(BF16)
        u_sc[...] = ub
        if gates:
            gi_ref[...] = _dot(ub, wgi_ref[...]) + bgi_ref[...]
            gf_ref[...] = _dot(ub, wgf_ref[...]) + bgf_ref[...]

    z = _dot(u_sc[...], w_ref[...])
    n_slab = z.shape[1] // LANES

    def store(scale):
        if rope:
            lane = lax.broadcasted_iota(jnp.int32, (1, LANES), 1)
            low_half = (lane & 63) < 32
            cos = cos_ref[...]
            sin = sin_ref[...]
            for s in range(n_slab):
                sl = slice(s * LANES, (s + 1) * LANES)
                r = _rope_slab(z[:, sl], cos, sin, low_half)
                if scale != 1.0:
                    r = r * scale
                z_ref[:, sl] = r.astype(BF16)
        elif scale != 1.0:
            z_ref[...] = (z * scale).astype(BF16)
        else:
            z_ref[...] = z.astype(BF16)

    is_q = j < n_q
    is_k = (j >= n_q) & (j < n_q + n_k)
    if rope:
        pl.when(is_q)(lambda: store(1.0))
        is_plain = j >= n_q + n_k
    else:
        is_plain = jnp.logical_not(is_k)
    pl.when(is_k)(lambda: store(k_scale))

    @pl.when(is_plain)
    def _():
        z_ref[...] = z.astype(BF16)


def _inproj(x, ng4, mod5, layer, w, *, seq_len, sample, n_q, n_k, k_scale, rope=None, gates=None):
    n_tok = x.shape[0]
    n_col = w.shape[1]
    tm, tn = TM_PROJ, TN_PROJ
    row = _mod_row(sample, seq_len, tm)
    in_specs = [
        pl.BlockSpec((tm, D_MODEL), lambda i, j: (i, 0)),
        pl.BlockSpec((None, None, 1, D_MODEL), lambda i, j: (layer, 0, 0, 0)),
        pl.BlockSpec((None, None, None, 1, D_MODEL), lambda i, j: (layer, row(i), 0, 0, 0)),
        pl.BlockSpec((None, None, None, 1, D_MODEL), lambda i, j: (layer, row(i), 1, 0, 0)),
        pl.BlockSpec((D_MODEL, tn), lambda i, j: (0, j)),
    ]
    args = [x, ng4, mod5, mod5, w]
    out_specs = [pl.BlockSpec((tm, tn), lambda i, j: (i, j))]
    out_shape = [jax.ShapeDtypeStruct((n_tok, n_col), BF16)]
    if gates is not None:
        in_specs += [pl.BlockSpec((D_MODEL, LANES), lambda i, j: (0, 0))] * 2
        in_specs += [pl.BlockSpec((1, LANES), lambda i, j: (0, 0))] * 2
        args += list(gates)
        out_specs += [pl.BlockSpec((tm, LANES), lambda i, j: (i, 0))] * 2
        out_shape += [jax.ShapeDtypeStruct((n_tok, LANES), F32)] * 2
    if rope is not None:
        tiles_per_seq = seq_len // tm
        in_specs += [pl.BlockSpec((tm, LANES), lambda i, j: (i % tiles_per_seq, 0))] * 2
        args += list(rope)
    kern = functools.partial(_inproj_kernel, n_q=n_q, n_k=n_k, k_scale=k_scale,
                             rope=rope is not None, gates=gates is not None)
    outs = pl.pallas_call(
        kern,
        grid=(n_tok // tm, n_col // tn),
        in_specs=in_specs,
        out_specs=out_specs,
        out_shape=out_shape,
        scratch_shapes=[pltpu.VMEM((tm, D_MODEL), BF16)],
        compiler_params=_params("parallel", "arbitrary"),
        name="inproj",
    )(*args)
    return outs


def _tri_masks():
    li = lax.broadcasted_iota(jnp.int32, (CHUNK, CHUNK), 0)
    si = lax.broadcasted_iota(jnp.int32, (CHUNK, CHUNK), 1)
    return si <= li, si >= li


def _split_dot(mask_b, x):
    hi = x.astype(BF16)
    r1 = x - hi.astype(F32)
    mid = r1.astype(BF16)
    lo = (r1 - mid.astype(F32)).astype(BF16)
    return _dot(mask_b, hi) + _dot(mask_b, mid) + _dot(mask_b, lo)


def _mlstm_kernel(*refs, n_chunks, carry_in, carry_out):
    q_ref, k_ref, v_ref, o_ref, gi_ref, gf_ref, gn_ref = refs[:7]
    pos = 7
    if carry_in:
        c0_ref, n0_ref, m0_ref = refs[pos:pos + 3]
        pos += 3
    h_ref = refs[pos]
    pos += 1
    if carry_out:
        cout_ref, nout_ref, mout_ref = refs[pos:pos + 3]
        pos += 3
    a_sc, b_sc, bt_sc, g_sc, bm_sc, mpf_sc, mpb_sc, c_sc, call_sc = refs[pos:pos + 9]

    head = pl.program_id(1)
    masks = _tri_masks()
    tril_b = masks[0].astype(BF16)
    ones_b = jnp.ones((CHUNK, LANES), BF16)
    lane = lax.broadcasted_iota(jnp.int32, (CHUNK, LANES), 1)
    lane_row = lax.broadcasted_iota(jnp.int32, (1, LANES), 1)
    fwd_lane = lane < A_HEADS
    cols = (head, head + A_HEADS)
    mp_sc = (mpf_sc, mpb_sc)

    def pick_col(x, col):
        return jnp.sum(jnp.where(lane == col, x, 0.0), axis=1, keepdims=True)

    def pick_scalar(row, col):
        return jnp.sum(jnp.where(lane_row == col, row, 0.0), axis=1, keepdims=True)

    def at(c):
        return pl.ds(pl.multiple_of(c * CHUNK, CHUNK), CHUNK)

    def gate_body(c, carry):
        gf = gf_ref[at(c), :]
        lf = jnp.minimum(gf, 0.0) - jnp.log1p(jnp.exp(-jnp.abs(gf)))
        a_f = _split_dot(tril_b, lf)
        tot = a_f[CHUNK - 1:CHUNK, :]
        a_all = jnp.where(fwd_lane, a_f, tot - a_f + lf)
        b_all = gi_ref[at(c), :] - a_all
        a_sc[at(c), :] = a_all
        b_sc[at(c), :] = b_all
        bt_sc[c] = b_all.T[0:2 * A_HEADS, :]
        g_sc[pl.ds(c, 1), :] = tot
        bm_sc[pl.ds(c, 1), :] = jnp.max(b_all, axis=0, keepdims=True)
        return carry

    lax.fori_loop(0, n_chunks, gate_body, 0, unroll=min(n_chunks, STATE_UNROLL))

    if carry_in:
        m_init = (jnp.broadcast_to(m0_ref[0], (1, LANES)), jnp.broadcast_to(m0_ref[1], (1, LANES)))
    else:
        m_init = (jnp.zeros((1, LANES), F32), jnp.zeros((1, LANES), F32))

    def m_body(i, carry):
        m_f, m_b = carry
        ib = n_chunks - 1 - i
        mpf_sc[pl.ds(i, 1), :] = m_f
        mpb_sc[pl.ds(ib, 1), :] = m_b
        m_f = g_sc[pl.ds(i, 1), :] + jnp.maximum(m_f, bm_sc[pl.ds(i, 1), :])
        m_b = g_sc[pl.ds(ib, 1), :] + jnp.maximum(m_b, bm_sc[pl.ds(ib, 1), :])
        return m_f, m_b

    m_last = lax.fori_loop(0, n_chunks, m_body, m_init)

    if carry_in:
        for dirn in range(2):
            c_sc[dirn, :, :A_DV] = c0_ref[dirn]
            c_sc[dirn, :, A_DV:] = jnp.broadcast_to(n0_ref[dirn], (A_DK, LANES))
    else:
        c_sc[...] = jnp.zeros_like(c_sc)

    def state_body(i, carry):
        for dirn in range(2):
            c = i if dirn == 0 else n_chunks - 1 - i
            mp_row = mp_sc[dirn][pl.ds(c, 1), :]
            m_top = pick_scalar(jnp.maximum(mp_row, bm_sc[pl.ds(c, 1), :]), cols[dirn])
            ws = jnp.exp(pick_col(b_sc[at(c), :], cols[dirn]) - m_top)
            dec = jnp.exp(pick_scalar(mp_row, cols[dirn]) - m_top)
            c_old = c_sc[dirn]
            call_sc[dirn, c] = c_old.astype(BF16)
            kw = (k_ref[at(c), :].astype(F32) * ws).astype(BF16)
            upd = jnp.concatenate([_dot_tn(kw, v_ref[at(c), :]), _dot_tn(kw, ones_b)], axis=1)
            c_sc[dirn] = dec * c_old + upd
        return carry

    lax.fori_loop(0, n_chunks, state_body, 0, unroll=min(n_chunks, STATE_UNROLL))
    if carry_out:
        for dirn in range(2):
            cout_ref[dirn] = c_sc[dirn, :, :A_DV]
            nout_ref[dirn] = c_sc[dirn, :, A_DV:A_DV + 1]
            mout_ref[dirn] = pick_scalar(m_last[dirn], cols[dirn])

    def out_body(c, carry):
        q = q_ref[at(c), :]
        v = v_ref[at(c), :]
        s_raw = _dot_nt(q, k_ref[at(c), :])
        a_chunk = a_sc[at(c), :]
        h = None
        for dirn in range(2):
            col = cols[dirn]
            m_prev = pick_scalar(mp_sc[dirn][pl.ds(c, 1), :], col)
            b_vis = jnp.where(masks[dirn], bt_sc[c, pl.ds(col, 1), :], -jnp.inf)
            m_row = jnp.maximum(m_prev, jnp.max(b_vis, axis=1, keepdims=True))
            sw = (s_raw * jnp.exp(b_vis - m_row)).astype(BF16)
            w_inter = jnp.exp(m_prev - m_row)
            floor = jnp.exp(-(pick_col(a_chunk, col) + m_row))
            inter = _dot(q, call_sc[dirn, c])
            num = _dot(sw, v) + w_inter * inter[:, :A_DV]
            den = _dot(sw, ones_b) + w_inter * inter[:, A_DV:]
            r = 1.0 / jnp.maximum(jnp.abs(den), floor)
            hd = num * jnp.concatenate([r, r], axis=1)
            h = hd if h is None else h + hd
        o = o_ref[at(c), :].astype(F32)
        h_ref[at(c), :] = (_layer_norm(h) * gn_ref[...] * jax.nn.sigmoid(o)).astype(BF16)
        return carry

    lax.fori_loop(0, n_chunks, out_body, 0, unroll=min(n_chunks, MLSTM_OUT_UNROLL))


def _mlstm_scan(z, gi, gf, gn4, j, *, batch, seq_len, state=None):
    n_tok = z.shape[0]
    n_chunks = seq_len // CHUNK
    t = seq_len
    carry_in = state is not None
    carry_out = not carry_in
    in_specs = [
        pl.BlockSpec((t, A_DK), lambda b, h: (b, h)),
        pl.BlockSpec((t, A_DK), lambda b, h: (b, A_HEADS + h)),
        pl.BlockSpec((t, A_DV), lambda b, h: (b, A_HEADS + h)),
        pl.BlockSpec((t, A_DV), lambda b, h: (b, 2 * A_HEADS + h)),
        pl.BlockSpec((t, LANES), lambda b, h: (b, 0)),
        pl.BlockSpec((t, LANES), lambda b, h: (b, 0)),
        pl.BlockSpec((None, None, 1, A_DV), lambda b, h: (j, h, 0, 0)),
    ]
    args = [z, z, z, z, gi, gf, gn4]
    if carry_in:
        c0, n0, m0 = state
        in_specs += [
            pl.BlockSpec((None, None, 2, None, A_DK, A_DV), lambda b, h: (b, j, 0, h, 0, 0)),
            pl.BlockSpec((None, None, 2, None, A_DK, 1), lambda b, h: (b, j, 0, h, 0, 0)),
            pl.BlockSpec((None, None, 2, None, 1, 1), lambda b, h: (b, j, 0, h, 0, 0)),
        ]
        args += [c0, n0, m0]
    out_specs = [pl.BlockSpec((t, A_DV), lambda b, h: (b, h))]
    out_shape = [jax.ShapeDtypeStruct((n_tok, A_HEADS * A_DV), BF16)]
    if carry_out:
        out_specs += [
            pl.BlockSpec((None, 2, None, A_DK, A_DV), lambda b, h: (b, 0, h, 0, 0)),
            pl.BlockSpec((None, 2, None, A_DK, 1), lambda b, h: (b, 0, h, 0, 0)),
            pl.BlockSpec((None, 2, None, 1, 1), lambda b, h: (b, 0, h, 0, 0)),
        ]
        out_shape += [
            jax.ShapeDtypeStruct((batch, 2, A_HEADS, A_DK, A_DV), F32),
            jax.ShapeDtypeStruct((batch, 2, A_HEADS, A_DK, 1), F32),
            jax.ShapeDtypeStruct((batch, 2, A_HEADS, 1, 1), F32),
        ]
    kern = functools.partial(_mlstm_kernel, n_chunks=n_chunks, carry_in=carry_in, carry_out=carry_out)
    return pl.pallas_call(
        kern,
        grid=(batch, A_HEADS),
        in_specs=in_specs,
        out_specs=out_specs,
        out_shape=out_shape,
        scratch_shapes=[
            pltpu.VMEM((t, LANES), F32),
            pltpu.VMEM((t, LANES), F32),
            pltpu.VMEM((n_chunks, 2 * A_HEADS, CHUNK), F32),
            pltpu.VMEM((n_chunks, LANES), F32),
            pltpu.VMEM((n_chunks, LANES), F32),
            pltpu.VMEM((n_chunks, LANES), F32),
            pltpu.VMEM((n_chunks, LANES), F32),
            pltpu.VMEM((2, A_DK, A_DV + LANES), F32),
            pltpu.VMEM((2, n_chunks, A_DK, A_DV + LANES), BF16),
        ],
        compiler_params=_params("parallel", "parallel"),
        name="mlstm_scan",
    )(*args)


def _ret_kernel(*refs, n_chunks, carry_in, carry_out):
    q_ref, k_ref, v_ref, gate_ref, dec_ref, gn_ref = refs[:6]
    pos = 6
    if carry_in:
        s0_ref = refs[pos]
        pos += 1
    h_ref = refs[pos]
    pos += 1
    if carry_out:
        sout_ref = refs[pos]
        pos += 1
    s_sc, sall_sc, dsum_sc, xi_sc, zeta_sc = refs[pos:pos + 5]

    lg = jnp.log1p(-jnp.exp(-dec_ref[...] * LN2))
    lg_f = lg[0:1, :]
    lg_b = lg[1:2, :]

    @pl.when(pl.program_id(1) == 0)
    def _():
        masks = _tri_masks()
        li = lax.broadcasted_iota(jnp.int32, (CHUNK, B_DV), 0).astype(F32)
        si = lax.broadcasted_iota(jnp.int32, (CHUNK, CHUNK), 1).astype(F32)
        lq = li[:, :CHUNK]
        dsum_sc[...] = (
            jnp.where(masks[0], jnp.exp(jnp.where(masks[0], lq - si, 0.0) * lg_f[:, :CHUNK]), 0.0)
            + jnp.where(masks[1], jnp.exp(jnp.where(masks[1], si - lq, 0.0) * lg_b[:, :CHUNK]), 0.0))
        xi_sc[0] = jnp.exp((li + 1.0) * lg_f)
        xi_sc[1] = jnp.exp((CHUNK - li) * lg_b)
        zeta_sc[0] = jnp.exp((CHUNK - 1.0 - lq) * lg_f[:, :CHUNK])
        zeta_sc[1] = jnp.exp(lq * lg_b[:, :CHUNK])

    cdec = (jnp.exp(CHUNK * lg_f), jnp.exp(CHUNK * lg_b))

    if carry_in:
        s_sc[...] = s0_ref[...]
    else:
        s_sc[...] = jnp.zeros_like(s_sc)

    def state_body(i, carry):
        for dirn in range(2):
            c = i if dirn == 0 else n_chunks - 1 - i
            t0 = pl.multiple_of(c * CHUNK, CHUNK)
            s_old = s_sc[dirn]
            sall_sc[dirn, c] = s_old.astype(BF16)
            kz = (k_ref[pl.ds(t0, CHUNK), :].astype(F32) * zeta_sc[dirn]).astype(BF16)
            s_sc[dirn] = cdec[dirn] * s_old + _dot_tn(kz, v_ref[pl.ds(t0, CHUNK), :])
        return carry

    lax.fori_loop(0, n_chunks, state_body, 0, unroll=min(n_chunks, STATE_UNROLL))
    if carry_out:
        sout_ref[...] = s_sc[...]

    def out_body(c, carry):
        t0 = pl.multiple_of(c * CHUNK, CHUNK)
        q = q_ref[pl.ds(t0, CHUNK), :]
        k = k_ref[pl.ds(t0, CHUNK), :]
        v = v_ref[pl.ds(t0, CHUNK), :]
        s = _dot_nt(q, k) * dsum_sc[...]
        h = (_dot(s.astype(BF16), v) + xi_sc[0] * _dot(q, sall_sc[0, c])
             + xi_sc[1] * _dot(q, sall_sc[1, c]))
        g = gate_ref[pl.ds(t0, CHUNK), :].astype(F32)
        h_ref[pl.ds(t0, CHUNK), :] = (_layer_norm(h) * gn_ref[...] * (g * jax.nn.sigmoid(g))).astype(BF16)
        return carry

    lax.fori_loop(0, n_chunks, out_body, 0, unroll=min(n_chunks, OUT_UNROLL))


def _ret_scan(z, dec_rep, gn4, j, *, batch, seq_len, state=None):
    n_tok = z.shape[0]
    n_chunks = seq_len // CHUNK
    t = seq_len
    carry_in = state is not None
    carry_out = not carry_in
    in_specs = [
        pl.BlockSpec((t, B_DK), lambda h, b: (b, h)),
        pl.BlockSpec((t, B_DK), lambda h, b: (b, B_HEADS + h)),
        pl.BlockSpec((t, B_DV), lambda h, b: (b, B_HEADS + h)),
        pl.BlockSpec((t, B_DV), lambda h, b: (b, 2 * B_HEADS + h)),
        pl.BlockSpec((None, None, 2, B_DV), lambda h, b: (j, h, 0, 0)),
        pl.BlockSpec((None, None, 1, B_DV), lambda h, b: (j, h, 0, 0)),
    ]
    args = [z, z, z, z, dec_rep, gn4]
    if carry_in:
        in_specs += [pl.BlockSpec((None, None, 2, None, B_DK, B_DV), lambda h, b: (b, j, 0, h, 0, 0))]
        args += [state]
    out_specs = [pl.BlockSpec((t, B_DV), lambda h, b: (b, h))]
    out_shape = [jax.ShapeDtypeStruct((n_tok, B_HEADS * B_DV), BF16)]
    if carry_out:
        out_specs += [pl.BlockSpec((None, 2, None, B_DK, B_DV), lambda h, b: (b, 0, h, 0, 0))]
        out_shape += [jax.ShapeDtypeStruct((batch, 2, B_HEADS, B_DK, B_DV), F32)]
    kern = functools.partial(_ret_kernel, n_chunks=n_chunks, carry_in=carry_in, carry_out=carry_out)
    return pl.pallas_call(
        kern,
        grid=(B_HEADS, batch),
        in_specs=in_specs,
        out_specs=out_specs,
        out_shape=out_shape,
        scratch_shapes=[
            pltpu.VMEM((2, B_DK, B_DV), F32),
            pltpu.VMEM((2, n_chunks, B_DK, B_DV), BF16),
            pltpu.VMEM((CHUNK, CHUNK), F32),
            pltpu.VMEM((2, CHUNK, B_DV), F32),
            pltpu.VMEM((2, CHUNK, B_DK), F32),
        ],
        compiler_params=_params("parallel", "arbitrary"),
        name="ret_scan",
    )(*args)


def _outproj_kernel(h_ref, w_ref, x_ref, g_ref, gate_ref, o_ref):
    y = _dot(h_ref[...], w_ref[...])
    o_ref[...] = x_ref[...] + gate_ref[...] * (_rms(y) * g_ref[...])


def _outproj(h, w, x, ng4, mod5, layer, *, seq_len, sample):
    n_tok, hv = h.shape
    tm = TM_OUT
    row = _mod_row(sample, seq_len, tm)
    return pl.pallas_call(
        _outproj_kernel,
        grid=(n_tok // tm,),
        in_specs=[
            pl.BlockSpec((tm, hv), lambda i: (i, 0)),
            pl.BlockSpec((hv, D_MODEL), lambda i: (0, 0)),
            pl.BlockSpec((tm, D_MODEL), lambda i: (i, 0)),
            pl.BlockSpec((None, None, 1, D_MODEL), lambda i: (layer, 1, 0, 0)),
            pl.BlockSpec((None, None, None, 1, D_MODEL), lambda i: (layer, row(i), 2, 0, 0)),
        ],
        out_specs=pl.BlockSpec((tm, D_MODEL), lambda i: (i, 0)),
        out_shape=jax.ShapeDtypeStruct((n_tok, D_MODEL), F32),
        compiler_params=_params("parallel"),
        name="outproj",
    )(h, w, x, ng4, mod5)


CONV_PAD = 8


def _conv3_act(hs_ref, half, h, cw, cb, seg, n_seg):
    for s in range(n_seg):
        base = CONV_PAD + s * (seg + CONV_PAD)
        h_seg = h[s * seg:(s + 1) * seg, :]
        hs_ref[2 * half, base + 1:base + 1 + seg, :] = h_seg
        hs_ref[2 * half + 1, base - 1:base - 1 + seg, :] = h_seg
    parts = []
    for s in range(n_seg):
        base = CONV_PAD + s * (seg + CONV_PAD)
        h_prev = hs_ref[2 * half, base:base + seg, :]
        h_next = hs_ref[2 * half + 1, base:base + seg, :]
        h_mid = h[s * seg:(s + 1) * seg, :]
        parts.append(h_prev * cw[0:1, :] + h_mid * cw[1:2, :] + h_next * cw[2:3, :] + cb)
    return parts


def _ffn_kernel(x_ref, g2_ref, sh_ref, sc_ref, wup_ref, cw_ref, cb_ref, wd_ref, g3_ref, gate_ref,
                o_ref, u_sc, act_sc, hs_sc, *, seg, n_seg, n_f, tf):
    u = _rms(x_ref[...]) * g2_ref[...] * (1.0 + sc_ref[...]) + sh_ref[...]
    u_sc[...] = u.astype(BF16)
    zero_rows = jnp.zeros((CONV_PAD, tf), F32)
    for s in range(n_seg):
        base = CONV_PAD + s * (seg + CONV_PAD)
        for half in range(2):
            hs_sc[2 * half, base:base + CONV_PAD, :] = zero_rows
            hs_sc[2 * half + 1, base + seg - CONV_PAD:base + seg, :] = zero_rows

    for cidx in range(n_f):
        ub = u_sc[...]
        hg = _conv3_act(hs_sc, 0, _dot(ub, wup_ref[cidx]), cw_ref[cidx], cb_ref[cidx], seg, n_seg)
        hu = _conv3_act(hs_sc, 1, _dot(ub, wup_ref[n_f + cidx]), cw_ref[n_f + cidx], cb_ref[n_f + cidx],
                        seg, n_seg)
        for s in range(n_seg):
            act = jax.nn.gelu(hg[s], approximate=True) * hu[s]
            act_sc[s * seg:(s + 1) * seg, cidx * tf:(cidx + 1) * tf] = act.astype(BF16)

    f = _dot(act_sc[...], wd_ref[...])
    o_ref[...] = x_ref[...] + gate_ref[...] * (_rms(f) * g3_ref[...])


def _ffn(x, ng4, mod5, layer, w_up3, conv_w3, conv_b3, w_down, *, seq_len, sample):
    n_tok = x.shape[0]
    tm, tf = TM_FFN, TF_FFN
    n_f = D_FF // tf
    row = _mod_row(sample, seq_len, tm)
    seg = GRID_W if sample else seq_len
    n_seg = tm // seg
    kern = functools.partial(_ffn_kernel, seg=seg, n_seg=n_seg, n_f=n_f, tf=tf)
    mod_spec = lambda k: pl.BlockSpec((None, None, None, 1, D_MODEL), lambda i: (layer, row(i), k, 0, 0))
    gain_spec = lambda k: pl.BlockSpec((None, None, 1, D_MODEL), lambda i: (layer, k, 0, 0))
    resident = lambda shape: pl.BlockSpec(shape, lambda i: (0,) * len(shape), pipeline_mode=pl.Buffered(1))
    return pl.pallas_call(
        kern,
        grid=(n_tok // tm,),
        in_specs=[
            pl.BlockSpec((tm, D_MODEL), lambda i: (i, 0)),
            gain_spec(2),
            mod_spec(3),
            mod_spec(4),
            resident((2 * n_f, D_MODEL, tf)),
            resident((2 * n_f, 3, tf)),
            resident((2 * n_f, 1, tf)),
            resident((D_FF, D_MODEL)),
            gain_spec(3),
            mod_spec(5),
        ],
        out_specs=pl.BlockSpec((tm, D_MODEL), lambda i: (i, 0)),
        out_shape=jax.ShapeDtypeStruct((n_tok, D_MODEL), F32),
        scratch_shapes=[
            pltpu.VMEM((tm, D_MODEL), BF16),
            pltpu.VMEM((tm, D_FF), BF16),
            pltpu.VMEM((4, CONV_PAD + n_seg * (seg + CONV_PAD), tf), F32),
        ],
        compiler_params=_params("parallel"),
        name="convffn",
    )(x, ng4, mod5, mod5, w_up3, conv_w3, conv_b3, w_down, ng4, mod5)


def _rope_tables(seq_len):
    quarter = B_DK // 4
    inv = ROPE_BASE ** (-jnp.arange(quarter, dtype=F32) / quarter)
    t = jnp.arange(seq_len)
    rows = (t // GRID_W).astype(F32)[:, None] * inv
    cols = (t % GRID_W).astype(F32)[:, None] * inv
    cos = jnp.concatenate([jnp.cos(rows)] * 2 + [jnp.cos(cols)] * 2, axis=-1)
    sin = jnp.concatenate([-jnp.sin(rows), jnp.sin(rows), -jnp.sin(cols), jnp.sin(cols)], axis=-1)
    return cos, sin


def _gate_weights(w_in_j, b_gate_j):
    wg = w_in_j[:, A_MAIN:].reshape(D_MODEL, 4, A_HEADS)
    pad_w = jnp.zeros((D_MODEL, LANES - 2 * A_HEADS), F32)
    pad_b = jnp.zeros((LANES - 2 * A_HEADS,), F32)
    wgi = jnp.concatenate([wg[:, 0], wg[:, 2], pad_w], axis=1).astype(BF16)
    wgf = jnp.concatenate([wg[:, 1], wg[:, 3], pad_w], axis=1).astype(BF16)
    bgi = jnp.concatenate([b_gate_j[0], b_gate_j[2], pad_b])[None, :]
    bgf = jnp.concatenate([b_gate_j[1], b_gate_j[3], pad_b])[None, :]
    return wgi, wgf, bgi, bgf


def kernel(x_prompt, x_sample, state_mlstm_C, state_mlstm_n, state_mlstm_m, state_ret_S, c, c_ctx,
           norm_gain, ada_w, ada_b, ml_w_in, ml_b_gate, ml_norm, ml_w_out,
           ret_w_in, ret_decay, ret_norm, ret_w_out, ffn_w_up, ffn_conv, ffn_conv_b, ffn_w_down):
    bp, tp, _ = x_prompt.shape
    bs, ts, _ = x_sample.shape
    n_a = ml_w_in.shape[0]
    n_b = ret_w_in.shape[0]

    cond = jnp.concatenate([c_ctx[None, :], c, jnp.zeros((MOD_ROWS - 1 - bs, D_MODEL), F32)], axis=0)
    mod5 = _modulation(cond, ada_w, ada_b).reshape(DEPTH, MOD_ROWS, 6, 1, D_MODEL)
    ng4 = norm_gain.reshape(DEPTH, 4, 1, D_MODEL)
    rope = _rope_tables(ts)

    ml_w_main = ml_w_in[:, :, :A_MAIN].astype(BF16)
    ml_w_out_b = ml_w_out.astype(BF16)
    ret_w_in_b = ret_w_in.astype(BF16)
    ret_w_out_b = ret_w_out.astype(BF16)
    n_f2 = 2 * D_FF // TF_FFN
    ffn_w_up_b = jnp.swapaxes(ffn_w_up.astype(BF16).reshape(DEPTH, D_MODEL, n_f2, TF_FFN), 1, 2)
    ffn_conv3 = jnp.swapaxes(ffn_conv.reshape(DEPTH, 3, n_f2, TF_FFN), 1, 2)
    ffn_conv_b3 = ffn_conv_b.reshape(DEPTH, n_f2, 1, TF_FFN)
    ffn_w_down_b = ffn_w_down.astype(BF16)
    ml_gn4 = ml_norm.reshape(n_a, A_HEADS, 1, A_DV)
    ret_gn4 = ret_norm.reshape(n_b, B_HEADS, 1, B_DV)
    dec_rep = jnp.broadcast_to(jnp.swapaxes(ret_decay, 1, 2)[..., None], (n_b, B_HEADS, 2, B_DV))
    st_c = state_mlstm_C
    st_n = state_mlstm_n.reshape(bs, n_a, 2, A_HEADS, A_DK, 1)
    st_m = state_mlstm_m.reshape(bs, n_a, 2, A_HEADS, 1, 1)

    groups = [
        dict(x=x_prompt.reshape(bp * tp, D_MODEL), batch=bp, seq_len=tp, sample=False),
        dict(x=x_sample.reshape(bs * ts, D_MODEL), batch=bs, seq_len=ts, sample=True),
    ]
    new_c, new_n, new_m, new_s = [], [], [], []
    for i in range(DEPTH):
        j = i // N_MIXERS
        for grp in groups:
            x = grp["x"]
            geo = dict(seq_len=grp["seq_len"], sample=grp["sample"])
            bt = dict(batch=grp["batch"], seq_len=grp["seq_len"])
            if i % N_MIXERS == 0:
                z, gi, gf = _inproj(x, ng4, mod5, i, ml_w_main[j], n_q=1, n_k=1, k_scale=A_DK ** -0.5,
                                    gates=_gate_weights(ml_w_in[j], ml_b_gate[j]), **geo)
                if grp["sample"]:
                    (h,) = _mlstm_scan(z, gi, gf, ml_gn4, j, state=(st_c, st_n, st_m), **bt)
                else:
                    h, c_new, n_new, m_new = _mlstm_scan(z, gi, gf, ml_gn4, j, **bt)
                    new_c.append(c_new)
                    new_n.append(n_new.reshape(bp, 2, A_HEADS, A_DK))
                    new_m.append(m_new.reshape(bp, 2, A_HEADS))
                x = _outproj(h, ml_w_out_b[j], x, ng4, mod5, i, **geo)
            else:
                (z,) = _inproj(x, ng4, mod5, i, ret_w_in_b[j], n_q=2, n_k=2, k_scale=B_DK ** -0.5,
                               rope=rope if grp["sample"] else None, **geo)
                if grp["sample"]:
                    (h,) = _ret_scan(z, dec_rep, ret_gn4, j, state=state_ret_S, **bt)
                else:
                    h, s_new = _ret_scan(z, dec_rep, ret_gn4, j, **bt)
                    new_s.append(s_new)
                x = _outproj(h, ret_w_out_b[j], x, ng4, mod5, i, **geo)
            grp["x"] = _ffn(x, ng4, mod5, i, ffn_w_up_b[i], ffn_conv3[i], ffn_conv_b3[i],
                            ffn_w_down_b[i], **geo)

    y_prompt = groups[0]["x"].reshape(bp, tp, D_MODEL)
    y_sample = groups[1]["x"].reshape(bs, ts, D_MODEL)
    return (y_prompt, y_sample, jnp.stack(new_c, 1), jnp.stack(new_n, 1), jnp.stack(new_m, 1),
            jnp.stack(new_s, 1))
```

```python
import functools
import math

import jax
import jax.numpy as jnp
from jax import lax
from jax.experimental import pallas as pl
from jax.experimental.pallas import tpu as pltpu

D_MODEL = 1024
DEPTH = 4
GRID_W = 64
CHUNK = 128
N_MIXERS = 2
A_HEADS = 4
A_DV = D_MODEL // A_HEADS
A_DK = A_DV // 2
A_MAIN = 2 * A_HEADS * A_DK + 2 * A_HEADS * A_DV
B_HEADS = 8
B_DK = D_MODEL // B_HEADS
B_DV = 2 * D_MODEL // B_HEADS
ROPE_BASE = 10000.0
D_FF = ((8 * D_MODEL // 3 + 127) // 128) * 128
EPS = 1e-6
LN2 = math.log(2.0)

F32 = jnp.float32
BF16 = jnp.bfloat16

LANES = 128
MOD_ROWS = 16
VMEM_LIMIT = 48 * 1024 * 1024

TM_PROJ = 512
TN_PROJ = 512
TM_OUT = 512
TM_FFN = 512
TF_FFN = 256
STATE_UNROLL = 4
OUT_UNROLL = 8
MLSTM_OUT_UNROLL = 4


def _dot(a, b):
    return jnp.dot(a, b, preferred_element_type=F32)


def _dot_nt(a, b):
    return lax.dot_general(a, b, (((1,), (1,)), ((), ())), preferred_element_type=F32)


def _dot_tn(a, b):
    return lax.dot_general(a, b, (((0,), (0,)), ((), ())), preferred_element_type=F32)


def _rms(x):
    return x * lax.rsqrt(jnp.mean(x * x, axis=-1, keepdims=True) + EPS)


def _layer_norm(h):
    d = h - jnp.mean(h, axis=-1, keepdims=True)
    return d * lax.rsqrt(jnp.mean(d * d, axis=-1, keepdims=True) + EPS)


def _params(*sem):
    return pltpu.CompilerParams(dimension_semantics=sem, vmem_limit_bytes=VMEM_LIMIT)


def _mod_kernel(cond_ref, w_ref, b_ref, o_ref):
    cnd = cond_ref[...]
    s = cnd * jax.nn.sigmoid(cnd)
    o_ref[...] = _dot(s.astype(BF16), w_ref[...].astype(BF16)) + b_ref[...]


def _modulation(cond, ada_w, ada_b):
    tn = 1024
    n_out = ada_w.shape[-1]
    return pl.pallas_call(
        _mod_kernel,
        grid=(DEPTH, n_out // tn),
        in_specs=[
            pl.BlockSpec((MOD_ROWS, D_MODEL), lambda l, j: (0, 0)),
            pl.BlockSpec((None, D_MODEL, tn), lambda l, j: (l, 0, j)),
            pl.BlockSpec((None, 1, tn), lambda l, j: (l, 0, j)),
        ],
        out_specs=pl.BlockSpec((None, MOD_ROWS, tn), lambda l, j: (l, 0, j)),
        out_shape=jax.ShapeDtypeStruct((DEPTH, MOD_ROWS, n_out), F32),
        compiler_params=_params("parallel", "parallel"),
        name="modulation",
    )(cond, ada_w, ada_b.reshape(DEPTH, 1, n_out))


def _mod_row(sample, seq_len, tm):
    if not sample:
        return lambda i: 0
    tiles_per_seq = seq_len // tm
    return lambda i: 1 + i // tiles_per_seq


def _rope_slab(x, cos, sin):
    return x * cos + pltpu.roll(x, 64, axis=1) * sin


def _inproj_kernel(*refs, n_q, n_k, k_scale, rope, gates, tn):
    x_ref, g_ref, sh_ref, sc_ref, w_ref = refs[:5]
    pos = 5
    if gates:
        wgi_ref, wgf_ref, bgi_ref, bgf_ref = refs[pos:pos + 4]
        pos += 4
    if rope:
        cos_ref, sin_ref = refs[pos:pos + 2]
        pos += 2
    z_ref = refs[pos]
    pos += 1
    if gates:
        gi_ref, gf_ref = refs[pos:pos + 2]
        pos += 2
    u_sc = refs[pos]

    u = _rms(x_ref[...]) * g_ref[...] * (1.0 + sc_ref[...]) + sh_ref[...]
    u_sc[...] = u.astype(BF16)
    if gates:
        gi_ref[...] = _dot(u_sc[...], wgi_ref[...]) + bgi_ref[...]
        gf_ref[...] = _dot(u_sc[...], wgf_ref[...]) + bgf_ref[...]

    for j in range(z_ref.shape[1] // tn):
        cols = slice(j * tn, (j + 1) * tn)
        z = _dot(u_sc[...], w_ref[:, cols])
        scale = k_scale if n_q <= j < n_q + n_k else 1.0
        if rope and j < n_q + n_k:
            for s in range(tn // LANES):
                r = _rope_slab(z[:, s * LANES:(s + 1) * LANES], cos_ref[...], sin_ref[...])
                if scale != 1.0:
                    r = r * scale
                z_ref[:, j * tn + s * LANES:j * tn + (s + 1) * LANES] = r.astype(BF16)
        elif scale != 1.0:
            z_ref[:, cols] = (z * scale).astype(BF16)
        else:
            z_ref[:, cols] = z.astype(BF16)


def _inproj(x, ng4, mod5, layer, w, *, seq_len, sample, n_q, n_k, k_scale, rope=None, gates=None):
    n_tok = x.shape[0]
    n_col = w.shape[1]
    tm, tn = TM_PROJ, TN_PROJ
    row = _mod_row(sample, seq_len, tm)
    resident = lambda shape: pl.BlockSpec(shape, lambda i: (0,) * len(shape), pipeline_mode=pl.Buffered(1))
    in_specs = [
        pl.BlockSpec((tm, D_MODEL), lambda i: (i, 0)),
        pl.BlockSpec((None, None, 1, D_MODEL), lambda i: (layer, 0, 0, 0)),
        pl.BlockSpec((None, None, None, 1, D_MODEL), lambda i: (layer, row(i), 0, 0, 0)),
        pl.BlockSpec((None, None, None, 1, D_MODEL), lambda i: (layer, row(i), 1, 0, 0)),
        resident((D_MODEL, n_col)),
    ]
    args = [x, ng4, mod5, mod5, w]
    out_specs = [pl.BlockSpec((tm, n_col), lambda i: (i, 0))]
    out_shape = [jax.ShapeDtypeStruct((n_tok, n_col), BF16)]
    if gates is not None:
        in_specs += [resident((D_MODEL, LANES))] * 2 + [resident((1, LANES))] * 2
        args += list(gates)
        out_specs += [pl.BlockSpec((tm, LANES), lambda i: (i, 0))] * 2
        out_shape += [jax.ShapeDtypeStruct((n_tok, LANES), F32)] * 2
    if rope is not None:
        tiles_per_seq = seq_len // tm
        in_specs += [pl.BlockSpec((tm, LANES), lambda i: (i % tiles_per_seq, 0))] * 2
        args += list(rope)
    kern = functools.partial(_inproj_kernel, n_q=n_q, n_k=n_k, k_scale=k_scale,
                             rope=rope is not None, gates=gates is not None, tn=tn)
    return pl.pallas_call(
        kern,
        grid=(n_tok // tm,),
        in_specs=in_specs,
        out_specs=out_specs,
        out_shape=out_shape,
        scratch_shapes=[pltpu.VMEM((tm, D_MODEL), BF16)],
        compiler_params=_params("parallel"),
        name="inproj",
    )(*args)


def _tri_masks():
    li = lax.broadcasted_iota(jnp.int32, (CHUNK, CHUNK), 0)
    si = lax.broadcasted_iota(jnp.int32, (CHUNK, CHUNK), 1)
    return si <= li, si >= li


def _split_dot(mask_b, x):
    hi = x.astype(BF16)
    r1 = x - hi.astype(F32)
    mid = r1.astype(BF16)
    lo = (r1 - mid.astype(F32)).astype(BF16)
    return _dot(mask_b, hi) + _dot(mask_b, mid) + _dot(mask_b, lo)


def _mlstm_kernel(*refs, n_chunks, carry_in, carry_out):
    q_ref, k_ref, v_ref, o_ref, gi_ref, gf_ref, gn_ref = refs[:7]
    pos = 7
    if carry_in:
        c0_ref, n0_ref, m0_ref = refs[pos:pos + 3]
        pos += 3
    h_ref = refs[pos]
    pos += 1
    if carry_out:
        cout_ref, nout_ref, mout_ref = refs[pos:pos + 3]
        pos += 3
    a_sc, b_sc, bt_sc, g_sc, bm_sc, mpf_sc, mpb_sc, c_sc, call_sc = refs[pos:pos + 9]

    head = pl.program_id(1)
    masks = _tri_masks()
    tril_b = masks[0].astype(BF16)
    ones_b = jnp.ones((CHUNK, LANES), BF16)
    lane = lax.broadcasted_iota(jnp.int32, (CHUNK, LANES), 1)
    lane_row = lax.broadcasted_iota(jnp.int32, (1, LANES), 1)
    fwd_lane = lane < A_HEADS
    cols = (head, head + A_HEADS)
    mp_sc = (mpf_sc, mpb_sc)

    def pick_col(x, col):
        return jnp.sum(jnp.where(lane == col, x, 0.0), axis=1, keepdims=True)

    def pick_scalar(row, col):
        return jnp.sum(jnp.where(lane_row == col, row, 0.0), axis=1, keepdims=True)

    def at(c):
        return pl.ds(pl.multiple_of(c * CHUNK, CHUNK), CHUNK)

    def gate_body(c, carry):
        gf = gf_ref[at(c), :]
        lf = jnp.minimum(gf, 0.0) - jnp.log1p(jnp.exp(-jnp.abs(gf)))
        a_f = _split_dot(tril_b, lf)
        tot = a_f[CHUNK - 1:CHUNK, :]
        a_all = jnp.where(fwd_lane, a_f, tot - a_f + lf)
        b_all = gi_ref[at(c), :] - a_all
        a_sc[at(c), :] = a_all
        b_sc[at(c), :] = b_all
        bt_sc[c] = b_all.T[0:2 * A_HEADS, :]
        g_sc[pl.ds(c, 1), :] = tot
        bm_sc[pl.ds(c, 1), :] = jnp.max(b_all, axis=0, keepdims=True)
        return carry

    lax.fori_loop(0, n_chunks, gate_body, 0, unroll=min(n_chunks, STATE_UNROLL))

    if carry_in:
        m_init = (jnp.broadcast_to(m0_ref[0], (1, LANES)), jnp.broadcast_to(m0_ref[1], (1, LANES)))
    else:
        m_init = (jnp.zeros((1, LANES), F32), jnp.zeros((1, LANES), F32))

    def m_body(i, carry):
        m_f, m_b = carry
        ib = n_chunks - 1 - i
        mpf_sc[pl.ds(i, 1), :] = m_f
        mpb_sc[pl.ds(ib, 1), :] = m_b
        m_f = g_sc[pl.ds(i, 1), :] + jnp.maximum(m_f, bm_sc[pl.ds(i, 1), :])
        m_b = g_sc[pl.ds(ib, 1), :] + jnp.maximum(m_b, bm_sc[pl.ds(ib, 1), :])
        return m_f, m_b

    m_last = lax.fori_loop(0, n_chunks, m_body, m_init)

    if carry_in:
        for dirn in range(2):
            c_sc[dirn, :, :A_DV] = c0_ref[dirn]
            c_sc[dirn, :, A_DV:] = jnp.broadcast_to(n0_ref[dirn], (A_DK, LANES))
    else:
        c_sc[...] = jnp.zeros_like(c_sc)

    def state_body(i, carry):
        for dirn in range(2):
            c = i if dirn == 0 else n_chunks - 1 - i
            mp_row = mp_sc[dirn][pl.ds(c, 1), :]
            m_top = pick_scalar(jnp.maximum(mp_row, bm_sc[pl.ds(c, 1), :]), cols[dirn])
            ws = jnp.exp(pick_col(b_sc[at(c), :], cols[dirn]) - m_top)
            dec = jnp.exp(pick_scalar(mp_row, cols[dirn]) - m_top)
            c_old = c_sc[dirn]
            call_sc[dirn, c] = c_old.astype(BF16)
            kw = (k_ref[at(c), :].astype(F32) * ws).astype(BF16)
            upd = jnp.concatenate([_dot_tn(kw, v_ref[at(c), :]), _dot_tn(kw, ones_b)], axis=1)
            c_sc[dirn] = dec * c_old + upd
        return carry

    lax.fori_loop(0, n_chunks, state_body, 0, unroll=min(n_chunks, STATE_UNROLL))
    if carry_out:
        for dirn in range(2):
            cout_ref[dirn] = c_sc[dirn, :, :A_DV]
            nout_ref[dirn] = c_sc[dirn, :, A_DV:A_DV + 1]
            mout_ref[dirn] = pick_scalar(m_last[dirn], cols[dirn])

    def out_body(c, carry):
        q = q_ref[at(c), :]
        v = v_ref[at(c), :]
        s_raw = _dot_nt(q, k_ref[at(c), :])
        a_chunk = a_sc[at(c), :]
        h = None
        for dirn in range(2):
            col = cols[dirn]
            m_prev = pick_scalar(mp_sc[dirn][pl.ds(c, 1), :], col)
            b_vis = jnp.where(masks[dirn], bt_sc[c, pl.ds(col, 1), :], -jnp.inf)
            m_row = jnp.maximum(m_prev, jnp.max(b_vis, axis=1, keepdims=True))
            sw = (s_raw * jnp.exp(b_vis - m_row)).astype(BF16)
            w_inter = jnp.exp(m_prev - m_row)
            floor = jnp.exp(-(pick_col(a_chunk, col) + m_row))
            inter = _dot(q, call_sc[dirn, c])
            num = _dot(sw, v) + w_inter * inter[:, :A_DV]
            den = _dot(sw, ones_b) + w_inter * inter[:, A_DV:]
            r = 1.0 / jnp.maximum(jnp.abs(den), floor)
            hd = num * jnp.concatenate([r, r], axis=1)
            h = hd if h is None else h + hd
        o = o_ref[at(c), :].astype(F32)
        h_ref[at(c), :] = (_layer_norm(h) * gn_ref[...] * jax.nn.sigmoid(o)).astype(BF16)
        return carry

    lax.fori_loop(0, n_chunks, out_body, 0, unroll=min(n_chunks, MLSTM_OUT_UNROLL))


def _mlstm_scan(z, gi, gf, gn4, j, *, batch, seq_len, state=None):
    n_tok = z.shape[0]
    n_chunks = seq_len // CHUNK
    t = seq_len
    carry_in = state is not None
    carry_out = not carry_in
    in_specs = [
        pl.BlockSpec((t, A_DK), lambda b, h: (b, h)),
        pl.BlockSpec((t, A_DK), lambda b, h: (b, A_HEADS + h)),
        pl.BlockSpec((t, A_DV), lambda b, h: (b, A_HEADS + h)),
        pl.BlockSpec((t, A_DV), lambda b, h: (b, 2 * A_HEADS + h)),
        pl.BlockSpec((t, LANES), lambda b, h: (b, 0)),
        pl.BlockSpec((t, LANES), lambda b, h: (b, 0)),
        pl.BlockSpec((None, None, 1, A_DV), lambda b, h: (j, h, 0, 0)),
    ]
    args = [z, z, z, z, gi, gf, gn4]
    if carry_in:
        c0, n0, m0 = state
        in_specs += [
            pl.BlockSpec((None, None, 2, None, A_DK, A_DV), lambda b, h: (b, j, 0, h, 0, 0)),
            pl.BlockSpec((None, None, 2, None, A_DK, 1), lambda b, h: (b, j, 0, h, 0, 0)),
            pl.BlockSpec((None, None, 2, None, 1, 1), lambda b, h: (b, j, 0, h, 0, 0)),
        ]
        args += [c0, n0, m0]
    out_specs = [pl.BlockSpec((t, A_DV), lambda b, h: (b, h))]
    out_shape = [jax.ShapeDtypeStruct((n_tok, A_HEADS * A_DV), BF16)]
    if carry_out:
        out_specs += [
            pl.BlockSpec((None, 2, None, A_DK, A_DV), lambda b, h: (b, 0, h, 0, 0)),
            pl.BlockSpec((None, 2, None, A_DK, 1), lambda b, h: (b, 0, h, 0, 0)),
            pl.BlockSpec((None, 2, None, 1, 1), lambda b, h: (b, 0, h, 0, 0)),
        ]
        out_shape += [
            jax.ShapeDtypeStruct((batch, 2, A_HEADS, A_DK, A_DV), F32),
            jax.ShapeDtypeStruct((batch, 2, A_HEADS, A_DK, 1), F32),
            jax.ShapeDtypeStruct((batch, 2, A_HEADS, 1, 1), F32),
        ]
    kern = functools.partial(_mlstm_kernel, n_chunks=n_chunks, carry_in=carry_in, carry_out=carry_out)
    return pl.pallas_call(
        kern,
        grid=(batch, A_HEADS),
        in_specs=in_specs,
        out_specs=out_specs,
        out_shape=out_shape,
        scratch_shapes=[
            pltpu.VMEM((t, LANES), F32),
            pltpu.VMEM((t, LANES), F32),
            pltpu.VMEM((n_chunks, 2 * A_HEADS, CHUNK), F32),
            pltpu.VMEM((n_chunks, LANES), F32),
            pltpu.VMEM((n_chunks, LANES), F32),
            pltpu.VMEM((n_chunks, LANES), F32),
            pltpu.VMEM((n_chunks, LANES), F32),
            pltpu.VMEM((2, A_DK, A_DV + LANES), F32),
            pltpu.VMEM((2, n_chunks, A_DK, A_DV + LANES), BF16),
        ],
        compiler_params=_params("parallel", "parallel"),
        name="mlstm_scan",
    )(*args)


def _ret_kernel(*refs, n_chunks, carry_in, carry_out):
    q_ref, k_ref, v_ref, gate_ref, dec_ref, gn_ref = refs[:6]
    pos = 6
    if carry_in:
        s0_ref = refs[pos]
        pos += 1
    h_ref = refs[pos]
    pos += 1
    if carry_out:
        sout_ref = refs[pos]
        pos += 1
    s_sc, sall_sc, dsum_sc, xi_sc, zeta_sc = refs[pos:pos + 5]

    lg = jnp.log1p(-jnp.exp(-dec_ref[...] * LN2))
    lg_f = lg[0:1, :]
    lg_b = lg[1:2, :]

    @pl.when(pl.program_id(1) == 0)
    def _():
        masks = _tri_masks()
        li = lax.broadcasted_iota(jnp.int32, (CHUNK, B_DV), 0).astype(F32)
        si = lax.broadcasted_iota(jnp.int32, (CHUNK, CHUNK), 1).astype(F32)
        lq = li[:, :CHUNK]
        dsum_sc[...] = (
            jnp.where(masks[0], jnp.exp(jnp.where(masks[0], lq - si, 0.0) * lg_f[:, :CHUNK]), 0.0)
            + jnp.where(masks[1], jnp.exp(jnp.where(masks[1], si - lq, 0.0) * lg_b[:, :CHUNK]), 0.0))
        xi_sc[0] = jnp.exp((li + 1.0) * lg_f)
        xi_sc[1] = jnp.exp((CHUNK - li) * lg_b)
        zeta_sc[0] = jnp.exp((CHUNK - 1.0 - lq) * lg_f[:, :CHUNK])
        zeta_sc[1] = jnp.exp(lq * lg_b[:, :CHUNK])

    cdec = (jnp.exp(CHUNK * lg_f), jnp.exp(CHUNK * lg_b))

    if carry_in:
        qr = B_DK // 4
        for dirn in range(2):
            for n, o in enumerate((0, 2, 1, 3)):
                s_sc[dirn, n * qr:(n + 1) * qr, :] = s0_ref[dirn, o * qr:(o + 1) * qr, :]
    else:
        s_sc[...] = jnp.zeros_like(s_sc)

    def state_body(i, carry):
        for dirn in range(2):
            c = i if dirn == 0 else n_chunks - 1 - i
            t0 = pl.multiple_of(c * CHUNK, CHUNK)
            s_old = s_sc[dirn]
            sall_sc[dirn, c] = s_old.astype(BF16)
            kz = (k_ref[pl.ds(t0, CHUNK), :].astype(F32) * zeta_sc[dirn]).astype(BF16)
            s_sc[dirn] = cdec[dirn] * s_old + _dot_tn(kz, v_ref[pl.ds(t0, CHUNK), :])
        return carry

    lax.fori_loop(0, n_chunks, state_body, 0, unroll=min(n_chunks, STATE_UNROLL))
    if carry_out:
        sout_ref[...] = s_sc[...]

    def out_body(c, carry):
        t0 = pl.multiple_of(c * CHUNK, CHUNK)
        q = q_ref[pl.ds(t0, CHUNK), :]
        k = k_ref[pl.ds(t0, CHUNK), :]
        v = v_ref[pl.ds(t0, CHUNK), :]
        s = _dot_nt(q, k) * dsum_sc[...]
        h = (_dot(s.astype(BF16), v) + xi_sc[0] * _dot(q, sall_sc[0, c])
             + xi_sc[1] * _dot(q, sall_sc[1, c]))
        g = gate_ref[pl.ds(t0, CHUNK), :].astype(F32)
        h_ref[pl.ds(t0, CHUNK), :] = (_layer_norm(h) * gn_ref[...] * (g * jax.nn.sigmoid(g))).astype(BF16)
        return carry

    lax.fori_loop(0, n_chunks, out_body, 0, unroll=min(n_chunks, OUT_UNROLL))


def _ret_scan(z, dec_rep, gn4, j, *, batch, seq_len, state=None):
    n_tok = z.shape[0]
    n_chunks = seq_len // CHUNK
    t = seq_len
    carry_in = state is not None
    carry_out = not carry_in
    in_specs = [
        pl.BlockSpec((t, B_DK), lambda h, b: (b, h)),
        pl.BlockSpec((t, B_DK), lambda h, b: (b, B_HEADS + h)),
        pl.BlockSpec((t, B_DV), lambda h, b: (b, B_HEADS + h)),
        pl.BlockSpec((t, B_DV), lambda h, b: (b, 2 * B_HEADS + h)),
        pl.BlockSpec((None, None, 2, B_DV), lambda h, b: (j, h, 0, 0)),
        pl.BlockSpec((None, None, 1, B_DV), lambda h, b: (j, h, 0, 0)),
    ]
    args = [z, z, z, z, dec_rep, gn4]
    if carry_in:
        in_specs += [pl.BlockSpec((None, None, 2, None, B_DK, B_DV), lambda h, b: (b, j, 0, h, 0, 0))]
        args += [state]
    out_specs = [pl.BlockSpec((t, B_DV), lambda h, b: (b, h))]
    out_shape = [jax.ShapeDtypeStruct((n_tok, B_HEADS * B_DV), BF16)]
    if carry_out:
        out_specs += [pl.BlockSpec((None, 2, None, B_DK, B_DV), lambda h, b: (b, 0, h, 0, 0))]
        out_shape += [jax.ShapeDtypeStruct((batch, 2, B_HEADS, B_DK, B_DV), F32)]
    kern = functools.partial(_ret_kernel, n_chunks=n_chunks, carry_in=carry_in, carry_out=carry_out)
    return pl.pallas_call(
        kern,
        grid=(B_HEADS, batch),
        in_specs=in_specs,
        out_specs=out_specs,
        out_shape=out_shape,
        scratch_shapes=[
            pltpu.VMEM((2, B_DK, B_DV), F32),
            pltpu.VMEM((2, n_chunks, B_DK, B_DV), BF16),
            pltpu.VMEM((CHUNK, CHUNK), F32),
            pltpu.VMEM((2, CHUNK, B_DV), F32),
            pltpu.VMEM((2, CHUNK, B_DK), F32),
        ],
        compiler_params=_params("parallel", "arbitrary"),
        name="ret_scan",
    )(*args)


def _outproj_kernel(h_ref, w_ref, x_ref, g_ref, gate_ref, o_ref):
    y = _dot(h_ref[...], w_ref[...])
    o_ref[...] = x_ref[...] + gate_ref[...] * (_rms(y) * g_ref[...])


def _outproj(h, w, x, ng4, mod5, layer, *, seq_len, sample):
    n_tok, hv = h.shape
    tm = TM_OUT
    row = _mod_row(sample, seq_len, tm)
    return pl.pallas_call(
        _outproj_kernel,
        grid=(n_tok // tm,),
        in_specs=[
            pl.BlockSpec((tm, hv), lambda i: (i, 0)),
            pl.BlockSpec((hv, D_MODEL), lambda i: (0, 0)),
            pl.BlockSpec((tm, D_MODEL), lambda i: (i, 0)),
            pl.BlockSpec((None, None, 1, D_MODEL), lambda i: (layer, 1, 0, 0)),
            pl.BlockSpec((None, None, None, 1, D_MODEL), lambda i: (layer, row(i), 2, 0, 0)),
        ],
        out_specs=pl.BlockSpec((tm, D_MODEL), lambda i: (i, 0)),
        out_shape=jax.ShapeDtypeStruct((n_tok, D_MODEL), F32),
        compiler_params=_params("parallel"),
        name="outproj",
    )(h, w, x, ng4, mod5)


CONV_PAD = 8


def _conv3_act(hs_ref, half, h, cw, cb, seg, n_seg):
    for s in range(n_seg):
        base = CONV_PAD + s * (seg + CONV_PAD)
        h_seg = h[s * seg:(s + 1) * seg, :]
        hs_ref[2 * half, base + 1:base + 1 + seg, :] = h_seg
        hs_ref[2 * half + 1, base - 1:base - 1 + seg, :] = h_seg
    parts = []
    for s in range(n_seg):
        base = CONV_PAD + s * (seg + CONV_PAD)
        h_prev = hs_ref[2 * half, base:base + seg, :]
        h_next = hs_ref[2 * half + 1, base:base + seg, :]
        h_mid = h[s * seg:(s + 1) * seg, :]
        parts.append(h_prev * cw[0:1, :] + h_mid * cw[1:2, :] + h_next * cw[2:3, :] + cb)
    return parts


def _ffn_kernel(x_ref, g2_ref, sh_ref, sc_ref, wup_ref, cw_ref, cb_ref, wd_ref, g3_ref, gate_ref,
                o_ref, u_sc, act_sc, hs_sc, *, seg, n_seg, n_f, tf):
    u = _rms(x_ref[...]) * g2_ref[...] * (1.0 + sc_ref[...]) + sh_ref[...]
    u_sc[...] = u.astype(BF16)
    zero_rows = jnp.zeros((CONV_PAD, tf), F32)
    for s in range(n_seg):
        base = CONV_PAD + s * (seg + CONV_PAD)
        for half in range(2):
            hs_sc[2 * half, base:base + CONV_PAD, :] = zero_rows
            hs_sc[2 * half + 1, base + seg - CONV_PAD:base + seg, :] = zero_rows

    for cidx in range(n_f):
        ub = u_sc[...]
        hg = _conv3_act(hs_sc, 0, _dot(ub, wup_ref[cidx]), cw_ref[cidx], cb_ref[cidx], seg, n_seg)
        hu = _conv3_act(hs_sc, 1, _dot(ub, wup_ref[n_f + cidx]), cw_ref[n_f + cidx], cb_ref[n_f + cidx],
                        seg, n_seg)
        for s in range(n_seg):
            act = jax.nn.gelu(hg[s], approximate=True) * hu[s]
            act_sc[s * seg:(s + 1) * seg, cidx * tf:(cidx + 1) * tf] = act.astype(BF16)

    f = _dot(act_sc[...], wd_ref[...])
    o_ref[...] = x_ref[...] + gate_ref[...] * (_rms(f) * g3_ref[...])


def _ffn(x, ng4, mod5, layer, w_up3, conv_w3, conv_b3, w_down, *, seq_len, sample):
    n_tok = x.shape[0]
    tm, tf = TM_FFN, TF_FFN
    n_f = D_FF // tf
    row = _mod_row(sample, seq_len, tm)
    seg = GRID_W if sample else seq_len
    n_seg = tm // seg
    kern = functools.partial(_ffn_kernel, seg=seg, n_seg=n_seg, n_f=n_f, tf=tf)
    mod_spec = lambda k: pl.BlockSpec((None, None, None, 1, D_MODEL), lambda i: (layer, row(i), k, 0, 0))
    gain_spec = lambda k: pl.BlockSpec((None, None, 1, D_MODEL), lambda i: (layer, k, 0, 0))
    resident = lambda shape: pl.BlockSpec(shape, lambda i: (0,) * len(shape), pipeline_mode=pl.Buffered(1))
    return pl.pallas_call(
        kern,
        grid=(n_tok // tm,),
        in_specs=[
            pl.BlockSpec((tm, D_MODEL), lambda i: (i, 0)),
            gain_spec(2),
            mod_spec(3),
            mod_spec(4),
            resident((2 * n_f, D_MODEL, tf)),
            resident((2 * n_f, 3, tf)),
            resident((2 * n_f, 1, tf)),
            resident((D_FF, D_MODEL)),
            gain_spec(3),
            mod_spec(5),
        ],
        out_specs=pl.BlockSpec((tm, D_MODEL), lambda i: (i, 0)),
        out_shape=jax.ShapeDtypeStruct((n_tok, D_MODEL), F32),
        scratch_shapes=[
            pltpu.VMEM((tm, D_MODEL), BF16),
            pltpu.VMEM((tm, D_FF), BF16),
            pltpu.VMEM((4, CONV_PAD + n_seg * (seg + CONV_PAD), tf), F32),
        ],
        compiler_params=_params("parallel"),
        name="convffn",
    )(x, ng4, mod5, mod5, w_up3, conv_w3, conv_b3, w_down, ng4, mod5)


def _rope_order():
    quarter = B_DK // 4
    return jnp.concatenate([jnp.arange(0, quarter), jnp.arange(2 * quarter, 3 * quarter),
                            jnp.arange(quarter, 2 * quarter), jnp.arange(3 * quarter, 4 * quarter)])


def _rope_tables(seq_len):
    quarter = B_DK // 4
    inv = ROPE_BASE ** (-jnp.arange(quarter, dtype=F32) / quarter)
    t = jnp.arange(seq_len)
    rows = (t // GRID_W).astype(F32)[:, None] * inv
    cols = (t % GRID_W).astype(F32)[:, None] * inv
    cos = jnp.concatenate([jnp.cos(rows), jnp.cos(cols)] * 2, axis=-1)
    sin = jnp.concatenate([-jnp.sin(rows), -jnp.sin(cols), jnp.sin(rows), jnp.sin(cols)], axis=-1)
    return cos, sin


def _rope_weight_order(w_in_b):
    n_qk = 2 * B_HEADS * B_DK
    idx = (jnp.arange(n_qk).reshape(-1, B_DK)[:, :1] + _rope_order()[None, :]).reshape(-1)
    return jnp.concatenate([w_in_b[:, :, idx], w_in_b[:, :, n_qk:]], axis=-1)


def _gate_weights(w_in_j, b_gate_j):
    wg = w_in_j[:, A_MAIN:].reshape(D_MODEL, 4, A_HEADS)
    pad_w = jnp.zeros((D_MODEL, LANES - 2 * A_HEADS), F32)
    pad_b = jnp.zeros((LANES - 2 * A_HEADS,), F32)
    wgi = jnp.concatenate([wg[:, 0], wg[:, 2], pad_w], axis=1).astype(BF16)
    wgf = jnp.concatenate([wg[:, 1], wg[:, 3], pad_w], axis=1).astype(BF16)
    bgi = jnp.concatenate([b_gate_j[0], b_gate_j[2], pad_b])[None, :]
    bgf = jnp.concatenate([b_gate_j[1], b_gate_j[3], pad_b])[None, :]
    return wgi, wgf, bgi, bgf


def kernel(x_prompt, x_sample, state_mlstm_C, state_mlstm_n, state_mlstm_m, state_ret_S, c, c_ctx,
           norm_gain, ada_w, ada_b, ml_w_in, ml_b_gate, ml_norm, ml_w_out,
           ret_w_in, ret_decay, ret_norm, ret_w_out, ffn_w_up, ffn_conv, ffn_conv_b, ffn_w_down):
    bp, tp, _ = x_prompt.shape
    bs, ts, _ = x_sample.shape
    n_a = ml_w_in.shape[0]
    n_b = ret_w_in.shape[0]

    cond = jnp.concatenate([c_ctx[None, :], c, jnp.zeros((MOD_ROWS - 1 - bs, D_MODEL), F32)], axis=0)
    mod5 = _modulation(cond, ada_w, ada_b).reshape(DEPTH, MOD_ROWS, 6, 1, D_MODEL)
    ng4 = norm_gain.reshape(DEPTH, 4, 1, D_MODEL)
    rope = _rope_tables(ts)

    ml_w_main = ml_w_in[:, :, :A_MAIN].astype(BF16)
    ml_w_out_b = ml_w_out.astype(BF16)
    ret_w_in_b = ret_w_in.astype(BF16)
    ret_w_in_rope = _rope_weight_order(ret_w_in_b)
    ret_w_out_b = ret_w_out.astype(BF16)
    n_f2 = 2 * D_FF // TF_FFN
    ffn_w_up_b = jnp.swapaxes(ffn_w_up.astype(BF16).reshape(DEPTH, D_MODEL, n_f2, TF_FFN), 1, 2)
    ffn_conv3 = jnp.swapaxes(ffn_conv.reshape(DEPTH, 3, n_f2, TF_FFN), 1, 2)
    ffn_conv_b3 = ffn_conv_b.reshape(DEPTH, n_f2, 1, TF_FFN)
    ffn_w_down_b = ffn_w_down.astype(BF16)
    ml_gn4 = ml_norm.reshape(n_a, A_HEADS, 1, A_DV)
    ret_gn4 = ret_norm.reshape(n_b, B_HEADS, 1, B_DV)
    dec_rep = jnp.broadcast_to(jnp.swapaxes(ret_decay, 1, 2)[..., None], (n_b, B_HEADS, 2, B_DV))
    st_c = state_mlstm_C
    st_n = state_mlstm_n.reshape(bs, n_a, 2, A_HEADS, A_DK, 1)
    st_m = state_mlstm_m.reshape(bs, n_a, 2, A_HEADS, 1, 1)

    groups = [
        dict(x=x_prompt.reshape(bp * tp, D_MODEL), batch=bp, seq_len=tp, sample=False),
        dict(x=x_sample.reshape(bs * ts, D_MODEL), batch=bs, seq_len=ts, sample=True),
    ]
    new_c, new_n, new_m, new_s = [], [], [], []
    for i in range(DEPTH):
        j = i // N_MIXERS
        for grp in groups:
            x = grp["x"]
            geo = dict(seq_len=grp["seq_len"], sample=grp["sample"])
            bt = dict(batch=grp["batch"], seq_len=grp["seq_len"])
            if i % N_MIXERS == 0:
                n_qk = A_HEADS * A_DK // TN_PROJ
                z, gi, gf = _inproj(x, ng4, mod5, i, ml_w_main[j], n_q=n_qk, n_k=n_qk, k_scale=A_DK ** -0.5,
                                    gates=_gate_weights(ml_w_in[j], ml_b_gate[j]), **geo)
                if grp["sample"]:
                    (h,) = _mlstm_scan(z, gi, gf, ml_gn4, j, state=(st_c, st_n, st_m), **bt)
                else:
                    h, c_new, n_new, m_new = _mlstm_scan(z, gi, gf, ml_gn4, j, **bt)
                    new_c.append(c_new)
                    new_n.append(n_new.reshape(bp, 2, A_HEADS, A_DK))
                    new_m.append(m_new.reshape(bp, 2, A_HEADS))
                x = _outproj(h, ml_w_out_b[j], x, ng4, mod5, i, **geo)
            else:
                n_qk = B_HEADS * B_DK // TN_PROJ
                w_in = ret_w_in_rope[j] if grp["sample"] else ret_w_in_b[j]
                (z,) = _inproj(x, ng4, mod5, i, w_in, n_q=n_qk, n_k=n_qk, k_scale=B_DK ** -0.5,
                               rope=rope if grp["sample"] else None, **geo)
                if grp["sample"]:
                    (h,) = _ret_scan(z, dec_rep, ret_gn4, j, state=state_ret_S, **bt)
                else:
                    h, s_new = _ret_scan(z, dec_rep, ret_gn4, j, **bt)
                    new_s.append(s_new)
                x = _outproj(h, ret_w_out_b[j], x, ng4, mod5, i, **geo)
            grp["x"] = _ffn(x, ng4, mod5, i, ffn_w_up_b[i], ffn_conv3[i], ffn_conv_b3[i],
                            ffn_w_down_b[i], **geo)

    y_prompt = groups[0]["x"].reshape(bp, tp, D_MODEL)
    y_sample = groups[1]["x"].reshape(bs, ts, D_MODEL)
    return (y_prompt, y_sample, jnp.stack(new_c, 1), jnp.stack(new_n, 1), jnp.stack(new_m, 1),
            jnp.stack(new_s, 1))
```

```python
import functools
import math

import jax
import jax.numpy as jnp
from jax import lax
from jax.experimental import pallas as pl
from jax.experimental.pallas import tpu as pltpu

D_MODEL = 1024
DEPTH = 4
GRID_W = 64
CHUNK = 128
N_MIXERS = 2
A_HEADS = 4
A_DV = D_MODEL // A_HEADS
A_DK = A_DV // 2
A_MAIN = 2 * A_HEADS * A_DK + 2 * A_HEADS * A_DV
B_HEADS = 8
B_DK = D_MODEL // B_HEADS
B_DV = 2 * D_MODEL // B_HEADS
B_QK = 2 * B_HEADS * B_DK
ROPE_BASE = 10000.0
D_FF = ((8 * D_MODEL // 3 + 127) // 128) * 128
EPS = 1e-6
LN2 = math.log(2.0)

F32 = jnp.float32
BF16 = jnp.bfloat16

LANES = 128
SUBLANES = 8
MOD_ROWS = 16
VMEM_LIMIT = 48 * 1024 * 1024

TM_PROJ = 512
TN_PROJ = 512
TM_OUT = 512
TM_FFN = 512
TF_FFN = 256
STATE_UNROLL = 4
OUT_UNROLL = 8
MLSTM_OUT_UNROLL = 4
SCAN_TOKENS = 1024


def _dot(a, b):
    return jnp.dot(a, b, preferred_element_type=F32)


def _dot_nt(a, b):
    return lax.dot_general(a, b, (((1,), (1,)), ((), ())), preferred_element_type=F32)


def _dot_tn(a, b):
    return lax.dot_general(a, b, (((0,), (0,)), ((), ())), preferred_element_type=F32)


def _rms(x):
    return x * lax.rsqrt(jnp.mean(x * x, axis=-1, keepdims=True) + EPS)


def _layer_norm(h):
    d = h - jnp.mean(h, axis=-1, keepdims=True)
    return d * lax.rsqrt(jnp.mean(d * d, axis=-1, keepdims=True) + EPS)


def _params(*sem):
    return pltpu.CompilerParams(dimension_semantics=sem, vmem_limit_bytes=VMEM_LIMIT)


def _resident(shape, index):
    return pl.BlockSpec(shape, lambda *_: index, pipeline_mode=pl.Buffered(1))


def _mod_kernel(cond_ref, w_ref, b_ref, o_ref):
    cnd = cond_ref[...]
    s = cnd * jax.nn.sigmoid(cnd)
    o_ref[...] = _dot(s.astype(BF16), w_ref[...].astype(BF16)) + b_ref[...]


def _modulation(cond, ada_w, ada_b):
    tn = 1024
    n_out = ada_w.shape[-1]
    return pl.pallas_call(
        _mod_kernel,
        grid=(DEPTH, n_out // tn),
        in_specs=[
            pl.BlockSpec((MOD_ROWS, D_MODEL), lambda l, j: (0, 0)),
            pl.BlockSpec((None, D_MODEL, tn), lambda l, j: (l, 0, j)),
            pl.BlockSpec((None, 1, tn), lambda l, j: (l, 0, j)),
        ],
        out_specs=pl.BlockSpec((None, MOD_ROWS, tn), lambda l, j: (l, 0, j)),
        out_shape=jax.ShapeDtypeStruct((DEPTH, MOD_ROWS, n_out), F32),
        compiler_params=_params("parallel", "parallel"),
        name="modulation",
    )(cond, ada_w, ada_b.reshape(DEPTH, 1, n_out))


def _mod_row(sample, seq_len, tm):
    if not sample:
        return lambda i: 0
    tiles_per_seq = seq_len // tm
    return lambda i: 1 + i // tiles_per_seq


def _rope_slab(x, cos, sin):
    return x * cos + pltpu.roll(x, 64, axis=1) * sin


def _inproj_kernel(*refs, n_w, n_q, n_k, k_scale, rope, gates, tn):
    x_ref, g_ref, sh_ref, sc_ref = refs[:4]
    w_refs = refs[4:4 + n_w]
    pos = 4 + n_w
    if gates:
        wgi_ref, wgf_ref, bgi_ref, bgf_ref = refs[pos:pos + 4]
        pos += 4
    if rope:
        cos_ref, sin_ref = refs[pos:pos + 2]
        pos += 2
    z_ref = refs[pos]
    pos += 1
    if gates:
        gi_ref, gf_ref = refs[pos:pos + 2]
        pos += 2
    u_sc = refs[pos]

    u = _rms(x_ref[...]) * g_ref[...] * (1.0 + sc_ref[...]) + sh_ref[...]
    u_sc[...] = u.astype(BF16)
    if gates:
        gi_ref[...] = _dot(u_sc[...], wgi_ref[...]) + bgi_ref[...]
        gf_ref[...] = _dot(u_sc[...], wgf_ref[...]) + bgf_ref[...]

    wb = w_refs[0].shape[1]
    for j in range(z_ref.shape[1] // tn):
        part, off = divmod(j * tn, wb)
        z = _dot(u_sc[...], w_refs[part][:, off:off + tn])
        scale = k_scale if n_q <= j < n_q + n_k else 1.0
        if rope and j < n_q + n_k:
            for s in range(tn // LANES):
                r = _rope_slab(z[:, s * LANES:(s + 1) * LANES], cos_ref[...], sin_ref[...])
                if scale != 1.0:
                    r = r * scale
                z_ref[:, j * tn + s * LANES:j * tn + (s + 1) * LANES] = r.astype(BF16)
        elif scale != 1.0:
            z_ref[:, j * tn:(j + 1) * tn] = (z * scale).astype(BF16)
        else:
            z_ref[:, j * tn:(j + 1) * tn] = z.astype(BF16)


def _inproj(x, ng4, mod5, layer, w_parts, *, seq_len, sample, n_q, n_k, k_scale, rope=None, gates=None):
    n_tok = x.shape[0]
    tm, tn = TM_PROJ, TN_PROJ
    wb = w_parts[0][0].shape[2] if len(w_parts) == 1 else B_QK
    n_col = wb * len(w_parts)
    row = _mod_row(sample, seq_len, tm)
    in_specs = [
        pl.BlockSpec((tm, D_MODEL), lambda i: (i, 0)),
        pl.BlockSpec((None, None, 1, D_MODEL), lambda i: (layer, 0, 0, 0)),
        pl.BlockSpec((None, None, None, 1, D_MODEL), lambda i: (layer, row(i), 0, 0, 0)),
        pl.BlockSpec((None, None, None, 1, D_MODEL), lambda i: (layer, row(i), 1, 0, 0)),
    ]
    in_specs += [_resident((None, D_MODEL, wb), (jl, 0, blk)) for _, jl, blk in w_parts]
    args = [x, ng4, mod5, mod5] + [w for w, _, _ in w_parts]
    out_specs = [pl.BlockSpec((tm, n_col), lambda i: (i, 0))]
    out_shape = [jax.ShapeDtypeStruct((n_tok, n_col), BF16)]
    if gates is not None:
        in_specs += [_resident((D_MODEL, LANES), (0, 0))] * 2 + [_resident((1, LANES), (0, 0))] * 2
        args += list(gates)
        out_specs += [pl.BlockSpec((tm, LANES), lambda i: (i, 0))] * 2
        out_shape += [jax.ShapeDtypeStruct((n_tok, LANES), F32)] * 2
    if rope is not None:
        tiles_per_seq = seq_len // tm
        in_specs += [pl.BlockSpec((tm, LANES), lambda i: (i % tiles_per_seq, 0))] * 2
        args += list(rope)
    kern = functools.partial(_inproj_kernel, n_w=len(w_parts), n_q=n_q, n_k=n_k, k_scale=k_scale,
                             rope=rope is not None, gates=gates is not None, tn=tn)
    return pl.pallas_call(
        kern,
        grid=(n_tok // tm,),
        in_specs=in_specs,
        out_specs=out_specs,
        out_shape=out_shape,
        scratch_shapes=[pltpu.VMEM((tm, D_MODEL), BF16)],
        compiler_params=_params("parallel"),
        name="inproj",
    )(*args)


def _tri_masks():
    li = lax.broadcasted_iota(jnp.int32, (CHUNK, CHUNK), 0)
    si = lax.broadcasted_iota(jnp.int32, (CHUNK, CHUNK), 1)
    return si <= li, si >= li


def _seqs_per_step(batch, seq_len, carry_in):
    if carry_in:
        return 1
    nb = max(1, SCAN_TOKENS // seq_len)
    while batch % nb:
        nb -= 1
    return nb


def _split_dot(mask_b, x):
    hi = x.astype(BF16)
    r1 = x - hi.astype(F32)
    mid = r1.astype(BF16)
    lo = (r1 - mid.astype(F32)).astype(BF16)
    return _dot(mask_b, hi) + _dot(mask_b, mid) + _dot(mask_b, lo)


def _mlstm_kernel(*refs, n_chunks, nb, carry_in, carry_out, n_alias):
    q_ref, k_ref, v_ref, o_ref, gi_ref, gf_ref, gn_ref = refs[:7]
    pos = 7
    if carry_in:
        c0_ref, n0_ref, m0_ref = refs[pos:pos + 3]
        pos += 3
    pos += n_alias
    h_ref = refs[pos]
    pos += 1
    if carry_out:
        cout_ref, nout_ref, mout_ref = refs[pos:pos + 3]
        pos += 3
    a_sc, b_sc, bt_sc, g_sc, bm_sc, mpf_sc, mpb_sc, c_sc, call_sc = refs[pos:pos + 9]

    head = pl.program_id(1)
    masks = _tri_masks()
    tril_b = masks[0].astype(BF16)
    ones_b = jnp.ones((CHUNK, LANES), BF16)
    lane = lax.broadcasted_iota(jnp.int32, (CHUNK, LANES), 1)
    lane_row = lax.broadcasted_iota(jnp.int32, (1, LANES), 1)
    fwd_lane = lane < A_HEADS
    cols = (head, head + A_HEADS)
    mp_sc = (mpf_sc, mpb_sc)
    t_seq = n_chunks * CHUNK

    def pick_col(x, col):
        return jnp.sum(jnp.where(lane == col, x, 0.0), axis=1, keepdims=True)

    def pick_scalar(row, col):
        return jnp.sum(jnp.where(lane_row == col, row, 0.0), axis=1, keepdims=True)

    def at(s, c):
        return pl.ds(pl.multiple_of(s * t_seq + c * CHUNK, CHUNK), CHUNK)

    def crow(s, c):
        return pl.ds(s * n_chunks + c, 1)

    def gate_body(c, carry):
        for s in range(nb):
            gf = gf_ref[at(s, c), :]
            lf = jnp.minimum(gf, 0.0) - jnp.log1p(jnp.exp(-jnp.abs(gf)))
            a_f = _split_dot(tril_b, lf)
            tot = a_f[CHUNK - 1:CHUNK, :]
            a_all = jnp.where(fwd_lane, a_f, tot - a_f + lf)
            b_all = gi_ref[at(s, c), :] - a_all
            a_sc[at(s, c), :] = a_all
            b_sc[at(s, c), :] = b_all
            bt_sc[s * n_chunks + c] = b_all.T[0:2 * A_HEADS, :]
            g_sc[crow(s, c), :] = tot
            bm_sc[crow(s, c), :] = jnp.max(b_all, axis=0, keepdims=True)
        return carry

    lax.fori_loop(0, n_chunks, gate_body, 0, unroll=min(n_chunks, STATE_UNROLL))

    if carry_in:
        m_init = (jnp.broadcast_to(m0_ref[0], (1, LANES)), jnp.broadcast_to(m0_ref[1], (1, LANES)))
    else:
        m_init = (jnp.zeros((1, LANES), F32),) * (2 * nb)

    def m_body(i, carry):
        ib = n_chunks - 1 - i
        out = []
        for s in range(nb):
            m_f, m_b = carry[2 * s], carry[2 * s + 1]
            mpf_sc[crow(s, i), :] = m_f
            mpb_sc[crow(s, ib), :] = m_b
            out.append(g_sc[crow(s, i), :] + jnp.maximum(m_f, bm_sc[crow(s, i), :]))
            out.append(g_sc[crow(s, ib), :] + jnp.maximum(m_b, bm_sc[crow(s, ib), :]))
        return tuple(out)

    m_last = lax.fori_loop(0, n_chunks, m_body, m_init)

    if carry_in:
        for dirn in range(2):
            c_sc[dirn, :, :A_DV] = c0_ref[dirn]
            c_sc[dirn, :, A_DV:] = jnp.broadcast_to(n0_ref[dirn], (A_DK, LANES))
    else:
        c_sc[...] = jnp.zeros_like(c_sc)

    def state_body(i, carry):
        for s in range(nb):
            for dirn in range(2):
                c = i if dirn == 0 else n_chunks - 1 - i
                mp_row = mp_sc[dirn][crow(s, c), :]
                m_top = pick_scalar(jnp.maximum(mp_row, bm_sc[crow(s, c), :]), cols[dirn])
                ws = jnp.exp(pick_col(b_sc[at(s, c), :], cols[dirn]) - m_top)
                dec = jnp.exp(pick_scalar(mp_row, cols[dirn]) - m_top)
                c_old = c_sc[2 * s + dirn]
                call_sc[2 * s + dirn, c] = c_old.astype(BF16)
                kw = (k_ref[at(s, c), :].astype(F32) * ws).astype(BF16)
                upd = jnp.concatenate([_dot_tn(kw, v_ref[at(s, c), :]), _dot_tn(kw, ones_b)], axis=1)
                c_sc[2 * s + dirn] = dec * c_old + upd
        return carry

    lax.fori_loop(0, n_chunks, state_body, 0, unroll=min(n_chunks, STATE_UNROLL))
    if carry_out:
        for s in range(nb):
            for dirn in range(2):
                cout_ref[s, dirn] = c_sc[2 * s + dirn, :, :A_DV]
                nout_ref[s, dirn] = c_sc[2 * s + dirn, :, A_DV:A_DV + 1]
                mout_ref[s, dirn] = pick_scalar(m_last[2 * s + dirn], cols[dirn])

    def out_body(c, carry):
        for s in range(nb):
            q = q_ref[at(s, c), :]
            v = v_ref[at(s, c), :]
            s_raw = _dot_nt(q, k_ref[at(s, c), :])
            a_chunk = a_sc[at(s, c), :]
            h = None
            for dirn in range(2):
                col = cols[dirn]
                m_prev = pick_scalar(mp_sc[dirn][crow(s, c), :], col)
                b_vis = jnp.where(masks[dirn], bt_sc[s * n_chunks + c, pl.ds(col, 1), :], -jnp.inf)
                m_row = jnp.maximum(m_prev, jnp.max(b_vis, axis=1, keepdims=True))
                sw = (s_raw * jnp.exp(b_vis - m_row)).astype(BF16)
                w_inter = jnp.exp(m_prev - m_row)
                floor = jnp.exp(-(pick_col(a_chunk, col) + m_row))
                inter = _dot(q, call_sc[2 * s + dirn, c])
                num = _dot(sw, v) + w_inter * inter[:, :A_DV]
                den = _dot(sw, ones_b) + w_inter * inter[:, A_DV:]
                r = 1.0 / jnp.maximum(jnp.abs(den), floor)
                hd = num * jnp.concatenate([r, r], axis=1)
                h = hd if h is None else h + hd
            o = o_ref[at(s, c), :].astype(F32)
            h_ref[at(s, c), :] = (_layer_norm(h) * gn_ref[...] * jax.nn.sigmoid(o)).astype(BF16)
        return carry

    lax.fori_loop(0, n_chunks, out_body, 0, unroll=min(n_chunks, max(1, MLSTM_OUT_UNROLL // nb)))


def _mlstm_scan(z, gi, gf, gn4, j, *, batch, seq_len, n_layers, state=None, prev=None):
    n_tok = z.shape[0]
    n_chunks = seq_len // CHUNK
    carry_in = state is not None
    carry_out = not carry_in
    nb = _seqs_per_step(batch, seq_len, carry_in)
    t = nb * seq_len
    in_specs = [
        pl.BlockSpec((t, A_DK), lambda b, h: (b, h)),
        pl.BlockSpec((t, A_DK), lambda b, h: (b, A_HEADS + h)),
        pl.BlockSpec((t, A_DV), lambda b, h: (b, A_HEADS + h)),
        pl.BlockSpec((t, A_DV), lambda b, h: (b, 2 * A_HEADS + h)),
        pl.BlockSpec((t, LANES), lambda b, h: (b, 0)),
        pl.BlockSpec((t, LANES), lambda b, h: (b, 0)),
        pl.BlockSpec((None, None, 1, A_DV), lambda b, h: (j, h, 0, 0)),
    ]
    args = [z, z, z, z, gi, gf, gn4]
    aliases = {}
    if carry_in:
        in_specs += [
            pl.BlockSpec((None, None, 2, None, A_DK, A_DV), lambda b, h: (b, j, 0, h, 0, 0)),
            pl.BlockSpec((None, None, 2, None, A_DK, 1), lambda b, h: (b, j, 0, h, 0, 0)),
            pl.BlockSpec((None, None, 2, None, 1, 1), lambda b, h: (b, j, 0, h, 0, 0)),
        ]
        args += list(state)
    out_specs = [pl.BlockSpec((t, A_DV), lambda b, h: (b, h))]
    out_shape = [jax.ShapeDtypeStruct((n_tok, A_HEADS * A_DV), BF16)]
    if carry_out:
        out_specs += [
            pl.BlockSpec((nb, None, 2, None, A_DK, A_DV), lambda b, h: (b, j, 0, h, 0, 0)),
            pl.BlockSpec((nb, None, 2, None, A_DK, 1), lambda b, h: (b, j, 0, h, 0, 0)),
            pl.BlockSpec((nb, None, 2, None, 1, 1), lambda b, h: (b, j, 0, h, 0, 0)),
        ]
        out_shape += [
            jax.ShapeDtypeStruct((batch, n_layers, 2, A_HEADS, A_DK, A_DV), F32),
            jax.ShapeDtypeStruct((batch, n_layers, 2, A_HEADS, A_DK, 1), F32),
            jax.ShapeDtypeStruct((batch, n_layers, 2, A_HEADS, 1, 1), F32),
        ]
        if prev is not None:
            aliases = {len(args) + k: 1 + k for k in range(3)}
            in_specs += [pl.BlockSpec(memory_space=pl.ANY)] * 3
            args += list(prev)
    kern = functools.partial(_mlstm_kernel, n_chunks=n_chunks, nb=nb, carry_in=carry_in,
                             carry_out=carry_out, n_alias=len(aliases))
    return pl.pallas_call(
        kern,
        grid=(batch // nb, A_HEADS),
        in_specs=in_specs,
        out_specs=out_specs,
        out_shape=out_shape,
        input_output_aliases=aliases,
        scratch_shapes=[
            pltpu.VMEM((t, LANES), F32),
            pltpu.VMEM((t, LANES), F32),
            pltpu.VMEM((nb * n_chunks, 2 * A_HEADS, CHUNK), F32),
            pltpu.VMEM((nb * n_chunks, LANES), F32),
            pltpu.VMEM((nb * n_chunks, LANES), F32),
            pltpu.VMEM((nb * n_chunks, LANES), F32),
            pltpu.VMEM((nb * n_chunks, LANES), F32),
            pltpu.VMEM((2 * nb, A_DK, A_DV + LANES), F32),
            pltpu.VMEM((2 * nb, n_chunks, A_DK, A_DV + LANES), BF16),
        ],
        compiler_params=_params("parallel", "parallel"),
        name="mlstm_scan",
    )(*args)


def _ret_kernel(*refs, n_chunks, nb, carry_in, carry_out, n_alias):
    q_ref, k_ref, v_ref, gate_ref, dec_ref, gn_ref = refs[:6]
    pos = 6
    if carry_in:
        s0_ref = refs[pos]
        pos += 1
    pos += n_alias
    h_ref = refs[pos]
    pos += 1
    if carry_out:
        sout_ref = refs[pos]
        pos += 1
    s_sc, sall_sc, dsum_sc, xi_sc, zeta_sc = refs[pos:pos + 5]
    t_seq = n_chunks * CHUNK

    def at(s, c):
        return pl.ds(pl.multiple_of(s * t_seq + c * CHUNK, CHUNK), CHUNK)

    lg = jnp.log1p(-jnp.exp(-dec_ref[...] * LN2))
    lg_f = lg[0:1, :]
    lg_b = lg[1:2, :]

    @pl.when(pl.program_id(1) == 0)
    def _():
        masks = _tri_masks()
        li = lax.broadcasted_iota(jnp.int32, (CHUNK, B_DV), 0).astype(F32)
        si = lax.broadcasted_iota(jnp.int32, (CHUNK, CHUNK), 1).astype(F32)
        lq = li[:, :CHUNK]
        dsum_sc[...] = (
            jnp.where(masks[0], jnp.exp(jnp.where(masks[0], lq - si, 0.0) * lg_f[:, :CHUNK]), 0.0)
            + jnp.where(masks[1], jnp.exp(jnp.where(masks[1], si - lq, 0.0) * lg_b[:, :CHUNK]), 0.0))
        xi_sc[0] = jnp.exp((li + 1.0) * lg_f)
        xi_sc[1] = jnp.exp((CHUNK - li) * lg_b)
        zeta_sc[0] = jnp.exp((CHUNK - 1.0 - lq) * lg_f[:, :CHUNK])
        zeta_sc[1] = jnp.exp(lq * lg_b[:, :CHUNK])

    cdec = (jnp.exp(CHUNK * lg_f), jnp.exp(CHUNK * lg_b))

    if carry_in:
        qr = B_DK // 4
        for dirn in range(2):
            for n, o in enumerate((0, 2, 1, 3)):
                s_sc[dirn, n * qr:(n + 1) * qr, :] = s0_ref[dirn, o * qr:(o + 1) * qr, :]
    else:
        s_sc[...] = jnp.zeros_like(s_sc)

    def state_body(i, carry):
        for s in range(nb):
            for dirn in range(2):
                c = i if dirn == 0 else n_chunks - 1 - i
                s_old = s_sc[2 * s + dirn]
                sall_sc[2 * s + dirn, c] = s_old.astype(BF16)
                kz = (k_ref[at(s, c), :].astype(F32) * zeta_sc[dirn]).astype(BF16)
                s_sc[2 * s + dirn] = cdec[dirn] * s_old + _dot_tn(kz, v_ref[at(s, c), :])
        return carry

    lax.fori_loop(0, n_chunks, state_body, 0, unroll=min(n_chunks, STATE_UNROLL))
    if carry_out:
        for s in range(nb):
            for dirn in range(2):
                sout_ref[s, dirn] = s_sc[2 * s + dirn]

    def out_body(c, carry):
        for s in range(nb):
            q = q_ref[at(s, c), :]
            v = v_ref[at(s, c), :]
            sw = _dot_nt(q, k_ref[at(s, c), :]) * dsum_sc[...]
            h = (_dot(sw.astype(BF16), v) + xi_sc[0] * _dot(q, sall_sc[2 * s, c])
                 + xi_sc[1] * _dot(q, sall_sc[2 * s + 1, c]))
            g = gate_ref[at(s, c), :].astype(F32)
            h_ref[at(s, c), :] = (_layer_norm(h) * gn_ref[...] * (g * jax.nn.sigmoid(g))).astype(BF16)
        return carry

    lax.fori_loop(0, n_chunks, out_body, 0, unroll=min(n_chunks, max(1, OUT_UNROLL // nb)))


def _ret_scan(z, dec_rep, gn4, j, *, batch, seq_len, n_layers, state=None, prev=None):
    n_tok = z.shape[0]
    n_chunks = seq_len // CHUNK
    carry_in = state is not None
    carry_out = not carry_in
    nb = _seqs_per_step(batch, seq_len, carry_in)
    t = nb * seq_len
    in_specs = [
        pl.BlockSpec((t, B_DK), lambda h, b: (b, h)),
        pl.BlockSpec((t, B_DK), lambda h, b: (b, B_HEADS + h)),
        pl.BlockSpec((t, B_DV), lambda h, b: (b, B_HEADS + h)),
        pl.BlockSpec((t, B_DV), lambda h, b: (b, 2 * B_HEADS + h)),
        pl.BlockSpec((None, None, 2, B_DV), lambda h, b: (j, h, 0, 0)),
        pl.BlockSpec((None, None, 1, B_DV), lambda h, b: (j, h, 0, 0)),
    ]
    args = [z, z, z, z, dec_rep, gn4]
    aliases = {}
    if carry_in:
        in_specs += [pl.BlockSpec((None, None, 2, None, B_DK, B_DV), lambda h, b: (b, j, 0, h, 0, 0))]
        args += [state]
    out_specs = [pl.BlockSpec((t, B_DV), lambda h, b: (b, h))]
    out_shape = [jax.ShapeDtypeStruct((n_tok, B_HEADS * B_DV), BF16)]
    if carry_out:
        out_specs += [pl.BlockSpec((nb, None, 2, None, B_DK, B_DV), lambda h, b: (b, j, 0, h, 0, 0))]
        out_shape += [jax.ShapeDtypeStruct((batch, n_layers, 2, B_HEADS, B_DK, B_DV), F32)]
        if prev is not None:
            aliases = {len(args): 1}
            in_specs += [pl.BlockSpec(memory_space=pl.ANY)]
            args += [prev]
    kern = functools.partial(_ret_kernel, n_chunks=n_chunks, nb=nb, carry_in=carry_in,
                             carry_out=carry_out, n_alias=len(aliases))
    return pl.pallas_call(
        kern,
        grid=(B_HEADS, batch // nb),
        in_specs=in_specs,
        out_specs=out_specs,
        out_shape=out_shape,
        input_output_aliases=aliases,
        scratch_shapes=[
            pltpu.VMEM((2 * nb, B_DK, B_DV), F32),
            pltpu.VMEM((2 * nb, n_chunks, B_DK, B_DV), BF16),
            pltpu.VMEM((CHUNK, CHUNK), F32),
            pltpu.VMEM((2, CHUNK, B_DV), F32),
            pltpu.VMEM((2, CHUNK, B_DK), F32),
        ],
        compiler_params=_params("parallel", "arbitrary"),
        name="ret_scan",
    )(*args)


def _outproj_kernel(h_ref, w_ref, x_ref, g_ref, gate_ref, o_ref):
    y = _dot(h_ref[...], w_ref[...])
    o_ref[...] = x_ref[...] + gate_ref[...] * (_rms(y) * g_ref[...])


def _outproj(h, w, j, x, ng4, mod5, layer, *, seq_len, sample):
    n_tok, hv = h.shape
    tm = TM_OUT
    row = _mod_row(sample, seq_len, tm)
    return pl.pallas_call(
        _outproj_kernel,
        grid=(n_tok // tm,),
        in_specs=[
            pl.BlockSpec((tm, hv), lambda i: (i, 0)),
            _resident((None, hv, D_MODEL), (j, 0, 0)),
            pl.BlockSpec((tm, D_MODEL), lambda i: (i, 0)),
            pl.BlockSpec((None, None, 1, D_MODEL), lambda i: (layer, 1, 0, 0)),
            pl.BlockSpec((None, None, None, 1, D_MODEL), lambda i: (layer, row(i), 2, 0, 0)),
        ],
        out_specs=pl.BlockSpec((tm, D_MODEL), lambda i: (i, 0)),
        out_shape=jax.ShapeDtypeStruct((n_tok, D_MODEL), F32),
        compiler_params=_params("parallel"),
        name="outproj",
    )(h, w, x, ng4, mod5)


def _conv3(hs_ref, half, h, cw, cb, seg, n_seg):
    for s in range(n_seg):
        base = SUBLANES + s * (seg + SUBLANES)
        h_seg = h[s * seg:(s + 1) * seg, :]
        hs_ref[2 * half, base + 1:base + 1 + seg, :] = h_seg
        hs_ref[2 * half + 1, base - 1:base - 1 + seg, :] = h_seg
    parts = []
    for s in range(n_seg):
        base = SUBLANES + s * (seg + SUBLANES)
        h_prev = hs_ref[2 * half, base:base + seg, :]
        h_next = hs_ref[2 * half + 1, base:base + seg, :]
        h_mid = h[s * seg:(s + 1) * seg, :]
        parts.append(h_prev * cw[0:1, :] + h_mid * cw[1:2, :] + h_next * cw[2:3, :] + cb)
    return parts


def _ffn_kernel(x_ref, g2_ref, sh_ref, sc_ref, wup_ref, cw_ref, cb_ref, wd_ref, g3_ref, gate_ref,
                o_ref, u_sc, act_sc, hs_sc, *, seg, n_seg, tf):
    u = _rms(x_ref[...]) * g2_ref[...] * (1.0 + sc_ref[...]) + sh_ref[...]
    u_sc[...] = u.astype(BF16)
    zero_rows = jnp.zeros((SUBLANES, tf), F32)
    for s in range(n_seg):
        base = SUBLANES + s * (seg + SUBLANES)
        for half in range(2):
            hs_sc[2 * half, base:base + SUBLANES, :] = zero_rows
            hs_sc[2 * half + 1, base + seg - SUBLANES:base + seg, :] = zero_rows

    for cidx in range(D_FF // tf):
        cg = slice(cidx * tf, (cidx + 1) * tf)
        cu = slice(D_FF + cidx * tf, D_FF + (cidx + 1) * tf)
        hg = _conv3(hs_sc, 0, _dot(u_sc[...], wup_ref[:, cg]), cw_ref[:, cg], cb_ref[:, cg], seg, n_seg)
        hu = _conv3(hs_sc, 1, _dot(u_sc[...], wup_ref[:, cu]), cw_ref[:, cu], cb_ref[:, cu], seg, n_seg)
        for s in range(n_seg):
            act = jax.nn.gelu(hg[s], approximate=True) * hu[s]
            act_sc[s * seg:(s + 1) * seg, cg] = act.astype(BF16)

    f = _dot(act_sc[...], wd_ref[...])
    o_ref[...] = x_ref[...] + gate_ref[...] * (_rms(f) * g3_ref[...])


def _ffn(x, ng4, mod5, layer, w_up, conv_w, conv_b, w_down, *, seq_len, sample):
    n_tok = x.shape[0]
    tm, tf = TM_FFN, TF_FFN
    row = _mod_row(sample, seq_len, tm)
    seg = GRID_W if sample else seq_len
    n_seg = tm // seg
    kern = functools.partial(_ffn_kernel, seg=seg, n_seg=n_seg, tf=tf)
    mod_spec = lambda k: pl.BlockSpec((None, None, None, 1, D_MODEL), lambda i: (layer, row(i), k, 0, 0))
    gain_spec = lambda k: pl.BlockSpec((None, None, 1, D_MODEL), lambda i: (layer, k, 0, 0))
    return pl.pallas_call(
        kern,
        grid=(n_tok // tm,),
        in_specs=[
            pl.BlockSpec((tm, D_MODEL), lambda i: (i, 0)),
            gain_spec(2),
            mod_spec(3),
            mod_spec(4),
            _resident((None, D_MODEL, 2 * D_FF), (layer, 0, 0)),
            _resident((None, 3, 2 * D_FF), (layer, 0, 0)),
            _resident((None, 1, 2 * D_FF), (layer, 0, 0)),
            _resident((None, D_FF, D_MODEL), (layer, 0, 0)),
            gain_spec(3),
            mod_spec(5),
        ],
        out_specs=pl.BlockSpec((tm, D_MODEL), lambda i: (i, 0)),
        out_shape=jax.ShapeDtypeStruct((n_tok, D_MODEL), F32),
        scratch_shapes=[
            pltpu.VMEM((tm, D_MODEL), BF16),
            pltpu.VMEM((tm, D_FF), BF16),
            pltpu.VMEM((4, SUBLANES + n_seg * (seg + SUBLANES), tf), F32),
        ],
        compiler_params=_params("parallel"),
        name="convffn",
    )(x, ng4, mod5, mod5, w_up, conv_w, conv_b, w_down, ng4, mod5)


def _rope_order():
    quarter = B_DK // 4
    return jnp.concatenate([jnp.arange(0, quarter), jnp.arange(2 * quarter, 3 * quarter),
                            jnp.arange(quarter, 2 * quarter), jnp.arange(3 * quarter, 4 * quarter)])


def _rope_tables(seq_len):
    quarter = B_DK // 4
    inv = ROPE_BASE ** (-jnp.arange(quarter, dtype=F32) / quarter)
    t = jnp.arange(seq_len)
    rows = (t // GRID_W).astype(F32)[:, None] * inv
    cols = (t % GRID_W).astype(F32)[:, None] * inv
    cos = jnp.concatenate([jnp.cos(rows), jnp.cos(cols)] * 2, axis=-1)
    sin = jnp.concatenate([-jnp.sin(rows), -jnp.sin(cols), jnp.sin(rows), jnp.sin(cols)], axis=-1)
    return cos, sin


def _rope_qk_weights(w_in):
    idx = (jnp.arange(B_QK).reshape(-1, B_DK)[:, :1] + _rope_order()[None, :]).reshape(-1)
    return w_in[:, :, idx].astype(BF16)


def _gate_weights(w_in_j, b_gate_j):
    wg = w_in_j[:, A_MAIN:].reshape(D_MODEL, 4, A_HEADS)
    pad_w = jnp.zeros((D_MODEL, LANES - 2 * A_HEADS), F32)
    pad_b = jnp.zeros((LANES - 2 * A_HEADS,), F32)
    wgi = jnp.concatenate([wg[:, 0], wg[:, 2], pad_w], axis=1).astype(BF16)
    wgf = jnp.concatenate([wg[:, 1], wg[:, 3], pad_w], axis=1).astype(BF16)
    bgi = jnp.concatenate([b_gate_j[0], b_gate_j[2], pad_b])[None, :]
    bgf = jnp.concatenate([b_gate_j[1], b_gate_j[3], pad_b])[None, :]
    return wgi, wgf, bgi, bgf


def kernel(x_prompt, x_sample, state_mlstm_C, state_mlstm_n, state_mlstm_m, state_ret_S, c, c_ctx,
           norm_gain, ada_w, ada_b, ml_w_in, ml_b_gate, ml_norm, ml_w_out,
           ret_w_in, ret_decay, ret_norm, ret_w_out, ffn_w_up, ffn_conv, ffn_conv_b, ffn_w_down):
    bp, tp, _ = x_prompt.shape
    bs, ts, _ = x_sample.shape
    n_a = ml_w_in.shape[0]
    n_b = ret_w_in.shape[0]

    cond = jnp.concatenate([c_ctx[None, :], c, jnp.zeros((MOD_ROWS - 1 - bs, D_MODEL), F32)], axis=0)
    mod5 = _modulation(cond, ada_w, ada_b).reshape(DEPTH, MOD_ROWS, 6, 1, D_MODEL)
    ng4 = norm_gain.reshape(DEPTH, 4, 1, D_MODEL)
    rope = _rope_tables(ts)

    ml_w_main = ml_w_in[:, :, :A_MAIN].astype(BF16)
    ml_w_out_b = ml_w_out.astype(BF16)
    ret_w_in_b = ret_w_in.astype(BF16)
    ret_w_qk_rope = _rope_qk_weights(ret_w_in)
    ret_w_out_b = ret_w_out.astype(BF16)
    ffn_w_up_b = ffn_w_up.astype(BF16)
    ffn_w_down_b = ffn_w_down.astype(BF16)
    ffn_conv_b3 = ffn_conv_b.reshape(DEPTH, 1, 2 * D_FF)
    ml_gn4 = ml_norm.reshape(n_a, A_HEADS, 1, A_DV)
    ret_gn4 = ret_norm.reshape(n_b, B_HEADS, 1, B_DV)
    dec_rep = jnp.broadcast_to(jnp.swapaxes(ret_decay, 1, 2)[..., None], (n_b, B_HEADS, 2, B_DV))
    st_c = state_mlstm_C
    st_n = state_mlstm_n.reshape(bs, n_a, 2, A_HEADS, A_DK, 1)
    st_m = state_mlstm_m.reshape(bs, n_a, 2, A_HEADS, 1, 1)

    groups = [
        dict(x=x_prompt.reshape(bp * tp, D_MODEL), batch=bp, seq_len=tp, sample=False),
        dict(x=x_sample.reshape(bs * ts, D_MODEL), batch=bs, seq_len=ts, sample=True),
    ]
    ml_states = None
    ret_states = None
    for i in range(DEPTH):
        j = i // N_MIXERS
        for grp in groups:
            x = grp["x"]
            geo = dict(seq_len=grp["seq_len"], sample=grp["sample"])
            bt = dict(batch=grp["batch"], seq_len=grp["seq_len"])
            if i % N_MIXERS == 0:
                n_qk = A_HEADS * A_DK // TN_PROJ
                z, gi, gf = _inproj(x, ng4, mod5, i, [(ml_w_main, j, 0)], n_q=n_qk, n_k=n_qk,
                                    k_scale=A_DK ** -0.5, gates=_gate_weights(ml_w_in[j], ml_b_gate[j]), **geo)
                if grp["sample"]:
                    (h,) = _mlstm_scan(z, gi, gf, ml_gn4, j, n_layers=n_a, state=(st_c, st_n, st_m), **bt)
                else:
                    h, *ml_states = _mlstm_scan(z, gi, gf, ml_gn4, j, n_layers=n_a, prev=ml_states, **bt)
                x = _outproj(h, ml_w_out_b, j, x, ng4, mod5, i, **geo)
            else:
                n_qk = B_HEADS * B_DK // TN_PROJ
                w_qk = ret_w_qk_rope if grp["sample"] else ret_w_in_b
                w_parts = [(w_qk, j, 0), (ret_w_in_b, j, 1), (ret_w_in_b, j, 2)]
                (z,) = _inproj(x, ng4, mod5, i, w_parts, n_q=n_qk, n_k=n_qk, k_scale=B_DK ** -0.5,
                               rope=rope if grp["sample"] else None, **geo)
                if grp["sample"]:
                    (h,) = _ret_scan(z, dec_rep, ret_gn4, j, n_layers=n_b, state=state_ret_S, **bt)
                else:
                    h, ret_states = _ret_scan(z, dec_rep, ret_gn4, j, n_layers=n_b, prev=ret_states, **bt)
                x = _outproj(h, ret_w_out_b, j, x, ng4, mod5, i, **geo)
            grp["x"] = _ffn(x, ng4, mod5, i, ffn_w_up_b, ffn_conv, ffn_conv_b3, ffn_w_down_b, **geo)

    y_prompt = groups[0]["x"].reshape(bp, tp, D_MODEL)
    y_sample = groups[1]["x"].reshape(bs, ts, D_MODEL)
    new_c, new_n, new_m = ml_states
    return (y_prompt, y_sample, new_c, new_n.reshape(bp, n_a, 2, A_HEADS, A_DK),
            new_m.reshape(bp, n_a, 2, A_HEADS), ret_states)
```

```python
import functools
import math

import jax
import jax.numpy as jnp
from jax import lax
from jax.experimental import pallas as pl
from jax.experimental.pallas import tpu as pltpu

D_MODEL = 1024
DEPTH = 4
GRID_W = 64
CHUNK = 128
N_MIXERS = 2
A_HEADS = 4
A_DV = D_MODEL // A_HEADS
A_DK = A_DV // 2
A_MAIN = 2 * A_HEADS * A_DK + 2 * A_HEADS * A_DV
B_HEADS = 8
B_DK = D_MODEL // B_HEADS
B_DV = 2 * D_MODEL // B_HEADS
B_QK = 2 * B_HEADS * B_DK
ROPE_BASE = 10000.0
D_FF = ((8 * D_MODEL // 3 + 127) // 128) * 128
EPS = 1e-6
LN2 = math.log(2.0)

F32 = jnp.float32
BF16 = jnp.bfloat16

LANES = 128
SUBLANES = 8
MOD_ROWS = 16
VMEM_LIMIT = 48 * 1024 * 1024

TM_PROJ = 512
TN_PROJ = 512
TM_OUT = 512
TM_FFN = 512
TF_FFN = 256
STATE_UNROLL = 4
OUT_UNROLL = 16
MLSTM_OUT_UNROLL = 8
SCAN_TOKENS = 1024


def _dot(a, b):
    return jnp.dot(a, b, preferred_element_type=F32)


def _dot_nt(a, b):
    return lax.dot_general(a, b, (((1,), (1,)), ((), ())), preferred_element_type=F32)


def _dot_tn(a, b):
    return lax.dot_general(a, b, (((0,), (0,)), ((), ())), preferred_element_type=F32)


def _rms(x):
    return x * lax.rsqrt(jnp.mean(x * x, axis=-1, keepdims=True) + EPS)


def _layer_norm(h):
    d = h - jnp.mean(h, axis=-1, keepdims=True)
    return d * lax.rsqrt(jnp.mean(d * d, axis=-1, keepdims=True) + EPS)


def _params(*sem):
    return pltpu.CompilerParams(dimension_semantics=sem, vmem_limit_bytes=VMEM_LIMIT)


def _resident(shape, index):
    return pl.BlockSpec(shape, lambda *_: index, pipeline_mode=pl.Buffered(1))


def _mod_kernel(cond_ref, w_ref, b_ref, o_ref):
    cnd = cond_ref[...]
    s = cnd * jax.nn.sigmoid(cnd)
    o_ref[...] = _dot(s.astype(BF16), w_ref[...].astype(BF16)) + b_ref[...]


def _modulation(cond, ada_w, ada_b):
    tn = 1024
    n_out = ada_w.shape[-1]
    return pl.pallas_call(
        _mod_kernel,
        grid=(DEPTH, n_out // tn),
        in_specs=[
            pl.BlockSpec((MOD_ROWS, D_MODEL), lambda l, j: (0, 0)),
            pl.BlockSpec((None, D_MODEL, tn), lambda l, j: (l, 0, j)),
            pl.BlockSpec((None, 1, tn), lambda l, j: (l, 0, j)),
        ],
        out_specs=pl.BlockSpec((None, MOD_ROWS, tn), lambda l, j: (l, 0, j)),
        out_shape=jax.ShapeDtypeStruct((DEPTH, MOD_ROWS, n_out), F32),
        compiler_params=_params("parallel", "parallel"),
        name="modulation",
    )(cond, ada_w, ada_b.reshape(DEPTH, 1, n_out))


def _mod_row(sample, seq_len, tm):
    if not sample:
        return lambda i: 0
    tiles_per_seq = seq_len // tm
    return lambda i: 1 + i // tiles_per_seq


def _rope_slab(x, cos, sin):
    return x * cos + pltpu.roll(x, 64, axis=1) * sin


def _inproj_kernel(*refs, n_w, n_q, n_k, k_scale, rope, gates, tn):
    x_ref, g_ref, sh_ref, sc_ref = refs[:4]
    w_refs = refs[4:4 + n_w]
    pos = 4 + n_w
    if gates:
        wgi_ref, wgf_ref, bgi_ref, bgf_ref = refs[pos:pos + 4]
        pos += 4
    if rope:
        cos_ref, sin_ref = refs[pos:pos + 2]
        pos += 2
    z_ref = refs[pos]
    pos += 1
    if gates:
        gi_ref, gf_ref = refs[pos:pos + 2]
        pos += 2
    u_sc = refs[pos]

    u = _rms(x_ref[...]) * g_ref[...] * (1.0 + sc_ref[...]) + sh_ref[...]
    u_sc[...] = u.astype(BF16)
    if gates:
        gi_ref[...] = _dot(u_sc[...], wgi_ref[...]) + bgi_ref[...]
        gf_ref[...] = _dot(u_sc[...], wgf_ref[...]) + bgf_ref[...]

    wb = w_refs[0].shape[1]
    for j in range(z_ref.shape[1] // tn):
        part, off = divmod(j * tn, wb)
        z = _dot(u_sc[...], w_refs[part][:, off:off + tn])
        scale = k_scale if n_q <= j < n_q + n_k else 1.0
        if rope and j < n_q + n_k:
            for s in range(tn // LANES):
                r = _rope_slab(z[:, s * LANES:(s + 1) * LANES], cos_ref[...], sin_ref[...])
                if scale != 1.0:
                    r = r * scale
                z_ref[:, j * tn + s * LANES:j * tn + (s + 1) * LANES] = r.astype(BF16)
        elif scale != 1.0:
            z_ref[:, j * tn:(j + 1) * tn] = (z * scale).astype(BF16)
        else:
            z_ref[:, j * tn:(j + 1) * tn] = z.astype(BF16)


def _inproj(x, ng4, mod5, layer, w_parts, *, seq_len, sample, n_q, n_k, k_scale, rope=None, gates=None):
    n_tok = x.shape[0]
    tm, tn = TM_PROJ, TN_PROJ
    wb = w_parts[0][0].shape[2] if len(w_parts) == 1 else B_QK
    n_col = wb * len(w_parts)
    row = _mod_row(sample, seq_len, tm)
    in_specs = [
        pl.BlockSpec((tm, D_MODEL), lambda i: (i, 0)),
        pl.BlockSpec((None, None, 1, D_MODEL), lambda i: (layer, 0, 0, 0)),
        pl.BlockSpec((None, None, None, 1, D_MODEL), lambda i: (layer, row(i), 0, 0, 0)),
        pl.BlockSpec((None, None, None, 1, D_MODEL), lambda i: (layer, row(i), 1, 0, 0)),
    ]
    in_specs += [_resident((None, D_MODEL, wb), (jl, 0, blk)) for _, jl, blk in w_parts]
    args = [x, ng4, mod5, mod5] + [w for w, _, _ in w_parts]
    out_specs = [pl.BlockSpec((tm, n_col), lambda i: (i, 0))]
    out_shape = [jax.ShapeDtypeStruct((n_tok, n_col), BF16)]
    if gates is not None:
        in_specs += [_resident((D_MODEL, LANES), (0, 0))] * 2 + [_resident((1, LANES), (0, 0))] * 2
        args += list(gates)
        out_specs += [pl.BlockSpec((tm, LANES), lambda i: (i, 0))] * 2
        out_shape += [jax.ShapeDtypeStruct((n_tok, LANES), F32)] * 2
    if rope is not None:
        tiles_per_seq = seq_len // tm
        in_specs += [pl.BlockSpec((tm, LANES), lambda i: (i % tiles_per_seq, 0))] * 2
        args += list(rope)
    kern = functools.partial(_inproj_kernel, n_w=len(w_parts), n_q=n_q, n_k=n_k, k_scale=k_scale,
                             rope=rope is not None, gates=gates is not None, tn=tn)
    return pl.pallas_call(
        kern,
        grid=(n_tok // tm,),
        in_specs=in_specs,
        out_specs=out_specs,
        out_shape=out_shape,
        scratch_shapes=[pltpu.VMEM((tm, D_MODEL), BF16)],
        compiler_params=_params("parallel"),
        name="inproj",
    )(*args)


def _tri_masks():
    li = lax.broadcasted_iota(jnp.int32, (CHUNK, CHUNK), 0)
    si = lax.broadcasted_iota(jnp.int32, (CHUNK, CHUNK), 1)
    return si <= li, si >= li


def _seqs_per_step(batch, seq_len, carry_in):
    if carry_in:
        return 1
    nb = max(1, SCAN_TOKENS // seq_len)
    while batch % nb:
        nb -= 1
    return nb


def _split_dot(mask_b, x):
    hi = x.astype(BF16)
    r1 = x - hi.astype(F32)
    mid = r1.astype(BF16)
    lo = (r1 - mid.astype(F32)).astype(BF16)
    return _dot(mask_b, hi) + _dot(mask_b, mid) + _dot(mask_b, lo)


def _mlstm_kernel(*refs, n_chunks, nb, carry_in, carry_out, n_alias):
    q_ref, k_ref, v_ref, o_ref, gi_ref, gf_ref, gn_ref = refs[:7]
    pos = 7
    if carry_in:
        c0_ref, n0_ref, m0_ref = refs[pos:pos + 3]
        pos += 3
    pos += n_alias
    h_ref = refs[pos]
    pos += 1
    if carry_out:
        cout_ref, nout_ref, mout_ref = refs[pos:pos + 3]
        pos += 3
    a_sc, b_sc, bt_sc, g_sc, bm_sc, mpf_sc, mpb_sc, c_sc, call_sc = refs[pos:pos + 9]

    head = pl.program_id(1)
    masks = _tri_masks()
    tril_b = masks[0].astype(BF16)
    ones_b = jnp.ones((CHUNK, LANES), BF16)
    lane = lax.broadcasted_iota(jnp.int32, (CHUNK, LANES), 1)
    lane_row = lax.broadcasted_iota(jnp.int32, (1, LANES), 1)
    fwd_lane = lane < A_HEADS
    cols = (head, head + A_HEADS)
    mp_sc = (mpf_sc, mpb_sc)
    t_seq = n_chunks * CHUNK

    def pick_col(x, col):
        return jnp.sum(jnp.where(lane == col, x, 0.0), axis=1, keepdims=True)

    def pick_scalar(row, col):
        return jnp.sum(jnp.where(lane_row == col, row, 0.0), axis=1, keepdims=True)

    def at(s, c):
        return pl.ds(pl.multiple_of(s * t_seq + c * CHUNK, CHUNK), CHUNK)

    def crow(s, c):
        return pl.ds(s * n_chunks + c, 1)

    def gate_body(c, carry):
        for s in range(nb):
            gf = gf_ref[at(s, c), :]
            lf = jnp.minimum(gf, 0.0) - jnp.log1p(jnp.exp(-jnp.abs(gf)))
            a_f = _split_dot(tril_b, lf)
            tot = a_f[CHUNK - 1:CHUNK, :]
            a_all = jnp.where(fwd_lane, a_f, tot - a_f + lf)
            b_all = gi_ref[at(s, c), :] - a_all
            a_sc[at(s, c), :] = a_all
            b_sc[at(s, c), :] = b_all
            bt_sc[s * n_chunks + c] = b_all.T[0:2 * A_HEADS, :]
            g_sc[crow(s, c), :] = tot
            bm_sc[crow(s, c), :] = jnp.max(b_all, axis=0, keepdims=True)
        return carry

    lax.fori_loop(0, n_chunks, gate_body, 0, unroll=min(n_chunks, STATE_UNROLL))

    if carry_in:
        m_init = (jnp.broadcast_to(m0_ref[0], (1, LANES)), jnp.broadcast_to(m0_ref[1], (1, LANES)))
    else:
        m_init = (jnp.zeros((1, LANES), F32),) * (2 * nb)

    def m_body(i, carry):
        ib = n_chunks - 1 - i
        out = []
        for s in range(nb):
            m_f, m_b = carry[2 * s], carry[2 * s + 1]
            mpf_sc[crow(s, i), :] = m_f
            mpb_sc[crow(s, ib), :] = m_b
            out.append(g_sc[crow(s, i), :] + jnp.maximum(m_f, bm_sc[crow(s, i), :]))
            out.append(g_sc[crow(s, ib), :] + jnp.maximum(m_b, bm_sc[crow(s, ib), :]))
        return tuple(out)

    m_last = lax.fori_loop(0, n_chunks, m_body, m_init)

    if carry_in:
        for dirn in range(2):
            c_sc[dirn, :, :A_DV] = c0_ref[dirn]
            c_sc[dirn, :, A_DV:] = jnp.broadcast_to(n0_ref[dirn], (A_DK, LANES))
    else:
        c_sc[...] = jnp.zeros_like(c_sc)

    def state_body(i, carry):
        for s in range(nb):
            for dirn in range(2):
                c = i if dirn == 0 else n_chunks - 1 - i
                mp_row = mp_sc[dirn][crow(s, c), :]
                m_top = pick_scalar(jnp.maximum(mp_row, bm_sc[crow(s, c), :]), cols[dirn])
                ws = jnp.exp(pick_col(b_sc[at(s, c), :], cols[dirn]) - m_top)
                dec = jnp.exp(pick_scalar(mp_row, cols[dirn]) - m_top)
                c_old = c_sc[2 * s + dirn]
                call_sc[2 * s + dirn, c] = c_old.astype(BF16)
                kw = (k_ref[at(s, c), :].astype(F32) * ws).astype(BF16)
                upd = jnp.concatenate([_dot_tn(kw, v_ref[at(s, c), :]), _dot_tn(kw, ones_b)], axis=1)
                c_sc[2 * s + dirn] = dec * c_old + upd
        return carry

    lax.fori_loop(0, n_chunks, state_body, 0, unroll=min(n_chunks, STATE_UNROLL))
    if carry_out:
        for s in range(nb):
            for dirn in range(2):
                cout_ref[s, dirn] = c_sc[2 * s + dirn, :, :A_DV]
                nout_ref[s, dirn] = c_sc[2 * s + dirn, :, A_DV:].T[0:1, :]
                mout_ref[s, dirn] = pick_scalar(m_last[2 * s + dirn], cols[dirn])

    def out_body(c, carry):
        for s in range(nb):
            q = q_ref[at(s, c), :]
            v = v_ref[at(s, c), :]
            s_raw = _dot_nt(q, k_ref[at(s, c), :])
            a_chunk = a_sc[at(s, c), :]
            h = None
            for dirn in range(2):
                col = cols[dirn]
                m_prev = pick_scalar(mp_sc[dirn][crow(s, c), :], col)
                b_vis = jnp.where(masks[dirn], bt_sc[s * n_chunks + c, pl.ds(col, 1), :], -jnp.inf)
                m_row = jnp.maximum(m_prev, jnp.max(b_vis, axis=1, keepdims=True))
                sw = (s_raw * jnp.exp(b_vis - m_row)).astype(BF16)
                w_inter = jnp.exp(m_prev - m_row)
                floor = jnp.exp(-(pick_col(a_chunk, col) + m_row))
                inter = _dot(q, call_sc[2 * s + dirn, c])
                num = _dot(sw, v) + w_inter * inter[:, :A_DV]
                den = _dot(sw, ones_b) + w_inter * inter[:, A_DV:]
                r = 1.0 / jnp.maximum(jnp.abs(den), floor)
                hd = num * jnp.concatenate([r, r], axis=1)
                h = hd if h is None else h + hd
            o = o_ref[at(s, c), :].astype(F32)
            h_ref[at(s, c), :] = (_layer_norm(h) * gn_ref[...] * jax.nn.sigmoid(o)).astype(BF16)
        return carry

    lax.fori_loop(0, n_chunks, out_body, 0, unroll=min(n_chunks, max(1, MLSTM_OUT_UNROLL // nb)))


def _mlstm_scan(z, gi, gf, gn4, j, *, batch, seq_len, n_layers, state=None, prev=None):
    n_tok = z.shape[0]
    n_chunks = seq_len // CHUNK
    carry_in = state is not None
    carry_out = not carry_in
    nb = _seqs_per_step(batch, seq_len, carry_in)
    t = nb * seq_len
    in_specs = [
        pl.BlockSpec((t, A_DK), lambda b, h: (b, h)),
        pl.BlockSpec((t, A_DK), lambda b, h: (b, A_HEADS + h)),
        pl.BlockSpec((t, A_DV), lambda b, h: (b, A_HEADS + h)),
        pl.BlockSpec((t, A_DV), lambda b, h: (b, 2 * A_HEADS + h)),
        pl.BlockSpec((t, LANES), lambda b, h: (b, 0)),
        pl.BlockSpec((t, LANES), lambda b, h: (b, 0)),
        pl.BlockSpec((None, None, 1, A_DV), lambda b, h: (j, h, 0, 0)),
    ]
    args = [z, z, z, z, gi, gf, gn4]
    aliases = {}
    if carry_in:
        in_specs += [
            pl.BlockSpec((None, None, 2, None, A_DK, A_DV), lambda b, h: (b, j, 0, h, 0, 0)),
            pl.BlockSpec((None, None, 2, None, A_DK, 1), lambda b, h: (b, j, 0, h, 0, 0)),
            pl.BlockSpec((None, None, 2, None, 1, 1), lambda b, h: (b, j, 0, h, 0, 0)),
        ]
        args += list(state)
    out_specs = [pl.BlockSpec((t, A_DV), lambda b, h: (b, h))]
    out_shape = [jax.ShapeDtypeStruct((n_tok, A_HEADS * A_DV), BF16)]
    if carry_out:
        out_specs += [
            pl.BlockSpec((nb, None, 2, None, A_DK, A_DV), lambda b, h: (b, j, 0, h, 0, 0)),
            pl.BlockSpec((nb, None, 2, None, 1, A_DK), lambda b, h: (b, j, 0, h, 0, 0)),
            pl.BlockSpec((nb, None, 2, None, 1, 1), lambda b, h: (b, j, 0, h, 0, 0)),
        ]
        out_shape += [
            jax.ShapeDtypeStruct((batch, n_layers, 2, A_HEADS, A_DK, A_DV), F32),
            jax.ShapeDtypeStruct((batch, n_layers, 2, A_HEADS, 1, A_DK), F32),
            jax.ShapeDtypeStruct((batch, n_layers, 2, A_HEADS, 1, 1), F32),
        ]
        if prev is not None:
            aliases = {len(args) + k: 1 + k for k in range(3)}
            in_specs += [pl.BlockSpec(memory_space=pl.ANY)] * 3
            args += list(prev)
    kern = functools.partial(_mlstm_kernel, n_chunks=n_chunks, nb=nb, carry_in=carry_in,
                             carry_out=carry_out, n_alias=len(aliases))
    return pl.pallas_call(
        kern,
        grid=(batch // nb, A_HEADS),
        in_specs=in_specs,
        out_specs=out_specs,
        out_shape=out_shape,
        input_output_aliases=aliases,
        scratch_shapes=[
            pltpu.VMEM((t, LANES), F32),
            pltpu.VMEM((t, LANES), F32),
            pltpu.VMEM((nb * n_chunks, 2 * A_HEADS, CHUNK), F32),
            pltpu.VMEM((nb * n_chunks, LANES), F32),
            pltpu.VMEM((nb * n_chunks, LANES), F32),
            pltpu.VMEM((nb * n_chunks, LANES), F32),
            pltpu.VMEM((nb * n_chunks, LANES), F32),
            pltpu.VMEM((2 * nb, A_DK, A_DV + LANES), F32),
            pltpu.VMEM((2 * nb, n_chunks, A_DK, A_DV + LANES), BF16),
        ],
        compiler_params=_params("parallel", "parallel"),
        name="mlstm_scan",
    )(*args)


def _ret_kernel(*refs, n_chunks, nb, carry_in, carry_out, n_alias):
    q_ref, k_ref, v_ref, gate_ref, dec_ref, gn_ref = refs[:6]
    pos = 6
    if carry_in:
        s0_ref = refs[pos]
        pos += 1
    pos += n_alias
    h_ref = refs[pos]
    pos += 1
    if carry_out:
        sout_ref = refs[pos]
        pos += 1
    s_sc, sall_sc, dsum_sc, xi_sc, zeta_sc = refs[pos:pos + 5]
    t_seq = n_chunks * CHUNK

    def at(s, c):
        return pl.ds(pl.multiple_of(s * t_seq + c * CHUNK, CHUNK), CHUNK)

    lg = jnp.log1p(-jnp.exp(-dec_ref[...] * LN2))
    lg_f = lg[0:1, :]
    lg_b = lg[1:2, :]

    @pl.when(pl.program_id(1) == 0)
    def _():
        masks = _tri_masks()
        li = lax.broadcasted_iota(jnp.int32, (CHUNK, B_DV), 0).astype(F32)
        si = lax.broadcasted_iota(jnp.int32, (CHUNK, CHUNK), 1).astype(F32)
        lq = li[:, :CHUNK]
        dsum_sc[...] = (
            jnp.where(masks[0], jnp.exp(jnp.where(masks[0], lq - si, 0.0) * lg_f[:, :CHUNK]), 0.0)
            + jnp.where(masks[1], jnp.exp(jnp.where(masks[1], si - lq, 0.0) * lg_b[:, :CHUNK]), 0.0))
        xi_sc[0] = jnp.exp((li + 1.0) * lg_f)
        xi_sc[1] = jnp.exp((CHUNK - li) * lg_b)
        zeta_sc[0] = jnp.exp((CHUNK - 1.0 - lq) * lg_f[:, :CHUNK])
        zeta_sc[1] = jnp.exp(lq * lg_b[:, :CHUNK])

    cdec = (jnp.exp(CHUNK * lg_f), jnp.exp(CHUNK * lg_b))

    if carry_in:
        qr = B_DK // 4
        for dirn in range(2):
            for n, o in enumerate((0, 2, 1, 3)):
                s_sc[dirn, n * qr:(n + 1) * qr, :] = s0_ref[dirn, o * qr:(o + 1) * qr, :]
    else:
        s_sc[...] = jnp.zeros_like(s_sc)

    def state_body(i, carry):
        for s in range(nb):
            for dirn in range(2):
                c = i if dirn == 0 else n_chunks - 1 - i
                s_old = s_sc[2 * s + dirn]
                sall_sc[2 * s + dirn, c] = s_old.astype(BF16)
                kz = (k_ref[at(s, c), :].astype(F32) * zeta_sc[dirn]).astype(BF16)
                s_sc[2 * s + dirn] = cdec[dirn] * s_old + _dot_tn(kz, v_ref[at(s, c), :])
        return carry

    lax.fori_loop(0, n_chunks, state_body, 0, unroll=min(n_chunks, STATE_UNROLL))
    if carry_out:
        for s in range(nb):
            for dirn in range(2):
                sout_ref[s, dirn] = s_sc[2 * s + dirn]

    def out_body(c, carry):
        for s in range(nb):
            q = q_ref[at(s, c), :]
            v = v_ref[at(s, c), :]
            sw = _dot_nt(q, k_ref[at(s, c), :]) * dsum_sc[...]
            h = (_dot(sw.astype(BF16), v) + xi_sc[0] * _dot(q, sall_sc[2 * s, c])
                 + xi_sc[1] * _dot(q, sall_sc[2 * s + 1, c]))
            g = gate_ref[at(s, c), :].astype(F32)
            h_ref[at(s, c), :] = (_layer_norm(h) * gn_ref[...] * (g * jax.nn.sigmoid(g))).astype(BF16)
        return carry

    lax.fori_loop(0, n_chunks, out_body, 0, unroll=min(n_chunks, max(1, OUT_UNROLL // nb)))


def _ret_scan(z, dec_rep, gn4, j, *, batch, seq_len, n_layers, state=None, prev=None):
    n_tok = z.shape[0]
    n_chunks = seq_len // CHUNK
    carry_in = state is not None
    carry_out = not carry_in
    nb = _seqs_per_step(batch, seq_len, carry_in)
    t = nb * seq_len
    in_specs = [
        pl.BlockSpec((t, B_DK), lambda h, b: (b, h)),
        pl.BlockSpec((t, B_DK), lambda h, b: (b, B_HEADS + h)),
        pl.BlockSpec((t, B_DV), lambda h, b: (b, B_HEADS + h)),
        pl.BlockSpec((t, B_DV), lambda h, b: (b, 2 * B_HEADS + h)),
        pl.BlockSpec((None, None, 2, B_DV), lambda h, b: (j, h, 0, 0)),
        pl.BlockSpec((None, None, 1, B_DV), lambda h, b: (j, h, 0, 0)),
    ]
    args = [z, z, z, z, dec_rep, gn4]
    aliases = {}
    if carry_in:
        in_specs += [pl.BlockSpec((None, None, 2, None, B_DK, B_DV), lambda h, b: (b, j, 0, h, 0, 0))]
        args += [state]
    out_specs = [pl.BlockSpec((t, B_DV), lambda h, b: (b, h))]
    out_shape = [jax.ShapeDtypeStruct((n_tok, B_HEADS * B_DV), BF16)]
    if carry_out:
        out_specs += [pl.BlockSpec((nb, None, 2, None, B_DK, B_DV), lambda h, b: (b, j, 0, h, 0, 0))]
        out_shape += [jax.ShapeDtypeStruct((batch, n_layers, 2, B_HEADS, B_DK, B_DV), F32)]
        if prev is not None:
            aliases = {len(args): 1}
            in_specs += [pl.BlockSpec(memory_space=pl.ANY)]
            args += [prev]
    kern = functools.partial(_ret_kernel, n_chunks=n_chunks, nb=nb, carry_in=carry_in,
                             carry_out=carry_out, n_alias=len(aliases))
    return pl.pallas_call(
        kern,
        grid=(B_HEADS, batch // nb),
        in_specs=in_specs,
        out_specs=out_specs,
        out_shape=out_shape,
        input_output_aliases=aliases,
        scratch_shapes=[
            pltpu.VMEM((2 * nb, B_DK, B_DV), F32),
            pltpu.VMEM((2 * nb, n_chunks, B_DK, B_DV), BF16),
            pltpu.VMEM((CHUNK, CHUNK), F32),
            pltpu.VMEM((2, CHUNK, B_DV), F32),
            pltpu.VMEM((2, CHUNK, B_DK), F32),
        ],
        compiler_params=_params("parallel", "arbitrary"),
        name="ret_scan",
    )(*args)


def _outproj_kernel(h_ref, w_ref, x_ref, g_ref, gate_ref, o_ref):
    y = _dot(h_ref[...], w_ref[...])
    o_ref[...] = x_ref[...] + gate_ref[...] * (_rms(y) * g_ref[...])


def _outproj(h, w, j, x, ng4, mod5, layer, *, seq_len, sample):
    n_tok, hv = h.shape
    tm = TM_OUT
    row = _mod_row(sample, seq_len, tm)
    return pl.pallas_call(
        _outproj_kernel,
        grid=(n_tok // tm,),
        in_specs=[
            pl.BlockSpec((tm, hv), lambda i: (i, 0)),
            _resident((None, hv, D_MODEL), (j, 0, 0)),
            pl.BlockSpec((tm, D_MODEL), lambda i: (i, 0)),
            pl.BlockSpec((None, None, 1, D_MODEL), lambda i: (layer, 1, 0, 0)),
            pl.BlockSpec((None, None, None, 1, D_MODEL), lambda i: (layer, row(i), 2, 0, 0)),
        ],
        out_specs=pl.BlockSpec((tm, D_MODEL), lambda i: (i, 0)),
        out_shape=jax.ShapeDtypeStruct((n_tok, D_MODEL), F32),
        compiler_params=_params("parallel"),
        name="outproj",
    )(h, w, x, ng4, mod5)


def _conv3(hs_ref, half, h, cw, cb, seg, n_seg):
    for s in range(n_seg):
        base = SUBLANES + s * (seg + SUBLANES)
        h_seg = h[s * seg:(s + 1) * seg, :]
        hs_ref[2 * half, base + 1:base + 1 + seg, :] = h_seg
        hs_ref[2 * half + 1, base - 1:base - 1 + seg, :] = h_seg
    parts = []
    for s in range(n_seg):
        base = SUBLANES + s * (seg + SUBLANES)
        h_prev = hs_ref[2 * half, base:base + seg, :]
        h_next = hs_ref[2 * half + 1, base:base + seg, :]
        h_mid = h[s * seg:(s + 1) * seg, :]
        parts.append(h_prev * cw[0:1, :] + h_mid * cw[1:2, :] + h_next * cw[2:3, :] + cb)
    return parts


def _ffn_kernel(x_ref, g2_ref, sh_ref, sc_ref, wup_ref, cw_ref, cb_ref, wd_ref, g3_ref, gate_ref,
                o_ref, u_sc, act_sc, hs_sc, *, seg, n_seg, tf):
    u = _rms(x_ref[...]) * g2_ref[...] * (1.0 + sc_ref[...]) + sh_ref[...]
    u_sc[...] = u.astype(BF16)
    zero_rows = jnp.zeros((SUBLANES, tf), F32)
    for s in range(n_seg):
        base = SUBLANES + s * (seg + SUBLANES)
        for half in range(2):
            hs_sc[2 * half, base:base + SUBLANES, :] = zero_rows
            hs_sc[2 * half + 1, base + seg - SUBLANES:base + seg, :] = zero_rows

    for cidx in range(D_FF // tf):
        cg = slice(cidx * tf, (cidx + 1) * tf)
        cu = slice(D_FF + cidx * tf, D_FF + (cidx + 1) * tf)
        hg = _conv3(hs_sc, 0, _dot(u_sc[...], wup_ref[:, cg]), cw_ref[:, cg], cb_ref[:, cg], seg, n_seg)
        hu = _conv3(hs_sc, 1, _dot(u_sc[...], wup_ref[:, cu]), cw_ref[:, cu], cb_ref[:, cu], seg, n_seg)
        for s in range(n_seg):
            act = jax.nn.gelu(hg[s], approximate=True) * hu[s]
            act_sc[s * seg:(s + 1) * seg, cg] = act.astype(BF16)

    f = _dot(act_sc[...], wd_ref[...])
    o_ref[...] = x_ref[...] + gate_ref[...] * (_rms(f) * g3_ref[...])


def _ffn(x, ng4, mod5, layer, w_up, conv_w, conv_b, w_down, *, seq_len, sample):
    n_tok = x.shape[0]
    tm, tf = TM_FFN, TF_FFN
    row = _mod_row(sample, seq_len, tm)
    seg = GRID_W if sample else seq_len
    n_seg = tm // seg
    kern = functools.partial(_ffn_kernel, seg=seg, n_seg=n_seg, tf=tf)
    mod_spec = lambda k: pl.BlockSpec((None, None, None, 1, D_MODEL), lambda i: (layer, row(i), k, 0, 0))
    gain_spec = lambda k: pl.BlockSpec((None, None, 1, D_MODEL), lambda i: (layer, k, 0, 0))
    return pl.pallas_call(
        kern,
        grid=(n_tok // tm,),
        in_specs=[
            pl.BlockSpec((tm, D_MODEL), lambda i: (i, 0)),
            gain_spec(2),
            mod_spec(3),
            mod_spec(4),
            _resident((None, D_MODEL, 2 * D_FF), (layer, 0, 0)),
            _resident((None, 3, 2 * D_FF), (layer, 0, 0)),
            _resident((None, 1, 2 * D_FF), (layer, 0, 0)),
            _resident((None, D_FF, D_MODEL), (layer, 0, 0)),
            gain_spec(3),
            mod_spec(5),
        ],
        out_specs=pl.BlockSpec((tm, D_MODEL), lambda i: (i, 0)),
        out_shape=jax.ShapeDtypeStruct((n_tok, D_MODEL), F32),
        scratch_shapes=[
            pltpu.VMEM((tm, D_MODEL), BF16),
            pltpu.VMEM((tm, D_FF), BF16),
            pltpu.VMEM((4, SUBLANES + n_seg * (seg + SUBLANES), tf), F32),
        ],
        compiler_params=_params("parallel"),
        name="convffn",
    )(x, ng4, mod5, mod5, w_up, conv_w, conv_b, w_down, ng4, mod5)


def _rope_tables(seq_len):
    quarter = B_DK // 4
    inv = ROPE_BASE ** (-jnp.arange(quarter, dtype=F32) / quarter)
    t = jnp.arange(seq_len)
    rows = (t // GRID_W).astype(F32)[:, None] * inv
    cols = (t % GRID_W).astype(F32)[:, None] * inv
    cos = jnp.concatenate([jnp.cos(rows), jnp.cos(cols)] * 2, axis=-1)
    sin = jnp.concatenate([-jnp.sin(rows), -jnp.sin(cols), jnp.sin(rows), jnp.sin(cols)], axis=-1)
    return cos, sin


def _rope_qk_weights(w_in):
    n_l = w_in.shape[0]
    quarter = B_DK // 4
    w_qk = w_in[:, :, :B_QK].astype(BF16).reshape(n_l, D_MODEL, 2 * B_HEADS, 2, 2, quarter)
    return jnp.swapaxes(w_qk, 3, 4).reshape(n_l, D_MODEL, B_QK)


def _gate_weights(w_in_j, b_gate_j):
    wg = w_in_j[:, A_MAIN:].reshape(D_MODEL, 4, A_HEADS)
    pad_w = jnp.zeros((D_MODEL, LANES - 2 * A_HEADS), F32)
    pad_b = jnp.zeros((LANES - 2 * A_HEADS,), F32)
    wgi = jnp.concatenate([wg[:, 0], wg[:, 2], pad_w], axis=1).astype(BF16)
    wgf = jnp.concatenate([wg[:, 1], wg[:, 3], pad_w], axis=1).astype(BF16)
    bgi = jnp.concatenate([b_gate_j[0], b_gate_j[2], pad_b])[None, :]
    bgf = jnp.concatenate([b_gate_j[1], b_gate_j[3], pad_b])[None, :]
    return wgi, wgf, bgi, bgf


def kernel(x_prompt, x_sample, state_mlstm_C, state_mlstm_n, state_mlstm_m, state_ret_S, c, c_ctx,
           norm_gain, ada_w, ada_b, ml_w_in, ml_b_gate, ml_norm, ml_w_out,
           ret_w_in, ret_decay, ret_norm, ret_w_out, ffn_w_up, ffn_conv, ffn_conv_b, ffn_w_down):
    bp, tp, _ = x_prompt.shape
    bs, ts, _ = x_sample.shape
    n_a = ml_w_in.shape[0]
    n_b = ret_w_in.shape[0]

    cond = jnp.concatenate([c_ctx[None, :], c, jnp.zeros((MOD_ROWS - 1 - bs, D_MODEL), F32)], axis=0)
    mod5 = _modulation(cond, ada_w, ada_b).reshape(DEPTH, MOD_ROWS, 6, 1, D_MODEL)
    ng4 = norm_gain.reshape(DEPTH, 4, 1, D_MODEL)
    rope = _rope_tables(ts)

    ml_w_main = ml_w_in[:, :, :A_MAIN].astype(BF16)
    ml_w_out_b = ml_w_out.astype(BF16)
    ret_w_in_b = ret_w_in.astype(BF16)
    ret_w_qk_rope = _rope_qk_weights(ret_w_in)
    ret_w_out_b = ret_w_out.astype(BF16)
    ffn_w_up_b = ffn_w_up.astype(BF16)
    ffn_w_down_b = ffn_w_down.astype(BF16)
    ffn_conv_b3 = ffn_conv_b.reshape(DEPTH, 1, 2 * D_FF)
    ml_gn4 = ml_norm.reshape(n_a, A_HEADS, 1, A_DV)
    ret_gn4 = ret_norm.reshape(n_b, B_HEADS, 1, B_DV)
    dec_rep = jnp.broadcast_to(jnp.swapaxes(ret_decay, 1, 2)[..., None], (n_b, B_HEADS, 2, B_DV))
    st_c = state_mlstm_C
    st_n = state_mlstm_n.reshape(bs, n_a, 2, A_HEADS, A_DK, 1)
    st_m = state_mlstm_m.reshape(bs, n_a, 2, A_HEADS, 1, 1)

    groups = [
        dict(x=x_prompt.reshape(bp * tp, D_MODEL), batch=bp, seq_len=tp, sample=False),
        dict(x=x_sample.reshape(bs * ts, D_MODEL), batch=bs, seq_len=ts, sample=True),
    ]
    ml_states = None
    ret_states = None
    for i in range(DEPTH):
        j = i // N_MIXERS
        for grp in groups:
            x = grp["x"]
            geo = dict(seq_len=grp["seq_len"], sample=grp["sample"])
            bt = dict(batch=grp["batch"], seq_len=grp["seq_len"])
            if i % N_MIXERS == 0:
                n_qk = A_HEADS * A_DK // TN_PROJ
                z, gi, gf = _inproj(x, ng4, mod5, i, [(ml_w_main, j, 0)], n_q=n_qk, n_k=n_qk,
                                    k_scale=A_DK ** -0.5, gates=_gate_weights(ml_w_in[j], ml_b_gate[j]), **geo)
                if grp["sample"]:
                    (h,) = _mlstm_scan(z, gi, gf, ml_gn4, j, n_layers=n_a, state=(st_c, st_n, st_m), **bt)
                else:
                    h, *ml_states = _mlstm_scan(z, gi, gf, ml_gn4, j, n_layers=n_a, prev=ml_states, **bt)
                x = _outproj(h, ml_w_out_b, j, x, ng4, mod5, i, **geo)
            else:
                n_qk = B_HEADS * B_DK // TN_PROJ
                w_qk = ret_w_qk_rope if grp["sample"] else ret_w_in_b
                w_parts = [(w_qk, j, 0), (ret_w_in_b, j, 1), (ret_w_in_b, j, 2)]
                (z,) = _inproj(x, ng4, mod5, i, w_parts, n_q=n_qk, n_k=n_qk, k_scale=B_DK ** -0.5,
                               rope=rope if grp["sample"] else None, **geo)
                if grp["sample"]:
                    (h,) = _ret_scan(z, dec_rep, ret_gn4, j, n_layers=n_b, state=state_ret_S, **bt)
                else:
                    h, ret_states = _ret_scan(z, dec_rep, ret_gn4, j, n_layers=n_b, prev=ret_states, **bt)
                x = _outproj(h, ret_w_out_b, j, x, ng4, mod5, i, **geo)
            grp["x"] = _ffn(x, ng4, mod5, i, ffn_w_up_b, ffn_conv, ffn_conv_b3, ffn_w_down_b, **geo)

    y_prompt = groups[0]["x"].reshape(bp, tp, D_MODEL)
    y_sample = groups[1]["x"].reshape(bs, ts, D_MODEL)
    new_c, new_n, new_m = ml_states
    return (y_prompt, y_sample, new_c, new_n.reshape(bp, n_a, 2, A_HEADS, A_DK),
            new_m.reshape(bp, n_a, 2, A_HEADS), ret_states)
```

```python
import functools
import math

import jax
import jax.numpy as jnp
from jax import lax
from jax.experimental import pallas as pl
from jax.experimental.pallas import tpu as pltpu

D_MODEL = 1024
DEPTH = 4
GRID_W = 64
CHUNK = 128
N_MIXERS = 2
A_HEADS = 4
A_DV = D_MODEL // A_HEADS
A_DK = A_DV // 2
A_MAIN = 2 * A_HEADS * A_DK + 2 * A_HEADS * A_DV
B_HEADS = 8
B_DK = D_MODEL // B_HEADS
B_DV = 2 * D_MODEL // B_HEADS
B_QK = 2 * B_HEADS * B_DK
ROPE_BASE = 10000.0
D_FF = ((8 * D_MODEL // 3 + 127) // 128) * 128
EPS = 1e-6
LN2 = math.log(2.0)

F32 = jnp.float32
BF16 = jnp.bfloat16

LANES = 128
SUBLANES = 8
MOD_ROWS = 16
VMEM_LIMIT = 48 * 1024 * 1024

TM_PROJ = 512
TN_PROJ = 512
TM_OUT = 1024
TR_OUT = 512
TM_FFN = 512
TF_FFN = 256
STATE_UNROLL = 8
OUT_UNROLL = 16
MLSTM_OUT_UNROLL = 8
SCAN_TOKENS = 1024


def _dot(a, b):
    return jnp.dot(a, b, preferred_element_type=F32)


def _dot_nt(a, b):
    return lax.dot_general(a, b, (((1,), (1,)), ((), ())), preferred_element_type=F32)


def _dot_tn(a, b):
    return lax.dot_general(a, b, (((0,), (0,)), ((), ())), preferred_element_type=F32)


def _rms(x):
    return x * lax.rsqrt(jnp.mean(x * x, axis=-1, keepdims=True) + EPS)


def _layer_norm(h):
    d = h - jnp.mean(h, axis=-1, keepdims=True)
    return d * lax.rsqrt(jnp.mean(d * d, axis=-1, keepdims=True) + EPS)


def _params(*sem):
    return pltpu.CompilerParams(dimension_semantics=sem, vmem_limit_bytes=VMEM_LIMIT)


def _resident(shape, index):
    return pl.BlockSpec(shape, lambda *_: index, pipeline_mode=pl.Buffered(1))


def _mod_kernel(cond_ref, w_ref, b_ref, o_ref):
    cnd = cond_ref[...]
    s = cnd * jax.nn.sigmoid(cnd)
    o_ref[...] = _dot(s.astype(BF16), w_ref[...].astype(BF16)) + b_ref[...]


def _modulation(cond, ada_w, ada_b):
    tn = 1024
    n_out = ada_w.shape[-1]
    return pl.pallas_call(
        _mod_kernel,
        grid=(DEPTH, n_out // tn),
        in_specs=[
            pl.BlockSpec((MOD_ROWS, D_MODEL), lambda l, j: (0, 0)),
            pl.BlockSpec((None, D_MODEL, tn), lambda l, j: (l, 0, j)),
            pl.BlockSpec((None, 1, tn), lambda l, j: (l, 0, j)),
        ],
        out_specs=pl.BlockSpec((None, MOD_ROWS, tn), lambda l, j: (l, 0, j)),
        out_shape=jax.ShapeDtypeStruct((DEPTH, MOD_ROWS, n_out), F32),
        compiler_params=_params("parallel", "parallel"),
        name="modulation",
    )(cond, ada_w, ada_b.reshape(DEPTH, 1, n_out))


def _mod_row(sample, seq_len, tm):
    if not sample:
        return lambda i: 0
    tiles_per_seq = seq_len // tm
    return lambda i: 1 + i // tiles_per_seq


def _rope_slab(x, cos, sin):
    return x * cos + pltpu.roll(x, 64, axis=1) * sin


def _inproj_kernel(*refs, n_w, n_q, n_k, k_scale, rope, gates, tn):
    x_ref, g_ref, sh_ref, sc_ref = refs[:4]
    w_refs = refs[4:4 + n_w]
    pos = 4 + n_w
    if gates:
        wgi_ref, wgf_ref, bgi_ref, bgf_ref = refs[pos:pos + 4]
        pos += 4
    if rope:
        cos_ref, sin_ref = refs[pos:pos + 2]
        pos += 2
    z_ref = refs[pos]
    pos += 1
    if gates:
        gi_ref, gf_ref = refs[pos:pos + 2]
        pos += 2
    u_sc = refs[pos]

    u = _rms(x_ref[...]) * g_ref[...] * (1.0 + sc_ref[...]) + sh_ref[...]
    u_sc[...] = u.astype(BF16)
    if gates:
        gi_ref[...] = _dot(u_sc[...], wgi_ref[...]) + bgi_ref[...]
        gf_ref[...] = _dot(u_sc[...], wgf_ref[...]) + bgf_ref[...]

    wb = w_refs[0].shape[1]
    for j in range(z_ref.shape[1] // tn):
        part, off = divmod(j * tn, wb)
        z = _dot(u_sc[...], w_refs[part][:, off:off + tn])
        scale = k_scale if n_q <= j < n_q + n_k else 1.0
        if rope and j < n_q + n_k:
            for s in range(tn // LANES):
                r = _rope_slab(z[:, s * LANES:(s + 1) * LANES], cos_ref[...], sin_ref[...])
                if scale != 1.0:
                    r = r * scale
                z_ref[:, j * tn + s * LANES:j * tn + (s + 1) * LANES] = r.astype(BF16)
        elif scale != 1.0:
            z_ref[:, j * tn:(j + 1) * tn] = (z * scale).astype(BF16)
        else:
            z_ref[:, j * tn:(j + 1) * tn] = z.astype(BF16)


def _inproj(x, ng4, mod5, layer, w_parts, *, seq_len, sample, n_q, n_k, k_scale, rope=None, gates=None):
    n_tok = x.shape[0]
    tm, tn = TM_PROJ, TN_PROJ
    wb = w_parts[0][0].shape[2] if len(w_parts) == 1 else B_QK
    n_col = wb * len(w_parts)
    row = _mod_row(sample, seq_len, tm)
    in_specs = [
        pl.BlockSpec((tm, D_MODEL), lambda i: (i, 0)),
        pl.BlockSpec((None, None, 1, D_MODEL), lambda i: (layer, 0, 0, 0)),
        pl.BlockSpec((None, None, None, 1, D_MODEL), lambda i: (layer, row(i), 0, 0, 0)),
        pl.BlockSpec((None, None, None, 1, D_MODEL), lambda i: (layer, row(i), 1, 0, 0)),
    ]
    in_specs += [_resident((None, D_MODEL, wb), (jl, 0, blk)) for _, jl, blk in w_parts]
    args = [x, ng4, mod5, mod5] + [w for w, _, _ in w_parts]
    out_specs = [pl.BlockSpec((tm, n_col), lambda i: (i, 0))]
    out_shape = [jax.ShapeDtypeStruct((n_tok, n_col), BF16)]
    if gates is not None:
        in_specs += [_resident((D_MODEL, LANES), (0, 0))] * 2 + [_resident((1, LANES), (0, 0))] * 2
        args += list(gates)
        out_specs += [pl.BlockSpec((tm, LANES), lambda i: (i, 0))] * 2
        out_shape += [jax.ShapeDtypeStruct((n_tok, LANES), F32)] * 2
    if rope is not None:
        tiles_per_seq = seq_len // tm
        in_specs += [pl.BlockSpec((tm, LANES), lambda i: (i % tiles_per_seq, 0))] * 2
        args += list(rope)
    kern = functools.partial(_inproj_kernel, n_w=len(w_parts), n_q=n_q, n_k=n_k, k_scale=k_scale,
                             rope=rope is not None, gates=gates is not None, tn=tn)
    return pl.pallas_call(
        kern,
        grid=(n_tok // tm,),
        in_specs=in_specs,
        out_specs=out_specs,
        out_shape=out_shape,
        scratch_shapes=[pltpu.VMEM((tm, D_MODEL), BF16)],
        compiler_params=_params("parallel"),
        name="inproj",
    )(*args)


def _tri_masks():
    li = lax.broadcasted_iota(jnp.int32, (CHUNK, CHUNK), 0)
    si = lax.broadcasted_iota(jnp.int32, (CHUNK, CHUNK), 1)
    return si <= li, si >= li


def _seqs_per_step(batch, seq_len, carry_in):
    if carry_in:
        return 1
    nb = max(1, SCAN_TOKENS // seq_len)
    while batch % nb:
        nb -= 1
    return nb


def _split_dot(mask_b, x):
    hi = x.astype(BF16)
    r1 = x - hi.astype(F32)
    mid = r1.astype(BF16)
    lo = (r1 - mid.astype(F32)).astype(BF16)
    return _dot(mask_b, hi) + _dot(mask_b, mid) + _dot(mask_b, lo)


def _mlstm_kernel(*refs, n_chunks, nb, carry_in, carry_out, n_alias):
    q_ref, k_ref, v_ref, o_ref, gi_ref, gf_ref, gn_ref = refs[:7]
    pos = 7
    if carry_in:
        c0_ref, n0_ref, m0_ref = refs[pos:pos + 3]
        pos += 3
    pos += n_alias
    h_ref = refs[pos]
    pos += 1
    if carry_out:
        cout_ref, nout_ref, mout_ref = refs[pos:pos + 3]
        pos += 3
    a_sc, b_sc, bt_sc, g_sc, bm_sc, mpf_sc, mpb_sc, c_sc, call_sc = refs[pos:pos + 9]

    head = pl.program_id(1)
    masks = _tri_masks()
    tril_b = masks[0].astype(BF16)
    ones_b = jnp.ones((CHUNK, LANES), BF16)
    lane = lax.broadcasted_iota(jnp.int32, (CHUNK, LANES), 1)
    lane_row = lax.broadcasted_iota(jnp.int32, (1, LANES), 1)
    fwd_lane = lane < A_HEADS
    cols = (head, head + A_HEADS)
    mp_sc = (mpf_sc, mpb_sc)
    t_seq = n_chunks * CHUNK

    def pick_col(x, col):
        return jnp.sum(jnp.where(lane == col, x, 0.0), axis=1, keepdims=True)

    def pick_scalar(row, col):
        return jnp.sum(jnp.where(lane_row == col, row, 0.0), axis=1, keepdims=True)

    def at(s, c):
        return pl.ds(pl.multiple_of(s * t_seq + c * CHUNK, CHUNK), CHUNK)

    def crow(s, c):
        return pl.ds(s * n_chunks + c, 1)

    def gate_body(c, carry):
        for s in range(nb):
            gf = gf_ref[at(s, c), :]
            lf = jnp.minimum(gf, 0.0) - jnp.log1p(jnp.exp(-jnp.abs(gf)))
            a_f = _split_dot(tril_b, lf)
            tot = a_f[CHUNK - 1:CHUNK, :]
            a_all = jnp.where(fwd_lane, a_f, tot - a_f + lf)
            b_all = gi_ref[at(s, c), :] - a_all
            a_sc[at(s, c), :] = a_all
            b_sc[at(s, c), :] = b_all
            bt_sc[s * n_chunks + c] = b_all.T[0:2 * A_HEADS, :]
            g_sc[crow(s, c), :] = tot
            bm_sc[crow(s, c), :] = jnp.max(b_all, axis=0, keepdims=True)
        return carry

    lax.fori_loop(0, n_chunks, gate_body, 0, unroll=min(n_chunks, STATE_UNROLL))

    if carry_in:
        m_init = (jnp.broadcast_to(m0_ref[0], (1, LANES)), jnp.broadcast_to(m0_ref[1], (1, LANES)))
    else:
        m_init = (jnp.zeros((1, LANES), F32),) * (2 * nb)

    def m_body(i, carry):
        ib = n_chunks - 1 - i
        out = []
        for s in range(nb):
            m_f, m_b = carry[2 * s], carry[2 * s + 1]
            mpf_sc[crow(s, i), :] = m_f
            mpb_sc[crow(s, ib), :] = m_b
            out.append(g_sc[crow(s, i), :] + jnp.maximum(m_f, bm_sc[crow(s, i), :]))
            out.append(g_sc[crow(s, ib), :] + jnp.maximum(m_b, bm_sc[crow(s, ib), :]))
        return tuple(out)

    m_last = lax.fori_loop(0, n_chunks, m_body, m_init)

    if carry_in:
        for dirn in range(2):
            c_sc[dirn, :, :A_DV] = c0_ref[dirn]
            c_sc[dirn, :, A_DV:] = jnp.broadcast_to(n0_ref[dirn], (A_DK, LANES))
    else:
        c_sc[...] = jnp.zeros_like(c_sc)

    def state_body(i, carry):
        for s in range(nb):
            for dirn in range(2):
                c = i if dirn == 0 else n_chunks - 1 - i
                mp_row = mp_sc[dirn][crow(s, c), :]
                m_top = pick_scalar(jnp.maximum(mp_row, bm_sc[crow(s, c), :]), cols[dirn])
                ws = jnp.exp(pick_col(b_sc[at(s, c), :], cols[dirn]) - m_top)
                dec = jnp.exp(pick_scalar(mp_row, cols[dirn]) - m_top)
                c_old = c_sc[2 * s + dirn]
                call_sc[2 * s + dirn, c] = c_old.astype(BF16)
                kw = (k_ref[at(s, c), :].astype(F32) * ws).astype(BF16)
                upd = jnp.concatenate([_dot_tn(kw, v_ref[at(s, c), :]), _dot_tn(kw, ones_b)], axis=1)
                c_sc[2 * s + dirn] = dec * c_old + upd
        return carry

    lax.fori_loop(0, n_chunks, state_body, 0, unroll=min(n_chunks, STATE_UNROLL))
    if carry_out:
        for s in range(nb):
            for dirn in range(2):
                cout_ref[s, dirn] = c_sc[2 * s + dirn, :, :A_DV]
                nout_ref[s, dirn] = c_sc[2 * s + dirn, :, A_DV:].T[0:1, :]
                mout_ref[s, dirn] = pick_scalar(m_last[2 * s + dirn], cols[dirn])

    def out_body(c, carry):
        for s in range(nb):
            q = q_ref[at(s, c), :]
            v = v_ref[at(s, c), :]
            s_raw = _dot_nt(q, k_ref[at(s, c), :])
            a_chunk = a_sc[at(s, c), :]
            h = None
            for dirn in range(2):
                col = cols[dirn]
                m_prev = pick_scalar(mp_sc[dirn][crow(s, c), :], col)
                b_vis = jnp.where(masks[dirn], bt_sc[s * n_chunks + c, pl.ds(col, 1), :], -jnp.inf)
                m_row = jnp.maximum(m_prev, jnp.max(b_vis, axis=1, keepdims=True))
                sw = (s_raw * jnp.exp(b_vis - m_row)).astype(BF16)
                w_inter = jnp.exp(m_prev - m_row)
                floor = jnp.exp(-(pick_col(a_chunk, col) + m_row))
                inter = _dot(q, call_sc[2 * s + dirn, c])
                num = _dot(sw, v) + w_inter * inter[:, :A_DV]
                den = _dot(sw, ones_b) + w_inter * inter[:, A_DV:]
                r = 1.0 / jnp.maximum(jnp.abs(den), floor)
                hd = num * jnp.concatenate([r, r], axis=1)
                h = hd if h is None else h + hd
            o = o_ref[at(s, c), :].astype(F32)
            h_ref[at(s, c), :] = (_layer_norm(h) * gn_ref[...] * jax.nn.sigmoid(o)).astype(BF16)
        return carry

    lax.fori_loop(0, n_chunks, out_body, 0, unroll=min(n_chunks, max(1, MLSTM_OUT_UNROLL // nb)))


def _mlstm_scan(z, gi, gf, gn4, j, *, batch, seq_len, n_layers, state=None, prev=None):
    n_tok = z.shape[0]
    n_chunks = seq_len // CHUNK
    carry_in = state is not None
    carry_out = not carry_in
    nb = _seqs_per_step(batch, seq_len, carry_in)
    t = nb * seq_len
    in_specs = [
        pl.BlockSpec((t, A_DK), lambda b, h: (b, h)),
        pl.BlockSpec((t, A_DK), lambda b, h: (b, A_HEADS + h)),
        pl.BlockSpec((t, A_DV), lambda b, h: (b, A_HEADS + h)),
        pl.BlockSpec((t, A_DV), lambda b, h: (b, 2 * A_HEADS + h)),
        pl.BlockSpec((t, LANES), lambda b, h: (b, 0)),
        pl.BlockSpec((t, LANES), lambda b, h: (b, 0)),
        pl.BlockSpec((None, None, 1, A_DV), lambda b, h: (j, h, 0, 0)),
    ]
    args = [z, z, z, z, gi, gf, gn4]
    aliases = {}
    if carry_in:
        in_specs += [
            pl.BlockSpec((None, None, 2, None, A_DK, A_DV), lambda b, h: (b, j, 0, h, 0, 0)),
            pl.BlockSpec((None, None, 2, None, A_DK, 1), lambda b, h: (b, j, 0, h, 0, 0)),
            pl.BlockSpec((None, None, 2, None, 1, 1), lambda b, h: (b, j, 0, h, 0, 0)),
        ]
        args += list(state)
    out_specs = [pl.BlockSpec((t, A_DV), lambda b, h: (b, h))]
    out_shape = [jax.ShapeDtypeStruct((n_tok, A_HEADS * A_DV), BF16)]
    if carry_out:
        out_specs += [
            pl.BlockSpec((nb, None, 2, None, A_DK, A_DV), lambda b, h: (b, j, 0, h, 0, 0)),
            pl.BlockSpec((nb, None, 2, None, 1, A_DK), lambda b, h: (b, j, 0, h, 0, 0)),
            pl.BlockSpec((nb, None, 2, None, 1, 1), lambda b, h: (b, j, 0, h, 0, 0)),
        ]
        out_shape += [
            jax.ShapeDtypeStruct((batch, n_layers, 2, A_HEADS, A_DK, A_DV), F32),
            jax.ShapeDtypeStruct((batch, n_layers, 2, A_HEADS, 1, A_DK), F32),
            jax.ShapeDtypeStruct((batch, n_layers, 2, A_HEADS, 1, 1), F32),
        ]
        if prev is not None:
            aliases = {len(args) + k: 1 + k for k in range(3)}
            in_specs += [pl.BlockSpec(memory_space=pl.ANY)] * 3
            args += list(prev)
    kern = functools.partial(_mlstm_kernel, n_chunks=n_chunks, nb=nb, carry_in=carry_in,
                             carry_out=carry_out, n_alias=len(aliases))
    return pl.pallas_call(
        kern,
        grid=(batch // nb, A_HEADS),
        in_specs=in_specs,
        out_specs=out_specs,
        out_shape=out_shape,
        input_output_aliases=aliases,
        scratch_shapes=[
            pltpu.VMEM((t, LANES), F32),
            pltpu.VMEM((t, LANES), F32),
            pltpu.VMEM((nb * n_chunks, 2 * A_HEADS, CHUNK), F32),
            pltpu.VMEM((nb * n_chunks, LANES), F32),
            pltpu.VMEM((nb * n_chunks, LANES), F32),
            pltpu.VMEM((nb * n_chunks, LANES), F32),
            pltpu.VMEM((nb * n_chunks, LANES), F32),
            pltpu.VMEM((2 * nb, A_DK, A_DV + LANES), F32),
            pltpu.VMEM((2 * nb, n_chunks, A_DK, A_DV + LANES), BF16),
        ],
        compiler_params=_params("parallel", "parallel"),
        name="mlstm_scan",
    )(*args)


def _ret_kernel(*refs, n_chunks, nb, carry_in, carry_out, n_alias):
    q_ref, k_ref, v_ref, gate_ref, dec_ref, gn_ref = refs[:6]
    pos = 6
    if carry_in:
        s0_ref = refs[pos]
        pos += 1
    pos += n_alias
    h_ref = refs[pos]
    pos += 1
    if carry_out:
        sout_ref = refs[pos]
        pos += 1
    s_sc, sall_sc, dsum_sc, xi_sc, zeta_sc = refs[pos:pos + 5]
    t_seq = n_chunks * CHUNK

    def at(s, c):
        return pl.ds(pl.multiple_of(s * t_seq + c * CHUNK, CHUNK), CHUNK)

    lg = jnp.log1p(-jnp.exp(-dec_ref[...] * LN2))
    lg_f = lg[0:1, :]
    lg_b = lg[1:2, :]

    @pl.when(pl.program_id(1) == 0)
    def _():
        masks = _tri_masks()
        li = lax.broadcasted_iota(jnp.int32, (CHUNK, B_DV), 0).astype(F32)
        si = lax.broadcasted_iota(jnp.int32, (CHUNK, CHUNK), 1).astype(F32)
        lq = li[:, :CHUNK]
        dsum_sc[...] = (
            jnp.where(masks[0], jnp.exp(jnp.where(masks[0], lq - si, 0.0) * lg_f[:, :CHUNK]), 0.0)
            + jnp.where(masks[1], jnp.exp(jnp.where(masks[1], si - lq, 0.0) * lg_b[:, :CHUNK]), 0.0))
        xi_sc[0] = jnp.exp((li + 1.0) * lg_f)
        xi_sc[1] = jnp.exp((CHUNK - li) * lg_b)
        zeta_sc[0] = jnp.exp((CHUNK - 1.0 - lq) * lg_f[:, :CHUNK])
        zeta_sc[1] = jnp.exp(lq * lg_b[:, :CHUNK])

    cdec = (jnp.exp(CHUNK * lg_f), jnp.exp(CHUNK * lg_b))

    if carry_in:
        qr = B_DK // 4
        for dirn in range(2):
            for n, o in enumerate((0, 2, 1, 3)):
                s_sc[dirn, n * qr:(n + 1) * qr, :] = s0_ref[dirn, o * qr:(o + 1) * qr, :]
    else:
        s_sc[...] = jnp.zeros_like(s_sc)

    def state_body(i, carry):
        for s in range(nb):
            for dirn in range(2):
                c = i if dirn == 0 else n_chunks - 1 - i
                s_old = s_sc[2 * s + dirn]
                sall_sc[2 * s + dirn, c] = s_old.astype(BF16)
                kz = (k_ref[at(s, c), :].astype(F32) * zeta_sc[dirn]).astype(BF16)
                s_sc[2 * s + dirn] = cdec[dirn] * s_old + _dot_tn(kz, v_ref[at(s, c), :])
        return carry

    lax.fori_loop(0, n_chunks, state_body, 0, unroll=min(n_chunks, STATE_UNROLL))
    if carry_out:
        for s in range(nb):
            for dirn in range(2):
                sout_ref[s, dirn] = s_sc[2 * s + dirn]

    def out_body(c, carry):
        for s in range(nb):
            q = q_ref[at(s, c), :]
            v = v_ref[at(s, c), :]
            sw = _dot_nt(q, k_ref[at(s, c), :]) * dsum_sc[...]
            h = (_dot(sw.astype(BF16), v) + xi_sc[0] * _dot(q, sall_sc[2 * s, c])
                 + xi_sc[1] * _dot(q, sall_sc[2 * s + 1, c]))
            g = gate_ref[at(s, c), :].astype(F32)
            h_ref[at(s, c), :] = (_layer_norm(h) * gn_ref[...] * (g * jax.nn.sigmoid(g))).astype(BF16)
        return carry

    lax.fori_loop(0, n_chunks, out_body, 0, unroll=min(n_chunks, max(1, OUT_UNROLL // nb)))


def _ret_scan(z, dec_rep, gn4, j, *, batch, seq_len, n_layers, state=None, prev=None):
    n_tok = z.shape[0]
    n_chunks = seq_len // CHUNK
    carry_in = state is not None
    carry_out = not carry_in
    nb = _seqs_per_step(batch, seq_len, carry_in)
    t = nb * seq_len
    in_specs = [
        pl.BlockSpec((t, B_DK), lambda h, b: (b, h)),
        pl.BlockSpec((t, B_DK), lambda h, b: (b, B_HEADS + h)),
        pl.BlockSpec((t, B_DV), lambda h, b: (b, B_HEADS + h)),
        pl.BlockSpec((t, B_DV), lambda h, b: (b, 2 * B_HEADS + h)),
        pl.BlockSpec((None, None, 2, B_DV), lambda h, b: (j, h, 0, 0)),
        pl.BlockSpec((None, None, 1, B_DV), lambda h, b: (j, h, 0, 0)),
    ]
    args = [z, z, z, z, dec_rep, gn4]
    aliases = {}
    if carry_in:
        in_specs += [pl.BlockSpec((None, None, 2, None, B_DK, B_DV), lambda h, b: (b, j, 0, h, 0, 0))]
        args += [state]
    out_specs = [pl.BlockSpec((t, B_DV), lambda h, b: (b, h))]
    out_shape = [jax.ShapeDtypeStruct((n_tok, B_HEADS * B_DV), BF16)]
    if carry_out:
        out_specs += [pl.BlockSpec((nb, None, 2, None, B_DK, B_DV), lambda h, b: (b, j, 0, h, 0, 0))]
        out_shape += [jax.ShapeDtypeStruct((batch, n_layers, 2, B_HEADS, B_DK, B_DV), F32)]
        if prev is not None:
            aliases = {len(args): 1}
            in_specs += [pl.BlockSpec(memory_space=pl.ANY)]
            args += [prev]
    kern = functools.partial(_ret_kernel, n_chunks=n_chunks, nb=nb, carry_in=carry_in,
                             carry_out=carry_out, n_alias=len(aliases))
    return pl.pallas_call(
        kern,
        grid=(B_HEADS, batch // nb),
        in_specs=in_specs,
        out_specs=out_specs,
        out_shape=out_shape,
        input_output_aliases=aliases,
        scratch_shapes=[
            pltpu.VMEM((2 * nb, B_DK, B_DV), F32),
            pltpu.VMEM((2 * nb, n_chunks, B_DK, B_DV), BF16),
            pltpu.VMEM((CHUNK, CHUNK), F32),
            pltpu.VMEM((2, CHUNK, B_DV), F32),
            pltpu.VMEM((2, CHUNK, B_DK), F32),
        ],
        compiler_params=_params("parallel", "arbitrary"),
        name="ret_scan",
    )(*args)


def _outproj_kernel(h_ref, w_ref, x_ref, g_ref, gate_ref, o_ref):
    for r in range(o_ref.shape[0] // TR_OUT):
        rows = slice(r * TR_OUT, (r + 1) * TR_OUT)
        y = _dot(h_ref[rows, :], w_ref[...])
        o_ref[rows, :] = x_ref[rows, :] + gate_ref[...] * (_rms(y) * g_ref[...])


def _outproj(h, w, j, x, ng4, mod5, layer, *, seq_len, sample):
    n_tok, hv = h.shape
    tm = TM_OUT
    row = _mod_row(sample, seq_len, tm)
    return pl.pallas_call(
        _outproj_kernel,
        grid=(n_tok // tm,),
        in_specs=[
            pl.BlockSpec((tm, hv), lambda i: (i, 0)),
            _resident((None, hv, D_MODEL), (j, 0, 0)),
            pl.BlockSpec((tm, D_MODEL), lambda i: (i, 0)),
            pl.BlockSpec((None, None, 1, D_MODEL), lambda i: (layer, 1, 0, 0)),
            pl.BlockSpec((None, None, None, 1, D_MODEL), lambda i: (layer, row(i), 2, 0, 0)),
        ],
        out_specs=pl.BlockSpec((tm, D_MODEL), lambda i: (i, 0)),
        out_shape=jax.ShapeDtypeStruct((n_tok, D_MODEL), F32),
        compiler_params=_params("parallel"),
        name="outproj",
    )(h, w, x, ng4, mod5)


def _conv3(hs_ref, half, h, cw, cb, seg, n_seg):
    for s in range(n_seg):
        base = SUBLANES + s * (seg + SUBLANES)
        h_seg = h[s * seg:(s + 1) * seg, :]
        hs_ref[2 * half, base + 1:base + 1 + seg, :] = h_seg
        hs_ref[2 * half + 1, base - 1:base - 1 + seg, :] = h_seg
    parts = []
    for s in range(n_seg):
        base = SUBLANES + s * (seg + SUBLANES)
        h_prev = hs_ref[2 * half, base:base + seg, :]
        h_next = hs_ref[2 * half + 1, base:base + seg, :]
        h_mid = h[s * seg:(s + 1) * seg, :]
        parts.append(h_prev * cw[0:1, :] + h_mid * cw[1:2, :] + h_next * cw[2:3, :] + cb)
    return parts


def _ffn_kernel(x_ref, g2_ref, sh_ref, sc_ref, wup_ref, cw_ref, cb_ref, wd_ref, g3_ref, gate_ref,
                o_ref, u_sc, act_sc, hs_sc, *, seg, n_seg, tf):
    u = _rms(x_ref[...]) * g2_ref[...] * (1.0 + sc_ref[...]) + sh_ref[...]
    u_sc[...] = u.astype(BF16)
    zero_rows = jnp.zeros((SUBLANES, tf), F32)
    for s in range(n_seg):
        base = SUBLANES + s * (seg + SUBLANES)
        for half in range(2):
            hs_sc[2 * half, base:base + SUBLANES, :] = zero_rows
            hs_sc[2 * half + 1, base + seg - SUBLANES:base + seg, :] = zero_rows

    for cidx in range(D_FF // tf):
        cg = slice(cidx * tf, (cidx + 1) * tf)
        cu = slice(D_FF + cidx * tf, D_FF + (cidx + 1) * tf)
        hg = _conv3(hs_sc, 0, _dot(u_sc[...], wup_ref[:, cg]), cw_ref[:, cg], cb_ref[:, cg], seg, n_seg)
        hu = _conv3(hs_sc, 1, _dot(u_sc[...], wup_ref[:, cu]), cw_ref[:, cu], cb_ref[:, cu], seg, n_seg)
        for s in range(n_seg):
            act = jax.nn.gelu(hg[s], approximate=True) * hu[s]
            act_sc[s * seg:(s + 1) * seg, cg] = act.astype(BF16)

    f = _dot(act_sc[...], wd_ref[...])
    o_ref[...] = x_ref[...] + gate_ref[...] * (_rms(f) * g3_ref[...])


def _ffn(x, ng4, mod5, layer, w_up, conv_w, conv_b, w_down, *, seq_len, sample):
    n_tok = x.shape[0]
    tm, tf = TM_FFN, TF_FFN
    row = _mod_row(sample, seq_len, tm)
    seg = GRID_W if sample else seq_len
    n_seg = tm // seg
    kern = functools.partial(_ffn_kernel, seg=seg, n_seg=n_seg, tf=tf)
    mod_spec = lambda k: pl.BlockSpec((None, None, None, 1, D_MODEL), lambda i: (layer, row(i), k, 0, 0))
    gain_spec = lambda k: pl.BlockSpec((None, None, 1, D_MODEL), lambda i: (layer, k, 0, 0))
    return pl.pallas_call(
        kern,
        grid=(n_tok // tm,),
        in_specs=[
            pl.BlockSpec((tm, D_MODEL), lambda i: (i, 0)),
            gain_spec(2),
            mod_spec(3),
            mod_spec(4),
            _resident((None, D_MODEL, 2 * D_FF), (layer, 0, 0)),
            _resident((None, 3, 2 * D_FF), (layer, 0, 0)),
            _resident((None, 1, 2 * D_FF), (layer, 0, 0)),
            _resident((None, D_FF, D_MODEL), (layer, 0, 0)),
            gain_spec(3),
            mod_spec(5),
        ],
        out_specs=pl.BlockSpec((tm, D_MODEL), lambda i: (i, 0)),
        out_shape=jax.ShapeDtypeStruct((n_tok, D_MODEL), F32),
        scratch_shapes=[
            pltpu.VMEM((tm, D_MODEL), BF16),
            pltpu.VMEM((tm, D_FF), BF16),
            pltpu.VMEM((4, SUBLANES + n_seg * (seg + SUBLANES), tf), F32),
        ],
        compiler_params=_params("parallel"),
        name="convffn",
    )(x, ng4, mod5, mod5, w_up, conv_w, conv_b, w_down, ng4, mod5)


def _rope_tables(seq_len):
    quarter = B_DK // 4
    inv = ROPE_BASE ** (-jnp.arange(quarter, dtype=F32) / quarter)
    t = jnp.arange(seq_len)
    rows = (t // GRID_W).astype(F32)[:, None] * inv
    cols = (t % GRID_W).astype(F32)[:, None] * inv
    cos = jnp.concatenate([jnp.cos(rows), jnp.cos(cols)] * 2, axis=-1)
    sin = jnp.concatenate([-jnp.sin(rows), -jnp.sin(cols), jnp.sin(rows), jnp.sin(cols)], axis=-1)
    return cos, sin


def _rope_qk_weights(w_in):
    n_l = w_in.shape[0]
    quarter = B_DK // 4
    w_qk = w_in[:, :, :B_QK].astype(BF16).reshape(n_l, D_MODEL, 2 * B_HEADS, 2, 2, quarter)
    return jnp.swapaxes(w_qk, 3, 4).reshape(n_l, D_MODEL, B_QK)


def _gate_weights(w_in_j, b_gate_j):
    wg = w_in_j[:, A_MAIN:].reshape(D_MODEL, 4, A_HEADS)
    pad_w = jnp.zeros((D_MODEL, LANES - 2 * A_HEADS), F32)
    pad_b = jnp.zeros((LANES - 2 * A_HEADS,), F32)
    wgi = jnp.concatenate([wg[:, 0], wg[:, 2], pad_w], axis=1).astype(BF16)
    wgf = jnp.concatenate([wg[:, 1], wg[:, 3], pad_w], axis=1).astype(BF16)
    bgi = jnp.concatenate([b_gate_j[0], b_gate_j[2], pad_b])[None, :]
    bgf = jnp.concatenate([b_gate_j[1], b_gate_j[3], pad_b])[None, :]
    return wgi, wgf, bgi, bgf


def kernel(x_prompt, x_sample, state_mlstm_C, state_mlstm_n, state_mlstm_m, state_ret_S, c, c_ctx,
           norm_gain, ada_w, ada_b, ml_w_in, ml_b_gate, ml_norm, ml_w_out,
           ret_w_in, ret_decay, ret_norm, ret_w_out, ffn_w_up, ffn_conv, ffn_conv_b, ffn_w_down):
    bp, tp, _ = x_prompt.shape
    bs, ts, _ = x_sample.shape
    n_a = ml_w_in.shape[0]
    n_b = ret_w_in.shape[0]

    cond = jnp.concatenate([c_ctx[None, :], c, jnp.zeros((MOD_ROWS - 1 - bs, D_MODEL), F32)], axis=0)
    mod5 = _modulation(cond, ada_w, ada_b).reshape(DEPTH, MOD_ROWS, 6, 1, D_MODEL)
    ng4 = norm_gain.reshape(DEPTH, 4, 1, D_MODEL)
    rope = _rope_tables(ts)

    ml_w_main = ml_w_in[:, :, :A_MAIN].astype(BF16)
    ml_w_out_b = ml_w_out.astype(BF16)
    ret_w_in_b = ret_w_in.astype(BF16)
    ret_w_qk_rope = _rope_qk_weights(ret_w_in)
    ret_w_out_b = ret_w_out.astype(BF16)
    ffn_w_up_b = ffn_w_up.astype(BF16)
    ffn_w_down_b = ffn_w_down.astype(BF16)
    ffn_conv_b3 = ffn_conv_b.reshape(DEPTH, 1, 2 * D_FF)
    ml_gn4 = ml_norm.reshape(n_a, A_HEADS, 1, A_DV)
    ret_gn4 = ret_norm.reshape(n_b, B_HEADS, 1, B_DV)
    dec_rep = jnp.broadcast_to(jnp.swapaxes(ret_decay, 1, 2)[..., None], (n_b, B_HEADS, 2, B_DV))
    st_c = state_mlstm_C
    st_n = state_mlstm_n.reshape(bs, n_a, 2, A_HEADS, A_DK, 1)
    st_m = state_mlstm_m.reshape(bs, n_a, 2, A_HEADS, 1, 1)

    groups = [
        dict(x=x_prompt.reshape(bp * tp, D_MODEL), batch=bp, seq_len=tp, sample=False),
        dict(x=x_sample.reshape(bs * ts, D_MODEL), batch=bs, seq_len=ts, sample=True),
    ]
    ml_states = None
    ret_states = None
    for i in range(DEPTH):
        j = i // N_MIXERS
        for grp in groups:
            x = grp["x"]
            geo = dict(seq_len=grp["seq_len"], sample=grp["sample"])
            bt = dict(batch=grp["batch"], seq_len=grp["seq_len"])
            if i % N_MIXERS == 0:
                n_qk = A_HEADS * A_DK // TN_PROJ
                z, gi, gf = _inproj(x, ng4, mod5, i, [(ml_w_main, j, 0)], n_q=n_qk, n_k=n_qk,
                                    k_scale=A_DK ** -0.5, gates=_gate_weights(ml_w_in[j], ml_b_gate[j]), **geo)
                if grp["sample"]:
                    (h,) = _mlstm_scan(z, gi, gf, ml_gn4, j, n_layers=n_a, state=(st_c, st_n, st_m), **bt)
                else:
                    h, *ml_states = _mlstm_scan(z, gi, gf, ml_gn4, j, n_layers=n_a, prev=ml_states, **bt)
                x = _outproj(h, ml_w_out_b, j, x, ng4, mod5, i, **geo)
            else:
                n_qk = B_HEADS * B_DK // TN_PROJ
                w_qk = ret_w_qk_rope if grp["sample"] else ret_w_in_b
                w_parts = [(w_qk, j, 0), (ret_w_in_b, j, 1), (ret_w_in_b, j, 2)]
                (z,) = _inproj(x, ng4, mod5, i, w_parts, n_q=n_qk, n_k=n_qk, k_scale=B_DK ** -0.5,
                               rope=rope if grp["sample"] else None, **geo)
                if grp["sample"]:
                    (h,) = _ret_scan(z, dec_rep, ret_gn4, j, n_layers=n_b, state=state_ret_S, **bt)
                else:
                    h, ret_states = _ret_scan(z, dec_rep, ret_gn4, j, n_layers=n_b, prev=ret_states, **bt)
                x = _outproj(h, ret_w_out_b, j, x, ng4, mod5, i, **geo)
            grp["x"] = _ffn(x, ng4, mod5, i, ffn_w_up_b, ffn_conv, ffn_conv_b3, ffn_w_down_b, **geo)

    y_prompt = groups[0]["x"].reshape(bp, tp, D_MODEL)
    y_sample = groups[1]["x"].reshape(bs, ts, D_MODEL)
    new_c, new_n, new_m = ml_states
    return (y_prompt, y_sample, new_c, new_n.reshape(bp, n_a, 2, A_HEADS, A_DK),
            new_m.reshape(bp, n_a, 2, A_HEADS), ret_states)
```

```python
import functools
import math

import jax
import jax.numpy as jnp
from jax import lax
from jax.experimental import pallas as pl
from jax.experimental.pallas import tpu as pltpu

D_MODEL = 1024
DEPTH = 4
GRID_W = 64
CHUNK = 128
N_MIXERS = 2
A_HEADS = 4
A_DV = D_MODEL // A_HEADS
A_DK = A_DV // 2
A_MAIN = 2 * A_HEADS * A_DK + 2 * A_HEADS * A_DV
B_HEADS = 8
B_DK = D_MODEL // B_HEADS
B_DV = 2 * D_MODEL // B_HEADS
B_QK = 2 * B_HEADS * B_DK
ROPE_BASE = 10000.0
D_FF = ((8 * D_MODEL // 3 + 127) // 128) * 128
EPS = 1e-6
LN2 = math.log(2.0)

F32 = jnp.float32
BF16 = jnp.bfloat16

LANES = 128
SUBLANES = 8
MOD_ROWS = 16
VMEM_LIMIT = 48 * 1024 * 1024

TM_PROJ = 512
TN_PROJ = 512
TM_OUT = 1024
TR_OUT = 512
TM_FFN = 512
TF_FFN = 256
STATE_UNROLL = 8
OUT_UNROLL = 16
MLSTM_OUT_UNROLL = 8
SCAN_TOKENS = 1024


def _dot(a, b):
    return jnp.dot(a, b, preferred_element_type=F32)


def _dot_nt(a, b):
    return lax.dot_general(a, b, (((1,), (1,)), ((), ())), preferred_element_type=F32)


def _dot_tn(a, b):
    return lax.dot_general(a, b, (((0,), (0,)), ((), ())), preferred_element_type=F32)


def _rms(x):
    return x * lax.rsqrt(jnp.mean(x * x, axis=-1, keepdims=True) + EPS)


def _layer_norm(h):
    d = h - jnp.mean(h, axis=-1, keepdims=True)
    return d * lax.rsqrt(jnp.mean(d * d, axis=-1, keepdims=True) + EPS)


def _params(*sem):
    return pltpu.CompilerParams(dimension_semantics=sem, vmem_limit_bytes=VMEM_LIMIT)


def _resident(shape, index):
    return pl.BlockSpec(shape, lambda *_: index, pipeline_mode=pl.Buffered(1))


def _mod_kernel(cond_ref, w_ref, b_ref, o_ref):
    cnd = cond_ref[...]
    s = cnd * jax.nn.sigmoid(cnd)
    o_ref[...] = _dot(s.astype(BF16), w_ref[...].astype(BF16)) + b_ref[...]


def _modulation(cond, ada_w, ada_b):
    tn = 1024
    n_out = ada_w.shape[-1]
    return pl.pallas_call(
        _mod_kernel,
        grid=(DEPTH, n_out // tn),
        in_specs=[
            pl.BlockSpec((MOD_ROWS, D_MODEL), lambda l, j: (0, 0)),
            pl.BlockSpec((None, D_MODEL, tn), lambda l, j: (l, 0, j)),
            pl.BlockSpec((None, 1, tn), lambda l, j: (l, 0, j)),
        ],
        out_specs=pl.BlockSpec((None, MOD_ROWS, tn), lambda l, j: (l, 0, j)),
        out_shape=jax.ShapeDtypeStruct((DEPTH, MOD_ROWS, n_out), F32),
        compiler_params=_params("parallel", "parallel"),
        name="modulation",
    )(cond, ada_w, ada_b.reshape(DEPTH, 1, n_out))


def _mod_row(sample, seq_len, tm):
    if not sample:
        return lambda i: 0
    tiles_per_seq = seq_len // tm
    return lambda i: 1 + i // tiles_per_seq


def _rope_slab(x, cos, sin):
    return x * cos + pltpu.roll(x, 64, axis=1) * sin


def _inproj_kernel(*refs, n_w, n_q, n_k, k_scale, rope, gates, tn):
    x_ref, g_ref, sh_ref, sc_ref = refs[:4]
    w_refs = refs[4:4 + n_w]
    pos = 4 + n_w
    if gates:
        wgi_ref, wgf_ref, bgi_ref, bgf_ref = refs[pos:pos + 4]
        pos += 4
    if rope:
        cos_ref, sin_ref = refs[pos:pos + 2]
        pos += 2
    z_ref = refs[pos]
    pos += 1
    if gates:
        gi_ref, gf_ref = refs[pos:pos + 2]
        pos += 2
    u_sc = refs[pos]

    u = _rms(x_ref[...]) * g_ref[...] * (1.0 + sc_ref[...]) + sh_ref[...]
    u_sc[...] = u.astype(BF16)
    if gates:
        gi_ref[...] = _dot(u_sc[...], wgi_ref[...]) + bgi_ref[...]
        gf_ref[...] = _dot(u_sc[...], wgf_ref[...]) + bgf_ref[...]

    wb = w_refs[0].shape[1]
    for j in range(z_ref.shape[1] // tn):
        part, off = divmod(j * tn, wb)
        z = _dot(u_sc[...], w_refs[part][:, off:off + tn])
        scale = k_scale if n_q <= j < n_q + n_k else 1.0
        if rope and j < n_q + n_k:
            for s in range(tn // LANES):
                r = _rope_slab(z[:, s * LANES:(s + 1) * LANES], cos_ref[...], sin_ref[...])
                if scale != 1.0:
                    r = r * scale
                z_ref[:, j * tn + s * LANES:j * tn + (s + 1) * LANES] = r.astype(BF16)
        elif scale != 1.0:
            z_ref[:, j * tn:(j + 1) * tn] = (z * scale).astype(BF16)
        else:
            z_ref[:, j * tn:(j + 1) * tn] = z.astype(BF16)


def _inproj(x, ng4, mod5, layer, w_parts, *, seq_len, sample, n_q, n_k, k_scale, rope=None, gates=None):
    n_tok = x.shape[0]
    tm, tn = TM_PROJ, TN_PROJ
    wb = w_parts[0][0].shape[2] if len(w_parts) == 1 else B_QK
    n_col = wb * len(w_parts)
    row = _mod_row(sample, seq_len, tm)
    in_specs = [
        pl.BlockSpec((tm, D_MODEL), lambda i: (i, 0)),
        pl.BlockSpec((None, None, 1, D_MODEL), lambda i: (layer, 0, 0, 0)),
        pl.BlockSpec((None, None, None, 1, D_MODEL), lambda i: (layer, row(i), 0, 0, 0)),
        pl.BlockSpec((None, None, None, 1, D_MODEL), lambda i: (layer, row(i), 1, 0, 0)),
    ]
    in_specs += [_resident((None, D_MODEL, wb), (jl, 0, blk)) for _, jl, blk in w_parts]
    args = [x, ng4, mod5, mod5] + [w for w, _, _ in w_parts]
    out_specs = [pl.BlockSpec((tm, n_col), lambda i: (i, 0))]
    out_shape = [jax.ShapeDtypeStruct((n_tok, n_col), BF16)]
    if gates is not None:
        in_specs += [_resident((D_MODEL, LANES), (0, 0))] * 2 + [_resident((1, LANES), (0, 0))] * 2
        args += list(gates)
        out_specs += [pl.BlockSpec((tm, LANES), lambda i: (i, 0))] * 2
        out_shape += [jax.ShapeDtypeStruct((n_tok, LANES), F32)] * 2
    if rope is not None:
        tiles_per_seq = seq_len // tm
        in_specs += [pl.BlockSpec((tm, LANES), lambda i: (i % tiles_per_seq, 0))] * 2
        args += list(rope)
    kern = functools.partial(_inproj_kernel, n_w=len(w_parts), n_q=n_q, n_k=n_k, k_scale=k_scale,
                             rope=rope is not None, gates=gates is not None, tn=tn)
    return pl.pallas_call(
        kern,
        grid=(n_tok // tm,),
        in_specs=in_specs,
        out_specs=out_specs,
        out_shape=out_shape,
        scratch_shapes=[pltpu.VMEM((tm, D_MODEL), BF16)],
        compiler_params=_params("parallel"),
        name="inproj",
    )(*args)


def _tri_masks():
    li = lax.broadcasted_iota(jnp.int32, (CHUNK, CHUNK), 0)
    si = lax.broadcasted_iota(jnp.int32, (CHUNK, CHUNK), 1)
    return si <= li, si >= li


def _layer_slot(ref, fresh_slot):
    if fresh_slot is None:
        return ref
    for other in range(ref.shape[1]):
        if other != fresh_slot:
            ref[:, other] = jnp.zeros(ref.shape[:1] + ref.shape[2:], ref.dtype)
    return ref.at[:, fresh_slot]


def _seqs_per_step(batch, seq_len, carry_in):
    if carry_in:
        return 1
    nb = max(1, SCAN_TOKENS // seq_len)
    while batch % nb:
        nb -= 1
    return nb


def _split_dot(mask_b, x):
    hi = x.astype(BF16)
    r1 = x - hi.astype(F32)
    mid = r1.astype(BF16)
    lo = (r1 - mid.astype(F32)).astype(BF16)
    return _dot(mask_b, hi) + _dot(mask_b, mid) + _dot(mask_b, lo)


def _mlstm_kernel(*refs, n_chunks, nb, carry_in, carry_out, n_alias, fresh_slot):
    q_ref, k_ref, v_ref, o_ref, gi_ref, gf_ref, gn_ref = refs[:7]
    pos = 7
    if carry_in:
        c0_ref, n0_ref, m0_ref = refs[pos:pos + 3]
        pos += 3
    pos += n_alias
    h_ref = refs[pos]
    pos += 1
    if carry_out:
        cout_ref, nout_ref, mout_ref = refs[pos:pos + 3]
        pos += 3
    a_sc, b_sc, bt_sc, g_sc, bm_sc, mpf_sc, mpb_sc, c_sc, call_sc = refs[pos:pos + 9]

    head = pl.program_id(1)
    masks = _tri_masks()
    tril_b = masks[0].astype(BF16)
    ones_b = jnp.ones((CHUNK, LANES), BF16)
    lane = lax.broadcasted_iota(jnp.int32, (CHUNK, LANES), 1)
    lane_row = lax.broadcasted_iota(jnp.int32, (1, LANES), 1)
    fwd_lane = lane < A_HEADS
    cols = (head, head + A_HEADS)
    mp_sc = (mpf_sc, mpb_sc)
    t_seq = n_chunks * CHUNK

    def pick_col(x, col):
        return jnp.sum(jnp.where(lane == col, x, 0.0), axis=1, keepdims=True)

    def pick_scalar(row, col):
        return jnp.sum(jnp.where(lane_row == col, row, 0.0), axis=1, keepdims=True)

    def at(s, c):
        return pl.ds(pl.multiple_of(s * t_seq + c * CHUNK, CHUNK), CHUNK)

    def crow(s, c):
        return pl.ds(s * n_chunks + c, 1)

    def gate_body(c, carry):
        for s in range(nb):
            gf = gf_ref[at(s, c), :]
            lf = jnp.minimum(gf, 0.0) - jnp.log1p(jnp.exp(-jnp.abs(gf)))
            a_f = _split_dot(tril_b, lf)
            tot = a_f[CHUNK - 1:CHUNK, :]
            a_all = jnp.where(fwd_lane, a_f, tot - a_f + lf)
            b_all = gi_ref[at(s, c), :] - a_all
            a_sc[at(s, c), :] = a_all
            b_sc[at(s, c), :] = b_all
            bt_sc[s * n_chunks + c] = b_all.T[0:2 * A_HEADS, :]
            g_sc[crow(s, c), :] = tot
            bm_sc[crow(s, c), :] = jnp.max(b_all, axis=0, keepdims=True)
        return carry

    lax.fori_loop(0, n_chunks, gate_body, 0, unroll=min(n_chunks, STATE_UNROLL))

    if carry_in:
        m_init = (jnp.broadcast_to(m0_ref[0], (1, LANES)), jnp.broadcast_to(m0_ref[1], (1, LANES)))
    else:
        m_init = (jnp.zeros((1, LANES), F32),) * (2 * nb)

    def m_body(i, carry):
        ib = n_chunks - 1 - i
        out = []
        for s in range(nb):
            m_f, m_b = carry[2 * s], carry[2 * s + 1]
            mpf_sc[crow(s, i), :] = m_f
            mpb_sc[crow(s, ib), :] = m_b
            out.append(g_sc[crow(s, i), :] + jnp.maximum(m_f, bm_sc[crow(s, i), :]))
            out.append(g_sc[crow(s, ib), :] + jnp.maximum(m_b, bm_sc[crow(s, ib), :]))
        return tuple(out)

    m_last = lax.fori_loop(0, n_chunks, m_body, m_init)

    if carry_in:
        for dirn in range(2):
            c_sc[dirn, :, :A_DV] = c0_ref[dirn]
            c_sc[dirn, :, A_DV:] = jnp.broadcast_to(n0_ref[dirn], (A_DK, LANES))
    else:
        c_sc[...] = jnp.zeros_like(c_sc)

    def state_body(i, carry):
        for s in range(nb):
            for dirn in range(2):
                c = i if dirn == 0 else n_chunks - 1 - i
                mp_row = mp_sc[dirn][crow(s, c), :]
                m_top = pick_scalar(jnp.maximum(mp_row, bm_sc[crow(s, c), :]), cols[dirn])
                ws = jnp.exp(pick_col(b_sc[at(s, c), :], cols[dirn]) - m_top)
                dec = jnp.exp(pick_scalar(mp_row, cols[dirn]) - m_top)
                c_old = c_sc[2 * s + dirn]
                call_sc[2 * s + dirn, c] = c_old.astype(BF16)
                kw = (k_ref[at(s, c), :].astype(F32) * ws).astype(BF16)
                upd = jnp.concatenate([_dot_tn(kw, v_ref[at(s, c), :]), _dot_tn(kw, ones_b)], axis=1)
                c_sc[2 * s + dirn] = dec * c_old + upd
        return carry

    lax.fori_loop(0, n_chunks, state_body, 0, unroll=min(n_chunks, STATE_UNROLL))
    if carry_out:
        outs = [_layer_slot(r, fresh_slot) for r in (cout_ref, nout_ref, mout_ref)]
        for s in range(nb):
            for dirn in range(2):
                outs[0][s, dirn] = c_sc[2 * s + dirn, :, :A_DV]
                outs[1][s, dirn] = c_sc[2 * s + dirn, :, A_DV:].T[0:1, :]
                outs[2][s, dirn] = pick_scalar(m_last[2 * s + dirn], cols[dirn])

    def out_body(c, carry):
        for s in range(nb):
            q = q_ref[at(s, c), :]
            v = v_ref[at(s, c), :]
            s_raw = _dot_nt(q, k_ref[at(s, c), :])
            a_chunk = a_sc[at(s, c), :]
            h = None
            for dirn in range(2):
                col = cols[dirn]
                m_prev = pick_scalar(mp_sc[dirn][crow(s, c), :], col)
                b_vis = jnp.where(masks[dirn], bt_sc[s * n_chunks + c, pl.ds(col, 1), :], -jnp.inf)
                m_row = jnp.maximum(m_prev, jnp.max(b_vis, axis=1, keepdims=True))
                sw = (s_raw * jnp.exp(b_vis - m_row)).astype(BF16)
                w_inter = jnp.exp(m_prev - m_row)
                floor = jnp.exp(-(pick_col(a_chunk, col) + m_row))
                inter = _dot(q, call_sc[2 * s + dirn, c])
                num = _dot(sw, v) + w_inter * inter[:, :A_DV]
                den = _dot(sw, ones_b) + w_inter * inter[:, A_DV:]
                r = 1.0 / jnp.maximum(jnp.abs(den), floor)
                hd = num * jnp.concatenate([r, r], axis=1)
                h = hd if h is None else h + hd
            o = o_ref[at(s, c), :].astype(F32)
            h_ref[at(s, c), :] = (_layer_norm(h) * gn_ref[...] * jax.nn.sigmoid(o)).astype(BF16)
        return carry

    lax.fori_loop(0, n_chunks, out_body, 0, unroll=min(n_chunks, max(1, MLSTM_OUT_UNROLL // nb)))


def _mlstm_scan(z, gi, gf, gn4, j, *, batch, seq_len, n_layers, state=None, prev=None):
    n_tok = z.shape[0]
    n_chunks = seq_len // CHUNK
    carry_in = state is not None
    carry_out = not carry_in
    nb = _seqs_per_step(batch, seq_len, carry_in)
    t = nb * seq_len
    in_specs = [
        pl.BlockSpec((t, A_DK), lambda b, h: (b, h)),
        pl.BlockSpec((t, A_DK), lambda b, h: (b, A_HEADS + h)),
        pl.BlockSpec((t, A_DV), lambda b, h: (b, A_HEADS + h)),
        pl.BlockSpec((t, A_DV), lambda b, h: (b, 2 * A_HEADS + h)),
        pl.BlockSpec((t, LANES), lambda b, h: (b, 0)),
        pl.BlockSpec((t, LANES), lambda b, h: (b, 0)),
        pl.BlockSpec((None, None, 1, A_DV), lambda b, h: (j, h, 0, 0)),
    ]
    args = [z, z, z, z, gi, gf, gn4]
    aliases = {}
    if carry_in:
        in_specs += [
            pl.BlockSpec((None, None, 2, None, A_DK, A_DV), lambda b, h: (b, j, 0, h, 0, 0)),
            pl.BlockSpec((None, None, 2, None, A_DK, 1), lambda b, h: (b, j, 0, h, 0, 0)),
            pl.BlockSpec((None, None, 2, None, 1, 1), lambda b, h: (b, j, 0, h, 0, 0)),
        ]
        args += list(state)
    out_specs = [pl.BlockSpec((t, A_DV), lambda b, h: (b, h))]
    out_shape = [jax.ShapeDtypeStruct((n_tok, A_HEADS * A_DV), BF16)]
    fresh_slot = j if carry_out and prev is None else None
    if carry_out:
        lay, jb = (n_layers, 0) if prev is None else (None, j)
        out_specs += [
            pl.BlockSpec((nb, lay, 2, None, A_DK, A_DV), lambda b, h: (b, jb, 0, h, 0, 0)),
            pl.BlockSpec((nb, lay, 2, None, 1, A_DK), lambda b, h: (b, jb, 0, h, 0, 0)),
            pl.BlockSpec((nb, lay, 2, None, 1, 1), lambda b, h: (b, jb, 0, h, 0, 0)),
        ]
        out_shape += [
            jax.ShapeDtypeStruct((batch, n_layers, 2, A_HEADS, A_DK, A_DV), F32),
            jax.ShapeDtypeStruct((batch, n_layers, 2, A_HEADS, 1, A_DK), F32),
            jax.ShapeDtypeStruct((batch, n_layers, 2, A_HEADS, 1, 1), F32),
        ]
        if prev is not None:
            aliases = {len(args) + k: 1 + k for k in range(3)}
            in_specs += [pl.BlockSpec(memory_space=pl.ANY)] * 3
            args += list(prev)
    kern = functools.partial(_mlstm_kernel, n_chunks=n_chunks, nb=nb, carry_in=carry_in,
                             carry_out=carry_out, n_alias=len(aliases), fresh_slot=fresh_slot)
    return pl.pallas_call(
        kern,
        grid=(batch // nb, A_HEADS),
        in_specs=in_specs,
        out_specs=out_specs,
        out_shape=out_shape,
        input_output_aliases=aliases,
        scratch_shapes=[
            pltpu.VMEM((t, LANES), F32),
            pltpu.VMEM((t, LANES), F32),
            pltpu.VMEM((nb * n_chunks, 2 * A_HEADS, CHUNK), F32),
            pltpu.VMEM((nb * n_chunks, LANES), F32),
            pltpu.VMEM((nb * n_chunks, LANES), F32),
            pltpu.VMEM((nb * n_chunks, LANES), F32),
            pltpu.VMEM((nb * n_chunks, LANES), F32),
            pltpu.VMEM((2 * nb, A_DK, A_DV + LANES), F32),
            pltpu.VMEM((2 * nb, n_chunks, A_DK, A_DV + LANES), BF16),
        ],
        compiler_params=_params("parallel", "parallel"),
        name="mlstm_scan",
    )(*args)


def _ret_kernel(*refs, n_chunks, nb, carry_in, carry_out, n_alias, fresh_slot):
    q_ref, k_ref, v_ref, gate_ref, dec_ref, gn_ref = refs[:6]
    pos = 6
    if carry_in:
        s0_ref = refs[pos]
        pos += 1
    pos += n_alias
    h_ref = refs[pos]
    pos += 1
    if carry_out:
        sout_ref = refs[pos]
        pos += 1
    s_sc, sall_sc, dsum_sc, xi_sc, zeta_sc = refs[pos:pos + 5]
    t_seq = n_chunks * CHUNK

    def at(s, c):
        return pl.ds(pl.multiple_of(s * t_seq + c * CHUNK, CHUNK), CHUNK)

    lg = jnp.log1p(-jnp.exp(-dec_ref[...] * LN2))
    lg_f = lg[0:1, :]
    lg_b = lg[1:2, :]

    @pl.when(pl.program_id(1) == 0)
    def _():
        masks = _tri_masks()
        li = lax.broadcasted_iota(jnp.int32, (CHUNK, B_DV), 0).astype(F32)
        si = lax.broadcasted_iota(jnp.int32, (CHUNK, CHUNK), 1).astype(F32)
        lq = li[:, :CHUNK]
        dsum_sc[...] = (
            jnp.where(masks[0], jnp.exp(jnp.where(masks[0], lq - si, 0.0) * lg_f[:, :CHUNK]), 0.0)
            + jnp.where(masks[1], jnp.exp(jnp.where(masks[1], si - lq, 0.0) * lg_b[:, :CHUNK]), 0.0))
        xi_sc[0] = jnp.exp((li + 1.0) * lg_f)
        xi_sc[1] = jnp.exp((CHUNK - li) * lg_b)
        zeta_sc[0] = jnp.exp((CHUNK - 1.0 - lq) * lg_f[:, :CHUNK])
        zeta_sc[1] = jnp.exp(lq * lg_b[:, :CHUNK])

    cdec = (jnp.exp(CHUNK * lg_f), jnp.exp(CHUNK * lg_b))

    if carry_in:
        qr = B_DK // 4
        for dirn in range(2):
            for n, o in enumerate((0, 2, 1, 3)):
                s_sc[dirn, n * qr:(n + 1) * qr, :] = s0_ref[dirn, o * qr:(o + 1) * qr, :]
    else:
        s_sc[...] = jnp.zeros_like(s_sc)

    def state_body(i, carry):
        for s in range(nb):
            for dirn in range(2):
                c = i if dirn == 0 else n_chunks - 1 - i
                s_old = s_sc[2 * s + dirn]
                sall_sc[2 * s + dirn, c] = s_old.astype(BF16)
                kz = (k_ref[at(s, c), :].astype(F32) * zeta_sc[dirn]).astype(BF16)
                s_sc[2 * s + dirn] = cdec[dirn] * s_old + _dot_tn(kz, v_ref[at(s, c), :])
        return carry

    lax.fori_loop(0, n_chunks, state_body, 0, unroll=min(n_chunks, STATE_UNROLL))
    if carry_out:
        s_out = _layer_slot(sout_ref, fresh_slot)
        for s in range(nb):
            for dirn in range(2):
                s_out[s, dirn] = s_sc[2 * s + dirn]

    def out_body(c, carry):
        for s in range(nb):
            q = q_ref[at(s, c), :]
            v = v_ref[at(s, c), :]
            sw = _dot_nt(q, k_ref[at(s, c), :]) * dsum_sc[...]
            h = (_dot(sw.astype(BF16), v) + xi_sc[0] * _dot(q, sall_sc[2 * s, c])
                 + xi_sc[1] * _dot(q, sall_sc[2 * s + 1, c]))
            g = gate_ref[at(s, c), :].astype(F32)
            h_ref[at(s, c), :] = (_layer_norm(h) * gn_ref[...] * (g * jax.nn.sigmoid(g))).astype(BF16)
        return carry

    lax.fori_loop(0, n_chunks, out_body, 0, unroll=min(n_chunks, max(1, OUT_UNROLL // nb)))


def _ret_scan(z, dec_rep, gn4, j, *, batch, seq_len, n_layers, state=None, prev=None):
    n_tok = z.shape[0]
    n_chunks = seq_len // CHUNK
    carry_in = state is not None
    carry_out = not carry_in
    nb = _seqs_per_step(batch, seq_len, carry_in)
    t = nb * seq_len
    in_specs = [
        pl.BlockSpec((t, B_DK), lambda h, b: (b, h)),
        pl.BlockSpec((t, B_DK), lambda h, b: (b, B_HEADS + h)),
        pl.BlockSpec((t, B_DV), lambda h, b: (b, B_HEADS + h)),
        pl.BlockSpec((t, B_DV), lambda h, b: (b, 2 * B_HEADS + h)),
        pl.BlockSpec((None, None, 2, B_DV), lambda h, b: (j, h, 0, 0)),
        pl.BlockSpec((None, None, 1, B_DV), lambda h, b: (j, h, 0, 0)),
    ]
    args = [z, z, z, z, dec_rep, gn4]
    aliases = {}
    if carry_in:
        in_specs += [pl.BlockSpec((None, None, 2, None, B_DK, B_DV), lambda h, b: (b, j, 0, h, 0, 0))]
        args += [state]
    out_specs = [pl.BlockSpec((t, B_DV), lambda h, b: (b, h))]
    out_shape = [jax.ShapeDtypeStruct((n_tok, B_HEADS * B_DV), BF16)]
    fresh_slot = j if carry_out and prev is None else None
    if carry_out:
        lay, jb = (n_layers, 0) if prev is None else (None, j)
        out_specs += [pl.BlockSpec((nb, lay, 2, None, B_DK, B_DV), lambda h, b: (b, jb, 0, h, 0, 0))]
        out_shape += [jax.ShapeDtypeStruct((batch, n_layers, 2, B_HEADS, B_DK, B_DV), F32)]
        if prev is not None:
            aliases = {len(args): 1}
            in_specs += [pl.BlockSpec(memory_space=pl.ANY)]
            args += [prev]
    kern = functools.partial(_ret_kernel, n_chunks=n_chunks, nb=nb, carry_in=carry_in,
                             carry_out=carry_out, n_alias=len(aliases), fresh_slot=fresh_slot)
    return pl.pallas_call(
        kern,
        grid=(B_HEADS, batch // nb),
        in_specs=in_specs,
        out_specs=out_specs,
        out_shape=out_shape,
        input_output_aliases=aliases,
        scratch_shapes=[
            pltpu.VMEM((2 * nb, B_DK, B_DV), F32),
            pltpu.VMEM((2 * nb, n_chunks, B_DK, B_DV), BF16),
            pltpu.VMEM((CHUNK, CHUNK), F32),
            pltpu.VMEM((2, CHUNK, B_DV), F32),
            pltpu.VMEM((2, CHUNK, B_DK), F32),
        ],
        compiler_params=_params("parallel", "arbitrary"),
        name="ret_scan",
    )(*args)


def _outproj_kernel(h_ref, w_ref, x_ref, g_ref, gate_ref, o_ref):
    for r in range(o_ref.shape[0] // TR_OUT):
        rows = slice(r * TR_OUT, (r + 1) * TR_OUT)
        y = _dot(h_ref[rows, :], w_ref[...])
        o_ref[rows, :] = x_ref[rows, :] + gate_ref[...] * (_rms(y) * g_ref[...])


def _outproj(h, w, j, x, ng4, mod5, layer, *, seq_len, sample):
    n_tok, hv = h.shape
    tm = TM_OUT
    row = _mod_row(sample, seq_len, tm)
    return pl.pallas_call(
        _outproj_kernel,
        grid=(n_tok // tm,),
        in_specs=[
            pl.BlockSpec((tm, hv), lambda i: (i, 0)),
            _resident((None, hv, D_MODEL), (j, 0, 0)),
            pl.BlockSpec((tm, D_MODEL), lambda i: (i, 0)),
            pl.BlockSpec((None, None, 1, D_MODEL), lambda i: (layer, 1, 0, 0)),
            pl.BlockSpec((None, None, None, 1, D_MODEL), lambda i: (layer, row(i), 2, 0, 0)),
        ],
        out_specs=pl.BlockSpec((tm, D_MODEL), lambda i: (i, 0)),
        out_shape=jax.ShapeDtypeStruct((n_tok, D_MODEL), F32),
        compiler_params=_params("parallel"),
        name="outproj",
    )(h, w, x, ng4, mod5)


def _conv3(hs_ref, half, h, cw, cb, seg, n_seg):
    for s in range(n_seg):
        base = SUBLANES + s * (seg + SUBLANES)
        h_seg = h[s * seg:(s + 1) * seg, :]
        hs_ref[2 * half, base + 1:base + 1 + seg, :] = h_seg
        hs_ref[2 * half + 1, base - 1:base - 1 + seg, :] = h_seg
    parts = []
    for s in range(n_seg):
        base = SUBLANES + s * (seg + SUBLANES)
        h_prev = hs_ref[2 * half, base:base + seg, :]
        h_next = hs_ref[2 * half + 1, base:base + seg, :]
        h_mid = h[s * seg:(s + 1) * seg, :]
        parts.append(h_prev * cw[0:1, :] + h_mid * cw[1:2, :] + h_next * cw[2:3, :] + cb)
    return parts


def _ffn_kernel(x_ref, g2_ref, sh_ref, sc_ref, wup_ref, cw_ref, cb_ref, wd_ref, g3_ref, gate_ref,
                o_ref, u_sc, act_sc, hs_sc, *, seg, n_seg, tf):
    u = _rms(x_ref[...]) * g2_ref[...] * (1.0 + sc_ref[...]) + sh_ref[...]
    u_sc[...] = u.astype(BF16)
    zero_rows = jnp.zeros((SUBLANES, tf), F32)
    for s in range(n_seg):
        base = SUBLANES + s * (seg + SUBLANES)
        for half in range(2):
            hs_sc[2 * half, base:base + SUBLANES, :] = zero_rows
            hs_sc[2 * half + 1, base + seg - SUBLANES:base + seg, :] = zero_rows

    for cidx in range(D_FF // tf):
        cg = slice(cidx * tf, (cidx + 1) * tf)
        cu = slice(D_FF + cidx * tf, D_FF + (cidx + 1) * tf)
        hg = _conv3(hs_sc, 0, _dot(u_sc[...], wup_ref[:, cg]), cw_ref[:, cg], cb_ref[:, cg], seg, n_seg)
        hu = _conv3(hs_sc, 1, _dot(u_sc[...], wup_ref[:, cu]), cw_ref[:, cu], cb_ref[:, cu], seg, n_seg)
        for s in range(n_seg):
            act = jax.nn.gelu(hg[s], approximate=True) * hu[s]
            act_sc[s * seg:(s + 1) * seg, cg] = act.astype(BF16)

    f = _dot(act_sc[...], wd_ref[...])
    o_ref[...] = x_ref[...] + gate_ref[...] * (_rms(f) * g3_ref[...])


def _ffn(x, ng4, mod5, layer, w_up, conv_w, conv_b, w_down, *, seq_len, sample):
    n_tok = x.shape[0]
    tm, tf = TM_FFN, TF_FFN
    row = _mod_row(sample, seq_len, tm)
    seg = GRID_W if sample else seq_len
    n_seg = tm // seg
    kern = functools.partial(_ffn_kernel, seg=seg, n_seg=n_seg, tf=tf)
    mod_spec = lambda k: pl.BlockSpec((None, None, None, 1, D_MODEL), lambda i: (layer, row(i), k, 0, 0))
    gain_spec = lambda k: pl.BlockSpec((None, None, 1, D_MODEL), lambda i: (layer, k, 0, 0))
    return pl.pallas_call(
        kern,
        grid=(n_tok // tm,),
        in_specs=[
            pl.BlockSpec((tm, D_MODEL), lambda i: (i, 0)),
            gain_spec(2),
            mod_spec(3),
            mod_spec(4),
            _resident((None, D_MODEL, 2 * D_FF), (layer, 0, 0)),
            _resident((None, 3, 2 * D_FF), (layer, 0, 0)),
            _resident((None, 1, 2 * D_FF), (layer, 0, 0)),
            _resident((None, D_FF, D_MODEL), (layer, 0, 0)),
            gain_spec(3),
            mod_spec(5),
        ],
        out_specs=pl.BlockSpec((tm, D_MODEL), lambda i: (i, 0)),
        out_shape=jax.ShapeDtypeStruct((n_tok, D_MODEL), F32),
        scratch_shapes=[
            pltpu.VMEM((tm, D_MODEL), BF16),
            pltpu.VMEM((tm, D_FF), BF16),
            pltpu.VMEM((4, SUBLANES + n_seg * (seg + SUBLANES), tf), F32),
        ],
        compiler_params=_params("parallel"),
        name="convffn",
    )(x, ng4, mod5, mod5, w_up, conv_w, conv_b, w_down, ng4, mod5)


def _rope_tables(seq_len):
    quarter = B_DK // 4
    inv = ROPE_BASE ** (-jnp.arange(quarter, dtype=F32) / quarter)
    t = jnp.arange(seq_len)
    rows = (t // GRID_W).astype(F32)[:, None] * inv
    cols = (t % GRID_W).astype(F32)[:, None] * inv
    cos = jnp.concatenate([jnp.cos(rows), jnp.cos(cols)] * 2, axis=-1)
    sin = jnp.concatenate([-jnp.sin(rows), -jnp.sin(cols), jnp.sin(rows), jnp.sin(cols)], axis=-1)
    return cos, sin


def _rope_qk_weights(w_in):
    n_l = w_in.shape[0]
    quarter = B_DK // 4
    w_qk = w_in[:, :, :B_QK].astype(BF16).reshape(n_l, D_MODEL, 2 * B_HEADS, 2, 2, quarter)
    return jnp.swapaxes(w_qk, 3, 4).reshape(n_l, D_MODEL, B_QK)


def _gate_weights(w_in_j, b_gate_j):
    wg = w_in_j[:, A_MAIN:].reshape(D_MODEL, 4, A_HEADS)
    pad_w = jnp.zeros((D_MODEL, LANES - 2 * A_HEADS), F32)
    pad_b = jnp.zeros((LANES - 2 * A_HEADS,), F32)
    wgi = jnp.concatenate([wg[:, 0], wg[:, 2], pad_w], axis=1).astype(BF16)
    wgf = jnp.concatenate([wg[:, 1], wg[:, 3], pad_w], axis=1).astype(BF16)
    bgi = jnp.concatenate([b_gate_j[0], b_gate_j[2], pad_b])[None, :]
    bgf = jnp.concatenate([b_gate_j[1], b_gate_j[3], pad_b])[None, :]
    return wgi, wgf, bgi, bgf


def kernel(x_prompt, x_sample, state_mlstm_C, state_mlstm_n, state_mlstm_m, state_ret_S, c, c_ctx,
           norm_gain, ada_w, ada_b, ml_w_in, ml_b_gate, ml_norm, ml_w_out,
           ret_w_in, ret_decay, ret_norm, ret_w_out, ffn_w_up, ffn_conv, ffn_conv_b, ffn_w_down):
    bp, tp, _ = x_prompt.shape
    bs, ts, _ = x_sample.shape
    n_a = ml_w_in.shape[0]
    n_b = ret_w_in.shape[0]

    cond = jnp.concatenate([c_ctx[None, :], c, jnp.zeros((MOD_ROWS - 1 - bs, D_MODEL), F32)], axis=0)
    mod5 = _modulation(cond, ada_w, ada_b).reshape(DEPTH, MOD_ROWS, 6, 1, D_MODEL)
    ng4 = norm_gain.reshape(DEPTH, 4, 1, D_MODEL)
    rope = _rope_tables(ts)

    ml_w_main = ml_w_in[:, :, :A_MAIN].astype(BF16)
    ml_w_out_b = ml_w_out.astype(BF16)
    ret_w_in_b = ret_w_in.astype(BF16)
    ret_w_qk_rope = _rope_qk_weights(ret_w_in)
    ret_w_out_b = ret_w_out.astype(BF16)
    ffn_w_up_b = ffn_w_up.astype(BF16)
    ffn_w_down_b = ffn_w_down.astype(BF16)
    ffn_conv_b3 = ffn_conv_b.reshape(DEPTH, 1, 2 * D_FF)
    ml_gn4 = ml_norm.reshape(n_a, A_HEADS, 1, A_DV)
    ret_gn4 = ret_norm.reshape(n_b, B_HEADS, 1, B_DV)
    dec_rep = jnp.broadcast_to(jnp.swapaxes(ret_decay, 1, 2)[..., None], (n_b, B_HEADS, 2, B_DV))
    st_c = state_mlstm_C
    st_n = state_mlstm_n.reshape(bs, n_a, 2, A_HEADS, A_DK, 1)
    st_m = state_mlstm_m.reshape(bs, n_a, 2, A_HEADS, 1, 1)

    groups = [
        dict(x=x_prompt.reshape(bp * tp, D_MODEL), batch=bp, seq_len=tp, sample=False),
        dict(x=x_sample.reshape(bs * ts, D_MODEL), batch=bs, seq_len=ts, sample=True),
    ]
    ml_states = None
    ret_states = None
    for i in range(DEPTH):
        j = i // N_MIXERS
        for grp in groups:
            x = grp["x"]
            geo = dict(seq_len=grp["seq_len"], sample=grp["sample"])
            bt = dict(batch=grp["batch"], seq_len=grp["seq_len"])
            if i % N_MIXERS == 0:
                n_qk = A_HEADS * A_DK // TN_PROJ
                z, gi, gf = _inproj(x, ng4, mod5, i, [(ml_w_main, j, 0)], n_q=n_qk, n_k=n_qk,
                                    k_scale=A_DK ** -0.5, gates=_gate_weights(ml_w_in[j], ml_b_gate[j]), **geo)
                if grp["sample"]:
                    (h,) = _mlstm_scan(z, gi, gf, ml_gn4, j, n_layers=n_a, state=(st_c, st_n, st_m), **bt)
                else:
                    h, *ml_states = _mlstm_scan(z, gi, gf, ml_gn4, j, n_layers=n_a, prev=ml_states, **bt)
                x = _outproj(h, ml_w_out_b, j, x, ng4, mod5, i, **geo)
            else:
                n_qk = B_HEADS * B_DK // TN_PROJ
                w_qk = ret_w_qk_rope if grp["sample"] else ret_w_in_b
                w_parts = [(w_qk, j, 0), (ret_w_in_b, j, 1), (ret_w_in_b, j, 2)]
                (z,) = _inproj(x, ng4, mod5, i, w_parts, n_q=n_qk, n_k=n_qk, k_scale=B_DK ** -0.5,
                               rope=rope if grp["sample"] else None, **geo)
                if grp["sample"]:
                    (h,) = _ret_scan(z, dec_rep, ret_gn4, j, n_layers=n_b, state=state_ret_S, **bt)
                else:
                    h, ret_states = _ret_scan(z, dec_rep, ret_gn4, j, n_layers=n_b, prev=ret_states, **bt)
                x = _outproj(h, ret_w_out_b, j, x, ng4, mod5, i, **geo)
            grp["x"] = _ffn(x, ng4, mod5, i, ffn_w_up_b, ffn_conv, ffn_conv_b3, ffn_w_down_b, **geo)

    y_prompt = groups[0]["x"].reshape(bp, tp, D_MODEL)
    y_sample = groups[1]["x"].reshape(bs, ts, D_MODEL)
    new_c, new_n, new_m = ml_states
    return (y_prompt, y_sample, new_c, new_n.reshape(bp, n_a, 2, A_HEADS, A_DK),
            new_m.reshape(bp, n_a, 2, A_HEADS), ret_states)
```

```python
import functools
import math

import jax
import jax.numpy as jnp
from jax import lax
from jax.experimental import pallas as pl
from jax.experimental.pallas import tpu as pltpu

D_MODEL = 1024
DEPTH = 4
GRID_W = 64
CHUNK = 128
N_MIXERS = 2
A_HEADS = 4
A_DV = D_MODEL // A_HEADS
A_DK = A_DV // 2
A_MAIN = 2 * A_HEADS * A_DK + 2 * A_HEADS * A_DV
B_HEADS = 8
B_DK = D_MODEL // B_HEADS
B_DV = 2 * D_MODEL // B_HEADS
B_QK = 2 * B_HEADS * B_DK
ROPE_BASE = 10000.0
D_FF = ((8 * D_MODEL // 3 + 127) // 128) * 128
EPS = 1e-6
LN2 = math.log(2.0)

F32 = jnp.float32
BF16 = jnp.bfloat16

LANES = 128
SUBLANES = 8
MOD_ROWS = 16
VMEM_LIMIT = 48 * 1024 * 1024

TM_PROJ = 512
TN_PROJ = 512
TM_OUT = 1024
TR_OUT = 512
TM_FFN = 512
TF_FFN = 256
GATE_UNROLL = 16
STATE_UNROLL = 8
OUT_UNROLL = 16
MLSTM_OUT_UNROLL = 8
SCAN_TOKENS = 1024


def _dot(a, b):
    return jnp.dot(a, b, preferred_element_type=F32)


def _dot_nt(a, b):
    return lax.dot_general(a, b, (((1,), (1,)), ((), ())), preferred_element_type=F32)


def _dot_tn(a, b):
    return lax.dot_general(a, b, (((0,), (0,)), ((), ())), preferred_element_type=F32)


def _rms(x):
    return x * lax.rsqrt(jnp.mean(x * x, axis=-1, keepdims=True) + EPS)


def _layer_norm(h):
    d = h - jnp.mean(h, axis=-1, keepdims=True)
    return d * lax.rsqrt(jnp.mean(d * d, axis=-1, keepdims=True) + EPS)


def _params(*sem):
    return pltpu.CompilerParams(dimension_semantics=sem, vmem_limit_bytes=VMEM_LIMIT)


def _resident(shape, index):
    return pl.BlockSpec(shape, lambda *_: index, pipeline_mode=pl.Buffered(1))


def _mod_kernel(cond_ref, w_ref, b_ref, o_ref):
    cnd = cond_ref[...]
    s = cnd * jax.nn.sigmoid(cnd)
    o_ref[...] = _dot(s.astype(BF16), w_ref[...].astype(BF16)) + b_ref[...]


def _modulation(cond, ada_w, ada_b):
    tn = 1024
    n_out = ada_w.shape[-1]
    return pl.pallas_call(
        _mod_kernel,
        grid=(DEPTH, n_out // tn),
        in_specs=[
            pl.BlockSpec((MOD_ROWS, D_MODEL), lambda l, j: (0, 0)),
            pl.BlockSpec((None, D_MODEL, tn), lambda l, j: (l, 0, j)),
            pl.BlockSpec((None, 1, tn), lambda l, j: (l, 0, j)),
        ],
        out_specs=pl.BlockSpec((None, MOD_ROWS, tn), lambda l, j: (l, 0, j)),
        out_shape=jax.ShapeDtypeStruct((DEPTH, MOD_ROWS, n_out), F32),
        compiler_params=_params("parallel", "parallel"),
        name="modulation",
    )(cond, ada_w, ada_b.reshape(DEPTH, 1, n_out))


def _mod_row(sample, seq_len, tm):
    if not sample:
        return lambda i: 0
    tiles_per_seq = seq_len // tm
    return lambda i: 1 + i // tiles_per_seq


def _rope_slab(x, cos, sin):
    return x * cos + pltpu.roll(x, 64, axis=1) * sin


def _inproj_kernel(*refs, n_w, n_q, n_k, k_scale, rope, gates, tn):
    x_ref, g_ref, sh_ref, sc_ref = refs[:4]
    w_refs = refs[4:4 + n_w]
    pos = 4 + n_w
    if gates:
        wg_ref, bg_ref = refs[pos:pos + 2]
        pos += 2
    if rope:
        cos_ref, sin_ref = refs[pos:pos + 2]
        pos += 2
    z_ref = refs[pos]
    pos += 1
    if gates:
        gates_ref = refs[pos]
        pos += 1
    u_sc = refs[pos]

    u = _rms(x_ref[...]) * g_ref[...] * (1.0 + sc_ref[...]) + sh_ref[...]
    u_sc[...] = u.astype(BF16)
    if gates:
        gates_ref[...] = _dot_nt(wg_ref[...], u_sc[...]) + bg_ref[...]

    wb = w_refs[0].shape[1]
    for j in range(z_ref.shape[1] // tn):
        part, off = divmod(j * tn, wb)
        z = _dot(u_sc[...], w_refs[part][:, off:off + tn])
        scale = k_scale if n_q <= j < n_q + n_k else 1.0
        if rope and j < n_q + n_k:
            for s in range(tn // LANES):
                r = _rope_slab(z[:, s * LANES:(s + 1) * LANES], cos_ref[...], sin_ref[...])
                if scale != 1.0:
                    r = r * scale
                z_ref[:, j * tn + s * LANES:j * tn + (s + 1) * LANES] = r.astype(BF16)
        elif scale != 1.0:
            z_ref[:, j * tn:(j + 1) * tn] = (z * scale).astype(BF16)
        else:
            z_ref[:, j * tn:(j + 1) * tn] = z.astype(BF16)


def _inproj(x, ng4, mod5, layer, w_parts, wb, *, seq_len, sample, n_q, n_k, k_scale, rope=None, gates=None):
    n_tok = x.shape[0]
    tm, tn = TM_PROJ, TN_PROJ
    n_col = wb * len(w_parts)
    row = _mod_row(sample, seq_len, tm)
    in_specs = [
        pl.BlockSpec((tm, D_MODEL), lambda i: (i, 0)),
        pl.BlockSpec((None, None, 1, D_MODEL), lambda i: (layer, 0, 0, 0)),
        pl.BlockSpec((None, None, None, 1, D_MODEL), lambda i: (layer, row(i), 0, 0, 0)),
        pl.BlockSpec((None, None, None, 1, D_MODEL), lambda i: (layer, row(i), 1, 0, 0)),
    ]
    in_specs += [_resident((None, D_MODEL, wb), (jl, 0, blk)) for _, jl, blk in w_parts]
    args = [x, ng4, mod5, mod5] + [w for w, _, _ in w_parts]
    out_specs = [pl.BlockSpec((tm, n_col), lambda i: (i, 0))]
    out_shape = [jax.ShapeDtypeStruct((n_tok, n_col), BF16)]
    if gates is not None:
        n_g = gates[0].shape[0]
        in_specs += [_resident((n_g, D_MODEL), (0, 0)), _resident((n_g, 1), (0, 0))]
        args += list(gates)
        out_specs += [pl.BlockSpec((n_g, tm), lambda i: (0, i))]
        out_shape += [jax.ShapeDtypeStruct((n_g, n_tok), F32)]
    if rope is not None:
        tiles_per_seq = seq_len // tm
        in_specs += [pl.BlockSpec((tm, LANES), lambda i: (i % tiles_per_seq, 0))] * 2
        args += list(rope)
    kern = functools.partial(_inproj_kernel, n_w=len(w_parts), n_q=n_q, n_k=n_k, k_scale=k_scale,
                             rope=rope is not None, gates=gates is not None, tn=tn)
    return pl.pallas_call(
        kern,
        grid=(n_tok // tm,),
        in_specs=in_specs,
        out_specs=out_specs,
        out_shape=out_shape,
        scratch_shapes=[pltpu.VMEM((tm, D_MODEL), BF16)],
        compiler_params=_params("parallel"),
        name="inproj",
    )(*args)


def _tri_masks():
    li = lax.broadcasted_iota(jnp.int32, (CHUNK, CHUNK), 0)
    si = lax.broadcasted_iota(jnp.int32, (CHUNK, CHUNK), 1)
    return si <= li, si >= li


def _layer_slot(ref, fresh_slot):
    if fresh_slot is None:
        return ref
    for other in range(ref.shape[1]):
        if other != fresh_slot:
            ref[:, other] = jnp.zeros(ref.shape[:1] + ref.shape[2:], ref.dtype)
    return ref.at[:, fresh_slot]


def _seqs_per_step(batch, seq_len, carry_in):
    if carry_in:
        return 1
    nb = max(1, SCAN_TOKENS // seq_len)
    while batch % nb:
        nb -= 1
    return nb


NG = 2 * A_HEADS


def _split_dot(x, mask_b):
    hi = x.astype(BF16)
    r1 = x - hi.astype(F32)
    mid = r1.astype(BF16)
    lo = (r1 - mid.astype(F32)).astype(BF16)
    return _dot(hi, mask_b) + _dot(mid, mask_b) + _dot(lo, mask_b)


def _mlstm_kernel(*refs, n_chunks, nb, carry_in, carry_out, n_alias, fresh_slot):
    q_ref, k_ref, v_ref, o_ref, g_ref, gn_ref = refs[:6]
    pos = 6
    if carry_in:
        c0_ref, n0_ref, m0_ref = refs[pos:pos + 3]
        pos += 3
    pos += n_alias
    h_ref = refs[pos]
    pos += 1
    if carry_out:
        cout_ref, nout_ref, mout_ref = refs[pos:pos + 3]
        pos += 3
    ab_sc, bt_sc, g_sc, bm_sc, mpf_sc, mpb_sc, c_sc, call_sc = refs[pos:pos + 8]

    head = pl.program_id(1)
    masks = _tri_masks()
    ones_b = jnp.ones((CHUNK, LANES), BF16)
    sum_b = jnp.concatenate([masks[1].astype(BF16), ones_b], axis=1)
    lane = lax.broadcasted_iota(jnp.int32, (CHUNK, LANES), 1)
    grow = lax.broadcasted_iota(jnp.int32, (2 * NG, CHUNK), 0)
    gsub = lax.broadcasted_iota(jnp.int32, (NG, LANES), 0)
    zpad = jnp.zeros((LANES - 2 * NG, CHUNK), F32)
    cols = (head, head + A_HEADS)
    mp_sc = (mpf_sc, mpb_sc)
    t_seq = n_chunks * CHUNK

    def pick_col(x, col):
        return jnp.sum(jnp.where(lane == col, x, 0.0), axis=1, keepdims=True)

    def at(s, c):
        return pl.ds(pl.multiple_of(s * t_seq + c * CHUNK, CHUNK), CHUNK)

    def gate_body(c, carry):
        for s in range(nb):
            idx = s * n_chunks + c
            gates = g_ref[:, at(s, c)]
            lf = jnp.minimum(gates, 0.0) - jnp.log1p(jnp.exp(-jnp.abs(gates)))
            lf = jnp.where(grow >= NG, lf, 0.0)
            sums = _split_dot(lf, sum_b)
            a_f = sums[:, :CHUNK]
            tot = sums[:, CHUNK:]
            a_all = jnp.where(grow < NG + A_HEADS, a_f, tot - a_f + lf)[NG:, :]
            b_all = gates[:NG, :] - a_all
            bt_sc[idx] = b_all
            ab_sc[at(s, c), :] = jnp.concatenate([a_all, b_all, zpad], axis=0).T
            g_sc[idx] = tot[NG:, :]
            bm_sc[idx] = jnp.broadcast_to(jnp.max(b_all, axis=1, keepdims=True), (NG, LANES))
        return carry

    lax.fori_loop(0, n_chunks, gate_body, 0, unroll=min(n_chunks, GATE_UNROLL))

    if carry_in:
        m_init = (jnp.broadcast_to(m0_ref[0], (NG, LANES)), jnp.broadcast_to(m0_ref[1], (NG, LANES)))
    else:
        m_init = (jnp.zeros((NG, LANES), F32),) * (2 * nb)

    def m_body(i, carry):
        out = []
        for s in range(nb):
            m_f, m_b = carry[2 * s], carry[2 * s + 1]
            jf = s * n_chunks + i
            jb = s * n_chunks + n_chunks - 1 - i
            mpf_sc[jf] = m_f
            mpb_sc[jb] = m_b
            out.append(g_sc[jf] + jnp.maximum(m_f, bm_sc[jf]))
            out.append(g_sc[jb] + jnp.maximum(m_b, bm_sc[jb]))
        return tuple(out)

    m_last = lax.fori_loop(0, n_chunks, m_body, m_init)

    if carry_in:
        for dirn in range(2):
            c_sc[dirn, :, :A_DV] = c0_ref[dirn]
            c_sc[dirn, :, A_DV:] = jnp.broadcast_to(n0_ref[dirn], (A_DK, LANES))
    else:
        c_sc[...] = jnp.zeros_like(c_sc)

    def state_body(i, carry):
        for s in range(nb):
            for dirn in range(2):
                c = i if dirn == 0 else n_chunks - 1 - i
                idx = s * n_chunks + c
                mp_row = mp_sc[dirn][idx, pl.ds(cols[dirn], 1), :]
                m_top = jnp.maximum(mp_row, bm_sc[idx, pl.ds(cols[dirn], 1), :])
                ws = jnp.exp(pick_col(ab_sc[at(s, c), :], NG + cols[dirn]) - m_top)
                dec = jnp.exp(mp_row - m_top)
                dec = jnp.concatenate([dec] * (c_sc.shape[2] // LANES), axis=1)
                c_old = c_sc[2 * s + dirn]
                call_sc[2 * s + dirn, c] = c_old.astype(BF16)
                kw = (k_ref[at(s, c), :].astype(F32) * ws).astype(BF16)
                upd = jnp.concatenate([_dot_tn(kw, v_ref[at(s, c), :]), _dot_tn(kw, ones_b)], axis=1)
                c_sc[2 * s + dirn] = dec * c_old + upd
        return carry

    lax.fori_loop(0, n_chunks, state_body, 0, unroll=min(n_chunks, STATE_UNROLL))
    if carry_out:
        outs = [_layer_slot(r, fresh_slot) for r in (cout_ref, nout_ref, mout_ref)]
        for s in range(nb):
            for dirn in range(2):
                outs[0][s, dirn] = c_sc[2 * s + dirn, :, :A_DV]
                outs[1][s, dirn] = c_sc[2 * s + dirn, :, A_DV:].T[0:1, :]
                m_end = jnp.where(gsub == cols[dirn], m_last[2 * s + dirn], 0.0)
                outs[2][s, dirn] = jnp.sum(m_end, axis=0, keepdims=True)[:, 0:1]

    def out_body(c, carry):
        for s in range(nb):
            q = q_ref[at(s, c), :]
            v = v_ref[at(s, c), :]
            s_raw = _dot_nt(q, k_ref[at(s, c), :])
            a_chunk = ab_sc[at(s, c), :]
            idx = s * n_chunks + c
            h = None
            for dirn in range(2):
                col = cols[dirn]
                m_prev = mp_sc[dirn][idx, pl.ds(col, 1), :]
                b_vis = jnp.where(masks[dirn], bt_sc[idx, pl.ds(col, 1), :], -jnp.inf)
                m_row = jnp.maximum(m_prev, jnp.max(b_vis, axis=1, keepdims=True))
                sw = (s_raw * jnp.exp(b_vis - m_row)).astype(BF16)
                w_inter = jnp.exp(m_prev - m_row)
                floor = jnp.exp(-(pick_col(a_chunk, col) + m_row))
                inter = _dot(q, call_sc[2 * s + dirn, c])
                num = _dot(sw, v) + jnp.concatenate([w_inter, w_inter], axis=1) * inter[:, :A_DV]
                den = _dot(sw, ones_b) + w_inter * inter[:, A_DV:]
                r = 1.0 / jnp.maximum(jnp.abs(den), floor)
                hd = num * jnp.concatenate([r, r], axis=1)
                h = hd if h is None else h + hd
            o = o_ref[at(s, c), :].astype(F32)
            h_ref[at(s, c), :] = (_layer_norm(h) * gn_ref[...] * jax.nn.sigmoid(o)).astype(BF16)
        return carry

    lax.fori_loop(0, n_chunks, out_body, 0, unroll=min(n_chunks, max(1, MLSTM_OUT_UNROLL // nb)))


def _mlstm_scan(z, gates, gn4, j, *, batch, seq_len, n_layers, state=None, prev=None):
    n_tok = z.shape[0]
    n_chunks = seq_len // CHUNK
    carry_in = state is not None
    carry_out = not carry_in
    nb = _seqs_per_step(batch, seq_len, carry_in)
    t = nb * seq_len
    in_specs = [
        pl.BlockSpec((t, A_DK), lambda b, h: (b, h)),
        pl.BlockSpec((t, A_DK), lambda b, h: (b, A_HEADS + h)),
        pl.BlockSpec((t, A_DV), lambda b, h: (b, A_HEADS + h)),
        pl.BlockSpec((t, A_DV), lambda b, h: (b, 2 * A_HEADS + h)),
        pl.BlockSpec((2 * NG, t), lambda b, h: (0, b)),
        pl.BlockSpec((None, None, 1, A_DV), lambda b, h: (j, h, 0, 0)),
    ]
    args = [z, z, z, z, gates, gn4]
    aliases = {}
    if carry_in:
        in_specs += [
            pl.BlockSpec((None, None, 2, None, A_DK, A_DV), lambda b, h: (b, j, 0, h, 0, 0)),
            pl.BlockSpec((None, None, 2, None, A_DK, 1), lambda b, h: (b, j, 0, h, 0, 0)),
            pl.BlockSpec((None, None, 2, None, 1, 1), lambda b, h: (b, j, 0, h, 0, 0)),
        ]
        args += list(state)
    out_specs = [pl.BlockSpec((t, A_DV), lambda b, h: (b, h))]
    out_shape = [jax.ShapeDtypeStruct((n_tok, A_HEADS * A_DV), BF16)]
    fresh_slot = j if carry_out and prev is None else None
    if carry_out:
        lay, jb = (n_layers, 0) if prev is None else (None, j)
        out_specs += [
            pl.BlockSpec((nb, lay, 2, None, A_DK, A_DV), lambda b, h: (b, jb, 0, h, 0, 0)),
            pl.BlockSpec((nb, lay, 2, None, 1, A_DK), lambda b, h: (b, jb, 0, h, 0, 0)),
            pl.BlockSpec((nb, lay, 2, None, 1, 1), lambda b, h: (b, jb, 0, h, 0, 0)),
        ]
        out_shape += [
            jax.ShapeDtypeStruct((batch, n_layers, 2, A_HEADS, A_DK, A_DV), F32),
            jax.ShapeDtypeStruct((batch, n_layers, 2, A_HEADS, 1, A_DK), F32),
            jax.ShapeDtypeStruct((batch, n_layers, 2, A_HEADS, 1, 1), F32),
        ]
        if prev is not None:
            aliases = {len(args) + k: 1 + k for k in range(3)}
            in_specs += [pl.BlockSpec(memory_space=pl.ANY)] * 3
            args += list(prev)
    kern = functools.partial(_mlstm_kernel, n_chunks=n_chunks, nb=nb, carry_in=carry_in,
                             carry_out=carry_out, n_alias=len(aliases), fresh_slot=fresh_slot)
    return pl.pallas_call(
        kern,
        grid=(batch // nb, A_HEADS),
        in_specs=in_specs,
        out_specs=out_specs,
        out_shape=out_shape,
        input_output_aliases=aliases,
        scratch_shapes=[
            pltpu.VMEM((t, LANES), F32),
            pltpu.VMEM((nb * n_chunks, NG, CHUNK), F32),
            pltpu.VMEM((nb * n_chunks, NG, LANES), F32),
            pltpu.VMEM((nb * n_chunks, NG, LANES), F32),
            pltpu.VMEM((nb * n_chunks, NG, LANES), F32),
            pltpu.VMEM((nb * n_chunks, NG, LANES), F32),
            pltpu.VMEM((2 * nb, A_DK, A_DV + LANES), F32),
            pltpu.VMEM((2 * nb, n_chunks, A_DK, A_DV + LANES), BF16),
        ],
        compiler_params=_params("parallel", "parallel"),
        name="mlstm_scan",
    )(*args)


def _ret_kernel(*refs, n_chunks, nb, carry_in, carry_out, n_alias, fresh_slot):
    q_ref, k_ref, v_ref, gate_ref, dec_ref, gn_ref = refs[:6]
    pos = 6
    if carry_in:
        s0_ref = refs[pos]
        pos += 1
    pos += n_alias
    h_ref = refs[pos]
    pos += 1
    if carry_out:
        sout_ref = refs[pos]
        pos += 1
    s_sc, sall_sc, dsum_sc, xi_sc, zeta_sc = refs[pos:pos + 5]
    t_seq = n_chunks * CHUNK

    def at(s, c):
        return pl.ds(pl.multiple_of(s * t_seq + c * CHUNK, CHUNK), CHUNK)

    lg = jnp.log1p(-jnp.exp(-dec_ref[...] * LN2))
    lg_f = lg[0:1, :]
    lg_b = lg[1:2, :]

    @pl.when(pl.program_id(1) == 0)
    def _():
        masks = _tri_masks()
        li = lax.broadcasted_iota(jnp.int32, (CHUNK, B_DV), 0).astype(F32)
        si = lax.broadcasted_iota(jnp.int32, (CHUNK, CHUNK), 1).astype(F32)
        lq = li[:, :CHUNK]
        dsum_sc[...] = (
            jnp.where(masks[0], jnp.exp(jnp.where(masks[0], lq - si, 0.0) * lg_f[:, :CHUNK]), 0.0)
            + jnp.where(masks[1], jnp.exp(jnp.where(masks[1], si - lq, 0.0) * lg_b[:, :CHUNK]), 0.0))
        xi_sc[0] = jnp.exp((li + 1.0) * lg_f)
        xi_sc[1] = jnp.exp((CHUNK - li) * lg_b)
        zeta_sc[0] = jnp.exp((CHUNK - 1.0 - lq) * lg_f[:, :CHUNK])
        zeta_sc[1] = jnp.exp(lq * lg_b[:, :CHUNK])

    cdec = (jnp.exp(CHUNK * lg_f), jnp.exp(CHUNK * lg_b))

    if carry_in:
        qr = B_DK // 4
        for dirn in range(2):
            for n, o in enumerate((0, 2, 1, 3)):
                s_sc[dirn, n * qr:(n + 1) * qr, :] = s0_ref[dirn, o * qr:(o + 1) * qr, :]
    else:
        s_sc[...] = jnp.zeros_like(s_sc)

    def state_body(i, carry):
        for s in range(nb):
            for dirn in range(2):
                c = i if dirn == 0 else n_chunks - 1 - i
                s_old = s_sc[2 * s + dirn]
                sall_sc[2 * s + dirn, c] = s_old.astype(BF16)
                kz = (k_ref[at(s, c), :].astype(F32) * zeta_sc[dirn]).astype(BF16)
                s_sc[2 * s + dirn] = cdec[dirn] * s_old + _dot_tn(kz, v_ref[at(s, c), :])
        return carry

    lax.fori_loop(0, n_chunks, state_body, 0, unroll=min(n_chunks, STATE_UNROLL))
    if carry_out:
        s_out = _layer_slot(sout_ref, fresh_slot)
        for s in range(nb):
            for dirn in range(2):
                s_out[s, dirn] = s_sc[2 * s + dirn]

    def out_body(c, carry):
        for s in range(nb):
            q = q_ref[at(s, c), :]
            v = v_ref[at(s, c), :]
            sw = _dot_nt(q, k_ref[at(s, c), :]) * dsum_sc[...]
            h = (_dot(sw.astype(BF16), v) + xi_sc[0] * _dot(q, sall_sc[2 * s, c])
                 + xi_sc[1] * _dot(q, sall_sc[2 * s + 1, c]))
            g = gate_ref[at(s, c), :].astype(F32)
            h_ref[at(s, c), :] = (_layer_norm(h) * gn_ref[...] * (g * jax.nn.sigmoid(g))).astype(BF16)
        return carry

    lax.fori_loop(0, n_chunks, out_body, 0, unroll=min(n_chunks, max(1, OUT_UNROLL // nb)))


def _ret_scan(z, dec_rep, gn4, j, *, batch, seq_len, n_layers, state=None, prev=None):
    n_tok = z.shape[0]
    n_chunks = seq_len // CHUNK
    carry_in = state is not None
    carry_out = not carry_in
    nb = _seqs_per_step(batch, seq_len, carry_in)
    t = nb * seq_len
    in_specs = [
        pl.BlockSpec((t, B_DK), lambda h, b: (b, h)),
        pl.BlockSpec((t, B_DK), lambda h, b: (b, B_HEADS + h)),
        pl.BlockSpec((t, B_DV), lambda h, b: (b, B_HEADS + h)),
        pl.BlockSpec((t, B_DV), lambda h, b: (b, 2 * B_HEADS + h)),
        pl.BlockSpec((None, None, 2, B_DV), lambda h, b: (j, h, 0, 0)),
        pl.BlockSpec((None, None, 1, B_DV), lambda h, b: (j, h, 0, 0)),
    ]
    args = [z, z, z, z, dec_rep, gn4]
    aliases = {}
    if carry_in:
        in_specs += [pl.BlockSpec((None, None, 2, None, B_DK, B_DV), lambda h, b: (b, j, 0, h, 0, 0))]
        args += [state]
    out_specs = [pl.BlockSpec((t, B_DV), lambda h, b: (b, h))]
    out_shape = [jax.ShapeDtypeStruct((n_tok, B_HEADS * B_DV), BF16)]
    fresh_slot = j if carry_out and prev is None else None
    if carry_out:
        lay, jb = (n_layers, 0) if prev is None else (None, j)
        out_specs += [pl.BlockSpec((nb, lay, 2, None, B_DK, B_DV), lambda h, b: (b, jb, 0, h, 0, 0))]
        out_shape += [jax.ShapeDtypeStruct((batch, n_layers, 2, B_HEADS, B_DK, B_DV), F32)]
        if prev is not None:
            aliases = {len(args): 1}
            in_specs += [pl.BlockSpec(memory_space=pl.ANY)]
            args += [prev]
    kern = functools.partial(_ret_kernel, n_chunks=n_chunks, nb=nb, carry_in=carry_in,
                             carry_out=carry_out, n_alias=len(aliases), fresh_slot=fresh_slot)
    return pl.pallas_call(
        kern,
        grid=(B_HEADS, batch // nb),
        in_specs=in_specs,
        out_specs=out_specs,
        out_shape=out_shape,
        input_output_aliases=aliases,
        scratch_shapes=[
            pltpu.VMEM((2 * nb, B_DK, B_DV), F32),
            pltpu.VMEM((2 * nb, n_chunks, B_DK, B_DV), BF16),
            pltpu.VMEM((CHUNK, CHUNK), F32),
            pltpu.VMEM((2, CHUNK, B_DV), F32),
            pltpu.VMEM((2, CHUNK, B_DK), F32),
        ],
        compiler_params=_params("parallel", "arbitrary"),
        name="ret_scan",
    )(*args)


def _outproj_kernel(h_ref, w_ref, x_ref, g_ref, gate_ref, o_ref):
    for r in range(o_ref.shape[0] // TR_OUT):
        rows = slice(r * TR_OUT, (r + 1) * TR_OUT)
        y = _dot(h_ref[rows, :], w_ref[...])
        o_ref[rows, :] = x_ref[rows, :] + gate_ref[...] * (_rms(y) * g_ref[...])


def _outproj(h, w, j, x, ng4, mod5, layer, *, seq_len, sample):
    n_tok, hv = h.shape
    tm = TM_OUT
    row = _mod_row(sample, seq_len, tm)
    return pl.pallas_call(
        _outproj_kernel,
        grid=(n_tok // tm,),
        in_specs=[
            pl.BlockSpec((tm, hv), lambda i: (i, 0)),
            _resident((None, hv, D_MODEL), (j, 0, 0)),
            pl.BlockSpec((tm, D_MODEL), lambda i: (i, 0)),
            pl.BlockSpec((None, None, 1, D_MODEL), lambda i: (layer, 1, 0, 0)),
            pl.BlockSpec((None, None, None, 1, D_MODEL), lambda i: (layer, row(i), 2, 0, 0)),
        ],
        out_specs=pl.BlockSpec((tm, D_MODEL), lambda i: (i, 0)),
        out_shape=jax.ShapeDtypeStruct((n_tok, D_MODEL), F32),
        compiler_params=_params("parallel"),
        name="outproj",
    )(h, w, x, ng4, mod5)


def _conv3(hs_ref, half, h, cw, cb, seg, n_seg):
    for s in range(n_seg):
        base = SUBLANES + s * (seg + SUBLANES)
        h_seg = h[s * seg:(s + 1) * seg, :]
        hs_ref[2 * half, base + 1:base + 1 + seg, :] = h_seg
        hs_ref[2 * half + 1, base - 1:base - 1 + seg, :] = h_seg
    parts = []
    for s in range(n_seg):
        base = SUBLANES + s * (seg + SUBLANES)
        h_prev = hs_ref[2 * half, base:base + seg, :]
        h_next = hs_ref[2 * half + 1, base:base + seg, :]
        h_mid = h[s * seg:(s + 1) * seg, :]
        parts.append(h_prev * cw[0:1, :] + h_mid * cw[1:2, :] + h_next * cw[2:3, :] + cb)
    return parts


def _ffn_kernel(x_ref, g2_ref, sh_ref, sc_ref, wup_ref, cw_ref, cb_ref, wd_ref, g3_ref, gate_ref,
                o_ref, u_sc, act_sc, hs_sc, *, seg, n_seg, tf):
    u = _rms(x_ref[...]) * g2_ref[...] * (1.0 + sc_ref[...]) + sh_ref[...]
    u_sc[...] = u.astype(BF16)
    zero_rows = jnp.zeros((SUBLANES, tf), F32)
    for s in range(n_seg):
        base = SUBLANES + s * (seg + SUBLANES)
        for half in range(2):
            hs_sc[2 * half, base:base + SUBLANES, :] = zero_rows
            hs_sc[2 * half + 1, base + seg - SUBLANES:base + seg, :] = zero_rows

    for cidx in range(D_FF // tf):
        cg = slice(cidx * tf, (cidx + 1) * tf)
        cu = slice(D_FF + cidx * tf, D_FF + (cidx + 1) * tf)
        hg = _conv3(hs_sc, 0, _dot(u_sc[...], wup_ref[:, cg]), cw_ref[:, cg], cb_ref[:, cg], seg, n_seg)
        hu = _conv3(hs_sc, 1, _dot(u_sc[...], wup_ref[:, cu]), cw_ref[:, cu], cb_ref[:, cu], seg, n_seg)
        for s in range(n_seg):
            act = jax.nn.gelu(hg[s], approximate=True) * hu[s]
            act_sc[s * seg:(s + 1) * seg, cg] = act.astype(BF16)

    f = _dot(act_sc[...], wd_ref[...])
    o_ref[...] = x_ref[...] + gate_ref[...] * (_rms(f) * g3_ref[...])


def _ffn(x, ng4, mod5, layer, w_up, conv_w, conv_b, w_down, *, seq_len, sample):
    n_tok = x.shape[0]
    tm, tf = TM_FFN, TF_FFN
    row = _mod_row(sample, seq_len, tm)
    seg = GRID_W if sample else seq_len
    n_seg = tm // seg
    kern = functools.partial(_ffn_kernel, seg=seg, n_seg=n_seg, tf=tf)
    mod_spec = lambda k: pl.BlockSpec((None, None, None, 1, D_MODEL), lambda i: (layer, row(i), k, 0, 0))
    gain_spec = lambda k: pl.BlockSpec((None, None, 1, D_MODEL), lambda i: (layer, k, 0, 0))
    return pl.pallas_call(
        kern,
        grid=(n_tok // tm,),
        in_specs=[
            pl.BlockSpec((tm, D_MODEL), lambda i: (i, 0)),
            gain_spec(2),
            mod_spec(3),
            mod_spec(4),
            _resident((None, D_MODEL, 2 * D_FF), (layer, 0, 0)),
            _resident((None, 3, 2 * D_FF), (layer, 0, 0)),
            _resident((None, 1, 2 * D_FF), (layer, 0, 0)),
            _resident((None, D_FF, D_MODEL), (layer, 0, 0)),
            gain_spec(3),
            mod_spec(5),
        ],
        out_specs=pl.BlockSpec((tm, D_MODEL), lambda i: (i, 0)),
        out_shape=jax.ShapeDtypeStruct((n_tok, D_MODEL), F32),
        scratch_shapes=[
            pltpu.VMEM((tm, D_MODEL), BF16),
            pltpu.VMEM((tm, D_FF), BF16),
            pltpu.VMEM((4, SUBLANES + n_seg * (seg + SUBLANES), tf), F32),
        ],
        compiler_params=_params("parallel"),
        name="convffn",
    )(x, ng4, mod5, mod5, w_up, conv_w, conv_b, w_down, ng4, mod5)


def _rope_tables(seq_len):
    quarter = B_DK // 4
    inv = ROPE_BASE ** (-jnp.arange(quarter, dtype=F32) / quarter)
    t = jnp.arange(seq_len)
    rows = (t // GRID_W).astype(F32)[:, None] * inv
    cols = (t % GRID_W).astype(F32)[:, None] * inv
    cos = jnp.concatenate([jnp.cos(rows), jnp.cos(cols)] * 2, axis=-1)
    sin = jnp.concatenate([-jnp.sin(rows), -jnp.sin(cols), jnp.sin(rows), jnp.sin(cols)], axis=-1)
    return cos, sin


def _rope_qk_weights(w_in):
    n_l = w_in.shape[0]
    quarter = B_DK // 4
    w_qk = w_in[:, :, :B_QK].astype(BF16).reshape(n_l, D_MODEL, 2 * B_HEADS, 2, 2, quarter)
    return jnp.swapaxes(w_qk, 3, 4).reshape(n_l, D_MODEL, B_QK)


def _gate_weights(w_in_j, b_gate_j):
    wg = w_in_j[:, A_MAIN:].reshape(D_MODEL, 4, A_HEADS)
    order = (0, 2, 1, 3)
    wg_t = jnp.concatenate([wg[:, g].T for g in order], axis=0).astype(BF16)
    bg = jnp.concatenate([b_gate_j[g] for g in order])[:, None]
    return wg_t, bg


def kernel(x_prompt, x_sample, state_mlstm_C, state_mlstm_n, state_mlstm_m, state_ret_S, c, c_ctx,
           norm_gain, ada_w, ada_b, ml_w_in, ml_b_gate, ml_norm, ml_w_out,
           ret_w_in, ret_decay, ret_norm, ret_w_out, ffn_w_up, ffn_conv, ffn_conv_b, ffn_w_down):
    bp, tp, _ = x_prompt.shape
    bs, ts, _ = x_sample.shape
    n_a = ml_w_in.shape[0]
    n_b = ret_w_in.shape[0]

    cond = jnp.concatenate([c_ctx[None, :], c, jnp.zeros((MOD_ROWS - 1 - bs, D_MODEL), F32)], axis=0)
    mod5 = _modulation(cond, ada_w, ada_b).reshape(DEPTH, MOD_ROWS, 6, 1, D_MODEL)
    ng4 = norm_gain.reshape(DEPTH, 4, 1, D_MODEL)
    rope = _rope_tables(ts)

    ml_w_in_b = ml_w_in.astype(BF16)
    ml_w_out_b = ml_w_out.astype(BF16)
    ret_w_in_b = ret_w_in.astype(BF16)
    ret_w_qk_rope = _rope_qk_weights(ret_w_in)
    ret_w_out_b = ret_w_out.astype(BF16)
    ffn_w_up_b = ffn_w_up.astype(BF16)
    ffn_w_down_b = ffn_w_down.astype(BF16)
    ffn_conv_b3 = ffn_conv_b.reshape(DEPTH, 1, 2 * D_FF)
    ml_gn4 = ml_norm.reshape(n_a, A_HEADS, 1, A_DV)
    ret_gn4 = ret_norm.reshape(n_b, B_HEADS, 1, B_DV)
    dec_rep = jnp.broadcast_to(jnp.swapaxes(ret_decay, 1, 2)[..., None], (n_b, B_HEADS, 2, B_DV))
    st_c = state_mlstm_C
    st_n = state_mlstm_n.reshape(bs, n_a, 2, A_HEADS, A_DK, 1)
    st_m = state_mlstm_m.reshape(bs, n_a, 2, A_HEADS, 1, 1)

    groups = [
        dict(x=x_prompt.reshape(bp * tp, D_MODEL), batch=bp, seq_len=tp, sample=False),
        dict(x=x_sample.reshape(bs * ts, D_MODEL), batch=bs, seq_len=ts, sample=True),
    ]
    ml_states = None
    ret_states = None
    for i in range(DEPTH):
        j = i // N_MIXERS
        for grp in groups:
            x = grp["x"]
            geo = dict(seq_len=grp["seq_len"], sample=grp["sample"])
            bt = dict(batch=grp["batch"], seq_len=grp["seq_len"])
            if i % N_MIXERS == 0:
                n_qk = A_HEADS * A_DK // TN_PROJ
                z, gates = _inproj(x, ng4, mod5, i, [(ml_w_in_b, j, 0)], A_MAIN, n_q=n_qk, n_k=n_qk,
                                    k_scale=A_DK ** -0.5, gates=_gate_weights(ml_w_in[j], ml_b_gate[j]), **geo)
                if grp["sample"]:
                    (h,) = _mlstm_scan(z, gates, ml_gn4, j, n_layers=n_a, state=(st_c, st_n, st_m), **bt)
                else:
                    h, *ml_states = _mlstm_scan(z, gates, ml_gn4, j, n_layers=n_a, prev=ml_states, **bt)
                x = _outproj(h, ml_w_out_b, j, x, ng4, mod5, i, **geo)
            else:
                n_qk = B_HEADS * B_DK // TN_PROJ
                w_qk = ret_w_qk_rope if grp["sample"] else ret_w_in_b
                w_parts = [(w_qk, j, 0), (ret_w_in_b, j, 1), (ret_w_in_b, j, 2)]
                (z,) = _inproj(x, ng4, mod5, i, w_parts, B_QK, n_q=n_qk, n_k=n_qk, k_scale=B_DK ** -0.5,
                               rope=rope if grp["sample"] else None, **geo)
                if grp["sample"]:
                    (h,) = _ret_scan(z, dec_rep, ret_gn4, j, n_layers=n_b, state=state_ret_S, **bt)
                else:
                    h, ret_states = _ret_scan(z, dec_rep, ret_gn4, j, n_layers=n_b, prev=ret_states, **bt)
                x = _outproj(h, ret_w_out_b, j, x, ng4, mod5, i, **geo)
            grp["x"] = _ffn(x, ng4, mod5, i, ffn_w_up_b, ffn_conv, ffn_conv_b3, ffn_w_down_b, **geo)

    y_prompt = groups[0]["x"].reshape(bp, tp, D_MODEL)
    y_sample = groups[1]["x"].reshape(bs, ts, D_MODEL)
    new_c, new_n, new_m = ml_states
    return (y_prompt, y_sample, new_c, new_n.reshape(bp, n_a, 2, A_HEADS, A_DK),
            new_m.reshape(bp, n_a, 2, A_HEADS), ret_states)
```

```python
import functools
import math

import jax
import jax.numpy as jnp
from jax import lax
from jax.experimental import pallas as pl
from jax.experimental.pallas import tpu as pltpu

D_MODEL = 1024
DEPTH = 4
GRID_W = 64
CHUNK = 128
N_MIXERS = 2
A_HEADS = 4
A_DV = D_MODEL // A_HEADS
A_DK = A_DV // 2
A_MAIN = 2 * A_HEADS * A_DK + 2 * A_HEADS * A_DV
B_HEADS = 8
B_DK = D_MODEL // B_HEADS
B_DV = 2 * D_MODEL // B_HEADS
B_QK = 2 * B_HEADS * B_DK
ROPE_BASE = 10000.0
D_FF = ((8 * D_MODEL // 3 + 127) // 128) * 128
EPS = 1e-6
LN2 = math.log(2.0)

F32 = jnp.float32
BF16 = jnp.bfloat16

LANES = 128
SUBLANES = 8
MOD_ROWS = 16
VMEM_LIMIT = 48 * 1024 * 1024

TM_PROJ = 512
TN_PROJ = 512
TM_OUT = 1024
TR_OUT = 512
TM_FFN = 512
TF_FFN = 256
GATE_UNROLL = 16
STATE_UNROLL = 8
OUT_UNROLL = 16
MLSTM_OUT_UNROLL = 8
SCAN_TOKENS = 1024


def _dot(a, b):
    return jnp.dot(a, b, preferred_element_type=F32)


def _dot_nt(a, b):
    return lax.dot_general(a, b, (((1,), (1,)), ((), ())), preferred_element_type=F32)


def _dot_tn(a, b):
    return lax.dot_general(a, b, (((0,), (0,)), ((), ())), preferred_element_type=F32)


def _rms(x):
    return x * lax.rsqrt(jnp.mean(x * x, axis=-1, keepdims=True) + EPS)


def _layer_norm(h):
    d = h - jnp.mean(h, axis=-1, keepdims=True)
    return d * lax.rsqrt(jnp.mean(d * d, axis=-1, keepdims=True) + EPS)


def _params(*sem):
    return pltpu.CompilerParams(dimension_semantics=sem, vmem_limit_bytes=VMEM_LIMIT)


def _resident(shape, index):
    return pl.BlockSpec(shape, lambda *_: index, pipeline_mode=pl.Buffered(1))


def _mod_kernel(cond_ref, w_ref, b_ref, o_ref):
    cnd = cond_ref[...]
    s = cnd * jax.nn.sigmoid(cnd)
    o_ref[...] = _dot(s.astype(BF16), w_ref[...].astype(BF16)) + b_ref[...]


def _modulation(cond, ada_w, ada_b):
    tn = 1024
    n_out = ada_w.shape[-1]
    return pl.pallas_call(
        _mod_kernel,
        grid=(DEPTH, n_out // tn),
        in_specs=[
            pl.BlockSpec((MOD_ROWS, D_MODEL), lambda l, j: (0, 0)),
            pl.BlockSpec((None, D_MODEL, tn), lambda l, j: (l, 0, j)),
            pl.BlockSpec((None, 1, tn), lambda l, j: (l, 0, j)),
        ],
        out_specs=pl.BlockSpec((None, MOD_ROWS, tn), lambda l, j: (l, 0, j)),
        out_shape=jax.ShapeDtypeStruct((DEPTH, MOD_ROWS, n_out), F32),
        compiler_params=_params("parallel", "parallel"),
        name="modulation",
    )(cond, ada_w, ada_b.reshape(DEPTH, 1, n_out))


def _mod_row(sample, seq_len, tm):
    if not sample:
        return lambda i: 0
    tiles_per_seq = seq_len // tm
    return lambda i: 1 + i // tiles_per_seq


def _rope_slab(x, cos, sin):
    return x * cos + pltpu.roll(x, 64, axis=1) * sin


def _inproj_kernel(*refs, n_w, n_q, n_k, k_scale, rope, gates, tn):
    x_ref, g_ref, sh_ref, sc_ref = refs[:4]
    w_refs = refs[4:4 + n_w]
    pos = 4 + n_w
    if gates:
        wg_ref, bg_ref = refs[pos:pos + 2]
        pos += 2
    if rope:
        cos_ref, sin_ref = refs[pos:pos + 2]
        pos += 2
    z_ref = refs[pos]
    pos += 1
    if gates:
        gates_ref = refs[pos]
        pos += 1
    u_sc = refs[pos]

    u = _rms(x_ref[...]) * g_ref[...] * (1.0 + sc_ref[...]) + sh_ref[...]
    u_sc[...] = u.astype(BF16)
    if gates:
        gates_ref[...] = _dot_nt(wg_ref[...], u_sc[...]) + bg_ref[...]

    wb = w_refs[0].shape[1]
    for j in range(z_ref.shape[1] // tn):
        part, off = divmod(j * tn, wb)
        z = _dot(u_sc[...], w_refs[part][:, off:off + tn])
        scale = k_scale if n_q <= j < n_q + n_k else 1.0
        if rope and j < n_q + n_k:
            for s in range(tn // LANES):
                r = _rope_slab(z[:, s * LANES:(s + 1) * LANES], cos_ref[...], sin_ref[...])
                if scale != 1.0:
                    r = r * scale
                z_ref[:, j * tn + s * LANES:j * tn + (s + 1) * LANES] = r.astype(BF16)
        elif scale != 1.0:
            z_ref[:, j * tn:(j + 1) * tn] = (z * scale).astype(BF16)
        else:
            z_ref[:, j * tn:(j + 1) * tn] = z.astype(BF16)


def _inproj(x, ng4, mod5, layer, w_parts, wb, *, seq_len, sample, n_q, n_k, k_scale, rope=None, gates=None):
    n_tok = x.shape[0]
    tm, tn = TM_PROJ, TN_PROJ
    n_col = wb * len(w_parts)
    row = _mod_row(sample, seq_len, tm)
    in_specs = [
        pl.BlockSpec((tm, D_MODEL), lambda i: (i, 0)),
        pl.BlockSpec((None, None, 1, D_MODEL), lambda i: (layer, 0, 0, 0)),
        pl.BlockSpec((None, None, None, 1, D_MODEL), lambda i: (layer, row(i), 0, 0, 0)),
        pl.BlockSpec((None, None, None, 1, D_MODEL), lambda i: (layer, row(i), 1, 0, 0)),
    ]
    in_specs += [_resident((None, D_MODEL, wb), (jl, 0, blk)) for _, jl, blk in w_parts]
    args = [x, ng4, mod5, mod5] + [w for w, _, _ in w_parts]
    out_specs = [pl.BlockSpec((tm, n_col), lambda i: (i, 0))]
    out_shape = [jax.ShapeDtypeStruct((n_tok, n_col), BF16)]
    if gates is not None:
        n_g = gates[0].shape[0]
        in_specs += [_resident((n_g, D_MODEL), (0, 0)), _resident((n_g, 1), (0, 0))]
        args += list(gates)
        out_specs += [pl.BlockSpec((n_g, tm), lambda i: (0, i))]
        out_shape += [jax.ShapeDtypeStruct((n_g, n_tok), F32)]
    if rope is not None:
        tiles_per_seq = seq_len // tm
        in_specs += [pl.BlockSpec((tm, LANES), lambda i: (i % tiles_per_seq, 0))] * 2
        args += list(rope)
    kern = functools.partial(_inproj_kernel, n_w=len(w_parts), n_q=n_q, n_k=n_k, k_scale=k_scale,
                             rope=rope is not None, gates=gates is not None, tn=tn)
    return pl.pallas_call(
        kern,
        grid=(n_tok // tm,),
        in_specs=in_specs,
        out_specs=out_specs,
        out_shape=out_shape,
        scratch_shapes=[pltpu.VMEM((tm, D_MODEL), BF16)],
        compiler_params=_params("parallel"),
        name="inproj",
    )(*args)


def _tri_masks():
    li = lax.broadcasted_iota(jnp.int32, (CHUNK, CHUNK), 0)
    si = lax.broadcasted_iota(jnp.int32, (CHUNK, CHUNK), 1)
    return si <= li, si >= li


def _layer_slot(ref, fresh_slot):
    if fresh_slot is None:
        return ref
    for other in range(ref.shape[1]):
        if other != fresh_slot:
            ref[:, other] = jnp.zeros(ref.shape[:1] + ref.shape[2:], ref.dtype)
    return ref.at[:, fresh_slot]


def _seqs_per_step(batch, seq_len, carry_in):
    if carry_in:
        return 1
    nb = max(1, SCAN_TOKENS // seq_len)
    while batch % nb:
        nb -= 1
    return nb


NG = 2 * A_HEADS


def _split_dot(x, mask_b):
    hi = x.astype(BF16)
    r1 = x - hi.astype(F32)
    mid = r1.astype(BF16)
    lo = (r1 - mid.astype(F32)).astype(BF16)
    return _dot(hi, mask_b) + _dot(mid, mask_b) + _dot(lo, mask_b)


def _mlstm_kernel(*refs, n_chunks, nb, carry_in, carry_out, n_alias, fresh_slot):
    q_ref, k_ref, v_ref, o_ref, g_ref, gn_ref = refs[:6]
    pos = 6
    if carry_in:
        c0_ref, n0_ref, m0_ref = refs[pos:pos + 3]
        pos += 3
    pos += n_alias
    h_ref = refs[pos]
    pos += 1
    if carry_out:
        cout_ref, nout_ref, mout_ref = refs[pos:pos + 3]
        pos += 3
    ab_sc, bt_sc, g_sc, bm_sc, mpf_sc, mpb_sc, c_sc, call_sc = refs[pos:pos + 8]

    head = pl.program_id(1)
    masks = _tri_masks()
    ones_b = jnp.ones((CHUNK, LANES), BF16)
    sum_b = jnp.concatenate([masks[1].astype(BF16), ones_b], axis=1)
    lane = lax.broadcasted_iota(jnp.int32, (CHUNK, LANES), 1)
    grow = lax.broadcasted_iota(jnp.int32, (2 * NG, CHUNK), 0)
    gsub = lax.broadcasted_iota(jnp.int32, (NG, LANES), 0)
    zpad = jnp.zeros((LANES - 2 * NG, CHUNK), F32)
    cols = (head, head + A_HEADS)
    mp_sc = (mpf_sc, mpb_sc)
    t_seq = n_chunks * CHUNK

    def pick_col(x, col):
        return jnp.sum(jnp.where(lane == col, x, 0.0), axis=1, keepdims=True)

    def at(s, c):
        return pl.ds(pl.multiple_of(s * t_seq + c * CHUNK, CHUNK), CHUNK)

    def gate_body(c, carry):
        for s in range(nb):
            idx = s * n_chunks + c
            gates = g_ref[:, at(s, c)]
            lf = jnp.minimum(gates, 0.0) - jnp.log1p(jnp.exp(-jnp.abs(gates)))
            lf = jnp.where(grow >= NG, lf, 0.0)
            sums = _split_dot(lf, sum_b)
            a_f = sums[:, :CHUNK]
            tot = sums[:, CHUNK:]
            a_all = jnp.where(grow < NG + A_HEADS, a_f, tot - a_f + lf)[NG:, :]
            b_all = gates[:NG, :] - a_all
            bt_sc[idx] = b_all
            ab_sc[at(s, c), :] = jnp.concatenate([a_all, b_all, zpad], axis=0).T
            g_sc[idx] = tot[NG:, :]
            bm_sc[idx] = jnp.broadcast_to(jnp.max(b_all, axis=1, keepdims=True), (NG, LANES))
        return carry

    lax.fori_loop(0, n_chunks, gate_body, 0, unroll=min(n_chunks, GATE_UNROLL))

    if carry_in:
        m_init = (jnp.broadcast_to(m0_ref[0], (NG, LANES)), jnp.broadcast_to(m0_ref[1], (NG, LANES)))
    else:
        m_init = (jnp.zeros((NG, LANES), F32),) * (2 * nb)

    def m_body(i, carry):
        out = []
        for s in range(nb):
            m_f, m_b = carry[2 * s], carry[2 * s + 1]
            jf = s * n_chunks + i
            jb = s * n_chunks + n_chunks - 1 - i
            mpf_sc[jf] = m_f
            mpb_sc[jb] = m_b
            out.append(g_sc[jf] + jnp.maximum(m_f, bm_sc[jf]))
            out.append(g_sc[jb] + jnp.maximum(m_b, bm_sc[jb]))
        return tuple(out)

    m_last = lax.fori_loop(0, n_chunks, m_body, m_init)

    if carry_in:
        for dirn in range(2):
            c_sc[dirn, :, :A_DV] = c0_ref[dirn]
            c_sc[dirn, :, A_DV:] = jnp.broadcast_to(n0_ref[dirn], (A_DK, LANES))
    else:
        c_sc[...] = jnp.zeros_like(c_sc)

    def state_body(i, carry):
        for s in range(nb):
            for dirn in range(2):
                c = i if dirn == 0 else n_chunks - 1 - i
                idx = s * n_chunks + c
                mp_row = mp_sc[dirn][idx, pl.ds(cols[dirn], 1), :]
                m_top = jnp.maximum(mp_row, bm_sc[idx, pl.ds(cols[dirn], 1), :])
                ws = jnp.exp(pick_col(ab_sc[at(s, c), :], NG + cols[dirn]) - m_top)
                dec = jnp.exp(mp_row - m_top)
                dec = jnp.concatenate([dec] * (c_sc.shape[2] // LANES), axis=1)
                c_old = c_sc[2 * s + dirn]
                call_sc[2 * s + dirn, c] = c_old.astype(BF16)
                kw = (k_ref[at(s, c), :].astype(F32) * ws).astype(BF16)
                upd = jnp.concatenate([_dot_tn(kw, v_ref[at(s, c), :]), _dot_tn(kw, ones_b)], axis=1)
                c_sc[2 * s + dirn] = dec * c_old + upd
        return carry

    lax.fori_loop(0, n_chunks, state_body, 0, unroll=min(n_chunks, STATE_UNROLL))
    if carry_out:
        outs = [_layer_slot(r, fresh_slot) for r in (cout_ref, nout_ref, mout_ref)]
        for s in range(nb):
            for dirn in range(2):
                outs[0][s, dirn] = c_sc[2 * s + dirn, :, :A_DV]
                outs[1][s, dirn] = c_sc[2 * s + dirn, :, A_DV:].T[0:1, :]
                m_end = jnp.where(gsub == cols[dirn], m_last[2 * s + dirn], 0.0)
                outs[2][s, dirn] = jnp.sum(m_end, axis=0, keepdims=True)[:, 0:1]

    def out_body(c, carry):
        for s in range(nb):
            q = q_ref[at(s, c), :]
            qf = q.astype(F32)
            v_ext = jnp.concatenate([v_ref[at(s, c), :], ones_b], axis=1)
            s_raw = _dot_nt(q, k_ref[at(s, c), :])
            a_chunk = ab_sc[at(s, c), :]
            idx = s * n_chunks + c
            h = None
            for dirn in range(2):
                col = cols[dirn]
                m_prev = mp_sc[dirn][idx, pl.ds(col, 1), :]
                b_vis = jnp.where(masks[dirn], bt_sc[idx, pl.ds(col, 1), :], -jnp.inf)
                m_row = jnp.maximum(m_prev, jnp.max(b_vis, axis=1, keepdims=True))
                sw = (s_raw * jnp.exp(b_vis - m_row)).astype(BF16)
                w_inter = jnp.exp(m_prev - m_row)
                floor = jnp.exp(-(pick_col(a_chunk, col) + m_row))
                qw = (qf * w_inter).astype(BF16)
                nd = _dot(sw, v_ext) + _dot(qw, call_sc[2 * s + dirn, c])
                r = 1.0 / jnp.maximum(jnp.abs(nd[:, A_DV:]), floor)
                hd = nd[:, :A_DV] * jnp.concatenate([r, r], axis=1)
                h = hd if h is None else h + hd
            o = o_ref[at(s, c), :].astype(F32)
            h_ref[at(s, c), :] = (_layer_norm(h) * gn_ref[...] * jax.nn.sigmoid(o)).astype(BF16)
        return carry

    lax.fori_loop(0, n_chunks, out_body, 0, unroll=min(n_chunks, max(1, MLSTM_OUT_UNROLL // nb)))


def _mlstm_scan(z, gates, gn4, j, *, batch, seq_len, n_layers, state=None, prev=None):
    n_tok = z.shape[0]
    n_chunks = seq_len // CHUNK
    carry_in = state is not None
    carry_out = not carry_in
    nb = _seqs_per_step(batch, seq_len, carry_in)
    t = nb * seq_len
    in_specs = [
        pl.BlockSpec((t, A_DK), lambda b, h: (b, h)),
        pl.BlockSpec((t, A_DK), lambda b, h: (b, A_HEADS + h)),
        pl.BlockSpec((t, A_DV), lambda b, h: (b, A_HEADS + h)),
        pl.BlockSpec((t, A_DV), lambda b, h: (b, 2 * A_HEADS + h)),
        pl.BlockSpec((2 * NG, t), lambda b, h: (0, b)),
        pl.BlockSpec((None, None, 1, A_DV), lambda b, h: (j, h, 0, 0)),
    ]
    args = [z, z, z, z, gates, gn4]
    aliases = {}
    if carry_in:
        in_specs += [
            pl.BlockSpec((None, None, 2, None, A_DK, A_DV), lambda b, h: (b, j, 0, h, 0, 0)),
            pl.BlockSpec((None, None, 2, None, A_DK, 1), lambda b, h: (b, j, 0, h, 0, 0)),
            pl.BlockSpec((None, None, 2, None, 1, 1), lambda b, h: (b, j, 0, h, 0, 0)),
        ]
        args += list(state)
    out_specs = [pl.BlockSpec((t, A_DV), lambda b, h: (b, h))]
    out_shape = [jax.ShapeDtypeStruct((n_tok, A_HEADS * A_DV), BF16)]
    fresh_slot = j if carry_out and prev is None else None
    if carry_out:
        lay, jb = (n_layers, 0) if prev is None else (None, j)
        out_specs += [
            pl.BlockSpec((nb, lay, 2, None, A_DK, A_DV), lambda b, h: (b, jb, 0, h, 0, 0)),
            pl.BlockSpec((nb, lay, 2, None, 1, A_DK), lambda b, h: (b, jb, 0, h, 0, 0)),
            pl.BlockSpec((nb, lay, 2, None, 1, 1), lambda b, h: (b, jb, 0, h, 0, 0)),
        ]
        out_shape += [
            jax.ShapeDtypeStruct((batch, n_layers, 2, A_HEADS, A_DK, A_DV), F32),
            jax.ShapeDtypeStruct((batch, n_layers, 2, A_HEADS, 1, A_DK), F32),
            jax.ShapeDtypeStruct((batch, n_layers, 2, A_HEADS, 1, 1), F32),
        ]
        if prev is not None:
            aliases = {len(args) + k: 1 + k for k in range(3)}
            in_specs += [pl.BlockSpec(memory_space=pl.ANY)] * 3
            args += list(prev)
    kern = functools.partial(_mlstm_kernel, n_chunks=n_chunks, nb=nb, carry_in=carry_in,
                             carry_out=carry_out, n_alias=len(aliases), fresh_slot=fresh_slot)
    return pl.pallas_call(
        kern,
        grid=(batch // nb, A_HEADS),
        in_specs=in_specs,
        out_specs=out_specs,
        out_shape=out_shape,
        input_output_aliases=aliases,
        scratch_shapes=[
            pltpu.VMEM((t, LANES), F32),
            pltpu.VMEM((nb * n_chunks, NG, CHUNK), F32),
            pltpu.VMEM((nb * n_chunks, NG, LANES), F32),
            pltpu.VMEM((nb * n_chunks, NG, LANES), F32),
            pltpu.VMEM((nb * n_chunks, NG, LANES), F32),
            pltpu.VMEM((nb * n_chunks, NG, LANES), F32),
            pltpu.VMEM((2 * nb, A_DK, A_DV + LANES), F32),
            pltpu.VMEM((2 * nb, n_chunks, A_DK, A_DV + LANES), BF16),
        ],
        compiler_params=_params("parallel", "parallel"),
        name="mlstm_scan",
    )(*args)


def _ret_kernel(*refs, n_chunks, nb, carry_in, carry_out, n_alias, fresh_slot):
    q_ref, k_ref, v_ref, gate_ref, dec_ref, gn_ref = refs[:6]
    pos = 6
    if carry_in:
        s0_ref = refs[pos]
        pos += 1
    pos += n_alias
    h_ref = refs[pos]
    pos += 1
    if carry_out:
        sout_ref = refs[pos]
        pos += 1
    s_sc, sall_sc, dsum_sc, xi_sc, zeta_sc = refs[pos:pos + 5]
    t_seq = n_chunks * CHUNK

    def at(s, c):
        return pl.ds(pl.multiple_of(s * t_seq + c * CHUNK, CHUNK), CHUNK)

    lg = jnp.log1p(-jnp.exp(-dec_ref[...] * LN2))
    lg_f = lg[0:1, :]
    lg_b = lg[1:2, :]

    @pl.when(pl.program_id(1) == 0)
    def _():
        masks = _tri_masks()
        li = lax.broadcasted_iota(jnp.int32, (CHUNK, B_DV), 0).astype(F32)
        si = lax.broadcasted_iota(jnp.int32, (CHUNK, CHUNK), 1).astype(F32)
        lq = li[:, :CHUNK]
        dsum_sc[...] = (
            jnp.where(masks[0], jnp.exp(jnp.where(masks[0], lq - si, 0.0) * lg_f[:, :CHUNK]), 0.0)
            + jnp.where(masks[1], jnp.exp(jnp.where(masks[1], si - lq, 0.0) * lg_b[:, :CHUNK]), 0.0))
        xi_sc[0] = jnp.exp((lq + 1.0) * lg_f[:, :CHUNK])
        xi_sc[1] = jnp.exp((CHUNK - lq) * lg_b[:, :CHUNK])
        zeta_sc[0] = jnp.exp((CHUNK - 1.0 - lq) * lg_f[:, :CHUNK])
        zeta_sc[1] = jnp.exp(lq * lg_b[:, :CHUNK])

    cdec = (jnp.exp(CHUNK * lg_f), jnp.exp(CHUNK * lg_b))

    if carry_in:
        qr = B_DK // 4
        for dirn in range(2):
            for n, o in enumerate((0, 2, 1, 3)):
                s_sc[dirn, n * qr:(n + 1) * qr, :] = s0_ref[dirn, o * qr:(o + 1) * qr, :]
    else:
        s_sc[...] = jnp.zeros_like(s_sc)

    def state_body(i, carry):
        for s in range(nb):
            for dirn in range(2):
                c = i if dirn == 0 else n_chunks - 1 - i
                s_old = s_sc[2 * s + dirn]
                sall_sc[s, c, dirn * B_DK:(dirn + 1) * B_DK, :] = s_old.astype(BF16)
                kz = (k_ref[at(s, c), :].astype(F32) * zeta_sc[dirn]).astype(BF16)
                s_sc[2 * s + dirn] = cdec[dirn] * s_old + _dot_tn(kz, v_ref[at(s, c), :])
        return carry

    lax.fori_loop(0, n_chunks, state_body, 0, unroll=min(n_chunks, STATE_UNROLL))
    if carry_out:
        s_out = _layer_slot(sout_ref, fresh_slot)
        for s in range(nb):
            for dirn in range(2):
                s_out[s, dirn] = s_sc[2 * s + dirn]

    def out_body(c, carry):
        for s in range(nb):
            q = q_ref[at(s, c), :]
            v = v_ref[at(s, c), :]
            sw = _dot_nt(q, k_ref[at(s, c), :]) * dsum_sc[...]
            qf = q.astype(F32)
            qx = jnp.concatenate([qf * xi_sc[0], qf * xi_sc[1]], axis=1).astype(BF16)
            h = _dot(sw.astype(BF16), v) + _dot(qx, sall_sc[s, c])
            g = gate_ref[at(s, c), :].astype(F32)
            h_ref[at(s, c), :] = (_layer_norm(h) * gn_ref[...] * (g * jax.nn.sigmoid(g))).astype(BF16)
        return carry

    lax.fori_loop(0, n_chunks, out_body, 0, unroll=min(n_chunks, max(1, OUT_UNROLL // nb)))


def _ret_scan(z, dec_rep, gn4, j, *, batch, seq_len, n_layers, state=None, prev=None):
    n_tok = z.shape[0]
    n_chunks = seq_len // CHUNK
    carry_in = state is not None
    carry_out = not carry_in
    nb = _seqs_per_step(batch, seq_len, carry_in)
    t = nb * seq_len
    in_specs = [
        pl.BlockSpec((t, B_DK), lambda h, b: (b, h)),
        pl.BlockSpec((t, B_DK), lambda h, b: (b, B_HEADS + h)),
        pl.BlockSpec((t, B_DV), lambda h, b: (b, B_HEADS + h)),
        pl.BlockSpec((t, B_DV), lambda h, b: (b, 2 * B_HEADS + h)),
        pl.BlockSpec((None, None, 2, B_DV), lambda h, b: (j, h, 0, 0)),
        pl.BlockSpec((None, None, 1, B_DV), lambda h, b: (j, h, 0, 0)),
    ]
    args = [z, z, z, z, dec_rep, gn4]
    aliases = {}
    if carry_in:
        in_specs += [pl.BlockSpec((None, None, 2, None, B_DK, B_DV), lambda h, b: (b, j, 0, h, 0, 0))]
        args += [state]
    out_specs = [pl.BlockSpec((t, B_DV), lambda h, b: (b, h))]
    out_shape = [jax.ShapeDtypeStruct((n_tok, B_HEADS * B_DV), BF16)]
    fresh_slot = j if carry_out and prev is None else None
    if carry_out:
        lay, jb = (n_layers, 0) if prev is None else (None, j)
        out_specs += [pl.BlockSpec((nb, lay, 2, None, B_DK, B_DV), lambda h, b: (b, jb, 0, h, 0, 0))]
        out_shape += [jax.ShapeDtypeStruct((batch, n_layers, 2, B_HEADS, B_DK, B_DV), F32)]
        if prev is not None:
            aliases = {len(args): 1}
            in_specs += [pl.BlockSpec(memory_space=pl.ANY)]
            args += [prev]
    kern = functools.partial(_ret_kernel, n_chunks=n_chunks, nb=nb, carry_in=carry_in,
                             carry_out=carry_out, n_alias=len(aliases), fresh_slot=fresh_slot)
    return pl.pallas_call(
        kern,
        grid=(B_HEADS, batch // nb),
        in_specs=in_specs,
        out_specs=out_specs,
        out_shape=out_shape,
        input_output_aliases=aliases,
        scratch_shapes=[
            pltpu.VMEM((2 * nb, B_DK, B_DV), F32),
            pltpu.VMEM((nb, n_chunks, 2 * B_DK, B_DV), BF16),
            pltpu.VMEM((CHUNK, CHUNK), F32),
            pltpu.VMEM((2, CHUNK, B_DK), F32),
            pltpu.VMEM((2, CHUNK, B_DK), F32),
        ],
        compiler_params=_params("parallel", "arbitrary"),
        name="ret_scan",
    )(*args)


def _outproj_kernel(h_ref, w_ref, x_ref, g_ref, gate_ref, o_ref):
    for r in range(o_ref.shape[0] // TR_OUT):
        rows = slice(r * TR_OUT, (r + 1) * TR_OUT)
        y = _dot(h_ref[rows, :], w_ref[...])
        o_ref[rows, :] = x_ref[rows, :] + gate_ref[...] * (_rms(y) * g_ref[...])


def _outproj(h, w, j, x, ng4, mod5, layer, *, seq_len, sample):
    n_tok, hv = h.shape
    tm = TM_OUT
    row = _mod_row(sample, seq_len, tm)
    return pl.pallas_call(
        _outproj_kernel,
        grid=(n_tok // tm,),
        in_specs=[
            pl.BlockSpec((tm, hv), lambda i: (i, 0)),
            _resident((None, hv, D_MODEL), (j, 0, 0)),
            pl.BlockSpec((tm, D_MODEL), lambda i: (i, 0)),
            pl.BlockSpec((None, None, 1, D_MODEL), lambda i: (layer, 1, 0, 0)),
            pl.BlockSpec((None, None, None, 1, D_MODEL), lambda i: (layer, row(i), 2, 0, 0)),
        ],
        out_specs=pl.BlockSpec((tm, D_MODEL), lambda i: (i, 0)),
        out_shape=jax.ShapeDtypeStruct((n_tok, D_MODEL), F32),
        compiler_params=_params("parallel"),
        name="outproj",
    )(h, w, x, ng4, mod5)


def _conv3(hs_ref, half, h, cw, cb, seg, n_seg):
    for s in range(n_seg):
        base = SUBLANES + s * (seg + SUBLANES)
        h_seg = h[s * seg:(s + 1) * seg, :]
        hs_ref[2 * half, base + 1:base + 1 + seg, :] = h_seg
        hs_ref[2 * half + 1, base - 1:base - 1 + seg, :] = h_seg
    parts = []
    for s in range(n_seg):
        base = SUBLANES + s * (seg + SUBLANES)
        h_prev = hs_ref[2 * half, base:base + seg, :]
        h_next = hs_ref[2 * half + 1, base:base + seg, :]
        h_mid = h[s * seg:(s + 1) * seg, :]
        parts.append(h_prev * cw[0:1, :] + h_mid * cw[1:2, :] + h_next * cw[2:3, :] + cb)
    return parts


def _ffn_kernel(x_ref, g2_ref, sh_ref, sc_ref, wup_ref, cw_ref, cb_ref, wd_ref, g3_ref, gate_ref,
                o_ref, u_sc, act_sc, hs_sc, *, seg, n_seg, tf):
    u = _rms(x_ref[...]) * g2_ref[...] * (1.0 + sc_ref[...]) + sh_ref[...]
    u_sc[...] = u.astype(BF16)
    zero_rows = jnp.zeros((SUBLANES, tf), F32)
    for s in range(n_seg):
        base = SUBLANES + s * (seg + SUBLANES)
        for half in range(2):
            hs_sc[2 * half, base:base + SUBLANES, :] = zero_rows
            hs_sc[2 * half + 1, base + seg - SUBLANES:base + seg, :] = zero_rows

    for cidx in range(D_FF // tf):
        cg = slice(cidx * tf, (cidx + 1) * tf)
        cu = slice(D_FF + cidx * tf, D_FF + (cidx + 1) * tf)
        hg = _conv3(hs_sc, 0, _dot(u_sc[...], wup_ref[:, cg]), cw_ref[:, cg], cb_ref[:, cg], seg, n_seg)
        hu = _conv3(hs_sc, 1, _dot(u_sc[...], wup_ref[:, cu]), cw_ref[:, cu], cb_ref[:, cu], seg, n_seg)
        for s in range(n_seg):
            act = jax.nn.gelu(hg[s], approximate=True) * hu[s]
            act_sc[s * seg:(s + 1) * seg, cg] = act.astype(BF16)

    f = _dot(act_sc[...], wd_ref[...])
    o_ref[...] = x_ref[...] + gate_ref[...] * (_rms(f) * g3_ref[...])


def _ffn(x, ng4, mod5, layer, w_up, conv_w, conv_b, w_down, *, seq_len, sample):
    n_tok = x.shape[0]
    tm, tf = TM_FFN, TF_FFN
    row = _mod_row(sample, seq_len, tm)
    seg = GRID_W if sample else seq_len
    n_seg = tm // seg
    kern = functools.partial(_ffn_kernel, seg=seg, n_seg=n_seg, tf=tf)
    mod_spec = lambda k: pl.BlockSpec((None, None, None, 1, D_MODEL), lambda i: (layer, row(i), k, 0, 0))
    gain_spec = lambda k: pl.BlockSpec((None, None, 1, D_MODEL), lambda i: (layer, k, 0, 0))
    return pl.pallas_call(
        kern,
        grid=(n_tok // tm,),
        in_specs=[
            pl.BlockSpec((tm, D_MODEL), lambda i: (i, 0)),
            gain_spec(2),
            mod_spec(3),
            mod_spec(4),
            _resident((None, D_MODEL, 2 * D_FF), (layer, 0, 0)),
            _resident((None, 3, 2 * D_FF), (layer, 0, 0)),
            _resident((None, 1, 2 * D_FF), (layer, 0, 0)),
            _resident((None, D_FF, D_MODEL), (layer, 0, 0)),
            gain_spec(3),
            mod_spec(5),
        ],
        out_specs=pl.BlockSpec((tm, D_MODEL), lambda i: (i, 0)),
        out_shape=jax.ShapeDtypeStruct((n_tok, D_MODEL), F32),
        scratch_shapes=[
            pltpu.VMEM((tm, D_MODEL), BF16),
            pltpu.VMEM((tm, D_FF), BF16),
            pltpu.VMEM((4, SUBLANES + n_seg * (seg + SUBLANES), tf), F32),
        ],
        compiler_params=_params("parallel"),
        name="convffn",
    )(x, ng4, mod5, mod5, w_up, conv_w, conv_b, w_down, ng4, mod5)


def _rope_tables(seq_len):
    quarter = B_DK // 4
    inv = ROPE_BASE ** (-jnp.arange(quarter, dtype=F32) / quarter)
    t = jnp.arange(seq_len)
    rows = (t // GRID_W).astype(F32)[:, None] * inv
    cols = (t % GRID_W).astype(F32)[:, None] * inv
    cos = jnp.concatenate([jnp.cos(rows), jnp.cos(cols)] * 2, axis=-1)
    sin = jnp.concatenate([-jnp.sin(rows), -jnp.sin(cols), jnp.sin(rows), jnp.sin(cols)], axis=-1)
    return cos, sin


def _rope_qk_weights(w_in):
    n_l = w_in.shape[0]
    quarter = B_DK // 4
    w_qk = w_in[:, :, :B_QK].astype(BF16).reshape(n_l, D_MODEL, 2 * B_HEADS, 2, 2, quarter)
    return jnp.swapaxes(w_qk, 3, 4).reshape(n_l, D_MODEL, B_QK)


def _gate_weights(w_in_j, b_gate_j):
    wg = w_in_j[:, A_MAIN:].reshape(D_MODEL, 4, A_HEADS)
    order = (0, 2, 1, 3)
    wg_t = jnp.concatenate([wg[:, g].T for g in order], axis=0).astype(BF16)
    bg = jnp.concatenate([b_gate_j[g] for g in order])[:, None]
    return wg_t, bg


def kernel(x_prompt, x_sample, state_mlstm_C, state_mlstm_n, state_mlstm_m, state_ret_S, c, c_ctx,
           norm_gain, ada_w, ada_b, ml_w_in, ml_b_gate, ml_norm, ml_w_out,
           ret_w_in, ret_decay, ret_norm, ret_w_out, ffn_w_up, ffn_conv, ffn_conv_b, ffn_w_down):
    bp, tp, _ = x_prompt.shape
    bs, ts, _ = x_sample.shape
    n_a = ml_w_in.shape[0]
    n_b = ret_w_in.shape[0]

    cond = jnp.concatenate([c_ctx[None, :], c, jnp.zeros((MOD_ROWS - 1 - bs, D_MODEL), F32)], axis=0)
    mod5 = _modulation(cond, ada_w, ada_b).reshape(DEPTH, MOD_ROWS, 6, 1, D_MODEL)
    ng4 = norm_gain.reshape(DEPTH, 4, 1, D_MODEL)
    rope = _rope_tables(ts)

    ml_w_in_b = ml_w_in.astype(BF16)
    ml_w_out_b = ml_w_out.astype(BF16)
    ret_w_in_b = ret_w_in.astype(BF16)
    ret_w_qk_rope = _rope_qk_weights(ret_w_in)
    ret_w_out_b = ret_w_out.astype(BF16)
    ffn_w_up_b = ffn_w_up.astype(BF16)
    ffn_w_down_b = ffn_w_down.astype(BF16)
    ffn_conv_b3 = ffn_conv_b.reshape(DEPTH, 1, 2 * D_FF)
    ml_gn4 = ml_norm.reshape(n_a, A_HEADS, 1, A_DV)
    ret_gn4 = ret_norm.reshape(n_b, B_HEADS, 1, B_DV)
    dec_rep = jnp.broadcast_to(jnp.swapaxes(ret_decay, 1, 2)[..., None], (n_b, B_HEADS, 2, B_DV))
    st_c = state_mlstm_C
    st_n = state_mlstm_n.reshape(bs, n_a, 2, A_HEADS, A_DK, 1)
    st_m = state_mlstm_m.reshape(bs, n_a, 2, A_HEADS, 1, 1)

    groups = [
        dict(x=x_prompt.reshape(bp * tp, D_MODEL), batch=bp, seq_len=tp, sample=False),
        dict(x=x_sample.reshape(bs * ts, D_MODEL), batch=bs, seq_len=ts, sample=True),
    ]
    ml_states = None
    ret_states = None
    for i in range(DEPTH):
        j = i // N_MIXERS
        for grp in groups:
            x = grp["x"]
            geo = dict(seq_len=grp["seq_len"], sample=grp["sample"])
            bt = dict(batch=grp["batch"], seq_len=grp["seq_len"])
            if i % N_MIXERS == 0:
                n_qk = A_HEADS * A_DK // TN_PROJ
                z, gates = _inproj(x, ng4, mod5, i, [(ml_w_in_b, j, 0)], A_MAIN, n_q=n_qk, n_k=n_qk,
                                    k_scale=A_DK ** -0.5, gates=_gate_weights(ml_w_in[j], ml_b_gate[j]), **geo)
                if grp["sample"]:
                    (h,) = _mlstm_scan(z, gates, ml_gn4, j, n_layers=n_a, state=(st_c, st_n, st_m), **bt)
                else:
                    h, *ml_states = _mlstm_scan(z, gates, ml_gn4, j, n_layers=n_a, prev=ml_states, **bt)
                x = _outproj(h, ml_w_out_b, j, x, ng4, mod5, i, **geo)
            else:
                n_qk = B_HEADS * B_DK // TN_PROJ
                w_qk = ret_w_qk_rope if grp["sample"] else ret_w_in_b
                w_parts = [(w_qk, j, 0), (ret_w_in_b, j, 1), (ret_w_in_b, j, 2)]
                (z,) = _inproj(x, ng4, mod5, i, w_parts, B_QK, n_q=n_qk, n_k=n_qk, k_scale=B_DK ** -0.5,
                               rope=rope if grp["sample"] else None, **geo)
                if grp["sample"]:
                    (h,) = _ret_scan(z, dec_rep, ret_gn4, j, n_layers=n_b, state=state_ret_S, **bt)
                else:
                    h, ret_states = _ret_scan(z, dec_rep, ret_gn4, j, n_layers=n_b, prev=ret_states, **bt)
                x = _outproj(h, ret_w_out_b, j, x, ng4, mod5, i, **geo)
            grp["x"] = _ffn(x, ng4, mod5, i, ffn_w_up_b, ffn_conv, ffn_conv_b3, ffn_w_down_b, **geo)

    y_prompt = groups[0]["x"].reshape(bp, tp, D_MODEL)
    y_sample = groups[1]["x"].reshape(bs, ts, D_MODEL)
    new_c, new_n, new_m = ml_states
    return (y_prompt, y_sample, new_c, new_n.reshape(bp, n_a, 2, A_HEADS, A_DK),
            new_m.reshape(bp, n_a, 2, A_HEADS), ret_states)
```

```python
import functools
import math

import jax
import jax.numpy as jnp
from jax import lax
from jax.experimental import pallas as pl
from jax.experimental.pallas import tpu as pltpu

D_MODEL = 1024
DEPTH = 4
GRID_W = 64
CHUNK = 128
N_MIXERS = 2
A_HEADS = 4
A_DV = D_MODEL // A_HEADS
A_DK = A_DV // 2
A_MAIN = 2 * A_HEADS * A_DK + 2 * A_HEADS * A_DV
B_HEADS = 8
B_DK = D_MODEL // B_HEADS
B_DV = 2 * D_MODEL // B_HEADS
B_QK = 2 * B_HEADS * B_DK
ROPE_BASE = 10000.0
D_FF = ((8 * D_MODEL // 3 + 127) // 128) * 128
EPS = 1e-6
LN2 = math.log(2.0)

F32 = jnp.float32
BF16 = jnp.bfloat16

LANES = 128
SUBLANES = 8
MOD_ROWS = 16
VMEM_LIMIT = 48 * 1024 * 1024

TM_PROJ = 512
TN_PROJ = 512
TM_OUT = 1024
TR_OUT = 512
TM_FFN = 512
TF_FFN = 256
GATE_UNROLL = 16
STATE_UNROLL = 8
OUT_UNROLL = 16
MLSTM_OUT_UNROLL = 8
SCAN_TOKENS = 2048


def _dot(a, b):
    return jnp.dot(a, b, preferred_element_type=F32)


def _dot_nt(a, b):
    return lax.dot_general(a, b, (((1,), (1,)), ((), ())), preferred_element_type=F32)


def _dot_tn(a, b):
    return lax.dot_general(a, b, (((0,), (0,)), ((), ())), preferred_element_type=F32)


def _rms(x):
    return x * lax.rsqrt(jnp.mean(x * x, axis=-1, keepdims=True) + EPS)


def _layer_norm(h):
    d = h - jnp.mean(h, axis=-1, keepdims=True)
    return d * lax.rsqrt(jnp.mean(d * d, axis=-1, keepdims=True) + EPS)


def _params(*sem):
    return pltpu.CompilerParams(dimension_semantics=sem, vmem_limit_bytes=VMEM_LIMIT)


def _resident(shape, index):
    return pl.BlockSpec(shape, lambda *_: index, pipeline_mode=pl.Buffered(1))


def _mod_kernel(cond_ref, w_ref, b_ref, o_ref):
    cnd = cond_ref[...]
    s = cnd * jax.nn.sigmoid(cnd)
    o_ref[...] = _dot(s.astype(BF16), w_ref[...].astype(BF16)) + b_ref[...]


def _modulation(cond, ada_w, ada_b):
    tn = 1024
    n_out = ada_w.shape[-1]
    return pl.pallas_call(
        _mod_kernel,
        grid=(DEPTH, n_out // tn),
        in_specs=[
            pl.BlockSpec((MOD_ROWS, D_MODEL), lambda l, j: (0, 0)),
            pl.BlockSpec((None, D_MODEL, tn), lambda l, j: (l, 0, j)),
            pl.BlockSpec((None, 1, tn), lambda l, j: (l, 0, j)),
        ],
        out_specs=pl.BlockSpec((None, MOD_ROWS, tn), lambda l, j: (l, 0, j)),
        out_shape=jax.ShapeDtypeStruct((DEPTH, MOD_ROWS, n_out), F32),
        compiler_params=_params("parallel", "parallel"),
        name="modulation",
    )(cond, ada_w, ada_b.reshape(DEPTH, 1, n_out))


def _mod_row(sample, seq_len, tm):
    if not sample:
        return lambda i: 0
    tiles_per_seq = seq_len // tm
    return lambda i: 1 + i // tiles_per_seq


def _rope_slab(x, cos, sin):
    return x * cos + pltpu.roll(x, 64, axis=1) * sin


def _inproj_kernel(*refs, n_w, n_q, n_k, k_scale, rope, gates, tn):
    x_ref, g_ref, sh_ref, sc_ref = refs[:4]
    w_refs = refs[4:4 + n_w]
    pos = 4 + n_w
    if gates:
        wg_ref, bg_ref = refs[pos:pos + 2]
        pos += 2
    if rope:
        cos_ref, sin_ref = refs[pos:pos + 2]
        pos += 2
    z_ref = refs[pos]
    pos += 1
    if gates:
        gates_ref = refs[pos]
        pos += 1
    u_sc = refs[pos]

    u = _rms(x_ref[...]) * g_ref[...] * (1.0 + sc_ref[...]) + sh_ref[...]
    u_sc[...] = u.astype(BF16)
    if gates:
        gates_ref[...] = _dot_nt(wg_ref[...], u_sc[...]) + bg_ref[...]

    wb = w_refs[0].shape[1]
    for j in range(z_ref.shape[1] // tn):
        part, off = divmod(j * tn, wb)
        z = _dot(u_sc[...], w_refs[part][:, off:off + tn])
        scale = k_scale if n_q <= j < n_q + n_k else 1.0
        if rope and j < n_q + n_k:
            for s in range(tn // LANES):
                r = _rope_slab(z[:, s * LANES:(s + 1) * LANES], cos_ref[...], sin_ref[...])
                if scale != 1.0:
                    r = r * scale
                z_ref[:, j * tn + s * LANES:j * tn + (s + 1) * LANES] = r.astype(BF16)
        elif scale != 1.0:
            z_ref[:, j * tn:(j + 1) * tn] = (z * scale).astype(BF16)
        else:
            z_ref[:, j * tn:(j + 1) * tn] = z.astype(BF16)


def _inproj(x, ng4, mod5, layer, w_parts, wb, *, seq_len, sample, n_q, n_k, k_scale, rope=None, gates=None):
    n_tok = x.shape[0]
    tm, tn = TM_PROJ, TN_PROJ
    n_col = wb * len(w_parts)
    row = _mod_row(sample, seq_len, tm)
    in_specs = [
        pl.BlockSpec((tm, D_MODEL), lambda i: (i, 0)),
        pl.BlockSpec((None, None, 1, D_MODEL), lambda i: (layer, 0, 0, 0)),
        pl.BlockSpec((None, None, None, 1, D_MODEL), lambda i: (layer, row(i), 0, 0, 0)),
        pl.BlockSpec((None, None, None, 1, D_MODEL), lambda i: (layer, row(i), 1, 0, 0)),
    ]
    in_specs += [_resident((None, D_MODEL, wb), (jl, 0, blk)) for _, jl, blk in w_parts]
    args = [x, ng4, mod5, mod5] + [w for w, _, _ in w_parts]
    out_specs = [pl.BlockSpec((tm, n_col), lambda i: (i, 0))]
    out_shape = [jax.ShapeDtypeStruct((n_tok, n_col), BF16)]
    if gates is not None:
        wg_t, bg, jg = gates
        n_g = wg_t.shape[1]
        in_specs += [_resident((None, n_g, D_MODEL), (jg, 0, 0)), _resident((None, n_g, 1), (jg, 0, 0))]
        args += [wg_t, bg]
        out_specs += [pl.BlockSpec((n_g, tm), lambda i: (0, i))]
        out_shape += [jax.ShapeDtypeStruct((n_g, n_tok), F32)]
    if rope is not None:
        tiles_per_seq = seq_len // tm
        in_specs += [pl.BlockSpec((tm, LANES), lambda i: (i % tiles_per_seq, 0))] * 2
        args += list(rope)
    kern = functools.partial(_inproj_kernel, n_w=len(w_parts), n_q=n_q, n_k=n_k, k_scale=k_scale,
                             rope=rope is not None, gates=gates is not None, tn=tn)
    return pl.pallas_call(
        kern,
        grid=(n_tok // tm,),
        in_specs=in_specs,
        out_specs=out_specs,
        out_shape=out_shape,
        scratch_shapes=[pltpu.VMEM((tm, D_MODEL), BF16)],
        compiler_params=_params("parallel"),
        name="inproj",
    )(*args)


def _tri_masks():
    li = lax.broadcasted_iota(jnp.int32, (CHUNK, CHUNK), 0)
    si = lax.broadcasted_iota(jnp.int32, (CHUNK, CHUNK), 1)
    return si <= li, si >= li


def _layer_slot(ref, fresh_slot):
    if fresh_slot is None:
        return ref
    for other in range(ref.shape[1]):
        if other != fresh_slot:
            ref[:, other] = jnp.zeros(ref.shape[:1] + ref.shape[2:], ref.dtype)
    return ref.at[:, fresh_slot]


def _seqs_per_step(batch, seq_len, carry_in):
    if carry_in:
        return 1
    nb = max(1, SCAN_TOKENS // seq_len)
    while batch % nb:
        nb -= 1
    return nb


NG = 2 * A_HEADS


def _split_dot(x, mask_b):
    hi = x.astype(BF16)
    r1 = x - hi.astype(F32)
    mid = r1.astype(BF16)
    lo = (r1 - mid.astype(F32)).astype(BF16)
    return _dot(hi, mask_b) + _dot(mid, mask_b) + _dot(lo, mask_b)


def _mlstm_kernel(*refs, n_chunks, nb, carry_in, carry_out, n_alias, fresh_slot):
    q_ref, k_ref, v_ref, o_ref, g_ref, gn_ref = refs[:6]
    pos = 6
    if carry_in:
        c0_ref, n0_ref, m0_ref = refs[pos:pos + 3]
        pos += 3
    pos += n_alias
    h_ref = refs[pos]
    pos += 1
    if carry_out:
        cout_ref, nout_ref, mout_ref = refs[pos:pos + 3]
        pos += 3
    ab_sc, bt_sc, g_sc, bm_sc, mpf_sc, mpb_sc, c_sc, call_sc = refs[pos:pos + 8]

    head = pl.program_id(1)
    masks = _tri_masks()
    ones_b = jnp.ones((CHUNK, LANES), BF16)
    sum_b = jnp.concatenate([masks[1].astype(BF16), ones_b], axis=1)
    lane = lax.broadcasted_iota(jnp.int32, (CHUNK, LANES), 1)
    grow = lax.broadcasted_iota(jnp.int32, (2 * NG, CHUNK), 0)
    gsub = lax.broadcasted_iota(jnp.int32, (NG, LANES), 0)
    zpad = jnp.zeros((LANES - 2 * NG, CHUNK), F32)
    cols = (head, head + A_HEADS)
    mp_sc = (mpf_sc, mpb_sc)
    t_seq = n_chunks * CHUNK

    def pick_col(x, col):
        return jnp.sum(jnp.where(lane == col, x, 0.0), axis=1, keepdims=True)

    def at(s, c):
        return pl.ds(pl.multiple_of(s * t_seq + c * CHUNK, CHUNK), CHUNK)

    def gate_body(c, carry):
        for s in range(nb):
            idx = s * n_chunks + c
            gates = g_ref[:, at(s, c)]
            lf = jnp.minimum(gates, 0.0) - jnp.log1p(jnp.exp(-jnp.abs(gates)))
            lf = jnp.where(grow >= NG, lf, 0.0)
            sums = _split_dot(lf, sum_b)
            a_f = sums[:, :CHUNK]
            tot = sums[:, CHUNK:]
            a_all = jnp.where(grow < NG + A_HEADS, a_f, tot - a_f + lf)[NG:, :]
            b_all = gates[:NG, :] - a_all
            bt_sc[idx] = b_all
            ab_sc[at(s, c), :] = jnp.concatenate([a_all, b_all, zpad], axis=0).T
            g_sc[idx] = tot[NG:, :]
            bm_sc[idx] = jnp.broadcast_to(jnp.max(b_all, axis=1, keepdims=True), (NG, LANES))
        return carry

    lax.fori_loop(0, n_chunks, gate_body, 0, unroll=min(n_chunks, GATE_UNROLL))

    if carry_in:
        m_init = (jnp.broadcast_to(m0_ref[0], (NG, LANES)), jnp.broadcast_to(m0_ref[1], (NG, LANES)))
    else:
        m_init = (jnp.zeros((NG, LANES), F32),) * (2 * nb)

    def m_body(i, carry):
        out = []
        for s in range(nb):
            m_f, m_b = carry[2 * s], carry[2 * s + 1]
            jf = s * n_chunks + i
            jb = s * n_chunks + n_chunks - 1 - i
            mpf_sc[jf] = m_f
            mpb_sc[jb] = m_b
            out.append(g_sc[jf] + jnp.maximum(m_f, bm_sc[jf]))
            out.append(g_sc[jb] + jnp.maximum(m_b, bm_sc[jb]))
        return tuple(out)

    m_last = lax.fori_loop(0, n_chunks, m_body, m_init)

    if carry_in:
        for dirn in range(2):
            c_sc[dirn, :, :A_DV] = c0_ref[dirn]
            c_sc[dirn, :, A_DV:] = jnp.broadcast_to(n0_ref[dirn], (A_DK, LANES))
    else:
        c_sc[...] = jnp.zeros_like(c_sc)

    def state_body(i, carry):
        for s in range(nb):
            for dirn in range(2):
                c = i if dirn == 0 else n_chunks - 1 - i
                idx = s * n_chunks + c
                mp_row = mp_sc[dirn][idx, pl.ds(cols[dirn], 1), :]
                m_top = jnp.maximum(mp_row, bm_sc[idx, pl.ds(cols[dirn], 1), :])
                ws = jnp.exp(pick_col(ab_sc[at(s, c), :], NG + cols[dirn]) - m_top)
                dec = jnp.exp(mp_row - m_top)
                dec = jnp.concatenate([dec] * (c_sc.shape[2] // LANES), axis=1)
                c_old = c_sc[2 * s + dirn]
                call_sc[2 * s + dirn, c] = c_old.astype(BF16)
                kw = (k_ref[at(s, c), :].astype(F32) * ws).astype(BF16)
                upd = jnp.concatenate([_dot_tn(kw, v_ref[at(s, c), :]), _dot_tn(kw, ones_b)], axis=1)
                c_sc[2 * s + dirn] = dec * c_old + upd
        return carry

    lax.fori_loop(0, n_chunks, state_body, 0, unroll=min(n_chunks, STATE_UNROLL))
    if carry_out:
        outs = [_layer_slot(r, fresh_slot) for r in (cout_ref, nout_ref, mout_ref)]
        for s in range(nb):
            for dirn in range(2):
                outs[0][s, dirn] = c_sc[2 * s + dirn, :, :A_DV]
                outs[1][s, dirn] = c_sc[2 * s + dirn, :, A_DV:].T[0:1, :]
                m_end = jnp.where(gsub == cols[dirn], m_last[2 * s + dirn], 0.0)
                outs[2][s, dirn] = jnp.sum(m_end, axis=0, keepdims=True)[:, 0:1]

    def out_body(c, carry):
        for s in range(nb):
            q = q_ref[at(s, c), :]
            qf = q.astype(F32)
            v_ext = jnp.concatenate([v_ref[at(s, c), :], ones_b], axis=1)
            s_raw = _dot_nt(q, k_ref[at(s, c), :])
            a_chunk = ab_sc[at(s, c), :]
            idx = s * n_chunks + c
            h = None
            for dirn in range(2):
                col = cols[dirn]
                m_prev = mp_sc[dirn][idx, pl.ds(col, 1), :]
                b_vis = jnp.where(masks[dirn], bt_sc[idx, pl.ds(col, 1), :], -jnp.inf)
                m_row = jnp.maximum(m_prev, jnp.max(b_vis, axis=1, keepdims=True))
                sw = (s_raw * jnp.exp(b_vis - m_row)).astype(BF16)
                w_inter = jnp.exp(m_prev - m_row)
                floor = jnp.exp(-(pick_col(a_chunk, col) + m_row))
                qw = (qf * w_inter).astype(BF16)
                nd = _dot(sw, v_ext) + _dot(qw, call_sc[2 * s + dirn, c])
                r = 1.0 / jnp.maximum(jnp.abs(nd[:, A_DV:]), floor)
                hd = nd[:, :A_DV] * jnp.concatenate([r, r], axis=1)
                h = hd if h is None else h + hd
            o = o_ref[at(s, c), :].astype(F32)
            h_ref[at(s, c), :] = (_layer_norm(h) * gn_ref[...] * jax.nn.sigmoid(o)).astype(BF16)
        return carry

    lax.fori_loop(0, n_chunks, out_body, 0, unroll=min(n_chunks, max(1, MLSTM_OUT_UNROLL // nb)))


def _mlstm_scan(z, gates, gn4, j, *, batch, seq_len, n_layers, state=None, prev=None):
    n_tok = z.shape[0]
    n_chunks = seq_len // CHUNK
    carry_in = state is not None
    carry_out = not carry_in
    nb = _seqs_per_step(batch, seq_len, carry_in)
    t = nb * seq_len
    in_specs = [
        pl.BlockSpec((t, A_DK), lambda b, h: (b, h)),
        pl.BlockSpec((t, A_DK), lambda b, h: (b, A_HEADS + h)),
        pl.BlockSpec((t, A_DV), lambda b, h: (b, A_HEADS + h)),
        pl.BlockSpec((t, A_DV), lambda b, h: (b, 2 * A_HEADS + h)),
        pl.BlockSpec((2 * NG, t), lambda b, h: (0, b)),
        pl.BlockSpec((None, None, 1, A_DV), lambda b, h: (j, h, 0, 0)),
    ]
    args = [z, z, z, z, gates, gn4]
    aliases = {}
    if carry_in:
        in_specs += [
            pl.BlockSpec((None, None, 2, None, A_DK, A_DV), lambda b, h: (b, j, 0, h, 0, 0)),
            pl.BlockSpec((None, None, 2, None, A_DK, 1), lambda b, h: (b, j, 0, h, 0, 0)),
            pl.BlockSpec((None, None, 2, None, 1, 1), lambda b, h: (b, j, 0, h, 0, 0)),
        ]
        args += list(state)
    out_specs = [pl.BlockSpec((t, A_DV), lambda b, h: (b, h))]
    out_shape = [jax.ShapeDtypeStruct((n_tok, A_HEADS * A_DV), BF16)]
    fresh_slot = j if carry_out and prev is None else None
    if carry_out:
        lay, jb = (n_layers, 0) if prev is None else (None, j)
        out_specs += [
            pl.BlockSpec((nb, lay, 2, None, A_DK, A_DV), lambda b, h: (b, jb, 0, h, 0, 0)),
            pl.BlockSpec((nb, lay, 2, None, 1, A_DK), lambda b, h: (b, jb, 0, h, 0, 0)),
            pl.BlockSpec((nb, lay, 2, None, 1, 1), lambda b, h: (b, jb, 0, h, 0, 0)),
        ]
        out_shape += [
            jax.ShapeDtypeStruct((batch, n_layers, 2, A_HEADS, A_DK, A_DV), F32),
            jax.ShapeDtypeStruct((batch, n_layers, 2, A_HEADS, 1, A_DK), F32),
            jax.ShapeDtypeStruct((batch, n_layers, 2, A_HEADS, 1, 1), F32),
        ]
        if prev is not None:
            aliases = {len(args) + k: 1 + k for k in range(3)}
            in_specs += [pl.BlockSpec(memory_space=pl.ANY)] * 3
            args += list(prev)
    kern = functools.partial(_mlstm_kernel, n_chunks=n_chunks, nb=nb, carry_in=carry_in,
                             carry_out=carry_out, n_alias=len(aliases), fresh_slot=fresh_slot)
    return pl.pallas_call(
        kern,
        grid=(batch // nb, A_HEADS),
        in_specs=in_specs,
        out_specs=out_specs,
        out_shape=out_shape,
        input_output_aliases=aliases,
        scratch_shapes=[
            pltpu.VMEM((t, LANES), F32),
            pltpu.VMEM((nb * n_chunks, NG, CHUNK), F32),
            pltpu.VMEM((nb * n_chunks, NG, LANES), F32),
            pltpu.VMEM((nb * n_chunks, NG, LANES), F32),
            pltpu.VMEM((nb * n_chunks, NG, LANES), F32),
            pltpu.VMEM((nb * n_chunks, NG, LANES), F32),
            pltpu.VMEM((2 * nb, A_DK, A_DV + LANES), F32),
            pltpu.VMEM((2 * nb, n_chunks, A_DK, A_DV + LANES), BF16),
        ],
        compiler_params=_params("parallel", "parallel"),
        name="mlstm_scan",
    )(*args)


def _ret_kernel(*refs, n_chunks, nb, carry_in, carry_out, n_alias, fresh_slot):
    q_ref, k_ref, v_ref, gate_ref, dec_ref, gn_ref = refs[:6]
    pos = 6
    if carry_in:
        s0_ref = refs[pos]
        pos += 1
    pos += n_alias
    h_ref = refs[pos]
    pos += 1
    if carry_out:
        sout_ref = refs[pos]
        pos += 1
    s_sc, sall_sc, dsum_sc, xi_sc, zeta_sc = refs[pos:pos + 5]
    t_seq = n_chunks * CHUNK

    def at(s, c):
        return pl.ds(pl.multiple_of(s * t_seq + c * CHUNK, CHUNK), CHUNK)

    lg = jnp.log1p(-jnp.exp(-dec_ref[...] * LN2))
    lg_f = lg[0:1, :]
    lg_b = lg[1:2, :]

    @pl.when(pl.program_id(1) == 0)
    def _():
        masks = _tri_masks()
        li = lax.broadcasted_iota(jnp.int32, (CHUNK, B_DV), 0).astype(F32)
        si = lax.broadcasted_iota(jnp.int32, (CHUNK, CHUNK), 1).astype(F32)
        lq = li[:, :CHUNK]
        dsum_sc[...] = (
            jnp.where(masks[0], jnp.exp(jnp.where(masks[0], lq - si, 0.0) * lg_f[:, :CHUNK]), 0.0)
            + jnp.where(masks[1], jnp.exp(jnp.where(masks[1], si - lq, 0.0) * lg_b[:, :CHUNK]), 0.0))
        xi_sc[0] = jnp.exp((lq + 1.0) * lg_f[:, :CHUNK])
        xi_sc[1] = jnp.exp((CHUNK - lq) * lg_b[:, :CHUNK])
        zeta_sc[0] = jnp.exp((CHUNK - 1.0 - lq) * lg_f[:, :CHUNK])
        zeta_sc[1] = jnp.exp(lq * lg_b[:, :CHUNK])

    cdec = (jnp.exp(CHUNK * lg_f), jnp.exp(CHUNK * lg_b))

    if carry_in:
        qr = B_DK // 4
        for dirn in range(2):
            for n, o in enumerate((0, 2, 1, 3)):
                s_sc[dirn, n * qr:(n + 1) * qr, :] = s0_ref[dirn, o * qr:(o + 1) * qr, :]
    else:
        s_sc[...] = jnp.zeros_like(s_sc)

    def state_body(i, carry):
        for s in range(nb):
            for dirn in range(2):
                c = i if dirn == 0 else n_chunks - 1 - i
                s_old = s_sc[2 * s + dirn]
                sall_sc[s, c, dirn * B_DK:(dirn + 1) * B_DK, :] = s_old.astype(BF16)
                kz = (k_ref[at(s, c), :].astype(F32) * zeta_sc[dirn]).astype(BF16)
                s_sc[2 * s + dirn] = cdec[dirn] * s_old + _dot_tn(kz, v_ref[at(s, c), :])
        return carry

    lax.fori_loop(0, n_chunks, state_body, 0, unroll=min(n_chunks, STATE_UNROLL))
    if carry_out:
        s_out = _layer_slot(sout_ref, fresh_slot)
        for s in range(nb):
            for dirn in range(2):
                s_out[s, dirn] = s_sc[2 * s + dirn]

    def out_body(c, carry):
        for s in range(nb):
            q = q_ref[at(s, c), :]
            v = v_ref[at(s, c), :]
            sw = _dot_nt(q, k_ref[at(s, c), :]) * dsum_sc[...]
            qf = q.astype(F32)
            qx = jnp.concatenate([qf * xi_sc[0], qf * xi_sc[1]], axis=1).astype(BF16)
            h = _dot(sw.astype(BF16), v) + _dot(qx, sall_sc[s, c])
            g = gate_ref[at(s, c), :].astype(F32)
            h_ref[at(s, c), :] = (_layer_norm(h) * gn_ref[...] * (g * jax.nn.sigmoid(g))).astype(BF16)
        return carry

    lax.fori_loop(0, n_chunks, out_body, 0, unroll=min(n_chunks, max(1, OUT_UNROLL // nb)))


def _ret_scan(z, dec_rep, gn4, j, *, batch, seq_len, n_layers, state=None, prev=None):
    n_tok = z.shape[0]
    n_chunks = seq_len // CHUNK
    carry_in = state is not None
    carry_out = not carry_in
    nb = _seqs_per_step(batch, seq_len, carry_in)
    t = nb * seq_len
    in_specs = [
        pl.BlockSpec((t, B_DK), lambda h, b: (b, h)),
        pl.BlockSpec((t, B_DK), lambda h, b: (b, B_HEADS + h)),
        pl.BlockSpec((t, B_DV), lambda h, b: (b, B_HEADS + h)),
        pl.BlockSpec((t, B_DV), lambda h, b: (b, 2 * B_HEADS + h)),
        pl.BlockSpec((None, None, 2, B_DV), lambda h, b: (j, h, 0, 0)),
        pl.BlockSpec((None, None, 1, B_DV), lambda h, b: (j, h, 0, 0)),
    ]
    args = [z, z, z, z, dec_rep, gn4]
    aliases = {}
    if carry_in:
        in_specs += [pl.BlockSpec((None, None, 2, None, B_DK, B_DV), lambda h, b: (b, j, 0, h, 0, 0))]
        args += [state]
    out_specs = [pl.BlockSpec((t, B_DV), lambda h, b: (b, h))]
    out_shape = [jax.ShapeDtypeStruct((n_tok, B_HEADS * B_DV), BF16)]
    fresh_slot = j if carry_out and prev is None else None
    if carry_out:
        lay, jb = (n_layers, 0) if prev is None else (None, j)
        out_specs += [pl.BlockSpec((nb, lay, 2, None, B_DK, B_DV), lambda h, b: (b, jb, 0, h, 0, 0))]
        out_shape += [jax.ShapeDtypeStruct((batch, n_layers, 2, B_HEADS, B_DK, B_DV), F32)]
        if prev is not None:
            aliases = {len(args): 1}
            in_specs += [pl.BlockSpec(memory_space=pl.ANY)]
            args += [prev]
    kern = functools.partial(_ret_kernel, n_chunks=n_chunks, nb=nb, carry_in=carry_in,
                             carry_out=carry_out, n_alias=len(aliases), fresh_slot=fresh_slot)
    return pl.pallas_call(
        kern,
        grid=(B_HEADS, batch // nb),
        in_specs=in_specs,
        out_specs=out_specs,
        out_shape=out_shape,
        input_output_aliases=aliases,
        scratch_shapes=[
            pltpu.VMEM((2 * nb, B_DK, B_DV), F32),
            pltpu.VMEM((nb, n_chunks, 2 * B_DK, B_DV), BF16),
            pltpu.VMEM((CHUNK, CHUNK), F32),
            pltpu.VMEM((2, CHUNK, B_DK), F32),
            pltpu.VMEM((2, CHUNK, B_DK), F32),
        ],
        compiler_params=_params("parallel", "arbitrary"),
        name="ret_scan",
    )(*args)


def _outproj_kernel(h_ref, w_ref, x_ref, g_ref, gate_ref, o_ref):
    for r in range(o_ref.shape[0] // TR_OUT):
        rows = slice(r * TR_OUT, (r + 1) * TR_OUT)
        y = _dot(h_ref[rows, :], w_ref[...])
        o_ref[rows, :] = x_ref[rows, :] + gate_ref[...] * (_rms(y) * g_ref[...])


def _outproj(h, w, j, x, ng4, mod5, layer, *, seq_len, sample):
    n_tok, hv = h.shape
    tm = TM_OUT
    row = _mod_row(sample, seq_len, tm)
    return pl.pallas_call(
        _outproj_kernel,
        grid=(n_tok // tm,),
        in_specs=[
            pl.BlockSpec((tm, hv), lambda i: (i, 0)),
            _resident((None, hv, D_MODEL), (j, 0, 0)),
            pl.BlockSpec((tm, D_MODEL), lambda i: (i, 0)),
            pl.BlockSpec((None, None, 1, D_MODEL), lambda i: (layer, 1, 0, 0)),
            pl.BlockSpec((None, None, None, 1, D_MODEL), lambda i: (layer, row(i), 2, 0, 0)),
        ],
        out_specs=pl.BlockSpec((tm, D_MODEL), lambda i: (i, 0)),
        out_shape=jax.ShapeDtypeStruct((n_tok, D_MODEL), F32),
        compiler_params=_params("parallel"),
        name="outproj",
    )(h, w, x, ng4, mod5)


def _conv3(hs_ref, half, h, cw, cb, seg, n_seg):
    for s in range(n_seg):
        base = SUBLANES + s * (seg + SUBLANES)
        h_seg = h[s * seg:(s + 1) * seg, :]
        hs_ref[2 * half, base + 1:base + 1 + seg, :] = h_seg
        hs_ref[2 * half + 1, base - 1:base - 1 + seg, :] = h_seg
    parts = []
    for s in range(n_seg):
        base = SUBLANES + s * (seg + SUBLANES)
        h_prev = hs_ref[2 * half, base:base + seg, :]
        h_next = hs_ref[2 * half + 1, base:base + seg, :]
        h_mid = h[s * seg:(s + 1) * seg, :]
        parts.append(h_prev * cw[0:1, :] + h_mid * cw[1:2, :] + h_next * cw[2:3, :] + cb)
    return parts


def _ffn_kernel(x_ref, g2_ref, sh_ref, sc_ref, wup_ref, cw_ref, cb_ref, wd_ref, g3_ref, gate_ref,
                o_ref, u_sc, act_sc, hs_sc, *, seg, n_seg, tf):
    u = _rms(x_ref[...]) * g2_ref[...] * (1.0 + sc_ref[...]) + sh_ref[...]
    u_sc[...] = u.astype(BF16)
    zero_rows = jnp.zeros((SUBLANES, tf), F32)
    for s in range(n_seg):
        base = SUBLANES + s * (seg + SUBLANES)
        for half in range(2):
            hs_sc[2 * half, base:base + SUBLANES, :] = zero_rows
            hs_sc[2 * half + 1, base + seg - SUBLANES:base + seg, :] = zero_rows

    for cidx in range(D_FF // tf):
        cg = slice(cidx * tf, (cidx + 1) * tf)
        cu = slice(D_FF + cidx * tf, D_FF + (cidx + 1) * tf)
        hg = _conv3(hs_sc, 0, _dot(u_sc[...], wup_ref[:, cg]), cw_ref[:, cg], cb_ref[:, cg], seg, n_seg)
        hu = _conv3(hs_sc, 1, _dot(u_sc[...], wup_ref[:, cu]), cw_ref[:, cu], cb_ref[:, cu], seg, n_seg)
        for s in range(n_seg):
            act = jax.nn.gelu(hg[s], approximate=True) * hu[s]
            act_sc[s * seg:(s + 1) * seg, cg] = act.astype(BF16)

    f = _dot(act_sc[...], wd_ref[...])
    o_ref[...] = x_ref[...] + gate_ref[...] * (_rms(f) * g3_ref[...])


def _ffn(x, ng4, mod5, layer, w_up, conv_w, conv_b, w_down, *, seq_len, sample):
    n_tok = x.shape[0]
    tm, tf = TM_FFN, TF_FFN
    row = _mod_row(sample, seq_len, tm)
    seg = GRID_W if sample else seq_len
    n_seg = tm // seg
    kern = functools.partial(_ffn_kernel, seg=seg, n_seg=n_seg, tf=tf)
    mod_spec = lambda k: pl.BlockSpec((None, None, None, 1, D_MODEL), lambda i: (layer, row(i), k, 0, 0))
    gain_spec = lambda k: pl.BlockSpec((None, None, 1, D_MODEL), lambda i: (layer, k, 0, 0))
    return pl.pallas_call(
        kern,
        grid=(n_tok // tm,),
        in_specs=[
            pl.BlockSpec((tm, D_MODEL), lambda i: (i, 0)),
            gain_spec(2),
            mod_spec(3),
            mod_spec(4),
            _resident((None, D_MODEL, 2 * D_FF), (layer, 0, 0)),
            _resident((None, 3, 2 * D_FF), (layer, 0, 0)),
            _resident((None, 1, 2 * D_FF), (layer, 0, 0)),
            _resident((None, D_FF, D_MODEL), (layer, 0, 0)),
            gain_spec(3),
            mod_spec(5),
        ],
        out_specs=pl.BlockSpec((tm, D_MODEL), lambda i: (i, 0)),
        out_shape=jax.ShapeDtypeStruct((n_tok, D_MODEL), F32),
        scratch_shapes=[
            pltpu.VMEM((tm, D_MODEL), BF16),
            pltpu.VMEM((tm, D_FF), BF16),
            pltpu.VMEM((4, SUBLANES + n_seg * (seg + SUBLANES), tf), F32),
        ],
        compiler_params=_params("parallel"),
        name="convffn",
    )(x, ng4, mod5, mod5, w_up, conv_w, conv_b, w_down, ng4, mod5)


def _rope_tables(seq_len):
    quarter = B_DK // 4
    inv = ROPE_BASE ** (-jnp.arange(quarter, dtype=F32) / quarter)
    t = jnp.arange(seq_len)
    rows = (t // GRID_W).astype(F32)[:, None] * inv
    cols = (t % GRID_W).astype(F32)[:, None] * inv
    cos = jnp.concatenate([jnp.cos(rows), jnp.cos(cols)] * 2, axis=-1)
    sin = jnp.concatenate([-jnp.sin(rows), -jnp.sin(cols), jnp.sin(rows), jnp.sin(cols)], axis=-1)
    return cos, sin


def _rope_qk_weights(w_in):
    n_l = w_in.shape[0]
    quarter = B_DK // 4
    w_qk = w_in[:, :, :B_QK].astype(BF16).reshape(n_l, D_MODEL, 2 * B_HEADS, 2, 2, quarter)
    return jnp.swapaxes(w_qk, 3, 4).reshape(n_l, D_MODEL, B_QK)


def _gate_weights(w_in, b_gate):
    n_l = w_in.shape[0]
    order = jnp.array((0, 2, 1, 3))
    wg = w_in[:, :, A_MAIN:].reshape(n_l, D_MODEL, 4, A_HEADS)[:, :, order, :]
    wg_t = jnp.transpose(wg, (0, 2, 3, 1)).reshape(n_l, 2 * NG, D_MODEL).astype(BF16)
    bg = b_gate[:, order, :].reshape(n_l, 2 * NG, 1)
    return wg_t, bg


def kernel(x_prompt, x_sample, state_mlstm_C, state_mlstm_n, state_mlstm_m, state_ret_S, c, c_ctx,
           norm_gain, ada_w, ada_b, ml_w_in, ml_b_gate, ml_norm, ml_w_out,
           ret_w_in, ret_decay, ret_norm, ret_w_out, ffn_w_up, ffn_conv, ffn_conv_b, ffn_w_down):
    bp, tp, _ = x_prompt.shape
    bs, ts, _ = x_sample.shape
    n_a = ml_w_in.shape[0]
    n_b = ret_w_in.shape[0]

    cond = jnp.concatenate([c_ctx[None, :], c, jnp.zeros((MOD_ROWS - 1 - bs, D_MODEL), F32)], axis=0)
    mod5 = _modulation(cond, ada_w, ada_b).reshape(DEPTH, MOD_ROWS, 6, 1, D_MODEL)
    ng4 = norm_gain.reshape(DEPTH, 4, 1, D_MODEL)
    rope = _rope_tables(ts)

    ml_w_in_b = ml_w_in.astype(BF16)
    ml_gates = _gate_weights(ml_w_in, ml_b_gate)
    ml_w_out_b = ml_w_out.astype(BF16)
    ret_w_in_b = ret_w_in.astype(BF16)
    ret_w_qk_rope = _rope_qk_weights(ret_w_in)
    ret_w_out_b = ret_w_out.astype(BF16)
    ffn_w_up_b = ffn_w_up.astype(BF16)
    ffn_w_down_b = ffn_w_down.astype(BF16)
    ffn_conv_b3 = ffn_conv_b.reshape(DEPTH, 1, 2 * D_FF)
    ml_gn4 = ml_norm.reshape(n_a, A_HEADS, 1, A_DV)
    ret_gn4 = ret_norm.reshape(n_b, B_HEADS, 1, B_DV)
    dec_rep = jnp.broadcast_to(jnp.swapaxes(ret_decay, 1, 2)[..., None], (n_b, B_HEADS, 2, B_DV))
    st_c = state_mlstm_C
    st_n = state_mlstm_n.reshape(bs, n_a, 2, A_HEADS, A_DK, 1)
    st_m = state_mlstm_m.reshape(bs, n_a, 2, A_HEADS, 1, 1)

    groups = [
        dict(x=x_prompt.reshape(bp * tp, D_MODEL), batch=bp, seq_len=tp, sample=False),
        dict(x=x_sample.reshape(bs * ts, D_MODEL), batch=bs, seq_len=ts, sample=True),
    ]
    ml_states = None
    ret_states = None
    for i in range(DEPTH):
        j = i // N_MIXERS
        for grp in groups:
            x = grp["x"]
            geo = dict(seq_len=grp["seq_len"], sample=grp["sample"])
            bt = dict(batch=grp["batch"], seq_len=grp["seq_len"])
            if i % N_MIXERS == 0:
                n_qk = A_HEADS * A_DK // TN_PROJ
                z, gates = _inproj(x, ng4, mod5, i, [(ml_w_in_b, j, 0)], A_MAIN, n_q=n_qk, n_k=n_qk,
                                    k_scale=A_DK ** -0.5, gates=ml_gates + (j,), **geo)
                if grp["sample"]:
                    (h,) = _mlstm_scan(z, gates, ml_gn4, j, n_layers=n_a, state=(st_c, st_n, st_m), **bt)
                else:
                    h, *ml_states = _mlstm_scan(z, gates, ml_gn4, j, n_layers=n_a, prev=ml_states, **bt)
                x = _outproj(h, ml_w_out_b, j, x, ng4, mod5, i, **geo)
            else:
                n_qk = B_HEADS * B_DK // TN_PROJ
                w_qk = ret_w_qk_rope if grp["sample"] else ret_w_in_b
                w_parts = [(w_qk, j, 0), (ret_w_in_b, j, 1), (ret_w_in_b, j, 2)]
                (z,) = _inproj(x, ng4, mod5, i, w_parts, B_QK, n_q=n_qk, n_k=n_qk, k_scale=B_DK ** -0.5,
                               rope=rope if grp["sample"] else None, **geo)
                if grp["sample"]:
                    (h,) = _ret_scan(z, dec_rep, ret_gn4, j, n_layers=n_b, state=state_ret_S, **bt)
                else:
                    h, ret_states = _ret_scan(z, dec_rep, ret_gn4, j, n_layers=n_b, prev=ret_states, **bt)
                x = _outproj(h, ret_w_out_b, j, x, ng4, mod5, i, **geo)
            grp["x"] = _ffn(x, ng4, mod5, i, ffn_w_up_b, ffn_conv, ffn_conv_b3, ffn_w_down_b, **geo)

    y_prompt = groups[0]["x"].reshape(bp, tp, D_MODEL)
    y_sample = groups[1]["x"].reshape(bs, ts, D_MODEL)
    new_c, new_n, new_m = ml_states
    return (y_prompt, y_sample, new_c, new_n.reshape(bp, n_a, 2, A_HEADS, A_DK),
            new_m.reshape(bp, n_a, 2, A_HEADS), ret_states)
```

```python
import functools
import math

import jax
import jax.numpy as jnp
from jax import lax
from jax.experimental import pallas as pl
from jax.experimental.pallas import tpu as pltpu

D_MODEL = 1024
DEPTH = 4
GRID_W = 64
CHUNK = 128
N_MIXERS = 2
A_HEADS = 4
A_DV = D_MODEL // A_HEADS
A_DK = A_DV // 2
A_MAIN = 2 * A_HEADS * A_DK + 2 * A_HEADS * A_DV
B_HEADS = 8
B_DK = D_MODEL // B_HEADS
B_DV = 2 * D_MODEL // B_HEADS
B_QK = 2 * B_HEADS * B_DK
ROPE_BASE = 10000.0
D_FF = ((8 * D_MODEL // 3 + 127) // 128) * 128
EPS = 1e-6
LN2 = math.log(2.0)

F32 = jnp.float32
BF16 = jnp.bfloat16

LANES = 128
SUBLANES = 8
MOD_ROWS = 16
VMEM_LIMIT = 48 * 1024 * 1024

TM_PROJ = 512
TN_PROJ = 512
TM_OUT = 1024
TR_OUT = 512
TM_FFN = 512
TF_FFN = 256
GATE_UNROLL = 16
STATE_UNROLL = 8
OUT_UNROLL = 8
MLSTM_OUT_UNROLL = 4
SCAN_TOKENS = 2048


def _dot(a, b):
    return jnp.dot(a, b, preferred_element_type=F32)


def _dot_nt(a, b):
    return lax.dot_general(a, b, (((1,), (1,)), ((), ())), preferred_element_type=F32)


def _dot_tn(a, b):
    return lax.dot_general(a, b, (((0,), (0,)), ((), ())), preferred_element_type=F32)


def _rms(x):
    return x * lax.rsqrt(jnp.mean(x * x, axis=-1, keepdims=True) + EPS)


def _layer_norm(h):
    d = h - jnp.mean(h, axis=-1, keepdims=True)
    return d * lax.rsqrt(jnp.mean(d * d, axis=-1, keepdims=True) + EPS)


def _params(*sem):
    return pltpu.CompilerParams(dimension_semantics=sem, vmem_limit_bytes=VMEM_LIMIT)


def _resident(shape, index):
    return pl.BlockSpec(shape, lambda *_: index, pipeline_mode=pl.Buffered(1))


def _mod_kernel(cond_ref, w_ref, b_ref, o_ref):
    cnd = cond_ref[...]
    s = cnd * jax.nn.sigmoid(cnd)
    o_ref[...] = _dot(s.astype(BF16), w_ref[...].astype(BF16)) + b_ref[...]


def _modulation(cond, ada_w, ada_b):
    tn = 1024
    n_out = ada_w.shape[-1]
    return pl.pallas_call(
        _mod_kernel,
        grid=(DEPTH, n_out // tn),
        in_specs=[
            pl.BlockSpec((MOD_ROWS, D_MODEL), lambda l, j: (0, 0)),
            pl.BlockSpec((None, D_MODEL, tn), lambda l, j: (l, 0, j)),
            pl.BlockSpec((None, 1, tn), lambda l, j: (l, 0, j)),
        ],
        out_specs=pl.BlockSpec((None, MOD_ROWS, tn), lambda l, j: (l, 0, j)),
        out_shape=jax.ShapeDtypeStruct((DEPTH, MOD_ROWS, n_out), F32),
        compiler_params=_params("parallel", "parallel"),
        name="modulation",
    )(cond, ada_w, ada_b.reshape(DEPTH, 1, n_out))


def _mod_row(sample, seq_len, tm):
    if not sample:
        return lambda i: 0
    tiles_per_seq = seq_len // tm
    return lambda i: 1 + i // tiles_per_seq


def _rope_slab(x, cos, sin):
    return x * cos + pltpu.roll(x, 64, axis=1) * sin


def _inproj_kernel(*refs, n_w, n_q, n_k, k_scale, rope, gates, tn):
    x_ref, g_ref, sh_ref, sc_ref = refs[:4]
    w_refs = refs[4:4 + n_w]
    pos = 4 + n_w
    if gates:
        wg_ref, bg_ref = refs[pos:pos + 2]
        pos += 2
    if rope:
        cos_ref, sin_ref = refs[pos:pos + 2]
        pos += 2
    z_ref = refs[pos]
    pos += 1
    if gates:
        gates_ref = refs[pos]
        pos += 1
    u_sc = refs[pos]

    u = _rms(x_ref[...]) * g_ref[...] * (1.0 + sc_ref[...]) + sh_ref[...]
    u_sc[...] = u.astype(BF16)
    if gates:
        gates_ref[...] = _dot_nt(wg_ref[...], u_sc[...]) + bg_ref[...]

    wb = w_refs[0].shape[1]
    for j in range(z_ref.shape[1] // tn):
        part, off = divmod(j * tn, wb)
        z = _dot(u_sc[...], w_refs[part][:, off:off + tn])
        scale = k_scale if n_q <= j < n_q + n_k else 1.0
        if rope and j < n_q + n_k:
            for s in range(tn // LANES):
                r = _rope_slab(z[:, s * LANES:(s + 1) * LANES], cos_ref[...], sin_ref[...])
                if scale != 1.0:
                    r = r * scale
                z_ref[:, j * tn + s * LANES:j * tn + (s + 1) * LANES] = r.astype(BF16)
        elif scale != 1.0:
            z_ref[:, j * tn:(j + 1) * tn] = (z * scale).astype(BF16)
        else:
            z_ref[:, j * tn:(j + 1) * tn] = z.astype(BF16)


def _inproj(x, ng4, mod5, layer, w_parts, wb, *, seq_len, sample, n_q, n_k, k_scale, rope=None, gates=None):
    n_tok = x.shape[0]
    tm, tn = TM_PROJ, TN_PROJ
    n_col = wb * len(w_parts)
    row = _mod_row(sample, seq_len, tm)
    in_specs = [
        pl.BlockSpec((tm, D_MODEL), lambda i: (i, 0)),
        pl.BlockSpec((None, None, 1, D_MODEL), lambda i: (layer, 0, 0, 0)),
        pl.BlockSpec((None, None, None, 1, D_MODEL), lambda i: (layer, row(i), 0, 0, 0)),
        pl.BlockSpec((None, None, None, 1, D_MODEL), lambda i: (layer, row(i), 1, 0, 0)),
    ]
    in_specs += [_resident((None, D_MODEL, wb), (jl, 0, blk)) for _, jl, blk in w_parts]
    args = [x, ng4, mod5, mod5] + [w for w, _, _ in w_parts]
    out_specs = [pl.BlockSpec((tm, n_col), lambda i: (i, 0))]
    out_shape = [jax.ShapeDtypeStruct((n_tok, n_col), BF16)]
    if gates is not None:
        wg_t, bg, jg = gates
        n_g = wg_t.shape[1]
        in_specs += [_resident((None, n_g, D_MODEL), (jg, 0, 0)), _resident((None, n_g, 1), (jg, 0, 0))]
        args += [wg_t, bg]
        out_specs += [pl.BlockSpec((n_g, tm), lambda i: (0, i))]
        out_shape += [jax.ShapeDtypeStruct((n_g, n_tok), F32)]
    if rope is not None:
        tiles_per_seq = seq_len // tm
        in_specs += [pl.BlockSpec((tm, LANES), lambda i: (i % tiles_per_seq, 0))] * 2
        args += list(rope)
    kern = functools.partial(_inproj_kernel, n_w=len(w_parts), n_q=n_q, n_k=n_k, k_scale=k_scale,
                             rope=rope is not None, gates=gates is not None, tn=tn)
    return pl.pallas_call(
        kern,
        grid=(n_tok // tm,),
        in_specs=in_specs,
        out_specs=out_specs,
        out_shape=out_shape,
        scratch_shapes=[pltpu.VMEM((tm, D_MODEL), BF16)],
        compiler_params=_params("parallel"),
        name="inproj",
    )(*args)


def _tri_masks():
    li = lax.broadcasted_iota(jnp.int32, (CHUNK, CHUNK), 0)
    si = lax.broadcasted_iota(jnp.int32, (CHUNK, CHUNK), 1)
    return si <= li, si >= li


def _layer_slot(ref, fresh_slot):
    if fresh_slot is None:
        return ref
    for other in range(ref.shape[1]):
        if other != fresh_slot:
            ref[:, other] = jnp.zeros(ref.shape[:1] + ref.shape[2:], ref.dtype)
    return ref.at[:, fresh_slot]


def _seqs_per_step(batch, seq_len, carry_in):
    if carry_in:
        return 1
    nb = max(1, SCAN_TOKENS // seq_len)
    while batch % nb:
        nb -= 1
    return nb


NG = 2 * A_HEADS


def _split_dot(x, mask_b):
    hi = x.astype(BF16)
    r1 = x - hi.astype(F32)
    mid = r1.astype(BF16)
    lo = (r1 - mid.astype(F32)).astype(BF16)
    return _dot(hi, mask_b) + _dot(mid, mask_b) + _dot(lo, mask_b)


def _mlstm_kernel(*refs, n_chunks, nb, carry_in, carry_out, n_alias, fresh_slot):
    q_ref, k_ref, v_ref, o_ref, g_ref, gn_ref = refs[:6]
    pos = 6
    if carry_in:
        c0_ref, n0_ref, m0_ref = refs[pos:pos + 3]
        pos += 3
    pos += n_alias
    h_ref = refs[pos]
    pos += 1
    if carry_out:
        cout_ref, nout_ref, mout_ref = refs[pos:pos + 3]
        pos += 3
    ab_sc, bt_sc, g_sc, bm_sc, mpf_sc, mpb_sc, c_sc, call_sc = refs[pos:pos + 8]

    head = pl.program_id(1)
    masks = _tri_masks()
    ones_b = jnp.ones((CHUNK, LANES), BF16)
    sum_b = jnp.concatenate([masks[1].astype(BF16), ones_b], axis=1)
    lane = lax.broadcasted_iota(jnp.int32, (CHUNK, LANES), 1)
    grow = lax.broadcasted_iota(jnp.int32, (2 * NG, CHUNK), 0)
    gsub = lax.broadcasted_iota(jnp.int32, (NG, LANES), 0)
    zpad = jnp.zeros((LANES - 2 * NG, CHUNK), F32)
    cols = (head, head + A_HEADS)
    mp_sc = (mpf_sc, mpb_sc)
    t_seq = n_chunks * CHUNK

    def pick_col(x, col):
        return jnp.sum(jnp.where(lane == col, x, 0.0), axis=1, keepdims=True)

    def at(s, c):
        return pl.ds(pl.multiple_of(s * t_seq + c * CHUNK, CHUNK), CHUNK)

    def gate_body(c, carry):
        for s in range(nb):
            idx = s * n_chunks + c
            gates = g_ref[:, at(s, c)]
            lf = jnp.minimum(gates, 0.0) - jnp.log1p(jnp.exp(-jnp.abs(gates)))
            lf = jnp.where(grow >= NG, lf, 0.0)
            sums = _split_dot(lf, sum_b)
            a_f = sums[:, :CHUNK]
            tot = sums[:, CHUNK:]
            a_all = jnp.where(grow < NG + A_HEADS, a_f, tot - a_f + lf)[NG:, :]
            b_all = gates[:NG, :] - a_all
            bt_sc[idx] = b_all
            ab_sc[at(s, c), :] = jnp.concatenate([a_all, b_all, zpad], axis=0).T
            g_sc[idx] = tot[NG:, :]
            bm_sc[idx] = jnp.broadcast_to(jnp.max(b_all, axis=1, keepdims=True), (NG, LANES))
        return carry

    lax.fori_loop(0, n_chunks, gate_body, 0, unroll=min(n_chunks, GATE_UNROLL))

    if carry_in:
        m_init = (jnp.broadcast_to(m0_ref[0], (NG, LANES)), jnp.broadcast_to(m0_ref[1], (NG, LANES)))
    else:
        m_init = (jnp.zeros((NG, LANES), F32),) * (2 * nb)

    def m_body(i, carry):
        out = []
        for s in range(nb):
            m_f, m_b = carry[2 * s], carry[2 * s + 1]
            jf = s * n_chunks + i
            jb = s * n_chunks + n_chunks - 1 - i
            mpf_sc[jf] = m_f
            mpb_sc[jb] = m_b
            out.append(g_sc[jf] + jnp.maximum(m_f, bm_sc[jf]))
            out.append(g_sc[jb] + jnp.maximum(m_b, bm_sc[jb]))
        return tuple(out)

    m_last = lax.fori_loop(0, n_chunks, m_body, m_init)

    if carry_in:
        for dirn in range(2):
            c_sc[dirn, :, :A_DV] = c0_ref[dirn]
            c_sc[dirn, :, A_DV:] = jnp.broadcast_to(n0_ref[dirn], (A_DK, LANES))
    else:
        c_sc[...] = jnp.zeros_like(c_sc)

    def state_body(i, carry):
        for s in range(nb):
            for dirn in range(2):
                c = i if dirn == 0 else n_chunks - 1 - i
                idx = s * n_chunks + c
                mp_row = mp_sc[dirn][idx, pl.ds(cols[dirn], 1), :]
                m_top = jnp.maximum(mp_row, bm_sc[idx, pl.ds(cols[dirn], 1), :])
                ws = jnp.exp(pick_col(ab_sc[at(s, c), :], NG + cols[dirn]) - m_top)
                dec = jnp.exp(mp_row - m_top)
                dec = jnp.concatenate([dec] * (c_sc.shape[2] // LANES), axis=1)
                c_old = c_sc[2 * s + dirn]
                call_sc[2 * s + dirn, c] = c_old.astype(BF16)
                kw = (k_ref[at(s, c), :].astype(F32) * ws).astype(BF16)
                upd = jnp.concatenate([_dot_tn(kw, v_ref[at(s, c), :]), _dot_tn(kw, ones_b)], axis=1)
                c_sc[2 * s + dirn] = dec * c_old + upd
        return carry

    lax.fori_loop(0, n_chunks, state_body, 0, unroll=min(n_chunks, STATE_UNROLL))
    if carry_out:
        outs = [_layer_slot(r, fresh_slot) for r in (cout_ref, nout_ref, mout_ref)]
        for s in range(nb):
            for dirn in range(2):
                outs[0][s, dirn] = c_sc[2 * s + dirn, :, :A_DV]
                outs[1][s, dirn] = c_sc[2 * s + dirn, :, A_DV:].T[0:1, :]
                m_end = jnp.where(gsub == cols[dirn], m_last[2 * s + dirn], 0.0)
                outs[2][s, dirn] = jnp.sum(m_end, axis=0, keepdims=True)[:, 0:1]

    def out_body(c, carry):
        for s in range(nb):
            q = q_ref[at(s, c), :]
            qf = q.astype(F32)
            v_ext = jnp.concatenate([v_ref[at(s, c), :], ones_b], axis=1)
            s_raw = _dot_nt(q, k_ref[at(s, c), :])
            a_chunk = ab_sc[at(s, c), :]
            idx = s * n_chunks + c
            h = None
            for dirn in range(2):
                col = cols[dirn]
                m_prev = mp_sc[dirn][idx, pl.ds(col, 1), :]
                b_vis = jnp.where(masks[dirn], bt_sc[idx, pl.ds(col, 1), :], -jnp.inf)
                m_row = jnp.maximum(m_prev, jnp.max(b_vis, axis=1, keepdims=True))
                sw = (s_raw * jnp.exp(b_vis - m_row)).astype(BF16)
                w_inter = jnp.exp(m_prev - m_row)
                floor = jnp.exp(-(pick_col(a_chunk, col) + m_row))
                qw = (qf * w_inter).astype(BF16)
                nd = _dot(sw, v_ext) + _dot(qw, call_sc[2 * s + dirn, c])
                r = 1.0 / jnp.maximum(jnp.abs(nd[:, A_DV:]), floor)
                hd = nd[:, :A_DV] * jnp.concatenate([r, r], axis=1)
                h = hd if h is None else h + hd
            o = o_ref[at(s, c), :].astype(F32)
            h_ref[at(s, c), :] = (_layer_norm(h) * gn_ref[...] * jax.nn.sigmoid(o)).astype(BF16)
        return carry

    lax.fori_loop(0, n_chunks, out_body, 0, unroll=min(n_chunks, max(1, MLSTM_OUT_UNROLL // nb)))


def _mlstm_scan(z, gates, gn4, j, *, batch, seq_len, n_layers, state=None, prev=None):
    n_tok = z.shape[0]
    n_chunks = seq_len // CHUNK
    carry_in = state is not None
    carry_out = not carry_in
    nb = _seqs_per_step(batch, seq_len, carry_in)
    t = nb * seq_len
    in_specs = [
        pl.BlockSpec((t, A_DK), lambda b, h: (b, h)),
        pl.BlockSpec((t, A_DK), lambda b, h: (b, A_HEADS + h)),
        pl.BlockSpec((t, A_DV), lambda b, h: (b, A_HEADS + h)),
        pl.BlockSpec((t, A_DV), lambda b, h: (b, 2 * A_HEADS + h)),
        pl.BlockSpec((2 * NG, t), lambda b, h: (0, b)),
        pl.BlockSpec((None, None, 1, A_DV), lambda b, h: (j, h, 0, 0)),
    ]
    args = [z, z, z, z, gates, gn4]
    aliases = {}
    if carry_in:
        in_specs += [
            pl.BlockSpec((None, None, 2, None, A_DK, A_DV), lambda b, h: (b, j, 0, h, 0, 0)),
            pl.BlockSpec((None, None, 2, None, A_DK, 1), lambda b, h: (b, j, 0, h, 0, 0)),
            pl.BlockSpec((None, None, 2, None, 1, 1), lambda b, h: (b, j, 0, h, 0, 0)),
        ]
        args += list(state)
    out_specs = [pl.BlockSpec((t, A_DV), lambda b, h: (b, h))]
    out_shape = [jax.ShapeDtypeStruct((n_tok, A_HEADS * A_DV), BF16)]
    fresh_slot = j if carry_out and prev is None else None
    if carry_out:
        lay, jb = (n_layers, 0) if prev is None else (None, j)
        out_specs += [
            pl.BlockSpec((nb, lay, 2, None, A_DK, A_DV), lambda b, h: (b, jb, 0, h, 0, 0)),
            pl.BlockSpec((nb, lay, 2, None, 1, A_DK), lambda b, h: (b, jb, 0, h, 0, 0)),
            pl.BlockSpec((nb, lay, 2, None, 1, 1), lambda b, h: (b, jb, 0, h, 0, 0)),
        ]
        out_shape += [
            jax.ShapeDtypeStruct((batch, n_layers, 2, A_HEADS, A_DK, A_DV), F32),
            jax.ShapeDtypeStruct((batch, n_layers, 2, A_HEADS, 1, A_DK), F32),
            jax.ShapeDtypeStruct((batch, n_layers, 2, A_HEADS, 1, 1), F32),
        ]
        if prev is not None:
            aliases = {len(args) + k: 1 + k for k in range(3)}
            in_specs += [pl.BlockSpec(memory_space=pl.ANY)] * 3
            args += list(prev)
    kern = functools.partial(_mlstm_kernel, n_chunks=n_chunks, nb=nb, carry_in=carry_in,
                             carry_out=carry_out, n_alias=len(aliases), fresh_slot=fresh_slot)
    return pl.pallas_call(
        kern,
        grid=(batch // nb, A_HEADS),
        in_specs=in_specs,
        out_specs=out_specs,
        out_shape=out_shape,
        input_output_aliases=aliases,
        scratch_shapes=[
            pltpu.VMEM((t, LANES), F32),
            pltpu.VMEM((nb * n_chunks, NG, CHUNK), F32),
            pltpu.VMEM((nb * n_chunks, NG, LANES), F32),
            pltpu.VMEM((nb * n_chunks, NG, LANES), F32),
            pltpu.VMEM((nb * n_chunks, NG, LANES), F32),
            pltpu.VMEM((nb * n_chunks, NG, LANES), F32),
            pltpu.VMEM((2 * nb, A_DK, A_DV + LANES), F32),
            pltpu.VMEM((2 * nb, n_chunks, A_DK, A_DV + LANES), BF16),
        ],
        compiler_params=_params("parallel", "parallel"),
        name="mlstm_scan",
    )(*args)


def _ret_kernel(*refs, n_chunks, nb, carry_in, carry_out, n_alias, fresh_slot):
    q_ref, k_ref, v_ref, gate_ref, dec_ref, gn_ref = refs[:6]
    pos = 6
    if carry_in:
        s0_ref = refs[pos]
        pos += 1
    pos += n_alias
    h_ref = refs[pos]
    pos += 1
    if carry_out:
        sout_ref = refs[pos]
        pos += 1
    s_sc, sall_sc, dsum_sc, xi_sc, zeta_sc = refs[pos:pos + 5]
    t_seq = n_chunks * CHUNK

    def at(s, c):
        return pl.ds(pl.multiple_of(s * t_seq + c * CHUNK, CHUNK), CHUNK)

    lg = jnp.log1p(-jnp.exp(-dec_ref[...] * LN2))
    lg_f = lg[0:1, :]
    lg_b = lg[1:2, :]

    @pl.when(pl.program_id(1) == 0)
    def _():
        masks = _tri_masks()
        li = lax.broadcasted_iota(jnp.int32, (CHUNK, B_DV), 0).astype(F32)
        si = lax.broadcasted_iota(jnp.int32, (CHUNK, CHUNK), 1).astype(F32)
        lq = li[:, :CHUNK]
        dsum_sc[...] = (
            jnp.where(masks[0], jnp.exp(jnp.where(masks[0], lq - si, 0.0) * lg_f[:, :CHUNK]), 0.0)
            + jnp.where(masks[1], jnp.exp(jnp.where(masks[1], si - lq, 0.0) * lg_b[:, :CHUNK]), 0.0))
        xi_sc[0] = jnp.exp((lq + 1.0) * lg_f[:, :CHUNK])
        xi_sc[1] = jnp.exp((CHUNK - lq) * lg_b[:, :CHUNK])
        zeta_sc[0] = jnp.exp((CHUNK - 1.0 - lq) * lg_f[:, :CHUNK])
        zeta_sc[1] = jnp.exp(lq * lg_b[:, :CHUNK])

    cdec = (jnp.exp(CHUNK * lg_f), jnp.exp(CHUNK * lg_b))

    if carry_in:
        qr = B_DK // 4
        for dirn in range(2):
            for n, o in enumerate((0, 2, 1, 3)):
                s_sc[dirn, n * qr:(n + 1) * qr, :] = s0_ref[dirn, o * qr:(o + 1) * qr, :]
    else:
        s_sc[...] = jnp.zeros_like(s_sc)

    def state_body(i, carry):
        for s in range(nb):
            for dirn in range(2):
                c = i if dirn == 0 else n_chunks - 1 - i
                s_old = s_sc[2 * s + dirn]
                sall_sc[s, c, dirn * B_DK:(dirn + 1) * B_DK, :] = s_old.astype(BF16)
                kz = (k_ref[at(s, c), :].astype(F32) * zeta_sc[dirn]).astype(BF16)
                s_sc[2 * s + dirn] = cdec[dirn] * s_old + _dot_tn(kz, v_ref[at(s, c), :])
        return carry

    lax.fori_loop(0, n_chunks, state_body, 0, unroll=min(n_chunks, STATE_UNROLL))
    if carry_out:
        s_out = _layer_slot(sout_ref, fresh_slot)
        for s in range(nb):
            for dirn in range(2):
                s_out[s, dirn] = s_sc[2 * s + dirn]

    def out_body(c, carry):
        for s in range(nb):
            q = q_ref[at(s, c), :]
            v = v_ref[at(s, c), :]
            sw = _dot_nt(q, k_ref[at(s, c), :]) * dsum_sc[...]
            qf = q.astype(F32)
            qx = jnp.concatenate([qf * xi_sc[0], qf * xi_sc[1]], axis=1).astype(BF16)
            h = _dot(sw.astype(BF16), v) + _dot(qx, sall_sc[s, c])
            g = gate_ref[at(s, c), :].astype(F32)
            h_ref[at(s, c), :] = (_layer_norm(h) * gn_ref[...] * (g * jax.nn.sigmoid(g))).astype(BF16)
        return carry

    lax.fori_loop(0, n_chunks, out_body, 0, unroll=min(n_chunks, max(1, OUT_UNROLL // nb)))


def _ret_scan(z, dec_rep, gn4, j, *, batch, seq_len, n_layers, state=None, prev=None):
    n_tok = z.shape[0]
    n_chunks = seq_len // CHUNK
    carry_in = state is not None
    carry_out = not carry_in
    nb = _seqs_per_step(batch, seq_len, carry_in)
    t = nb * seq_len
    in_specs = [
        pl.BlockSpec((t, B_DK), lambda h, b: (b, h)),
        pl.BlockSpec((t, B_DK), lambda h, b: (b, B_HEADS + h)),
        pl.BlockSpec((t, B_DV), lambda h, b: (b, B_HEADS + h)),
        pl.BlockSpec((t, B_DV), lambda h, b: (b, 2 * B_HEADS + h)),
        pl.BlockSpec((None, None, 2, B_DV), lambda h, b: (j, h, 0, 0)),
        pl.BlockSpec((None, None, 1, B_DV), lambda h, b: (j, h, 0, 0)),
    ]
    args = [z, z, z, z, dec_rep, gn4]
    aliases = {}
    if carry_in:
        in_specs += [pl.BlockSpec((None, None, 2, None, B_DK, B_DV), lambda h, b: (b, j, 0, h, 0, 0))]
        args += [state]
    out_specs = [pl.BlockSpec((t, B_DV), lambda h, b: (b, h))]
    out_shape = [jax.ShapeDtypeStruct((n_tok, B_HEADS * B_DV), BF16)]
    fresh_slot = j if carry_out and prev is None else None
    if carry_out:
        lay, jb = (n_layers, 0) if prev is None else (None, j)
        out_specs += [pl.BlockSpec((nb, lay, 2, None, B_DK, B_DV), lambda h, b: (b, jb, 0, h, 0, 0))]
        out_shape += [jax.ShapeDtypeStruct((batch, n_layers, 2, B_HEADS, B_DK, B_DV), F32)]
        if prev is not None:
            aliases = {len(args): 1}
            in_specs += [pl.BlockSpec(memory_space=pl.ANY)]
            args += [prev]
    kern = functools.partial(_ret_kernel, n_chunks=n_chunks, nb=nb, carry_in=carry_in,
                             carry_out=carry_out, n_alias=len(aliases), fresh_slot=fresh_slot)
    return pl.pallas_call(
        kern,
        grid=(B_HEADS, batch // nb),
        in_specs=in_specs,
        out_specs=out_specs,
        out_shape=out_shape,
        input_output_aliases=aliases,
        scratch_shapes=[
            pltpu.VMEM((2 * nb, B_DK, B_DV), F32),
            pltpu.VMEM((nb, n_chunks, 2 * B_DK, B_DV), BF16),
            pltpu.VMEM((CHUNK, CHUNK), F32),
            pltpu.VMEM((2, CHUNK, B_DK), F32),
            pltpu.VMEM((2, CHUNK, B_DK), F32),
        ],
        compiler_params=_params("parallel", "arbitrary"),
        name="ret_scan",
    )(*args)


def _outproj_kernel(h_ref, w_ref, x_ref, g_ref, gate_ref, o_ref):
    for r in range(o_ref.shape[0] // TR_OUT):
        rows = slice(r * TR_OUT, (r + 1) * TR_OUT)
        y = _dot(h_ref[rows, :], w_ref[...])
        o_ref[rows, :] = x_ref[rows, :] + gate_ref[...] * (_rms(y) * g_ref[...])


def _outproj(h, w, j, x, ng4, mod5, layer, *, seq_len, sample):
    n_tok, hv = h.shape
    tm = TM_OUT
    row = _mod_row(sample, seq_len, tm)
    return pl.pallas_call(
        _outproj_kernel,
        grid=(n_tok // tm,),
        in_specs=[
            pl.BlockSpec((tm, hv), lambda i: (i, 0)),
            _resident((None, hv, D_MODEL), (j, 0, 0)),
            pl.BlockSpec((tm, D_MODEL), lambda i: (i, 0)),
            pl.BlockSpec((None, None, 1, D_MODEL), lambda i: (layer, 1, 0, 0)),
            pl.BlockSpec((None, None, None, 1, D_MODEL), lambda i: (layer, row(i), 2, 0, 0)),
        ],
        out_specs=pl.BlockSpec((tm, D_MODEL), lambda i: (i, 0)),
        out_shape=jax.ShapeDtypeStruct((n_tok, D_MODEL), F32),
        compiler_params=_params("parallel"),
        name="outproj",
    )(h, w, x, ng4, mod5)


def _conv3(hs_ref, half, h, cw, cb, seg, n_seg):
    for s in range(n_seg):
        base = SUBLANES + s * (seg + SUBLANES)
        h_seg = h[s * seg:(s + 1) * seg, :]
        hs_ref[2 * half, base + 1:base + 1 + seg, :] = h_seg
        hs_ref[2 * half + 1, base - 1:base - 1 + seg, :] = h_seg
    parts = []
    for s in range(n_seg):
        base = SUBLANES + s * (seg + SUBLANES)
        h_prev = hs_ref[2 * half, base:base + seg, :]
        h_next = hs_ref[2 * half + 1, base:base + seg, :]
        h_mid = h[s * seg:(s + 1) * seg, :]
        parts.append(h_prev * cw[0:1, :] + h_mid * cw[1:2, :] + h_next * cw[2:3, :] + cb)
    return parts


def _ffn_kernel(x_ref, g2_ref, sh_ref, sc_ref, wup_ref, cw_ref, cb_ref, wd_ref, g3_ref, gate_ref,
                o_ref, u_sc, act_sc, hs_sc, *, seg, n_seg, tf):
    u = _rms(x_ref[...]) * g2_ref[...] * (1.0 + sc_ref[...]) + sh_ref[...]
    u_sc[...] = u.astype(BF16)
    zero_rows = jnp.zeros((SUBLANES, tf), F32)
    for s in range(n_seg):
        base = SUBLANES + s * (seg + SUBLANES)
        for half in range(2):
            hs_sc[2 * half, base:base + SUBLANES, :] = zero_rows
            hs_sc[2 * half + 1, base + seg - SUBLANES:base + seg, :] = zero_rows

    for cidx in range(D_FF // tf):
        cg = slice(cidx * tf, (cidx + 1) * tf)
        cu = slice(D_FF + cidx * tf, D_FF + (cidx + 1) * tf)
        hg = _conv3(hs_sc, 0, _dot(u_sc[...], wup_ref[:, cg]), cw_ref[:, cg], cb_ref[:, cg], seg, n_seg)
        hu = _conv3(hs_sc, 1, _dot(u_sc[...], wup_ref[:, cu]), cw_ref[:, cu], cb_ref[:, cu], seg, n_seg)
        for s in range(n_seg):
            act = jax.nn.gelu(hg[s], approximate=True) * hu[s]
            act_sc[s * seg:(s + 1) * seg, cg] = act.astype(BF16)

    f = _dot(act_sc[...], wd_ref[...])
    o_ref[...] = x_ref[...] + gate_ref[...] * (_rms(f) * g3_ref[...])


def _ffn(x, ng4, mod5, layer, w_up, conv_w, conv_b, w_down, *, seq_len, sample):
    n_tok = x.shape[0]
    tm, tf = TM_FFN, TF_FFN
    row = _mod_row(sample, seq_len, tm)
    seg = GRID_W if sample else seq_len
    n_seg = tm // seg
    kern = functools.partial(_ffn_kernel, seg=seg, n_seg=n_seg, tf=tf)
    mod_spec = lambda k: pl.BlockSpec((None, None, None, 1, D_MODEL), lambda i: (layer, row(i), k, 0, 0))
    gain_spec = lambda k: pl.BlockSpec((None, None, 1, D_MODEL), lambda i: (layer, k, 0, 0))
    return pl.pallas_call(
        kern,
        grid=(n_tok // tm,),
        in_specs=[
            pl.BlockSpec((tm, D_MODEL), lambda i: (i, 0)),
            gain_spec(2),
            mod_spec(3),
            mod_spec(4),
            _resident((None, D_MODEL, 2 * D_FF), (layer, 0, 0)),
            _resident((None, 3, 2 * D_FF), (layer, 0, 0)),
            _resident((None, 1, 2 * D_FF), (layer, 0, 0)),
            _resident((None, D_FF, D_MODEL), (layer, 0, 0)),
            gain_spec(3),
            mod_spec(5),
        ],
        out_specs=pl.BlockSpec((tm, D_MODEL), lambda i: (i, 0)),
        out_shape=jax.ShapeDtypeStruct((n_tok, D_MODEL), F32),
        scratch_shapes=[
            pltpu.VMEM((tm, D_MODEL), BF16),
            pltpu.VMEM((tm, D_FF), BF16),
            pltpu.VMEM((4, SUBLANES + n_seg * (seg + SUBLANES), tf), F32),
        ],
        compiler_params=_params("parallel"),
        name="convffn",
    )(x, ng4, mod5, mod5, w_up, conv_w, conv_b, w_down, ng4, mod5)


def _rope_tables(seq_len):
    quarter = B_DK // 4
    inv = ROPE_BASE ** (-jnp.arange(quarter, dtype=F32) / quarter)
    t = jnp.arange(seq_len)
    rows = (t // GRID_W).astype(F32)[:, None] * inv
    cols = (t % GRID_W).astype(F32)[:, None] * inv
    cos = jnp.concatenate([jnp.cos(rows), jnp.cos(cols)] * 2, axis=-1)
    sin = jnp.concatenate([-jnp.sin(rows), -jnp.sin(cols), jnp.sin(rows), jnp.sin(cols)], axis=-1)
    return cos, sin


def _rope_qk_weights(w_in):
    n_l = w_in.shape[0]
    quarter = B_DK // 4
    w_qk = w_in[:, :, :B_QK].astype(BF16).reshape(n_l, D_MODEL, 2 * B_HEADS, 2, 2, quarter)
    return jnp.swapaxes(w_qk, 3, 4).reshape(n_l, D_MODEL, B_QK)


def _gate_weights(w_in, b_gate):
    n_l = w_in.shape[0]
    order = jnp.array((0, 2, 1, 3))
    wg = w_in[:, :, A_MAIN:].reshape(n_l, D_MODEL, 4, A_HEADS)[:, :, order, :]
    wg_t = jnp.transpose(wg, (0, 2, 3, 1)).reshape(n_l, 2 * NG, D_MODEL).astype(BF16)
    bg = b_gate[:, order, :].reshape(n_l, 2 * NG, 1)
    return wg_t, bg


def kernel(x_prompt, x_sample, state_mlstm_C, state_mlstm_n, state_mlstm_m, state_ret_S, c, c_ctx,
           norm_gain, ada_w, ada_b, ml_w_in, ml_b_gate, ml_norm, ml_w_out,
           ret_w_in, ret_decay, ret_norm, ret_w_out, ffn_w_up, ffn_conv, ffn_conv_b, ffn_w_down):
    bp, tp, _ = x_prompt.shape
    bs, ts, _ = x_sample.shape
    n_a = ml_w_in.shape[0]
    n_b = ret_w_in.shape[0]

    cond = jnp.concatenate([c_ctx[None, :], c, jnp.zeros((MOD_ROWS - 1 - bs, D_MODEL), F32)], axis=0)
    mod5 = _modulation(cond, ada_w, ada_b).reshape(DEPTH, MOD_ROWS, 6, 1, D_MODEL)
    ng4 = norm_gain.reshape(DEPTH, 4, 1, D_MODEL)
    rope = _rope_tables(ts)

    ml_w_in_b = ml_w_in.astype(BF16)
    ml_gates = _gate_weights(ml_w_in, ml_b_gate)
    ml_w_out_b = ml_w_out.astype(BF16)
    ret_w_in_b = ret_w_in.astype(BF16)
    ret_w_qk_rope = _rope_qk_weights(ret_w_in)
    ret_w_out_b = ret_w_out.astype(BF16)
    ffn_w_up_b = ffn_w_up.astype(BF16)
    ffn_w_down_b = ffn_w_down.astype(BF16)
    ffn_conv_b3 = ffn_conv_b.reshape(DEPTH, 1, 2 * D_FF)
    ml_gn4 = ml_norm.reshape(n_a, A_HEADS, 1, A_DV)
    ret_gn4 = ret_norm.reshape(n_b, B_HEADS, 1, B_DV)
    dec_rep = jnp.broadcast_to(jnp.swapaxes(ret_decay, 1, 2)[..., None], (n_b, B_HEADS, 2, B_DV))
    st_c = state_mlstm_C
    st_n = state_mlstm_n.reshape(bs, n_a, 2, A_HEADS, A_DK, 1)
    st_m = state_mlstm_m.reshape(bs, n_a, 2, A_HEADS, 1, 1)

    groups = [
        dict(x=x_prompt.reshape(bp * tp, D_MODEL), batch=bp, seq_len=tp, sample=False),
        dict(x=x_sample.reshape(bs * ts, D_MODEL), batch=bs, seq_len=ts, sample=True),
    ]
    ml_states = None
    ret_states = None
    for i in range(DEPTH):
        j = i // N_MIXERS
        for grp in groups:
            x = grp["x"]
            geo = dict(seq_len=grp["seq_len"], sample=grp["sample"])
            bt = dict(batch=grp["batch"], seq_len=grp["seq_len"])
            if i % N_MIXERS == 0:
                n_qk = A_HEADS * A_DK // TN_PROJ
                z, gates = _inproj(x, ng4, mod5, i, [(ml_w_in_b, j, 0)], A_MAIN, n_q=n_qk, n_k=n_qk,
                                    k_scale=A_DK ** -0.5, gates=ml_gates + (j,), **geo)
                if grp["sample"]:
                    (h,) = _mlstm_scan(z, gates, ml_gn4, j, n_layers=n_a, state=(st_c, st_n, st_m), **bt)
                else:
                    h, *ml_states = _mlstm_scan(z, gates, ml_gn4, j, n_layers=n_a, prev=ml_states, **bt)
                x = _outproj(h, ml_w_out_b, j, x, ng4, mod5, i, **geo)
            else:
                n_qk = B_HEADS * B_DK // TN_PROJ
                w_qk = ret_w_qk_rope if grp["sample"] else ret_w_in_b
                w_parts = [(w_qk, j, 0), (ret_w_in_b, j, 1), (ret_w_in_b, j, 2)]
                (z,) = _inproj(x, ng4, mod5, i, w_parts, B_QK, n_q=n_qk, n_k=n_qk, k_scale=B_DK ** -0.5,
                               rope=rope if grp["sample"] else None, **geo)
                if grp["sample"]:
                    (h,) = _ret_scan(z, dec_rep, ret_gn4, j, n_layers=n_b, state=state_ret_S, **bt)
                else:
                    h, ret_states = _ret_scan(z, dec_rep, ret_gn4, j, n_layers=n_b, prev=ret_states, **bt)
                x = _outproj(h, ret_w_out_b, j, x, ng4, mod5, i, **geo)
            grp["x"] = _ffn(x, ng4, mod5, i, ffn_w_up_b, ffn_conv, ffn_conv_b3, ffn_w_down_b, **geo)

    y_prompt = groups[0]["x"].reshape(bp, tp, D_MODEL)
    y_sample = groups[1]["x"].reshape(bs, ts, D_MODEL)
    new_c, new_n, new_m = ml_states
    return (y_prompt, y_sample, new_c, new_n.reshape(bp, n_a, 2, A_HEADS, A_DK),
            new_m.reshape(bp, n_a, 2, A_HEADS), ret_states)
```

```python
import functools
import math

import jax
import jax.numpy as jnp
from jax import lax
from jax.experimental import pallas as pl
from jax.experimental.pallas import tpu as pltpu

D_MODEL = 1024
DEPTH = 4
GRID_W = 64
CHUNK = 128
N_MIXERS = 2
A_HEADS = 4
A_DV = D_MODEL // A_HEADS
A_DK = A_DV // 2
A_MAIN = 2 * A_HEADS * A_DK + 2 * A_HEADS * A_DV
B_HEADS = 8
B_DK = D_MODEL // B_HEADS
B_DV = 2 * D_MODEL // B_HEADS
B_QK = 2 * B_HEADS * B_DK
ROPE_BASE = 10000.0
D_FF = ((8 * D_MODEL // 3 + 127) // 128) * 128
EPS = 1e-6
LN2 = math.log(2.0)

F32 = jnp.float32
BF16 = jnp.bfloat16

LANES = 128
SUBLANES = 8
MOD_ROWS = 16
VMEM_LIMIT = 48 * 1024 * 1024

TM_PROJ = 512
TN_PROJ = 512
TM_OUT = 1024
TR_OUT = 512
TM_FFN = 512
TF_FFN = 256
GATE_UNROLL = 16
STATE_UNROLL = 16
OUT_UNROLL = 16
MLSTM_OUT_UNROLL = 16
SCAN_TOKENS = 2048


def _dot(a, b):
    return jnp.dot(a, b, preferred_element_type=F32)


def _dot_nt(a, b):
    return lax.dot_general(a, b, (((1,), (1,)), ((), ())), preferred_element_type=F32)


def _dot_tn(a, b):
    return lax.dot_general(a, b, (((0,), (0,)), ((), ())), preferred_element_type=F32)


def _rms(x):
    return x * lax.rsqrt(jnp.mean(x * x, axis=-1, keepdims=True) + EPS)


def _layer_norm(h):
    d = h - jnp.mean(h, axis=-1, keepdims=True)
    return d * lax.rsqrt(jnp.mean(d * d, axis=-1, keepdims=True) + EPS)


def _params(*sem):
    return pltpu.CompilerParams(dimension_semantics=sem, vmem_limit_bytes=VMEM_LIMIT)


def _resident(shape, index):
    return pl.BlockSpec(shape, lambda *_: index, pipeline_mode=pl.Buffered(1))


def _mod_kernel(cond_ref, w_ref, b_ref, o_ref):
    cnd = cond_ref[...]
    s = cnd * jax.nn.sigmoid(cnd)
    o_ref[...] = _dot(s.astype(BF16), w_ref[...].astype(BF16)) + b_ref[...]


def _modulation(cond, ada_w, ada_b):
    tn = 1024
    n_out = ada_w.shape[-1]
    return pl.pallas_call(
        _mod_kernel,
        grid=(DEPTH, n_out // tn),
        in_specs=[
            pl.BlockSpec((MOD_ROWS, D_MODEL), lambda l, j: (0, 0)),
            pl.BlockSpec((None, D_MODEL, tn), lambda l, j: (l, 0, j)),
            pl.BlockSpec((None, 1, tn), lambda l, j: (l, 0, j)),
        ],
        out_specs=pl.BlockSpec((None, MOD_ROWS, tn), lambda l, j: (l, 0, j)),
        out_shape=jax.ShapeDtypeStruct((DEPTH, MOD_ROWS, n_out), F32),
        compiler_params=_params("parallel", "parallel"),
        name="modulation",
    )(cond, ada_w, ada_b.reshape(DEPTH, 1, n_out))


def _mod_row(sample, seq_len, tm):
    if not sample:
        return lambda i: 0
    tiles_per_seq = seq_len // tm
    return lambda i: 1 + i // tiles_per_seq


def _rope_slab(x, cos, sin):
    return x * cos + pltpu.roll(x, 64, axis=1) * sin


def _inproj_kernel(*refs, n_w, n_q, n_k, k_scale, rope, gates, tn):
    x_ref, g_ref, sh_ref, sc_ref = refs[:4]
    w_refs = refs[4:4 + n_w]
    pos = 4 + n_w
    if gates:
        wg_ref, bg_ref = refs[pos:pos + 2]
        pos += 2
    if rope:
        cos_ref, sin_ref = refs[pos:pos + 2]
        pos += 2
    z_ref = refs[pos]
    pos += 1
    if gates:
        gates_ref = refs[pos]
        pos += 1
    u_sc = refs[pos]

    u = _rms(x_ref[...]) * g_ref[...] * (1.0 + sc_ref[...]) + sh_ref[...]
    u_sc[...] = u.astype(BF16)
    if gates:
        gates_ref[...] = _dot_nt(wg_ref[...], u_sc[...]) + bg_ref[...]

    wb = w_refs[0].shape[1]
    for j in range(z_ref.shape[1] // tn):
        part, off = divmod(j * tn, wb)
        z = _dot(u_sc[...], w_refs[part][:, off:off + tn])
        scale = k_scale if n_q <= j < n_q + n_k else 1.0
        if rope and j < n_q + n_k:
            for s in range(tn // LANES):
                r = _rope_slab(z[:, s * LANES:(s + 1) * LANES], cos_ref[...], sin_ref[...])
                if scale != 1.0:
                    r = r * scale
                z_ref[:, j * tn + s * LANES:j * tn + (s + 1) * LANES] = r.astype(BF16)
        elif scale != 1.0:
            z_ref[:, j * tn:(j + 1) * tn] = (z * scale).astype(BF16)
        else:
            z_ref[:, j * tn:(j + 1) * tn] = z.astype(BF16)


def _inproj(x, ng4, mod5, layer, w_parts, wb, *, seq_len, sample, n_q, n_k, k_scale, rope=None, gates=None):
    n_tok = x.shape[0]
    tm, tn = TM_PROJ, TN_PROJ
    n_col = wb * len(w_parts)
    row = _mod_row(sample, seq_len, tm)
    in_specs = [
        pl.BlockSpec((tm, D_MODEL), lambda i: (i, 0)),
        pl.BlockSpec((None, None, 1, D_MODEL), lambda i: (layer, 0, 0, 0)),
        pl.BlockSpec((None, None, None, 1, D_MODEL), lambda i: (layer, row(i), 0, 0, 0)),
        pl.BlockSpec((None, None, None, 1, D_MODEL), lambda i: (layer, row(i), 1, 0, 0)),
    ]
    in_specs += [_resident((None, D_MODEL, wb), (jl, 0, blk)) for _, jl, blk in w_parts]
    args = [x, ng4, mod5, mod5] + [w for w, _, _ in w_parts]
    out_specs = [pl.BlockSpec((tm, n_col), lambda i: (i, 0))]
    out_shape = [jax.ShapeDtypeStruct((n_tok, n_col), BF16)]
    if gates is not None:
        wg_t, bg, jg = gates
        n_g = wg_t.shape[1]
        in_specs += [_resident((None, n_g, D_MODEL), (jg, 0, 0)), _resident((None, n_g, 1), (jg, 0, 0))]
        args += [wg_t, bg]
        out_specs += [pl.BlockSpec((n_g, tm), lambda i: (0, i))]
        out_shape += [jax.ShapeDtypeStruct((n_g, n_tok), F32)]
    if rope is not None:
        tiles_per_seq = seq_len // tm
        in_specs += [pl.BlockSpec((tm, LANES), lambda i: (i % tiles_per_seq, 0))] * 2
        args += list(rope)
    kern = functools.partial(_inproj_kernel, n_w=len(w_parts), n_q=n_q, n_k=n_k, k_scale=k_scale,
                             rope=rope is not None, gates=gates is not None, tn=tn)
    return pl.pallas_call(
        kern,
        grid=(n_tok // tm,),
        in_specs=in_specs,
        out_specs=out_specs,
        out_shape=out_shape,
        scratch_shapes=[pltpu.VMEM((tm, D_MODEL), BF16)],
        compiler_params=_params("parallel"),
        name="inproj",
    )(*args)


def _tri_masks():
    li = lax.broadcasted_iota(jnp.int32, (CHUNK, CHUNK), 0)
    si = lax.broadcasted_iota(jnp.int32, (CHUNK, CHUNK), 1)
    return si <= li, si >= li


def _layer_slot(ref, fresh_slot):
    if fresh_slot is None:
        return ref
    for other in range(ref.shape[1]):
        if other != fresh_slot:
            ref[:, other] = jnp.zeros(ref.shape[:1] + ref.shape[2:], ref.dtype)
    return ref.at[:, fresh_slot]


def _seqs_per_step(batch, seq_len, carry_in):
    if carry_in:
        return 1
    nb = max(1, SCAN_TOKENS // seq_len)
    while batch % nb:
        nb -= 1
    return nb


NG = 2 * A_HEADS


def _split_dot(x, mask_b):
    hi = x.astype(BF16)
    r1 = x - hi.astype(F32)
    mid = r1.astype(BF16)
    lo = (r1 - mid.astype(F32)).astype(BF16)
    return _dot(hi, mask_b) + _dot(mid, mask_b) + _dot(lo, mask_b)


def _mlstm_kernel(*refs, n_chunks, nb, carry_in, carry_out, n_alias, fresh_slot):
    q_ref, k_ref, v_ref, o_ref, g_ref, gn_ref = refs[:6]
    pos = 6
    if carry_in:
        c0_ref, n0_ref, m0_ref = refs[pos:pos + 3]
        pos += 3
    pos += n_alias
    h_ref = refs[pos]
    pos += 1
    if carry_out:
        cout_ref, nout_ref, mout_ref = refs[pos:pos + 3]
        pos += 3
    ab_sc, bt_sc, g_sc, bm_sc, mpf_sc, mpb_sc, c_sc, call_sc = refs[pos:pos + 8]

    head = pl.program_id(1)
    masks = _tri_masks()
    ones_b = jnp.ones((CHUNK, LANES), BF16)
    sum_b = jnp.concatenate([masks[1].astype(BF16), ones_b], axis=1)
    lane = lax.broadcasted_iota(jnp.int32, (CHUNK, LANES), 1)
    grow = lax.broadcasted_iota(jnp.int32, (2 * NG, CHUNK), 0)
    gsub = lax.broadcasted_iota(jnp.int32, (NG, LANES), 0)
    zpad = jnp.zeros((LANES - 2 * NG, CHUNK), F32)
    cols = (head, head + A_HEADS)
    mp_sc = (mpf_sc, mpb_sc)
    t_seq = n_chunks * CHUNK

    def pick_col(x, col):
        return jnp.sum(jnp.where(lane == col, x, 0.0), axis=1, keepdims=True)

    def at(s, c):
        return pl.ds(pl.multiple_of(s * t_seq + c * CHUNK, CHUNK), CHUNK)

    def gate_body(c, carry):
        for s in range(nb):
            idx = s * n_chunks + c
            gates = g_ref[:, at(s, c)]
            lf = jnp.minimum(gates, 0.0) - jnp.log1p(jnp.exp(-jnp.abs(gates)))
            lf = jnp.where(grow >= NG, lf, 0.0)
            sums = _split_dot(lf, sum_b)
            a_f = sums[:, :CHUNK]
            tot = sums[:, CHUNK:]
            a_all = jnp.where(grow < NG + A_HEADS, a_f, tot - a_f + lf)[NG:, :]
            b_all = gates[:NG, :] - a_all
            bt_sc[idx] = b_all
            ab_sc[at(s, c), :] = jnp.concatenate([a_all, b_all, zpad], axis=0).T
            g_sc[idx] = tot[NG:, :]
            bm_sc[idx] = jnp.broadcast_to(jnp.max(b_all, axis=1, keepdims=True), (NG, LANES))
        return carry

    lax.fori_loop(0, n_chunks, gate_body, 0, unroll=min(n_chunks, GATE_UNROLL))

    if carry_in:
        m_init = (jnp.broadcast_to(m0_ref[0], (NG, LANES)), jnp.broadcast_to(m0_ref[1], (NG, LANES)))
    else:
        m_init = (jnp.zeros((NG, LANES), F32),) * (2 * nb)

    def m_body(i, carry):
        out = []
        for s in range(nb):
            m_f, m_b = carry[2 * s], carry[2 * s + 1]
            jf = s * n_chunks + i
            jb = s * n_chunks + n_chunks - 1 - i
            mpf_sc[jf] = m_f
            mpb_sc[jb] = m_b
            out.append(g_sc[jf] + jnp.maximum(m_f, bm_sc[jf]))
            out.append(g_sc[jb] + jnp.maximum(m_b, bm_sc[jb]))
        return tuple(out)

    m_last = lax.fori_loop(0, n_chunks, m_body, m_init)

    if carry_in:
        for dirn in range(2):
            c_sc[dirn, :, :A_DV] = c0_ref[dirn]
            c_sc[dirn, :, A_DV:] = jnp.broadcast_to(n0_ref[dirn], (A_DK, LANES))
    else:
        c_sc[...] = jnp.zeros_like(c_sc)

    def state_body(i, carry):
        for s in range(nb):
            for dirn in range(2):
                c = i if dirn == 0 else n_chunks - 1 - i
                idx = s * n_chunks + c
                mp_row = mp_sc[dirn][idx, pl.ds(cols[dirn], 1), :]
                m_top = jnp.maximum(mp_row, bm_sc[idx, pl.ds(cols[dirn], 1), :])
                ws = jnp.exp(pick_col(ab_sc[at(s, c), :], NG + cols[dirn]) - m_top)
                dec = jnp.exp(mp_row - m_top)
                dec = jnp.concatenate([dec] * (c_sc.shape[2] // LANES), axis=1)
                c_old = c_sc[2 * s + dirn]
                call_sc[2 * s + dirn, c] = c_old.astype(BF16)
                kw = (k_ref[at(s, c), :].astype(F32) * ws).astype(BF16)
                upd = jnp.concatenate([_dot_tn(kw, v_ref[at(s, c), :]), _dot_tn(kw, ones_b)], axis=1)
                c_sc[2 * s + dirn] = dec * c_old + upd
        return carry

    lax.fori_loop(0, n_chunks, state_body, 0, unroll=min(n_chunks, STATE_UNROLL))
    if carry_out:
        outs = [_layer_slot(r, fresh_slot) for r in (cout_ref, nout_ref, mout_ref)]
        for s in range(nb):
            for dirn in range(2):
                outs[0][s, dirn] = c_sc[2 * s + dirn, :, :A_DV]
                outs[1][s, dirn] = c_sc[2 * s + dirn, :, A_DV:].T[0:1, :]
                m_end = jnp.where(gsub == cols[dirn], m_last[2 * s + dirn], 0.0)
                outs[2][s, dirn] = jnp.sum(m_end, axis=0, keepdims=True)[:, 0:1]

    def out_body(c, carry):
        for s in range(nb):
            q = q_ref[at(s, c), :]
            qf = q.astype(F32)
            v_ext = jnp.concatenate([v_ref[at(s, c), :], ones_b], axis=1)
            s_raw = _dot_nt(q, k_ref[at(s, c), :])
            a_chunk = ab_sc[at(s, c), :]
            idx = s * n_chunks + c
            h = None
            for dirn in range(2):
                col = cols[dirn]
                m_prev = mp_sc[dirn][idx, pl.ds(col, 1), :]
                b_vis = jnp.where(masks[dirn], bt_sc[idx, pl.ds(col, 1), :], -jnp.inf)
                m_row = jnp.maximum(m_prev, jnp.max(b_vis, axis=1, keepdims=True))
                sw = (s_raw * jnp.exp(b_vis - m_row)).astype(BF16)
                w_inter = jnp.exp(m_prev - m_row)
                floor = jnp.exp(-(pick_col(a_chunk, col) + m_row))
                qw = (qf * w_inter).astype(BF16)
                nd = _dot(sw, v_ext) + _dot(qw, call_sc[2 * s + dirn, c])
                r = 1.0 / jnp.maximum(jnp.abs(nd[:, A_DV:]), floor)
                hd = nd[:, :A_DV] * jnp.concatenate([r, r], axis=1)
                h = hd if h is None else h + hd
            o = o_ref[at(s, c), :].astype(F32)
            h_ref[at(s, c), :] = (_layer_norm(h) * gn_ref[...] * jax.nn.sigmoid(o)).astype(BF16)
        return carry

    lax.fori_loop(0, n_chunks, out_body, 0, unroll=min(n_chunks, max(1, MLSTM_OUT_UNROLL // nb)))


def _mlstm_scan(z, gates, gn4, j, *, batch, seq_len, n_layers, state=None, prev=None):
    n_tok = z.shape[0]
    n_chunks = seq_len // CHUNK
    carry_in = state is not None
    carry_out = not carry_in
    nb = _seqs_per_step(batch, seq_len, carry_in)
    t = nb * seq_len
    in_specs = [
        pl.BlockSpec((t, A_DK), lambda b, h: (b, h)),
        pl.BlockSpec((t, A_DK), lambda b, h: (b, A_HEADS + h)),
        pl.BlockSpec((t, A_DV), lambda b, h: (b, A_HEADS + h)),
        pl.BlockSpec((t, A_DV), lambda b, h: (b, 2 * A_HEADS + h)),
        pl.BlockSpec((2 * NG, t), lambda b, h: (0, b)),
        pl.BlockSpec((None, None, 1, A_DV), lambda b, h: (j, h, 0, 0)),
    ]
    args = [z, z, z, z, gates, gn4]
    aliases = {}
    if carry_in:
        in_specs += [
            pl.BlockSpec((None, None, 2, None, A_DK, A_DV), lambda b, h: (b, j, 0, h, 0, 0)),
            pl.BlockSpec((None, None, 2, None, A_DK, 1), lambda b, h: (b, j, 0, h, 0, 0)),
            pl.BlockSpec((None, None, 2, None, 1, 1), lambda b, h: (b, j, 0, h, 0, 0)),
        ]
        args += list(state)
    out_specs = [pl.BlockSpec((t, A_DV), lambda b, h: (b, h))]
    out_shape = [jax.ShapeDtypeStruct((n_tok, A_HEADS * A_DV), BF16)]
    fresh_slot = j if carry_out and prev is None else None
    if carry_out:
        lay, jb = (n_layers, 0) if prev is None else (None, j)
        out_specs += [
            pl.BlockSpec((nb, lay, 2, None, A_DK, A_DV), lambda b, h: (b, jb, 0, h, 0, 0)),
            pl.BlockSpec((nb, lay, 2, None, 1, A_DK), lambda b, h: (b, jb, 0, h, 0, 0)),
            pl.BlockSpec((nb, lay, 2, None, 1, 1), lambda b, h: (b, jb, 0, h, 0, 0)),
        ]
        out_shape += [
            jax.ShapeDtypeStruct((batch, n_layers, 2, A_HEADS, A_DK, A_DV), F32),
            jax.ShapeDtypeStruct((batch, n_layers, 2, A_HEADS, 1, A_DK), F32),
            jax.ShapeDtypeStruct((batch, n_layers, 2, A_HEADS, 1, 1), F32),
        ]
        if prev is not None:
            aliases = {len(args) + k: 1 + k for k in range(3)}
            in_specs += [pl.BlockSpec(memory_space=pl.ANY)] * 3
            args += list(prev)
    kern = functools.partial(_mlstm_kernel, n_chunks=n_chunks, nb=nb, carry_in=carry_in,
                             carry_out=carry_out, n_alias=len(aliases), fresh_slot=fresh_slot)
    return pl.pallas_call(
        kern,
        grid=(batch // nb, A_HEADS),
        in_specs=in_specs,
        out_specs=out_specs,
        out_shape=out_shape,
        input_output_aliases=aliases,
        scratch_shapes=[
            pltpu.VMEM((t, LANES), F32),
            pltpu.VMEM((nb * n_chunks, NG, CHUNK), F32),
            pltpu.VMEM((nb * n_chunks, NG, LANES), F32),
            pltpu.VMEM((nb * n_chunks, NG, LANES), F32),
            pltpu.VMEM((nb * n_chunks, NG, LANES), F32),
            pltpu.VMEM((nb * n_chunks, NG, LANES), F32),
            pltpu.VMEM((2 * nb, A_DK, A_DV + LANES), F32),
            pltpu.VMEM((2 * nb, n_chunks, A_DK, A_DV + LANES), BF16),
        ],
        compiler_params=_params("parallel", "parallel"),
        name="mlstm_scan",
    )(*args)


def _ret_kernel(*refs, n_chunks, nb, carry_in, carry_out, n_alias, fresh_slot):
    q_ref, k_ref, v_ref, gate_ref, dec_ref, gn_ref = refs[:6]
    pos = 6
    if carry_in:
        s0_ref = refs[pos]
        pos += 1
    pos += n_alias
    h_ref = refs[pos]
    pos += 1
    if carry_out:
        sout_ref = refs[pos]
        pos += 1
    s_sc, sall_sc, dsum_sc, xi_sc, zeta_sc = refs[pos:pos + 5]
    t_seq = n_chunks * CHUNK

    def at(s, c):
        return pl.ds(pl.multiple_of(s * t_seq + c * CHUNK, CHUNK), CHUNK)

    lg = jnp.log1p(-jnp.exp(-dec_ref[...] * LN2))
    lg_f = lg[0:1, :]
    lg_b = lg[1:2, :]

    @pl.when(pl.program_id(1) == 0)
    def _():
        masks = _tri_masks()
        li = lax.broadcasted_iota(jnp.int32, (CHUNK, B_DV), 0).astype(F32)
        si = lax.broadcasted_iota(jnp.int32, (CHUNK, CHUNK), 1).astype(F32)
        lq = li[:, :CHUNK]
        dsum_sc[...] = (
            jnp.where(masks[0], jnp.exp(jnp.where(masks[0], lq - si, 0.0) * lg_f[:, :CHUNK]), 0.0)
            + jnp.where(masks[1], jnp.exp(jnp.where(masks[1], si - lq, 0.0) * lg_b[:, :CHUNK]), 0.0))
        xi_sc[0] = jnp.exp((lq + 1.0) * lg_f[:, :CHUNK])
        xi_sc[1] = jnp.exp((CHUNK - lq) * lg_b[:, :CHUNK])
        zeta_sc[0] = jnp.exp((CHUNK - 1.0 - lq) * lg_f[:, :CHUNK])
        zeta_sc[1] = jnp.exp(lq * lg_b[:, :CHUNK])

    cdec = (jnp.exp(CHUNK * lg_f), jnp.exp(CHUNK * lg_b))

    if carry_in:
        qr = B_DK // 4
        for dirn in range(2):
            for n, o in enumerate((0, 2, 1, 3)):
                s_sc[dirn, n * qr:(n + 1) * qr, :] = s0_ref[dirn, o * qr:(o + 1) * qr, :]
    else:
        s_sc[...] = jnp.zeros_like(s_sc)

    def state_body(i, carry):
        for s in range(nb):
            for dirn in range(2):
                c = i if dirn == 0 else n_chunks - 1 - i
                s_old = s_sc[2 * s + dirn]
                sall_sc[s, c, dirn * B_DK:(dirn + 1) * B_DK, :] = s_old.astype(BF16)
                kz = (k_ref[at(s, c), :].astype(F32) * zeta_sc[dirn]).astype(BF16)
                s_sc[2 * s + dirn] = cdec[dirn] * s_old + _dot_tn(kz, v_ref[at(s, c), :])
        return carry

    lax.fori_loop(0, n_chunks, state_body, 0, unroll=min(n_chunks, STATE_UNROLL))
    if carry_out:
        s_out = _layer_slot(sout_ref, fresh_slot)
        for s in range(nb):
            for dirn in range(2):
                s_out[s, dirn] = s_sc[2 * s + dirn]

    def out_body(c, carry):
        for s in range(nb):
            q = q_ref[at(s, c), :]
            v = v_ref[at(s, c), :]
            sw = _dot_nt(q, k_ref[at(s, c), :]) * dsum_sc[...]
            qf = q.astype(F32)
            qx = jnp.concatenate([qf * xi_sc[0], qf * xi_sc[1]], axis=1).astype(BF16)
            h = _dot(sw.astype(BF16), v) + _dot(qx, sall_sc[s, c])
            g = gate_ref[at(s, c), :].astype(F32)
            h_ref[at(s, c), :] = (_layer_norm(h) * gn_ref[...] * (g * jax.nn.sigmoid(g))).astype(BF16)
        return carry

    lax.fori_loop(0, n_chunks, out_body, 0, unroll=min(n_chunks, max(1, OUT_UNROLL // nb)))


def _ret_scan(z, dec_rep, gn4, j, *, batch, seq_len, n_layers, state=None, prev=None):
    n_tok = z.shape[0]
    n_chunks = seq_len // CHUNK
    carry_in = state is not None
    carry_out = not carry_in
    nb = _seqs_per_step(batch, seq_len, carry_in)
    t = nb * seq_len
    in_specs = [
        pl.BlockSpec((t, B_DK), lambda h, b: (b, h)),
        pl.BlockSpec((t, B_DK), lambda h, b: (b, B_HEADS + h)),
        pl.BlockSpec((t, B_DV), lambda h, b: (b, B_HEADS + h)),
        pl.BlockSpec((t, B_DV), lambda h, b: (b, 2 * B_HEADS + h)),
        pl.BlockSpec((None, None, 2, B_DV), lambda h, b: (j, h, 0, 0)),
        pl.BlockSpec((None, None, 1, B_DV), lambda h, b: (j, h, 0, 0)),
    ]
    args = [z, z, z, z, dec_rep, gn4]
    aliases = {}
    if carry_in:
        in_specs += [pl.BlockSpec((None, None, 2, None, B_DK, B_DV), lambda h, b: (b, j, 0, h, 0, 0))]
        args += [state]
    out_specs = [pl.BlockSpec((t, B_DV), lambda h, b: (b, h))]
    out_shape = [jax.ShapeDtypeStruct((n_tok, B_HEADS * B_DV), BF16)]
    fresh_slot = j if carry_out and prev is None else None
    if carry_out:
        lay, jb = (n_layers, 0) if prev is None else (None, j)
        out_specs += [pl.BlockSpec((nb, lay, 2, None, B_DK, B_DV), lambda h, b: (b, jb, 0, h, 0, 0))]
        out_shape += [jax.ShapeDtypeStruct((batch, n_layers, 2, B_HEADS, B_DK, B_DV), F32)]
        if prev is not None:
            aliases = {len(args): 1}
            in_specs += [pl.BlockSpec(memory_space=pl.ANY)]
            args += [prev]
    kern = functools.partial(_ret_kernel, n_chunks=n_chunks, nb=nb, carry_in=carry_in,
                             carry_out=carry_out, n_alias=len(aliases), fresh_slot=fresh_slot)
    return pl.pallas_call(
        kern,
        grid=(B_HEADS, batch // nb),
        in_specs=in_specs,
        out_specs=out_specs,
        out_shape=out_shape,
        input_output_aliases=aliases,
        scratch_shapes=[
            pltpu.VMEM((2 * nb, B_DK, B_DV), F32),
            pltpu.VMEM((nb, n_chunks, 2 * B_DK, B_DV), BF16),
            pltpu.VMEM((CHUNK, CHUNK), F32),
            pltpu.VMEM((2, CHUNK, B_DK), F32),
            pltpu.VMEM((2, CHUNK, B_DK), F32),
        ],
        compiler_params=_params("parallel", "arbitrary"),
        name="ret_scan",
    )(*args)


def _outproj_kernel(h_ref, w_ref, x_ref, g_ref, gate_ref, o_ref):
    for r in range(o_ref.shape[0] // TR_OUT):
        rows = slice(r * TR_OUT, (r + 1) * TR_OUT)
        y = _dot(h_ref[rows, :], w_ref[...])
        o_ref[rows, :] = x_ref[rows, :] + gate_ref[...] * (_rms(y) * g_ref[...])


def _outproj(h, w, j, x, ng4, mod5, layer, *, seq_len, sample):
    n_tok, hv = h.shape
    tm = TM_OUT
    row = _mod_row(sample, seq_len, tm)
    return pl.pallas_call(
        _outproj_kernel,
        grid=(n_tok // tm,),
        in_specs=[
            pl.BlockSpec((tm, hv), lambda i: (i, 0)),
            _resident((None, hv, D_MODEL), (j, 0, 0)),
            pl.BlockSpec((tm, D_MODEL), lambda i: (i, 0)),
            pl.BlockSpec((None, None, 1, D_MODEL), lambda i: (layer, 1, 0, 0)),
            pl.BlockSpec((None, None, None, 1, D_MODEL), lambda i: (layer, row(i), 2, 0, 0)),
        ],
        out_specs=pl.BlockSpec((tm, D_MODEL), lambda i: (i, 0)),
        out_shape=jax.ShapeDtypeStruct((n_tok, D_MODEL), F32),
        compiler_params=_params("parallel"),
        name="outproj",
    )(h, w, x, ng4, mod5)


def _conv3(hs_ref, half, h, cw, cb, seg, n_seg):
    for s in range(n_seg):
        base = SUBLANES + s * (seg + SUBLANES)
        h_seg = h[s * seg:(s + 1) * seg, :]
        hs_ref[2 * half, base + 1:base + 1 + seg, :] = h_seg
        hs_ref[2 * half + 1, base - 1:base - 1 + seg, :] = h_seg
    parts = []
    for s in range(n_seg):
        base = SUBLANES + s * (seg + SUBLANES)
        h_prev = hs_ref[2 * half, base:base + seg, :]
        h_next = hs_ref[2 * half + 1, base:base + seg, :]
        h_mid = h[s * seg:(s + 1) * seg, :]
        parts.append(h_prev * cw[0:1, :] + h_mid * cw[1:2, :] + h_next * cw[2:3, :] + cb)
    return parts


def _ffn_kernel(x_ref, g2_ref, sh_ref, sc_ref, wup_ref, cw_ref, cb_ref, wd_ref, g3_ref, gate_ref,
                o_ref, u_sc, act_sc, hs_sc, *, seg, n_seg, tf):
    u = _rms(x_ref[...]) * g2_ref[...] * (1.0 + sc_ref[...]) + sh_ref[...]
    u_sc[...] = u.astype(BF16)
    zero_rows = jnp.zeros((SUBLANES, tf), F32)
    for s in range(n_seg):
        base = SUBLANES + s * (seg + SUBLANES)
        for half in range(2):
            hs_sc[2 * half, base:base + SUBLANES, :] = zero_rows
            hs_sc[2 * half + 1, base + seg - SUBLANES:base + seg, :] = zero_rows

    for cidx in range(D_FF // tf):
        cg = slice(cidx * tf, (cidx + 1) * tf)
        cu = slice(D_FF + cidx * tf, D_FF + (cidx + 1) * tf)
        hg = _conv3(hs_sc, 0, _dot(u_sc[...], wup_ref[:, cg]), cw_ref[:, cg], cb_ref[:, cg], seg, n_seg)
        hu = _conv3(hs_sc, 1, _dot(u_sc[...], wup_ref[:, cu]), cw_ref[:, cu], cb_ref[:, cu], seg, n_seg)
        for s in range(n_seg):
            act = jax.nn.gelu(hg[s], approximate=True) * hu[s]
            act_sc[s * seg:(s + 1) * seg, cg] = act.astype(BF16)

    f = _dot(act_sc[...], wd_ref[...])
    o_ref[...] = x_ref[...] + gate_ref[...] * (_rms(f) * g3_ref[...])


def _ffn(x, ng4, mod5, layer, w_up, conv_w, conv_b, w_down, *, seq_len, sample):
    n_tok = x.shape[0]
    tm, tf = TM_FFN, TF_FFN
    row = _mod_row(sample, seq_len, tm)
    seg = GRID_W if sample else seq_len
    n_seg = tm // seg
    kern = functools.partial(_ffn_kernel, seg=seg, n_seg=n_seg, tf=tf)
    mod_spec = lambda k: pl.BlockSpec((None, None, None, 1, D_MODEL), lambda i: (layer, row(i), k, 0, 0))
    gain_spec = lambda k: pl.BlockSpec((None, None, 1, D_MODEL), lambda i: (layer, k, 0, 0))
    return pl.pallas_call(
        kern,
        grid=(n_tok // tm,),
        in_specs=[
            pl.BlockSpec((tm, D_MODEL), lambda i: (i, 0)),
            gain_spec(2),
            mod_spec(3),
            mod_spec(4),
            _resident((None, D_MODEL, 2 * D_FF), (layer, 0, 0)),
            _resident((None, 3, 2 * D_FF), (layer, 0, 0)),
            _resident((None, 1, 2 * D_FF), (layer, 0, 0)),
            _resident((None, D_FF, D_MODEL), (layer, 0, 0)),
            gain_spec(3),
            mod_spec(5),
        ],
        out_specs=pl.BlockSpec((tm, D_MODEL), lambda i: (i, 0)),
        out_shape=jax.ShapeDtypeStruct((n_tok, D_MODEL), F32),
        scratch_shapes=[
            pltpu.VMEM((tm, D_MODEL), BF16),
            pltpu.VMEM((tm, D_FF), BF16),
            pltpu.VMEM((4, SUBLANES + n_seg * (seg + SUBLANES), tf), F32),
        ],
        compiler_params=_params("parallel"),
        name="convffn",
    )(x, ng4, mod5, mod5, w_up, conv_w, conv_b, w_down, ng4, mod5)


def _rope_tables(seq_len):
    quarter = B_DK // 4
    inv = ROPE_BASE ** (-jnp.arange(quarter, dtype=F32) / quarter)
    t = jnp.arange(seq_len)
    rows = (t // GRID_W).astype(F32)[:, None] * inv
    cols = (t % GRID_W).astype(F32)[:, None] * inv
    cos = jnp.concatenate([jnp.cos(rows), jnp.cos(cols)] * 2, axis=-1)
    sin = jnp.concatenate([-jnp.sin(rows), -jnp.sin(cols), jnp.sin(rows), jnp.sin(cols)], axis=-1)
    return cos, sin


def _rope_qk_weights(w_in):
    n_l = w_in.shape[0]
    quarter = B_DK // 4
    w_qk = w_in[:, :, :B_QK].astype(BF16).reshape(n_l, D_MODEL, 2 * B_HEADS, 2, 2, quarter)
    return jnp.swapaxes(w_qk, 3, 4).reshape(n_l, D_MODEL, B_QK)


def _gate_weights(w_in, b_gate):
    n_l = w_in.shape[0]
    order = jnp.array((0, 2, 1, 3))
    wg = w_in[:, :, A_MAIN:].reshape(n_l, D_MODEL, 4, A_HEADS)[:, :, order, :]
    wg_t = jnp.transpose(wg, (0, 2, 3, 1)).reshape(n_l, 2 * NG, D_MODEL).astype(BF16)
    bg = b_gate[:, order, :].reshape(n_l, 2 * NG, 1)
    return wg_t, bg


def kernel(x_prompt, x_sample, state_mlstm_C, state_mlstm_n, state_mlstm_m, state_ret_S, c, c_ctx,
           norm_gain, ada_w, ada_b, ml_w_in, ml_b_gate, ml_norm, ml_w_out,
           ret_w_in, ret_decay, ret_norm, ret_w_out, ffn_w_up, ffn_conv, ffn_conv_b, ffn_w_down):
    bp, tp, _ = x_prompt.shape
    bs, ts, _ = x_sample.shape
    n_a = ml_w_in.shape[0]
    n_b = ret_w_in.shape[0]

    cond = jnp.concatenate([c_ctx[None, :], c, jnp.zeros((MOD_ROWS - 1 - bs, D_MODEL), F32)], axis=0)
    mod5 = _modulation(cond, ada_w, ada_b).reshape(DEPTH, MOD_ROWS, 6, 1, D_MODEL)
    ng4 = norm_gain.reshape(DEPTH, 4, 1, D_MODEL)
    rope = _rope_tables(ts)

    ml_w_in_b = ml_w_in.astype(BF16)
    ml_gates = _gate_weights(ml_w_in, ml_b_gate)
    ml_w_out_b = ml_w_out.astype(BF16)
    ret_w_in_b = ret_w_in.astype(BF16)
    ret_w_qk_rope = _rope_qk_weights(ret_w_in)
    ret_w_out_b = ret_w_out.astype(BF16)
    ffn_w_up_b = ffn_w_up.astype(BF16)
    ffn_w_down_b = ffn_w_down.astype(BF16)
    ffn_conv_b3 = ffn_conv_b.reshape(DEPTH, 1, 2 * D_FF)
    ml_gn4 = ml_norm.reshape(n_a, A_HEADS, 1, A_DV)
    ret_gn4 = ret_norm.reshape(n_b, B_HEADS, 1, B_DV)
    dec_rep = jnp.broadcast_to(jnp.swapaxes(ret_decay, 1, 2)[..., None], (n_b, B_HEADS, 2, B_DV))
    st_c = state_mlstm_C
    st_n = state_mlstm_n.reshape(bs, n_a, 2, A_HEADS, A_DK, 1)
    st_m = state_mlstm_m.reshape(bs, n_a, 2, A_HEADS, 1, 1)

    groups = [
        dict(x=x_prompt.reshape(bp * tp, D_MODEL), batch=bp, seq_len=tp, sample=False),
        dict(x=x_sample.reshape(bs * ts, D_MODEL), batch=bs, seq_len=ts, sample=True),
    ]
    ml_states = None
    ret_states = None
    for i in range(DEPTH):
        j = i // N_MIXERS
        for grp in groups:
            x = grp["x"]
            geo = dict(seq_len=grp["seq_len"], sample=grp["sample"])
            bt = dict(batch=grp["batch"], seq_len=grp["seq_len"])
            if i % N_MIXERS == 0:
                n_qk = A_HEADS * A_DK // TN_PROJ
                z, gates = _inproj(x, ng4, mod5, i, [(ml_w_in_b, j, 0)], A_MAIN, n_q=n_qk, n_k=n_qk,
                                    k_scale=A_DK ** -0.5, gates=ml_gates + (j,), **geo)
                if grp["sample"]:
                    (h,) = _mlstm_scan(z, gates, ml_gn4, j, n_layers=n_a, state=(st_c, st_n, st_m), **bt)
                else:
                    h, *ml_states = _mlstm_scan(z, gates, ml_gn4, j, n_layers=n_a, prev=ml_states, **bt)
                x = _outproj(h, ml_w_out_b, j, x, ng4, mod5, i, **geo)
            else:
                n_qk = B_HEADS * B_DK // TN_PROJ
                w_qk = ret_w_qk_rope if grp["sample"] else ret_w_in_b
                w_parts = [(w_qk, j, 0), (ret_w_in_b, j, 1), (ret_w_in_b, j, 2)]
                (z,) = _inproj(x, ng4, mod5, i, w_parts, B_QK, n_q=n_qk, n_k=n_qk, k_scale=B_DK ** -0.5,
                               rope=rope if grp["sample"] else None, **geo)
                if grp["sample"]:
                    (h,) = _ret_scan(z, dec_rep, ret_gn4, j, n_layers=n_b, state=state_ret_S, **bt)
                else:
                    h, ret_states = _ret_scan(z, dec_rep, ret_gn4, j, n_layers=n_b, prev=ret_states, **bt)
                x = _outproj(h, ret_w_out_b, j, x, ng4, mod5, i, **geo)
            grp["x"] = _ffn(x, ng4, mod5, i, ffn_w_up_b, ffn_conv, ffn_conv_b3, ffn_w_down_b, **geo)

    y_prompt = groups[0]["x"].reshape(bp, tp, D_MODEL)
    y_sample = groups[1]["x"].reshape(bs, ts, D_MODEL)
    new_c, new_n, new_m = ml_states
    return (y_prompt, y_sample, new_c, new_n.reshape(bp, n_a, 2, A_HEADS, A_DK),
            new_m.reshape(bp, n_a, 2, A_HEADS), ret_states)
```

```python
import functools
import math

import jax
import jax.numpy as jnp
from jax import lax
from jax.experimental import pallas as pl
from jax.experimental.pallas import tpu as pltpu

D_MODEL = 1024
DEPTH = 4
GRID_W = 64
CHUNK = 128
N_MIXERS = 2
A_HEADS = 4
A_DV = D_MODEL // A_HEADS
A_DK = A_DV // 2
A_MAIN = 2 * A_HEADS * A_DK + 2 * A_HEADS * A_DV
B_HEADS = 8
B_DK = D_MODEL // B_HEADS
B_DV = 2 * D_MODEL // B_HEADS
B_QK = 2 * B_HEADS * B_DK
ROPE_BASE = 10000.0
D_FF = ((8 * D_MODEL // 3 + 127) // 128) * 128
EPS = 1e-6
LN2 = math.log(2.0)

F32 = jnp.float32
BF16 = jnp.bfloat16

LANES = 128
SUBLANES = 8
MOD_ROWS = 16
VMEM_LIMIT = 48 * 1024 * 1024

TM_PROJ = 512
TN_PROJ = 512
TM_OUT = 1024
TR_OUT = 512
TM_FFN = 512
TF_FFN = 256
GATE_UNROLL = 16
STATE_UNROLL = 16
OUT_UNROLL = 16
MLSTM_OUT_UNROLL = 16
SCAN_TOKENS = 2048


def _dot(a, b):
    return jnp.dot(a, b, preferred_element_type=F32)


def _dot_nt(a, b):
    return lax.dot_general(a, b, (((1,), (1,)), ((), ())), preferred_element_type=F32)


def _dot_tn(a, b):
    return lax.dot_general(a, b, (((0,), (0,)), ((), ())), preferred_element_type=F32)


def _rms(x):
    return x * lax.rsqrt(jnp.mean(x * x, axis=-1, keepdims=True) + EPS)


def _layer_norm(h):
    d = h - jnp.mean(h, axis=-1, keepdims=True)
    return d * lax.rsqrt(jnp.mean(d * d, axis=-1, keepdims=True) + EPS)


def _params(*sem):
    return pltpu.CompilerParams(dimension_semantics=sem, vmem_limit_bytes=VMEM_LIMIT)


def _resident(shape, index):
    return pl.BlockSpec(shape, lambda *_: index, pipeline_mode=pl.Buffered(1))


def _mod_kernel(cond_ref, w_ref, b_ref, o_ref):
    cnd = cond_ref[...]
    s = cnd * jax.nn.sigmoid(cnd)
    o_ref[...] = _dot(s.astype(BF16), w_ref[...].astype(BF16)) + b_ref[...]


def _modulation(cond, ada_w, ada_b):
    tn = 1024
    n_out = ada_w.shape[-1]
    return pl.pallas_call(
        _mod_kernel,
        grid=(DEPTH, n_out // tn),
        in_specs=[
            pl.BlockSpec((MOD_ROWS, D_MODEL), lambda l, j: (0, 0)),
            pl.BlockSpec((None, D_MODEL, tn), lambda l, j: (l, 0, j)),
            pl.BlockSpec((None, 1, tn), lambda l, j: (l, 0, j)),
        ],
        out_specs=pl.BlockSpec((None, MOD_ROWS, tn), lambda l, j: (l, 0, j)),
        out_shape=jax.ShapeDtypeStruct((DEPTH, MOD_ROWS, n_out), F32),
        compiler_params=_params("parallel", "parallel"),
        name="modulation",
    )(cond, ada_w, ada_b.reshape(DEPTH, 1, n_out))


def _mod_row(sample, seq_len, tm):
    if not sample:
        return lambda i: 0
    tiles_per_seq = seq_len // tm
    return lambda i: 1 + i // tiles_per_seq


def _rope_slab(x, cos, sin):
    return x * cos + pltpu.roll(x, 64, axis=1) * sin


def _inproj_kernel(*refs, n_w, n_q, n_k, k_scale, rope, gates, tn):
    x_ref, g_ref, sh_ref, sc_ref = refs[:4]
    w_refs = refs[4:4 + n_w]
    pos = 4 + n_w
    if gates:
        wg_ref, bg_ref = refs[pos:pos + 2]
        pos += 2
    if rope:
        cos_ref, sin_ref = refs[pos:pos + 2]
        pos += 2
    z_ref = refs[pos]
    pos += 1
    if gates:
        gates_ref = refs[pos]
        pos += 1
    u_sc = refs[pos]

    u = _rms(x_ref[...]) * g_ref[...] * (1.0 + sc_ref[...]) + sh_ref[...]
    u_sc[...] = u.astype(BF16)
    if gates:
        gates_ref[...] = _dot_nt(wg_ref[...], u_sc[...]) + bg_ref[...]

    wb = w_refs[0].shape[1]
    for j in range(z_ref.shape[1] // tn):
        part, off = divmod(j * tn, wb)
        z = _dot(u_sc[...], w_refs[part][:, off:off + tn])
        scale = k_scale if n_q <= j < n_q + n_k else 1.0
        if rope and j < n_q + n_k:
            for s in range(tn // LANES):
                r = _rope_slab(z[:, s * LANES:(s + 1) * LANES], cos_ref[...], sin_ref[...])
                if scale != 1.0:
                    r = r * scale
                z_ref[:, j * tn + s * LANES:j * tn + (s + 1) * LANES] = r.astype(BF16)
        elif scale != 1.0:
            z_ref[:, j * tn:(j + 1) * tn] = (z * scale).astype(BF16)
        else:
            z_ref[:, j * tn:(j + 1) * tn] = z.astype(BF16)


def _inproj(x, ng4, mod5, layer, w_parts, wb, *, seq_len, sample, n_q, n_k, k_scale, rope=None, gates=None):
    n_tok = x.shape[0]
    tm, tn = TM_PROJ, TN_PROJ
    n_col = wb * len(w_parts)
    row = _mod_row(sample, seq_len, tm)
    in_specs = [
        pl.BlockSpec((tm, D_MODEL), lambda i: (i, 0)),
        pl.BlockSpec((None, None, 1, D_MODEL), lambda i: (layer, 0, 0, 0)),
        pl.BlockSpec((None, None, None, 1, D_MODEL), lambda i: (layer, row(i), 0, 0, 0)),
        pl.BlockSpec((None, None, None, 1, D_MODEL), lambda i: (layer, row(i), 1, 0, 0)),
    ]
    in_specs += [_resident((None, D_MODEL, wb), (jl, 0, blk)) for _, jl, blk in w_parts]
    args = [x, ng4, mod5, mod5] + [w for w, _, _ in w_parts]
    out_specs = [pl.BlockSpec((tm, n_col), lambda i: (i, 0))]
    out_shape = [jax.ShapeDtypeStruct((n_tok, n_col), BF16)]
    if gates is not None:
        wg_t, bg, jg = gates
        n_g = wg_t.shape[1]
        in_specs += [_resident((None, n_g, D_MODEL), (jg, 0, 0)), _resident((None, n_g, 1), (jg, 0, 0))]
        args += [wg_t, bg]
        out_specs += [pl.BlockSpec((n_g, tm), lambda i: (0, i))]
        out_shape += [jax.ShapeDtypeStruct((n_g, n_tok), F32)]
    if rope is not None:
        tiles_per_seq = seq_len // tm
        in_specs += [pl.BlockSpec((tm, LANES), lambda i: (i % tiles_per_seq, 0))] * 2
        args += list(rope)
    kern = functools.partial(_inproj_kernel, n_w=len(w_parts), n_q=n_q, n_k=n_k, k_scale=k_scale,
                             rope=rope is not None, gates=gates is not None, tn=tn)
    return pl.pallas_call(
        kern,
        grid=(n_tok // tm,),
        in_specs=in_specs,
        out_specs=out_specs,
        out_shape=out_shape,
        scratch_shapes=[pltpu.VMEM((tm, D_MODEL), BF16)],
        compiler_params=_params("parallel"),
        name="inproj",
    )(*args)


def _tri_masks():
    li = lax.broadcasted_iota(jnp.int32, (CHUNK, CHUNK), 0)
    si = lax.broadcasted_iota(jnp.int32, (CHUNK, CHUNK), 1)
    return si <= li, si >= li


def _layer_slot(ref, fresh_slot):
    if fresh_slot is None:
        return ref
    for other in range(ref.shape[1]):
        if other != fresh_slot:
            ref[:, other] = jnp.zeros(ref.shape[:1] + ref.shape[2:], ref.dtype)
    return ref.at[:, fresh_slot]


def _seqs_per_step(batch, seq_len, carry_in):
    if carry_in:
        return 1
    nb = max(1, SCAN_TOKENS // seq_len)
    while batch % nb:
        nb -= 1
    return nb


NG = 2 * A_HEADS


def _split_dot(x, mask_b):
    hi = x.astype(BF16)
    r1 = x - hi.astype(F32)
    mid = r1.astype(BF16)
    lo = (r1 - mid.astype(F32)).astype(BF16)
    return _dot(hi, mask_b) + _dot(mid, mask_b) + _dot(lo, mask_b)


def _mlstm_kernel(*refs, n_chunks, nb, carry_in, carry_out, n_alias, fresh_slot):
    q_ref, k_ref, v_ref, o_ref, g_ref, gn_ref = refs[:6]
    pos = 6
    if carry_in:
        c0_ref, n0_ref, m0_ref = refs[pos:pos + 3]
        pos += 3
    pos += n_alias
    h_ref = refs[pos]
    pos += 1
    if carry_out:
        cout_ref, nout_ref, mout_ref = refs[pos:pos + 3]
        pos += 3
    ab_sc, bt_sc, g_sc, bm_sc, mpf_sc, mpb_sc, c_sc, call_sc = refs[pos:pos + 8]

    head = pl.program_id(1)
    masks = _tri_masks()
    ones_b = jnp.ones((CHUNK, LANES), BF16)
    sum_b = jnp.concatenate([masks[1].astype(BF16), ones_b], axis=1)
    lane = lax.broadcasted_iota(jnp.int32, (CHUNK, LANES), 1)
    grow = lax.broadcasted_iota(jnp.int32, (2 * NG, CHUNK), 0)
    gsub = lax.broadcasted_iota(jnp.int32, (NG, LANES), 0)
    zpad = jnp.zeros((LANES - 2 * NG, CHUNK), F32)
    cols = (head, head + A_HEADS)
    mp_sc = (mpf_sc, mpb_sc)
    t_seq = n_chunks * CHUNK

    def pick_col(x, col):
        return jnp.sum(jnp.where(lane == col, x, 0.0), axis=1, keepdims=True)

    def at(s, c):
        return pl.ds(pl.multiple_of(s * t_seq + c * CHUNK, CHUNK), CHUNK)

    def gate_body(c, carry):
        for s in range(nb):
            idx = s * n_chunks + c
            gates = g_ref[:, at(s, c)]
            lf = jnp.minimum(gates, 0.0) - jnp.log1p(jnp.exp(-jnp.abs(gates)))
            lf = jnp.where(grow >= NG, lf, 0.0)
            sums = _split_dot(lf, sum_b)
            a_f = sums[:, :CHUNK]
            tot = sums[:, CHUNK:]
            a_all = jnp.where(grow < NG + A_HEADS, a_f, tot - a_f + lf)[NG:, :]
            b_all = gates[:NG, :] - a_all
            bt_sc[idx] = b_all
            ab_sc[at(s, c), :] = jnp.concatenate([a_all, b_all, zpad], axis=0).T
            g_sc[idx] = tot[NG:, :]
            bm_sc[idx] = jnp.broadcast_to(jnp.max(b_all, axis=1, keepdims=True), (NG, LANES))
        return carry

    lax.fori_loop(0, n_chunks, gate_body, 0, unroll=min(n_chunks, GATE_UNROLL))

    if carry_in:
        m_init = (jnp.broadcast_to(m0_ref[0], (NG, LANES)), jnp.broadcast_to(m0_ref[1], (NG, LANES)))
    else:
        m_init = (jnp.zeros((NG, LANES), F32),) * (2 * nb)

    def m_body(i, carry):
        out = []
        for s in range(nb):
            m_f, m_b = carry[2 * s], carry[2 * s + 1]
            jf = s * n_chunks + i
            jb = s * n_chunks + n_chunks - 1 - i
            mpf_sc[jf] = m_f
            mpb_sc[jb] = m_b
            out.append(g_sc[jf] + jnp.maximum(m_f, bm_sc[jf]))
            out.append(g_sc[jb] + jnp.maximum(m_b, bm_sc[jb]))
        return tuple(out)

    m_last = lax.fori_loop(0, n_chunks, m_body, m_init, unroll=min(n_chunks, STATE_UNROLL))

    if carry_in:
        for dirn in range(2):
            c_sc[dirn, :, :A_DV] = c0_ref[dirn]
            c_sc[dirn, :, A_DV:] = jnp.broadcast_to(n0_ref[dirn], (A_DK, LANES))
    else:
        c_sc[...] = jnp.zeros_like(c_sc)

    def state_body(i, carry):
        for s in range(nb):
            for dirn in range(2):
                c = i if dirn == 0 else n_chunks - 1 - i
                idx = s * n_chunks + c
                mp_row = mp_sc[dirn][idx, pl.ds(cols[dirn], 1), :]
                m_top = jnp.maximum(mp_row, bm_sc[idx, pl.ds(cols[dirn], 1), :])
                ws = jnp.exp(pick_col(ab_sc[at(s, c), :], NG + cols[dirn]) - m_top)
                dec = jnp.exp(mp_row - m_top)
                dec = jnp.concatenate([dec] * (c_sc.shape[2] // LANES), axis=1)
                c_old = c_sc[2 * s + dirn]
                call_sc[2 * s + dirn, c] = c_old.astype(BF16)
                kw = (k_ref[at(s, c), :].astype(F32) * ws).astype(BF16)
                upd = jnp.concatenate([_dot_tn(kw, v_ref[at(s, c), :]), _dot_tn(kw, ones_b)], axis=1)
                c_sc[2 * s + dirn] = dec * c_old + upd
        return carry

    lax.fori_loop(0, n_chunks, state_body, 0, unroll=min(n_chunks, STATE_UNROLL))
    if carry_out:
        outs = [_layer_slot(r, fresh_slot) for r in (cout_ref, nout_ref, mout_ref)]
        for s in range(nb):
            for dirn in range(2):
                outs[0][s, dirn] = c_sc[2 * s + dirn, :, :A_DV]
                outs[1][s, dirn] = c_sc[2 * s + dirn, :, A_DV:].T[0:1, :]
                m_end = jnp.where(gsub == cols[dirn], m_last[2 * s + dirn], 0.0)
                outs[2][s, dirn] = jnp.sum(m_end, axis=0, keepdims=True)[:, 0:1]

    def out_body(c, carry):
        for s in range(nb):
            q = q_ref[at(s, c), :]
            qf = q.astype(F32)
            v_ext = jnp.concatenate([v_ref[at(s, c), :], ones_b], axis=1)
            s_raw = _dot_nt(q, k_ref[at(s, c), :])
            a_chunk = ab_sc[at(s, c), :]
            idx = s * n_chunks + c
            h = None
            for dirn in range(2):
                col = cols[dirn]
                m_prev = mp_sc[dirn][idx, pl.ds(col, 1), :]
                b_vis = jnp.where(masks[dirn], bt_sc[idx, pl.ds(col, 1), :], -jnp.inf)
                m_row = jnp.maximum(m_prev, jnp.max(b_vis, axis=1, keepdims=True))
                sw = (s_raw * jnp.exp(b_vis - m_row)).astype(BF16)
                w_inter = jnp.exp(m_prev - m_row)
                floor = jnp.exp(-(pick_col(a_chunk, col) + m_row))
                qw = (qf * w_inter).astype(BF16)
                nd = _dot(sw, v_ext) + _dot(qw, call_sc[2 * s + dirn, c])
                r = 1.0 / jnp.maximum(jnp.abs(nd[:, A_DV:]), floor)
                hd = nd[:, :A_DV] * jnp.concatenate([r, r], axis=1)
                h = hd if h is None else h + hd
            o = o_ref[at(s, c), :].astype(F32)
            h_ref[at(s, c), :] = (_layer_norm(h) * gn_ref[...] * jax.nn.sigmoid(o)).astype(BF16)
        return carry

    lax.fori_loop(0, n_chunks, out_body, 0, unroll=min(n_chunks, max(1, MLSTM_OUT_UNROLL // nb)))


def _mlstm_scan(z, gates, gn4, j, *, batch, seq_len, n_layers, state=None, prev=None):
    n_tok = z.shape[0]
    n_chunks = seq_len // CHUNK
    carry_in = state is not None
    carry_out = not carry_in
    nb = _seqs_per_step(batch, seq_len, carry_in)
    t = nb * seq_len
    in_specs = [
        pl.BlockSpec((t, A_DK), lambda b, h: (b, h)),
        pl.BlockSpec((t, A_DK), lambda b, h: (b, A_HEADS + h)),
        pl.BlockSpec((t, A_DV), lambda b, h: (b, A_HEADS + h)),
        pl.BlockSpec((t, A_DV), lambda b, h: (b, 2 * A_HEADS + h)),
        pl.BlockSpec((2 * NG, t), lambda b, h: (0, b)),
        pl.BlockSpec((None, None, 1, A_DV), lambda b, h: (j, h, 0, 0)),
    ]
    args = [z, z, z, z, gates, gn4]
    aliases = {}
    if carry_in:
        in_specs += [
            pl.BlockSpec((None, None, 2, None, A_DK, A_DV), lambda b, h: (b, j, 0, h, 0, 0)),
            pl.BlockSpec((None, None, 2, None, A_DK, 1), lambda b, h: (b, j, 0, h, 0, 0)),
            pl.BlockSpec((None, None, 2, None, 1, 1), lambda b, h: (b, j, 0, h, 0, 0)),
        ]
        args += list(state)
    out_specs = [pl.BlockSpec((t, A_DV), lambda b, h: (b, h))]
    out_shape = [jax.ShapeDtypeStruct((n_tok, A_HEADS * A_DV), BF16)]
    fresh_slot = j if carry_out and prev is None else None
    if carry_out:
        lay, jb = (n_layers, 0) if prev is None else (None, j)
        out_specs += [
            pl.BlockSpec((nb, lay, 2, None, A_DK, A_DV), lambda b, h: (b, jb, 0, h, 0, 0)),
            pl.BlockSpec((nb, lay, 2, None, 1, A_DK), lambda b, h: (b, jb, 0, h, 0, 0)),
            pl.BlockSpec((nb, lay, 2, None, 1, 1), lambda b, h: (b, jb, 0, h, 0, 0)),
        ]
        out_shape += [
            jax.ShapeDtypeStruct((batch, n_layers, 2, A_HEADS, A_DK, A_DV), F32),
            jax.ShapeDtypeStruct((batch, n_layers, 2, A_HEADS, 1, A_DK), F32),
            jax.ShapeDtypeStruct((batch, n_layers, 2, A_HEADS, 1, 1), F32),
        ]
        if prev is not None:
            aliases = {len(args) + k: 1 + k for k in range(3)}
            in_specs += [pl.BlockSpec(memory_space=pl.ANY)] * 3
            args += list(prev)
    kern = functools.partial(_mlstm_kernel, n_chunks=n_chunks, nb=nb, carry_in=carry_in,
                             carry_out=carry_out, n_alias=len(aliases), fresh_slot=fresh_slot)
    return pl.pallas_call(
        kern,
        grid=(batch // nb, A_HEADS),
        in_specs=in_specs,
        out_specs=out_specs,
        out_shape=out_shape,
        input_output_aliases=aliases,
        scratch_shapes=[
            pltpu.VMEM((t, LANES), F32),
            pltpu.VMEM((nb * n_chunks, NG, CHUNK), F32),
            pltpu.VMEM((nb * n_chunks, NG, LANES), F32),
            pltpu.VMEM((nb * n_chunks, NG, LANES), F32),
            pltpu.VMEM((nb * n_chunks, NG, LANES), F32),
            pltpu.VMEM((nb * n_chunks, NG, LANES), F32),
            pltpu.VMEM((2 * nb, A_DK, A_DV + LANES), F32),
            pltpu.VMEM((2 * nb, n_chunks, A_DK, A_DV + LANES), BF16),
        ],
        compiler_params=_params("parallel", "parallel"),
        name="mlstm_scan",
    )(*args)


def _ret_kernel(*refs, n_chunks, nb, carry_in, carry_out, n_alias, fresh_slot):
    q_ref, k_ref, v_ref, gate_ref, dec_ref, gn_ref = refs[:6]
    pos = 6
    if carry_in:
        s0_ref = refs[pos]
        pos += 1
    pos += n_alias
    h_ref = refs[pos]
    pos += 1
    if carry_out:
        sout_ref = refs[pos]
        pos += 1
    s_sc, sall_sc, dsum_sc, xi_sc, zeta_sc = refs[pos:pos + 5]
    t_seq = n_chunks * CHUNK

    def at(s, c):
        return pl.ds(pl.multiple_of(s * t_seq + c * CHUNK, CHUNK), CHUNK)

    lg = jnp.log1p(-jnp.exp(-dec_ref[...] * LN2))
    lg_f = lg[0:1, :]
    lg_b = lg[1:2, :]

    @pl.when(pl.program_id(1) == 0)
    def _():
        masks = _tri_masks()
        li = lax.broadcasted_iota(jnp.int32, (CHUNK, B_DV), 0).astype(F32)
        si = lax.broadcasted_iota(jnp.int32, (CHUNK, CHUNK), 1).astype(F32)
        lq = li[:, :CHUNK]
        dsum_sc[...] = (
            jnp.where(masks[0], jnp.exp(jnp.where(masks[0], lq - si, 0.0) * lg_f[:, :CHUNK]), 0.0)
            + jnp.where(masks[1], jnp.exp(jnp.where(masks[1], si - lq, 0.0) * lg_b[:, :CHUNK]), 0.0))
        xi_sc[0] = jnp.exp((lq + 1.0) * lg_f[:, :CHUNK])
        xi_sc[1] = jnp.exp((CHUNK - lq) * lg_b[:, :CHUNK])
        zeta_sc[0] = jnp.exp((CHUNK - 1.0 - lq) * lg_f[:, :CHUNK])
        zeta_sc[1] = jnp.exp(lq * lg_b[:, :CHUNK])

    cdec = (jnp.exp(CHUNK * lg_f), jnp.exp(CHUNK * lg_b))

    if carry_in:
        qr = B_DK // 4
        for dirn in range(2):
            for n, o in enumerate((0, 2, 1, 3)):
                s_sc[dirn, n * qr:(n + 1) * qr, :] = s0_ref[dirn, o * qr:(o + 1) * qr, :]
    else:
        s_sc[...] = jnp.zeros_like(s_sc)

    def state_body(i, carry):
        for s in range(nb):
            for dirn in range(2):
                c = i if dirn == 0 else n_chunks - 1 - i
                s_old = s_sc[2 * s + dirn]
                sall_sc[s, c, dirn * B_DK:(dirn + 1) * B_DK, :] = s_old.astype(BF16)
                kz = (k_ref[at(s, c), :].astype(F32) * zeta_sc[dirn]).astype(BF16)
                s_sc[2 * s + dirn] = cdec[dirn] * s_old + _dot_tn(kz, v_ref[at(s, c), :])
        return carry

    lax.fori_loop(0, n_chunks, state_body, 0, unroll=min(n_chunks, STATE_UNROLL))
    if carry_out:
        s_out = _layer_slot(sout_ref, fresh_slot)
        for s in range(nb):
            for dirn in range(2):
                s_out[s, dirn] = s_sc[2 * s + dirn]

    def out_body(c, carry):
        for s in range(nb):
            q = q_ref[at(s, c), :]
            v = v_ref[at(s, c), :]
            sw = _dot_nt(q, k_ref[at(s, c), :]) * dsum_sc[...]
            qf = q.astype(F32)
            qx = jnp.concatenate([qf * xi_sc[0], qf * xi_sc[1]], axis=1).astype(BF16)
            h = _dot(sw.astype(BF16), v) + _dot(qx, sall_sc[s, c])
            g = gate_ref[at(s, c), :].astype(F32)
            h_ref[at(s, c), :] = (_layer_norm(h) * gn_ref[...] * (g * jax.nn.sigmoid(g))).astype(BF16)
        return carry

    lax.fori_loop(0, n_chunks, out_body, 0, unroll=min(n_chunks, max(1, OUT_UNROLL // nb)))


def _ret_scan(z, dec_rep, gn4, j, *, batch, seq_len, n_layers, state=None, prev=None):
    n_tok = z.shape[0]
    n_chunks = seq_len // CHUNK
    carry_in = state is not None
    carry_out = not carry_in
    nb = _seqs_per_step(batch, seq_len, carry_in)
    t = nb * seq_len
    in_specs = [
        pl.BlockSpec((t, B_DK), lambda h, b: (b, h)),
        pl.BlockSpec((t, B_DK), lambda h, b: (b, B_HEADS + h)),
        pl.BlockSpec((t, B_DV), lambda h, b: (b, B_HEADS + h)),
        pl.BlockSpec((t, B_DV), lambda h, b: (b, 2 * B_HEADS + h)),
        pl.BlockSpec((None, None, 2, B_DV), lambda h, b: (j, h, 0, 0)),
        pl.BlockSpec((None, None, 1, B_DV), lambda h, b: (j, h, 0, 0)),
    ]
    args = [z, z, z, z, dec_rep, gn4]
    aliases = {}
    if carry_in:
        in_specs += [pl.BlockSpec((None, None, 2, None, B_DK, B_DV), lambda h, b: (b, j, 0, h, 0, 0))]
        args += [state]
    out_specs = [pl.BlockSpec((t, B_DV), lambda h, b: (b, h))]
    out_shape = [jax.ShapeDtypeStruct((n_tok, B_HEADS * B_DV), BF16)]
    fresh_slot = j if carry_out and prev is None else None
    if carry_out:
        lay, jb = (n_layers, 0) if prev is None else (None, j)
        out_specs += [pl.BlockSpec((nb, lay, 2, None, B_DK, B_DV), lambda h, b: (b, jb, 0, h, 0, 0))]
        out_shape += [jax.ShapeDtypeStruct((batch, n_layers, 2, B_HEADS, B_DK, B_DV), F32)]
        if prev is not None:
            aliases = {len(args): 1}
            in_specs += [pl.BlockSpec(memory_space=pl.ANY)]
            args += [prev]
    kern = functools.partial(_ret_kernel, n_chunks=n_chunks, nb=nb, carry_in=carry_in,
                             carry_out=carry_out, n_alias=len(aliases), fresh_slot=fresh_slot)
    return pl.pallas_call(
        kern,
        grid=(B_HEADS, batch // nb),
        in_specs=in_specs,
        out_specs=out_specs,
        out_shape=out_shape,
        input_output_aliases=aliases,
        scratch_shapes=[
            pltpu.VMEM((2 * nb, B_DK, B_DV), F32),
            pltpu.VMEM((nb, n_chunks, 2 * B_DK, B_DV), BF16),
            pltpu.VMEM((CHUNK, CHUNK), F32),
            pltpu.VMEM((2, CHUNK, B_DK), F32),
            pltpu.VMEM((2, CHUNK, B_DK), F32),
        ],
        compiler_params=_params("parallel", "arbitrary"),
        name="ret_scan",
    )(*args)


def _outproj_kernel(h_ref, w_ref, x_ref, g_ref, gate_ref, o_ref):
    for r in range(o_ref.shape[0] // TR_OUT):
        rows = slice(r * TR_OUT, (r + 1) * TR_OUT)
        y = _dot(h_ref[rows, :], w_ref[...])
        o_ref[rows, :] = x_ref[rows, :] + gate_ref[...] * (_rms(y) * g_ref[...])


def _outproj(h, w, j, x, ng4, mod5, layer, *, seq_len, sample):
    n_tok, hv = h.shape
    tm = TM_OUT
    row = _mod_row(sample, seq_len, tm)
    return pl.pallas_call(
        _outproj_kernel,
        grid=(n_tok // tm,),
        in_specs=[
            pl.BlockSpec((tm, hv), lambda i: (i, 0)),
            _resident((None, hv, D_MODEL), (j, 0, 0)),
            pl.BlockSpec((tm, D_MODEL), lambda i: (i, 0)),
            pl.BlockSpec((None, None, 1, D_MODEL), lambda i: (layer, 1, 0, 0)),
            pl.BlockSpec((None, None, None, 1, D_MODEL), lambda i: (layer, row(i), 2, 0, 0)),
        ],
        out_specs=pl.BlockSpec((tm, D_MODEL), lambda i: (i, 0)),
        out_shape=jax.ShapeDtypeStruct((n_tok, D_MODEL), F32),
        compiler_params=_params("parallel"),
        name="outproj",
    )(h, w, x, ng4, mod5)


def _conv3(hs_ref, half, h, cw, cb, seg, n_seg):
    for s in range(n_seg):
        base = SUBLANES + s * (seg + SUBLANES)
        h_seg = h[s * seg:(s + 1) * seg, :]
        hs_ref[2 * half, base + 1:base + 1 + seg, :] = h_seg
        hs_ref[2 * half + 1, base - 1:base - 1 + seg, :] = h_seg
    parts = []
    for s in range(n_seg):
        base = SUBLANES + s * (seg + SUBLANES)
        h_prev = hs_ref[2 * half, base:base + seg, :]
        h_next = hs_ref[2 * half + 1, base:base + seg, :]
        h_mid = h[s * seg:(s + 1) * seg, :]
        parts.append(h_prev * cw[0:1, :] + h_mid * cw[1:2, :] + h_next * cw[2:3, :] + cb)
    return parts


def _ffn_kernel(x_ref, g2_ref, sh_ref, sc_ref, wup_ref, cw_ref, cb_ref, wd_ref, g3_ref, gate_ref,
                o_ref, u_sc, act_sc, hs_sc, *, seg, n_seg, tf):
    u = _rms(x_ref[...]) * g2_ref[...] * (1.0 + sc_ref[...]) + sh_ref[...]
    u_sc[...] = u.astype(BF16)
    zero_rows = jnp.zeros((SUBLANES, tf), F32)
    for s in range(n_seg):
        base = SUBLANES + s * (seg + SUBLANES)
        for half in range(2):
            hs_sc[2 * half, base:base + SUBLANES, :] = zero_rows
            hs_sc[2 * half + 1, base + seg - SUBLANES:base + seg, :] = zero_rows

    for cidx in range(D_FF // tf):
        cg = slice(cidx * tf, (cidx + 1) * tf)
        cu = slice(D_FF + cidx * tf, D_FF + (cidx + 1) * tf)
        hg = _conv3(hs_sc, 0, _dot(u_sc[...], wup_ref[:, cg]), cw_ref[:, cg], cb_ref[:, cg], seg, n_seg)
        hu = _conv3(hs_sc, 1, _dot(u_sc[...], wup_ref[:, cu]), cw_ref[:, cu], cb_ref[:, cu], seg, n_seg)
        for s in range(n_seg):
            act = jax.nn.gelu(hg[s], approximate=True) * hu[s]
            act_sc[s * seg:(s + 1) * seg, cg] = act.astype(BF16)

    f = _dot(act_sc[...], wd_ref[...])
    o_ref[...] = x_ref[...] + gate_ref[...] * (_rms(f) * g3_ref[...])


def _ffn(x, ng4, mod5, layer, w_up, conv_w, conv_b, w_down, *, seq_len, sample):
    n_tok = x.shape[0]
    tm, tf = TM_FFN, TF_FFN
    row = _mod_row(sample, seq_len, tm)
    seg = GRID_W if sample else seq_len
    n_seg = tm // seg
    kern = functools.partial(_ffn_kernel, seg=seg, n_seg=n_seg, tf=tf)
    mod_spec = lambda k: pl.BlockSpec((None, None, None, 1, D_MODEL), lambda i: (layer, row(i), k, 0, 0))
    gain_spec = lambda k: pl.BlockSpec((None, None, 1, D_MODEL), lambda i: (layer, k, 0, 0))
    return pl.pallas_call(
        kern,
        grid=(n_tok // tm,),
        in_specs=[
            pl.BlockSpec((tm, D_MODEL), lambda i: (i, 0)),
            gain_spec(2),
            mod_spec(3),
            mod_spec(4),
            _resident((None, D_MODEL, 2 * D_FF), (layer, 0, 0)),
            _resident((None, 3, 2 * D_FF), (layer, 0, 0)),
            _resident((None, 1, 2 * D_FF), (layer, 0, 0)),
            _resident((None, D_FF, D_MODEL), (layer, 0, 0)),
            gain_spec(3),
            mod_spec(5),
        ],
        out_specs=pl.BlockSpec((tm, D_MODEL), lambda i: (i, 0)),
        out_shape=jax.ShapeDtypeStruct((n_tok, D_MODEL), F32),
        scratch_shapes=[
            pltpu.VMEM((tm, D_MODEL), BF16),
            pltpu.VMEM((tm, D_FF), BF16),
            pltpu.VMEM((4, SUBLANES + n_seg * (seg + SUBLANES), tf), F32),
        ],
        compiler_params=_params("parallel"),
        name="convffn",
    )(x, ng4, mod5, mod5, w_up, conv_w, conv_b, w_down, ng4, mod5)


def _rope_tables(seq_len):
    quarter = B_DK // 4
    inv = ROPE_BASE ** (-jnp.arange(quarter, dtype=F32) / quarter)
    t = jnp.arange(seq_len)
    rows = (t // GRID_W).astype(F32)[:, None] * inv
    cols = (t % GRID_W).astype(F32)[:, None] * inv
    cos = jnp.concatenate([jnp.cos(rows), jnp.cos(cols)] * 2, axis=-1)
    sin = jnp.concatenate([-jnp.sin(rows), -jnp.sin(cols), jnp.sin(rows), jnp.sin(cols)], axis=-1)
    return cos, sin


def _rope_qk_weights(w_in):
    n_l = w_in.shape[0]
    quarter = B_DK // 4
    w_qk = w_in[:, :, :B_QK].astype(BF16).reshape(n_l, D_MODEL, 2 * B_HEADS, 2, 2, quarter)
    return jnp.swapaxes(w_qk, 3, 4).reshape(n_l, D_MODEL, B_QK)


def _gate_weights(w_in, b_gate):
    n_l = w_in.shape[0]
    order = jnp.array((0, 2, 1, 3))
    wg = w_in[:, :, A_MAIN:].reshape(n_l, D_MODEL, 4, A_HEADS)[:, :, order, :]
    wg_t = jnp.transpose(wg, (0, 2, 3, 1)).reshape(n_l, 2 * NG, D_MODEL).astype(BF16)
    bg = b_gate[:, order, :].reshape(n_l, 2 * NG, 1)
    return wg_t, bg


def kernel(x_prompt, x_sample, state_mlstm_C, state_mlstm_n, state_mlstm_m, state_ret_S, c, c_ctx,
           norm_gain, ada_w, ada_b, ml_w_in, ml_b_gate, ml_norm, ml_w_out,
           ret_w_in, ret_decay, ret_norm, ret_w_out, ffn_w_up, ffn_conv, ffn_conv_b, ffn_w_down):
    bp, tp, _ = x_prompt.shape
    bs, ts, _ = x_sample.shape
    n_a = ml_w_in.shape[0]
    n_b = ret_w_in.shape[0]

    cond = jnp.concatenate([c_ctx[None, :], c, jnp.zeros((MOD_ROWS - 1 - bs, D_MODEL), F32)], axis=0)
    mod5 = _modulation(cond, ada_w, ada_b).reshape(DEPTH, MOD_ROWS, 6, 1, D_MODEL)
    ng4 = norm_gain.reshape(DEPTH, 4, 1, D_MODEL)
    rope = _rope_tables(ts)

    ml_w_in_b = ml_w_in.astype(BF16)
    ml_gates = _gate_weights(ml_w_in, ml_b_gate)
    ml_w_out_b = ml_w_out.astype(BF16)
    ret_w_in_b = ret_w_in.astype(BF16)
    ret_w_qk_rope = _rope_qk_weights(ret_w_in)
    ret_w_out_b = ret_w_out.astype(BF16)
    ffn_w_up_b = ffn_w_up.astype(BF16)
    ffn_w_down_b = ffn_w_down.astype(BF16)
    ffn_conv_b3 = ffn_conv_b.reshape(DEPTH, 1, 2 * D_FF)
    ml_gn4 = ml_norm.reshape(n_a, A_HEADS, 1, A_DV)
    ret_gn4 = ret_norm.reshape(n_b, B_HEADS, 1, B_DV)
    dec_rep = jnp.broadcast_to(jnp.swapaxes(ret_decay, 1, 2)[..., None], (n_b, B_HEADS, 2, B_DV))
    st_c = state_mlstm_C
    st_n = state_mlstm_n.reshape(bs, n_a, 2, A_HEADS, A_DK, 1)
    st_m = state_mlstm_m.reshape(bs, n_a, 2, A_HEADS, 1, 1)

    groups = [
        dict(x=x_prompt.reshape(bp * tp, D_MODEL), batch=bp, seq_len=tp, sample=False),
        dict(x=x_sample.reshape(bs * ts, D_MODEL), batch=bs, seq_len=ts, sample=True),
    ]
    ml_states = None
    ret_states = None
    for i in range(DEPTH):
        j = i // N_MIXERS
        for grp in groups:
            x = grp["x"]
            geo = dict(seq_len=grp["seq_len"], sample=grp["sample"])
            bt = dict(batch=grp["batch"], seq_len=grp["seq_len"])
            if i % N_MIXERS == 0:
                n_qk = A_HEADS * A_DK // TN_PROJ
                z, gates = _inproj(x, ng4, mod5, i, [(ml_w_in_b, j, 0)], A_MAIN, n_q=n_qk, n_k=n_qk,
                                    k_scale=A_DK ** -0.5, gates=ml_gates + (j,), **geo)
                if grp["sample"]:
                    (h,) = _mlstm_scan(z, gates, ml_gn4, j, n_layers=n_a, state=(st_c, st_n, st_m), **bt)
                else:
                    h, *ml_states = _mlstm_scan(z, gates, ml_gn4, j, n_layers=n_a, prev=ml_states, **bt)
                x = _outproj(h, ml_w_out_b, j, x, ng4, mod5, i, **geo)
            else:
                n_qk = B_HEADS * B_DK // TN_PROJ
                w_qk = ret_w_qk_rope if grp["sample"] else ret_w_in_b
                w_parts = [(w_qk, j, 0), (ret_w_in_b, j, 1), (ret_w_in_b, j, 2)]
                (z,) = _inproj(x, ng4, mod5, i, w_parts, B_QK, n_q=n_qk, n_k=n_qk, k_scale=B_DK ** -0.5,
                               rope=rope if grp["sample"] else None, **geo)
                if grp["sample"]:
                    (h,) = _ret_scan(z, dec_rep, ret_gn4, j, n_layers=n_b, state=state_ret_S, **bt)
                else:
                    h, ret_states = _ret_scan(z, dec_rep, ret_gn4, j, n_layers=n_b, prev=ret_states, **bt)
                x = _outproj(h, ret_w_out_b, j, x, ng4, mod5, i, **geo)
            grp["x"] = _ffn(x, ng4, mod5, i, ffn_w_up_b, ffn_conv, ffn_conv_b3, ffn_w_down_b, **geo)

    y_prompt = groups[0]["x"].reshape(bp, tp, D_MODEL)
    y_sample = groups[1]["x"].reshape(bs, ts, D_MODEL)
    new_c, new_n, new_m = ml_states
    return (y_prompt, y_sample, new_c, new_n.reshape(bp, n_a, 2, A_HEADS, A_DK),
            new_m.reshape(bp, n_a, 2, A_HEADS), ret_states)
```

```python
import functools
import math

import jax
import jax.numpy as jnp
from jax import lax
from jax.experimental import pallas as pl
from jax.experimental.pallas import tpu as pltpu

D_MODEL = 1024
DEPTH = 4
GRID_W = 64
CHUNK = 128
N_MIXERS = 2
A_HEADS = 4
A_DV = D_MODEL // A_HEADS
A_DK = A_DV // 2
A_MAIN = 2 * A_HEADS * A_DK + 2 * A_HEADS * A_DV
B_HEADS = 8
B_DK = D_MODEL // B_HEADS
B_DV = 2 * D_MODEL // B_HEADS
B_QK = 2 * B_HEADS * B_DK
ROPE_BASE = 10000.0
D_FF = ((8 * D_MODEL // 3 + 127) // 128) * 128
EPS = 1e-6
LN2 = math.log(2.0)

F32 = jnp.float32
BF16 = jnp.bfloat16

LANES = 128
SUBLANES = 8
MOD_ROWS = 16
VMEM_LIMIT = 48 * 1024 * 1024

TM_PROJ = 512
TN_PROJ = 512
TM_OUT = 1024
TR_OUT = 512
TM_FFN = 512
TF_FFN = 256
FFN_DOWN_BLOCKS = 2
GATE_UNROLL = 16
STATE_UNROLL = 16
OUT_UNROLL = 16
MLSTM_OUT_UNROLL = 16
SCAN_TOKENS = 2048


def _dot(a, b):
    return jnp.dot(a, b, preferred_element_type=F32)


def _dot_nt(a, b):
    return lax.dot_general(a, b, (((1,), (1,)), ((), ())), preferred_element_type=F32)


def _dot_tn(a, b):
    return lax.dot_general(a, b, (((0,), (0,)), ((), ())), preferred_element_type=F32)


def _rms(x):
    return x * lax.rsqrt(jnp.mean(x * x, axis=-1, keepdims=True) + EPS)


def _layer_norm(h):
    d = h - jnp.mean(h, axis=-1, keepdims=True)
    return d * lax.rsqrt(jnp.mean(d * d, axis=-1, keepdims=True) + EPS)


def _params(*sem):
    return pltpu.CompilerParams(dimension_semantics=sem, vmem_limit_bytes=VMEM_LIMIT)


def _resident(shape, index):
    return pl.BlockSpec(shape, lambda *_: index, pipeline_mode=pl.Buffered(1))


def _mod_kernel(cond_ref, w_ref, b_ref, o_ref):
    cnd = cond_ref[...]
    s = cnd * jax.nn.sigmoid(cnd)
    o_ref[...] = _dot(s.astype(BF16), w_ref[...].astype(BF16)) + b_ref[...]


def _modulation(cond, ada_w, ada_b):
    tn = 1024
    n_out = ada_w.shape[-1]
    return pl.pallas_call(
        _mod_kernel,
        grid=(DEPTH, n_out // tn),
        in_specs=[
            pl.BlockSpec((MOD_ROWS, D_MODEL), lambda l, j: (0, 0)),
            pl.BlockSpec((None, D_MODEL, tn), lambda l, j: (l, 0, j)),
            pl.BlockSpec((None, 1, tn), lambda l, j: (l, 0, j)),
        ],
        out_specs=pl.BlockSpec((None, MOD_ROWS, tn), lambda l, j: (l, 0, j)),
        out_shape=jax.ShapeDtypeStruct((DEPTH, MOD_ROWS, n_out), F32),
        compiler_params=_params("parallel", "parallel"),
        name="modulation",
    )(cond, ada_w, ada_b.reshape(DEPTH, 1, n_out))


def _mod_row(sample, seq_len, tm):
    if not sample:
        return lambda i: 0
    tiles_per_seq = seq_len // tm
    return lambda i: 1 + i // tiles_per_seq


def _rope_slab(x, cos, sin):
    return x * cos + pltpu.roll(x, 64, axis=1) * sin


def _inproj_kernel(*refs, n_w, n_q, n_k, k_scale, rope, gates, tn):
    x_ref, g_ref, sh_ref, sc_ref = refs[:4]
    w_refs = refs[4:4 + n_w]
    pos = 4 + n_w
    if gates:
        wg_ref, bg_ref = refs[pos:pos + 2]
        pos += 2
    if rope:
        cos_ref, sin_ref = refs[pos:pos + 2]
        pos += 2
    z_ref = refs[pos]
    pos += 1
    if gates:
        gates_ref = refs[pos]
        pos += 1
    u_sc = refs[pos]

    u = _rms(x_ref[...]) * g_ref[...] * (1.0 + sc_ref[...]) + sh_ref[...]
    u_sc[...] = u.astype(BF16)
    if gates:
        gates_ref[...] = _dot_nt(wg_ref[...], u_sc[...]) + bg_ref[...]

    wb = w_refs[0].shape[1]
    for j in range(z_ref.shape[1] // tn):
        part, off = divmod(j * tn, wb)
        z = _dot(u_sc[...], w_refs[part][:, off:off + tn])
        scale = k_scale if n_q <= j < n_q + n_k else 1.0
        if rope and j < n_q + n_k:
            for s in range(tn // LANES):
                r = _rope_slab(z[:, s * LANES:(s + 1) * LANES], cos_ref[...], sin_ref[...])
                if scale != 1.0:
                    r = r * scale
                z_ref[:, j * tn + s * LANES:j * tn + (s + 1) * LANES] = r.astype(BF16)
        elif scale != 1.0:
            z_ref[:, j * tn:(j + 1) * tn] = (z * scale).astype(BF16)
        else:
            z_ref[:, j * tn:(j + 1) * tn] = z.astype(BF16)


def _inproj(x, ng4, mod5, layer, w_parts, wb, *, seq_len, sample, n_q, n_k, k_scale, rope=None, gates=None):
    n_tok = x.shape[0]
    tm, tn = TM_PROJ, TN_PROJ
    n_col = wb * len(w_parts)
    row = _mod_row(sample, seq_len, tm)
    in_specs = [
        pl.BlockSpec((tm, D_MODEL), lambda i: (i, 0)),
        pl.BlockSpec((None, None, 1, D_MODEL), lambda i: (layer, 0, 0, 0)),
        pl.BlockSpec((None, None, None, 1, D_MODEL), lambda i: (layer, row(i), 0, 0, 0)),
        pl.BlockSpec((None, None, None, 1, D_MODEL), lambda i: (layer, row(i), 1, 0, 0)),
    ]
    in_specs += [_resident((None, D_MODEL, wb), (jl, 0, blk)) for _, jl, blk in w_parts]
    args = [x, ng4, mod5, mod5] + [w for w, _, _ in w_parts]
    out_specs = [pl.BlockSpec((tm, n_col), lambda i: (i, 0))]
    out_shape = [jax.ShapeDtypeStruct((n_tok, n_col), BF16)]
    if gates is not None:
        wg_t, bg, jg = gates
        n_g = wg_t.shape[1]
        in_specs += [_resident((None, n_g, D_MODEL), (jg, 0, 0)), _resident((None, n_g, 1), (jg, 0, 0))]
        args += [wg_t, bg]
        out_specs += [pl.BlockSpec((n_g, tm), lambda i: (0, i))]
        out_shape += [jax.ShapeDtypeStruct((n_g, n_tok), F32)]
    if rope is not None:
        tiles_per_seq = seq_len // tm
        in_specs += [pl.BlockSpec((tm, LANES), lambda i: (i % tiles_per_seq, 0))] * 2
        args += list(rope)
    kern = functools.partial(_inproj_kernel, n_w=len(w_parts), n_q=n_q, n_k=n_k, k_scale=k_scale,
                             rope=rope is not None, gates=gates is not None, tn=tn)
    return pl.pallas_call(
        kern,
        grid=(n_tok // tm,),
        in_specs=in_specs,
        out_specs=out_specs,
        out_shape=out_shape,
        scratch_shapes=[pltpu.VMEM((tm, D_MODEL), BF16)],
        compiler_params=_params("parallel"),
        name="inproj",
    )(*args)


def _tri_masks():
    li = lax.broadcasted_iota(jnp.int32, (CHUNK, CHUNK), 0)
    si = lax.broadcasted_iota(jnp.int32, (CHUNK, CHUNK), 1)
    return si <= li, si >= li


def _layer_slot(ref, fresh_slot):
    if fresh_slot is None:
        return ref
    for other in range(ref.shape[1]):
        if other != fresh_slot:
            ref[:, other] = jnp.zeros(ref.shape[:1] + ref.shape[2:], ref.dtype)
    return ref.at[:, fresh_slot]


def _seqs_per_step(batch, seq_len, carry_in):
    if carry_in:
        return 1
    nb = max(1, SCAN_TOKENS // seq_len)
    while batch % nb:
        nb -= 1
    return nb


NG = 2 * A_HEADS


def _split_dot(x, mask_b):
    hi = x.astype(BF16)
    r1 = x - hi.astype(F32)
    mid = r1.astype(BF16)
    lo = (r1 - mid.astype(F32)).astype(BF16)
    return _dot(hi, mask_b) + _dot(mid, mask_b) + _dot(lo, mask_b)


def _mlstm_kernel(*refs, n_chunks, nb, carry_in, carry_out, n_alias, fresh_slot):
    q_ref, k_ref, v_ref, o_ref, g_ref, gn_ref = refs[:6]
    pos = 6
    if carry_in:
        c0_ref, n0_ref, m0_ref = refs[pos:pos + 3]
        pos += 3
    pos += n_alias
    h_ref = refs[pos]
    pos += 1
    if carry_out:
        cout_ref, nout_ref, mout_ref = refs[pos:pos + 3]
        pos += 3
    ab_sc, bt_sc, g_sc, bm_sc, mpf_sc, mpb_sc, c_sc, call_sc = refs[pos:pos + 8]

    head = pl.program_id(1)
    masks = _tri_masks()
    ones_b = jnp.ones((CHUNK, LANES), BF16)
    sum_b = jnp.concatenate([masks[1].astype(BF16), ones_b], axis=1)
    lane = lax.broadcasted_iota(jnp.int32, (CHUNK, LANES), 1)
    grow = lax.broadcasted_iota(jnp.int32, (2 * NG, CHUNK), 0)
    gsub = lax.broadcasted_iota(jnp.int32, (NG, LANES), 0)
    zpad = jnp.zeros((LANES - 2 * NG, CHUNK), F32)
    cols = (head, head + A_HEADS)
    mp_sc = (mpf_sc, mpb_sc)
    t_seq = n_chunks * CHUNK

    def pick_col(x, col):
        return jnp.sum(jnp.where(lane == col, x, 0.0), axis=1, keepdims=True)

    def at(s, c):
        return pl.ds(pl.multiple_of(s * t_seq + c * CHUNK, CHUNK), CHUNK)

    def gate_body(c, carry):
        for s in range(nb):
            idx = s * n_chunks + c
            gates = g_ref[:, at(s, c)]
            lf = jnp.minimum(gates, 0.0) - jnp.log1p(jnp.exp(-jnp.abs(gates)))
            lf = jnp.where(grow >= NG, lf, 0.0)
            sums = _split_dot(lf, sum_b)
            a_f = sums[:, :CHUNK]
            tot = sums[:, CHUNK:]
            a_all = jnp.where(grow < NG + A_HEADS, a_f, tot - a_f + lf)[NG:, :]
            b_all = gates[:NG, :] - a_all
            bt_sc[idx] = b_all
            ab_sc[at(s, c), :] = jnp.concatenate([a_all, b_all, zpad], axis=0).T
            g_sc[idx] = tot[NG:, :]
            bm_sc[idx] = jnp.broadcast_to(jnp.max(b_all, axis=1, keepdims=True), (NG, LANES))
        return carry

    lax.fori_loop(0, n_chunks, gate_body, 0, unroll=min(n_chunks, GATE_UNROLL))

    if carry_in:
        m_init = (jnp.broadcast_to(m0_ref[0], (NG, LANES)), jnp.broadcast_to(m0_ref[1], (NG, LANES)))
    else:
        m_init = (jnp.zeros((NG, LANES), F32),) * (2 * nb)

    def m_body(i, carry):
        out = []
        for s in range(nb):
            m_f, m_b = carry[2 * s], carry[2 * s + 1]
            jf = s * n_chunks + i
            jb = s * n_chunks + n_chunks - 1 - i
            mpf_sc[jf] = m_f
            mpb_sc[jb] = m_b
            out.append(g_sc[jf] + jnp.maximum(m_f, bm_sc[jf]))
            out.append(g_sc[jb] + jnp.maximum(m_b, bm_sc[jb]))
        return tuple(out)

    m_last = lax.fori_loop(0, n_chunks, m_body, m_init, unroll=min(n_chunks, STATE_UNROLL))

    if carry_in:
        for dirn in range(2):
            c_sc[dirn, :, :A_DV] = c0_ref[dirn]
            c_sc[dirn, :, A_DV:] = jnp.broadcast_to(n0_ref[dirn], (A_DK, LANES))
    else:
        c_sc[...] = jnp.zeros_like(c_sc)

    def state_body(i, carry):
        for s in range(nb):
            for dirn in range(2):
                c = i if dirn == 0 else n_chunks - 1 - i
                idx = s * n_chunks + c
                mp_row = mp_sc[dirn][idx, pl.ds(cols[dirn], 1), :]
                m_top = jnp.maximum(mp_row, bm_sc[idx, pl.ds(cols[dirn], 1), :])
                ws = jnp.exp(pick_col(ab_sc[at(s, c), :], NG + cols[dirn]) - m_top)
                dec = jnp.exp(mp_row - m_top)
                dec = jnp.concatenate([dec] * (c_sc.shape[2] // LANES), axis=1)
                c_old = c_sc[2 * s + dirn]
                call_sc[2 * s + dirn, c] = c_old.astype(BF16)
                kw = (k_ref[at(s, c), :].astype(F32) * ws).astype(BF16)
                upd = jnp.concatenate([_dot_tn(kw, v_ref[at(s, c), :]), _dot_tn(kw, ones_b)], axis=1)
                c_sc[2 * s + dirn] = dec * c_old + upd
        return carry

    lax.fori_loop(0, n_chunks, state_body, 0, unroll=min(n_chunks, STATE_UNROLL))
    if carry_out:
        outs = [_layer_slot(r, fresh_slot) for r in (cout_ref, nout_ref, mout_ref)]
        for s in range(nb):
            for dirn in range(2):
                outs[0][s, dirn] = c_sc[2 * s + dirn, :, :A_DV]
                outs[1][s, dirn] = c_sc[2 * s + dirn, :, A_DV:].T[0:1, :]
                m_end = jnp.where(gsub == cols[dirn], m_last[2 * s + dirn], 0.0)
                outs[2][s, dirn] = jnp.sum(m_end, axis=0, keepdims=True)[:, 0:1]

    def out_body(c, carry):
        for s in range(nb):
            q = q_ref[at(s, c), :]
            qf = q.astype(F32)
            v_ext = jnp.concatenate([v_ref[at(s, c), :], ones_b], axis=1)
            s_raw = _dot_nt(q, k_ref[at(s, c), :])
            a_chunk = ab_sc[at(s, c), :]
            idx = s * n_chunks + c
            h = None
            for dirn in range(2):
                col = cols[dirn]
                m_prev = mp_sc[dirn][idx, pl.ds(col, 1), :]
                b_vis = jnp.where(masks[dirn], bt_sc[idx, pl.ds(col, 1), :], -jnp.inf)
                m_row = jnp.maximum(m_prev, jnp.max(b_vis, axis=1, keepdims=True))
                sw = (s_raw * jnp.exp(b_vis - m_row)).astype(BF16)
                w_inter = jnp.exp(m_prev - m_row)
                floor = jnp.exp(-(pick_col(a_chunk, col) + m_row))
                qw = (qf * w_inter).astype(BF16)
                nd = _dot(sw, v_ext) + _dot(qw, call_sc[2 * s + dirn, c])
                r = 1.0 / jnp.maximum(jnp.abs(nd[:, A_DV:]), floor)
                hd = nd[:, :A_DV] * jnp.concatenate([r, r], axis=1)
                h = hd if h is None else h + hd
            o = o_ref[at(s, c), :].astype(F32)
            h_ref[at(s, c), :] = (_layer_norm(h) * gn_ref[...] * jax.nn.sigmoid(o)).astype(BF16)
        return carry

    lax.fori_loop(0, n_chunks, out_body, 0, unroll=min(n_chunks, max(1, MLSTM_OUT_UNROLL // nb)))


def _mlstm_scan(z, gates, gn4, j, *, batch, seq_len, n_layers, state=None, prev=None):
    n_tok = z.shape[0]
    n_chunks = seq_len // CHUNK
    carry_in = state is not None
    carry_out = not carry_in
    nb = _seqs_per_step(batch, seq_len, carry_in)
    t = nb * seq_len
    in_specs = [
        pl.BlockSpec((t, A_DK), lambda b, h: (b, h)),
        pl.BlockSpec((t, A_DK), lambda b, h: (b, A_HEADS + h)),
        pl.BlockSpec((t, A_DV), lambda b, h: (b, A_HEADS + h)),
        pl.BlockSpec((t, A_DV), lambda b, h: (b, 2 * A_HEADS + h)),
        pl.BlockSpec((2 * NG, t), lambda b, h: (0, b)),
        pl.BlockSpec((None, None, 1, A_DV), lambda b, h: (j, h, 0, 0)),
    ]
    args = [z, z, z, z, gates, gn4]
    aliases = {}
    if carry_in:
        in_specs += [
            pl.BlockSpec((None, None, 2, None, A_DK, A_DV), lambda b, h: (b, j, 0, h, 0, 0)),
            pl.BlockSpec((None, None, 2, None, A_DK, 1), lambda b, h: (b, j, 0, h, 0, 0)),
            pl.BlockSpec((None, None, 2, None, 1, 1), lambda b, h: (b, j, 0, h, 0, 0)),
        ]
        args += list(state)
    out_specs = [pl.BlockSpec((t, A_DV), lambda b, h: (b, h))]
    out_shape = [jax.ShapeDtypeStruct((n_tok, A_HEADS * A_DV), BF16)]
    fresh_slot = j if carry_out and prev is None else None
    if carry_out:
        lay, jb = (n_layers, 0) if prev is None else (None, j)
        out_specs += [
            pl.BlockSpec((nb, lay, 2, None, A_DK, A_DV), lambda b, h: (b, jb, 0, h, 0, 0)),
            pl.BlockSpec((nb, lay, 2, None, 1, A_DK), lambda b, h: (b, jb, 0, h, 0, 0)),
            pl.BlockSpec((nb, lay, 2, None, 1, 1), lambda b, h: (b, jb, 0, h, 0, 0)),
        ]
        out_shape += [
            jax.ShapeDtypeStruct((batch, n_layers, 2, A_HEADS, A_DK, A_DV), F32),
            jax.ShapeDtypeStruct((batch, n_layers, 2, A_HEADS, 1, A_DK), F32),
            jax.ShapeDtypeStruct((batch, n_layers, 2, A_HEADS, 1, 1), F32),
        ]
        if prev is not None:
            aliases = {len(args) + k: 1 + k for k in range(3)}
            in_specs += [pl.BlockSpec(memory_space=pl.ANY)] * 3
            args += list(prev)
    kern = functools.partial(_mlstm_kernel, n_chunks=n_chunks, nb=nb, carry_in=carry_in,
                             carry_out=carry_out, n_alias=len(aliases), fresh_slot=fresh_slot)
    return pl.pallas_call(
        kern,
        grid=(batch // nb, A_HEADS),
        in_specs=in_specs,
        out_specs=out_specs,
        out_shape=out_shape,
        input_output_aliases=aliases,
        scratch_shapes=[
            pltpu.VMEM((t, LANES), F32),
            pltpu.VMEM((nb * n_chunks, NG, CHUNK), F32),
            pltpu.VMEM((nb * n_chunks, NG, LANES), F32),
            pltpu.VMEM((nb * n_chunks, NG, LANES), F32),
            pltpu.VMEM((nb * n_chunks, NG, LANES), F32),
            pltpu.VMEM((nb * n_chunks, NG, LANES), F32),
            pltpu.VMEM((2 * nb, A_DK, A_DV + LANES), F32),
            pltpu.VMEM((2 * nb, n_chunks, A_DK, A_DV + LANES), BF16),
        ],
        compiler_params=_params("parallel", "parallel"),
        name="mlstm_scan",
    )(*args)


def _ret_kernel(*refs, n_chunks, nb, carry_in, carry_out, n_alias, fresh_slot):
    q_ref, k_ref, v_ref, gate_ref, dec_ref, gn_ref = refs[:6]
    pos = 6
    if carry_in:
        s0_ref = refs[pos]
        pos += 1
    pos += n_alias
    h_ref = refs[pos]
    pos += 1
    if carry_out:
        sout_ref = refs[pos]
        pos += 1
    s_sc, sall_sc, dsum_sc, xi_sc, zeta_sc = refs[pos:pos + 5]
    t_seq = n_chunks * CHUNK

    def at(s, c):
        return pl.ds(pl.multiple_of(s * t_seq + c * CHUNK, CHUNK), CHUNK)

    lg = jnp.log1p(-jnp.exp(-dec_ref[...] * LN2))
    lg_f = lg[0:1, :]
    lg_b = lg[1:2, :]

    @pl.when(pl.program_id(1) == 0)
    def _():
        masks = _tri_masks()
        li = lax.broadcasted_iota(jnp.int32, (CHUNK, B_DV), 0).astype(F32)
        si = lax.broadcasted_iota(jnp.int32, (CHUNK, CHUNK), 1).astype(F32)
        lq = li[:, :CHUNK]
        dsum_sc[...] = (
            jnp.where(masks[0], jnp.exp(jnp.where(masks[0], lq - si, 0.0) * lg_f[:, :CHUNK]), 0.0)
            + jnp.where(masks[1], jnp.exp(jnp.where(masks[1], si - lq, 0.0) * lg_b[:, :CHUNK]), 0.0))
        xi_sc[0] = jnp.exp((lq + 1.0) * lg_f[:, :CHUNK])
        xi_sc[1] = jnp.exp((CHUNK - lq) * lg_b[:, :CHUNK])
        zeta_sc[0] = jnp.exp((CHUNK - 1.0 - lq) * lg_f[:, :CHUNK])
        zeta_sc[1] = jnp.exp(lq * lg_b[:, :CHUNK])

    cdec = (jnp.exp(CHUNK * lg_f), jnp.exp(CHUNK * lg_b))

    if carry_in:
        qr = B_DK // 4
        for dirn in range(2):
            for n, o in enumerate((0, 2, 1, 3)):
                s_sc[dirn, n * qr:(n + 1) * qr, :] = s0_ref[dirn, o * qr:(o + 1) * qr, :]
    else:
        s_sc[...] = jnp.zeros_like(s_sc)

    def state_body(i, carry):
        for s in range(nb):
            for dirn in range(2):
                c = i if dirn == 0 else n_chunks - 1 - i
                s_old = s_sc[2 * s + dirn]
                sall_sc[s, c, dirn * B_DK:(dirn + 1) * B_DK, :] = s_old.astype(BF16)
                kz = (k_ref[at(s, c), :].astype(F32) * zeta_sc[dirn]).astype(BF16)
                s_sc[2 * s + dirn] = cdec[dirn] * s_old + _dot_tn(kz, v_ref[at(s, c), :])
        return carry

    lax.fori_loop(0, n_chunks, state_body, 0, unroll=min(n_chunks, STATE_UNROLL))
    if carry_out:
        s_out = _layer_slot(sout_ref, fresh_slot)
        for s in range(nb):
            for dirn in range(2):
                s_out[s, dirn] = s_sc[2 * s + dirn]

    def out_body(c, carry):
        for s in range(nb):
            q = q_ref[at(s, c), :]
            v = v_ref[at(s, c), :]
            sw = _dot_nt(q, k_ref[at(s, c), :]) * dsum_sc[...]
            qf = q.astype(F32)
            qx = jnp.concatenate([qf * xi_sc[0], qf * xi_sc[1]], axis=1).astype(BF16)
            h = _dot(sw.astype(BF16), v) + _dot(qx, sall_sc[s, c])
            g = gate_ref[at(s, c), :].astype(F32)
            h_ref[at(s, c), :] = (_layer_norm(h) * gn_ref[...] * (g * jax.nn.sigmoid(g))).astype(BF16)
        return carry

    lax.fori_loop(0, n_chunks, out_body, 0, unroll=min(n_chunks, max(1, OUT_UNROLL // nb)))


def _ret_scan(z, dec_rep, gn4, j, *, batch, seq_len, n_layers, state=None, prev=None):
    n_tok = z.shape[0]
    n_chunks = seq_len // CHUNK
    carry_in = state is not None
    carry_out = not carry_in
    nb = _seqs_per_step(batch, seq_len, carry_in)
    t = nb * seq_len
    in_specs = [
        pl.BlockSpec((t, B_DK), lambda h, b: (b, h)),
        pl.BlockSpec((t, B_DK), lambda h, b: (b, B_HEADS + h)),
        pl.BlockSpec((t, B_DV), lambda h, b: (b, B_HEADS + h)),
        pl.BlockSpec((t, B_DV), lambda h, b: (b, 2 * B_HEADS + h)),
        pl.BlockSpec((None, None, 2, B_DV), lambda h, b: (j, h, 0, 0)),
        pl.BlockSpec((None, None, 1, B_DV), lambda h, b: (j, h, 0, 0)),
    ]
    args = [z, z, z, z, dec_rep, gn4]
    aliases = {}
    if carry_in:
        in_specs += [pl.BlockSpec((None, None, 2, None, B_DK, B_DV), lambda h, b: (b, j, 0, h, 0, 0))]
        args += [state]
    out_specs = [pl.BlockSpec((t, B_DV), lambda h, b: (b, h))]
    out_shape = [jax.ShapeDtypeStruct((n_tok, B_HEADS * B_DV), BF16)]
    fresh_slot = j if carry_out and prev is None else None
    if carry_out:
        lay, jb = (n_layers, 0) if prev is None else (None, j)
        out_specs += [pl.BlockSpec((nb, lay, 2, None, B_DK, B_DV), lambda h, b: (b, jb, 0, h, 0, 0))]
        out_shape += [jax.ShapeDtypeStruct((batch, n_layers, 2, B_HEADS, B_DK, B_DV), F32)]
        if prev is not None:
            aliases = {len(args): 1}
            in_specs += [pl.BlockSpec(memory_space=pl.ANY)]
            args += [prev]
    kern = functools.partial(_ret_kernel, n_chunks=n_chunks, nb=nb, carry_in=carry_in,
                             carry_out=carry_out, n_alias=len(aliases), fresh_slot=fresh_slot)
    return pl.pallas_call(
        kern,
        grid=(B_HEADS, batch // nb),
        in_specs=in_specs,
        out_specs=out_specs,
        out_shape=out_shape,
        input_output_aliases=aliases,
        scratch_shapes=[
            pltpu.VMEM((2 * nb, B_DK, B_DV), F32),
            pltpu.VMEM((nb, n_chunks, 2 * B_DK, B_DV), BF16),
            pltpu.VMEM((CHUNK, CHUNK), F32),
            pltpu.VMEM((2, CHUNK, B_DK), F32),
            pltpu.VMEM((2, CHUNK, B_DK), F32),
        ],
        compiler_params=_params("parallel", "arbitrary"),
        name="ret_scan",
    )(*args)


def _outproj_kernel(h_ref, w_ref, x_ref, g_ref, gate_ref, o_ref):
    for r in range(o_ref.shape[0] // TR_OUT):
        rows = slice(r * TR_OUT, (r + 1) * TR_OUT)
        y = _dot(h_ref[rows, :], w_ref[...])
        o_ref[rows, :] = x_ref[rows, :] + gate_ref[...] * (_rms(y) * g_ref[...])


def _outproj(h, w, j, x, ng4, mod5, layer, *, seq_len, sample):
    n_tok, hv = h.shape
    tm = TM_OUT
    row = _mod_row(sample, seq_len, tm)
    return pl.pallas_call(
        _outproj_kernel,
        grid=(n_tok // tm,),
        in_specs=[
            pl.BlockSpec((tm, hv), lambda i: (i, 0)),
            _resident((None, hv, D_MODEL), (j, 0, 0)),
            pl.BlockSpec((tm, D_MODEL), lambda i: (i, 0)),
            pl.BlockSpec((None, None, 1, D_MODEL), lambda i: (layer, 1, 0, 0)),
            pl.BlockSpec((None, None, None, 1, D_MODEL), lambda i: (layer, row(i), 2, 0, 0)),
        ],
        out_specs=pl.BlockSpec((tm, D_MODEL), lambda i: (i, 0)),
        out_shape=jax.ShapeDtypeStruct((n_tok, D_MODEL), F32),
        compiler_params=_params("parallel"),
        name="outproj",
    )(h, w, x, ng4, mod5)


def _conv3(hs_ref, half, h, cw, cb, seg, n_seg):
    for s in range(n_seg):
        base = SUBLANES + s * (seg + SUBLANES)
        h_seg = h[s * seg:(s + 1) * seg, :]
        hs_ref[2 * half, base + 1:base + 1 + seg, :] = h_seg
        hs_ref[2 * half + 1, base - 1:base - 1 + seg, :] = h_seg
    parts = []
    for s in range(n_seg):
        base = SUBLANES + s * (seg + SUBLANES)
        h_prev = hs_ref[2 * half, base:base + seg, :]
        h_next = hs_ref[2 * half + 1, base:base + seg, :]
        h_mid = h[s * seg:(s + 1) * seg, :]
        parts.append(h_prev * cw[0:1, :] + h_mid * cw[1:2, :] + h_next * cw[2:3, :] + cb)
    return parts


def _ffn_kernel(x_ref, g2_ref, sh_ref, sc_ref, wup_ref, cw_ref, cb_ref, wd_ref, g3_ref, gate_ref,
                o_ref, u_sc, act_sc, hs_sc, *, seg, n_seg, tf):
    u = _rms(x_ref[...]) * g2_ref[...] * (1.0 + sc_ref[...]) + sh_ref[...]
    u_sc[...] = u.astype(BF16)
    zero_rows = jnp.zeros((SUBLANES, tf), F32)
    for s in range(n_seg):
        base = SUBLANES + s * (seg + SUBLANES)
        for half in range(2):
            hs_sc[2 * half, base:base + SUBLANES, :] = zero_rows
            hs_sc[2 * half + 1, base + seg - SUBLANES:base + seg, :] = zero_rows

    for cidx in range(D_FF // tf):
        cg = slice(cidx * tf, (cidx + 1) * tf)
        cu = slice(D_FF + cidx * tf, D_FF + (cidx + 1) * tf)
        hg = _conv3(hs_sc, 0, _dot(u_sc[...], wup_ref[:, cg]), cw_ref[:, cg], cb_ref[:, cg], seg, n_seg)
        hu = _conv3(hs_sc, 1, _dot(u_sc[...], wup_ref[:, cu]), cw_ref[:, cu], cb_ref[:, cu], seg, n_seg)
        for s in range(n_seg):
            act = jax.nn.gelu(hg[s], approximate=True) * hu[s]
            act_sc[s * seg:(s + 1) * seg, cg] = act.astype(BF16)

    tr = o_ref.shape[0] // FFN_DOWN_BLOCKS
    for r in range(FFN_DOWN_BLOCKS):
        rows = slice(r * tr, (r + 1) * tr)
        f = _dot(act_sc[rows, :], wd_ref[...])
        o_ref[rows, :] = x_ref[rows, :] + gate_ref[...] * (_rms(f) * g3_ref[...])


def _ffn(x, ng4, mod5, layer, w_up, conv_w, conv_b, w_down, *, seq_len, sample):
    n_tok = x.shape[0]
    tm, tf = TM_FFN, TF_FFN
    row = _mod_row(sample, seq_len, tm)
    seg = GRID_W if sample else seq_len
    n_seg = tm // seg
    kern = functools.partial(_ffn_kernel, seg=seg, n_seg=n_seg, tf=tf)
    mod_spec = lambda k: pl.BlockSpec((None, None, None, 1, D_MODEL), lambda i: (layer, row(i), k, 0, 0))
    gain_spec = lambda k: pl.BlockSpec((None, None, 1, D_MODEL), lambda i: (layer, k, 0, 0))
    return pl.pallas_call(
        kern,
        grid=(n_tok // tm,),
        in_specs=[
            pl.BlockSpec((tm, D_MODEL), lambda i: (i, 0)),
            gain_spec(2),
            mod_spec(3),
            mod_spec(4),
            _resident((None, D_MODEL, 2 * D_FF), (layer, 0, 0)),
            _resident((None, 3, 2 * D_FF), (layer, 0, 0)),
            _resident((None, 1, 2 * D_FF), (layer, 0, 0)),
            _resident((None, D_FF, D_MODEL), (layer, 0, 0)),
            gain_spec(3),
            mod_spec(5),
        ],
        out_specs=pl.BlockSpec((tm, D_MODEL), lambda i: (i, 0)),
        out_shape=jax.ShapeDtypeStruct((n_tok, D_MODEL), F32),
        scratch_shapes=[
            pltpu.VMEM((tm, D_MODEL), BF16),
            pltpu.VMEM((tm, D_FF), BF16),
            pltpu.VMEM((4, SUBLANES + n_seg * (seg + SUBLANES), tf), F32),
        ],
        compiler_params=_params("parallel"),
        name="convffn",
    )(x, ng4, mod5, mod5, w_up, conv_w, conv_b, w_down, ng4, mod5)


def _rope_tables(seq_len):
    quarter = B_DK // 4
    inv = ROPE_BASE ** (-jnp.arange(quarter, dtype=F32) / quarter)
    t = jnp.arange(seq_len)
    rows = (t // GRID_W).astype(F32)[:, None] * inv
    cols = (t % GRID_W).astype(F32)[:, None] * inv
    cos = jnp.concatenate([jnp.cos(rows), jnp.cos(cols)] * 2, axis=-1)
    sin = jnp.concatenate([-jnp.sin(rows), -jnp.sin(cols), jnp.sin(rows), jnp.sin(cols)], axis=-1)
    return cos, sin


def _rope_qk_weights(w_in):
    n_l = w_in.shape[0]
    quarter = B_DK // 4
    w_qk = w_in[:, :, :B_QK].astype(BF16).reshape(n_l, D_MODEL, 2 * B_HEADS, 2, 2, quarter)
    return jnp.swapaxes(w_qk, 3, 4).reshape(n_l, D_MODEL, B_QK)


def _gate_weights(w_in, b_gate):
    n_l = w_in.shape[0]
    order = jnp.array((0, 2, 1, 3))
    wg = w_in[:, :, A_MAIN:].reshape(n_l, D_MODEL, 4, A_HEADS)[:, :, order, :]
    wg_t = jnp.transpose(wg, (0, 2, 3, 1)).reshape(n_l, 2 * NG, D_MODEL).astype(BF16)
    bg = b_gate[:, order, :].reshape(n_l, 2 * NG, 1)
    return wg_t, bg


def kernel(x_prompt, x_sample, state_mlstm_C, state_mlstm_n, state_mlstm_m, state_ret_S, c, c_ctx,
           norm_gain, ada_w, ada_b, ml_w_in, ml_b_gate, ml_norm, ml_w_out,
           ret_w_in, ret_decay, ret_norm, ret_w_out, ffn_w_up, ffn_conv, ffn_conv_b, ffn_w_down):
    bp, tp, _ = x_prompt.shape
    bs, ts, _ = x_sample.shape
    n_a = ml_w_in.shape[0]
    n_b = ret_w_in.shape[0]

    cond = jnp.concatenate([c_ctx[None, :], c, jnp.zeros((MOD_ROWS - 1 - bs, D_MODEL), F32)], axis=0)
    mod5 = _modulation(cond, ada_w, ada_b).reshape(DEPTH, MOD_ROWS, 6, 1, D_MODEL)
    ng4 = norm_gain.reshape(DEPTH, 4, 1, D_MODEL)
    rope = _rope_tables(ts)

    ml_w_in_b = ml_w_in.astype(BF16)
    ml_gates = _gate_weights(ml_w_in, ml_b_gate)
    ml_w_out_b = ml_w_out.astype(BF16)
    ret_w_in_b = ret_w_in.astype(BF16)
    ret_w_qk_rope = _rope_qk_weights(ret_w_in)
    ret_w_out_b = ret_w_out.astype(BF16)
    ffn_w_up_b = ffn_w_up.astype(BF16)
    ffn_w_down_b = ffn_w_down.astype(BF16)
    ffn_conv_b3 = ffn_conv_b.reshape(DEPTH, 1, 2 * D_FF)
    ml_gn4 = ml_norm.reshape(n_a, A_HEADS, 1, A_DV)
    ret_gn4 = ret_norm.reshape(n_b, B_HEADS, 1, B_DV)
    dec_rep = jnp.broadcast_to(jnp.swapaxes(ret_decay, 1, 2)[..., None], (n_b, B_HEADS, 2, B_DV))
    st_c = state_mlstm_C
    st_n = state_mlstm_n.reshape(bs, n_a, 2, A_HEADS, A_DK, 1)
    st_m = state_mlstm_m.reshape(bs, n_a, 2, A_HEADS, 1, 1)

    groups = [
        dict(x=x_prompt.reshape(bp * tp, D_MODEL), batch=bp, seq_len=tp, sample=False),
        dict(x=x_sample.reshape(bs * ts, D_MODEL), batch=bs, seq_len=ts, sample=True),
    ]
    ml_states = None
    ret_states = None
    for i in range(DEPTH):
        j = i // N_MIXERS
        for grp in groups:
            x = grp["x"]
            geo = dict(seq_len=grp["seq_len"], sample=grp["sample"])
            bt = dict(batch=grp["batch"], seq_len=grp["seq_len"])
            if i % N_MIXERS == 0:
                n_qk = A_HEADS * A_DK // TN_PROJ
                z, gates = _inproj(x, ng4, mod5, i, [(ml_w_in_b, j, 0)], A_MAIN, n_q=n_qk, n_k=n_qk,
                                    k_scale=A_DK ** -0.5, gates=ml_gates + (j,), **geo)
                if grp["sample"]:
                    (h,) = _mlstm_scan(z, gates, ml_gn4, j, n_layers=n_a, state=(st_c, st_n, st_m), **bt)
                else:
                    h, *ml_states = _mlstm_scan(z, gates, ml_gn4, j, n_layers=n_a, prev=ml_states, **bt)
                x = _outproj(h, ml_w_out_b, j, x, ng4, mod5, i, **geo)
            else:
                n_qk = B_HEADS * B_DK // TN_PROJ
                w_qk = ret_w_qk_rope if grp["sample"] else ret_w_in_b
                w_parts = [(w_qk, j, 0), (ret_w_in_b, j, 1), (ret_w_in_b, j, 2)]
                (z,) = _inproj(x, ng4, mod5, i, w_parts, B_QK, n_q=n_qk, n_k=n_qk, k_scale=B_DK ** -0.5,
                               rope=rope if grp["sample"] else None, **geo)
                if grp["sample"]:
                    (h,) = _ret_scan(z, dec_rep, ret_gn4, j, n_layers=n_b, state=state_ret_S, **bt)
                else:
                    h, ret_states = _ret_scan(z, dec_rep, ret_gn4, j, n_layers=n_b, prev=ret_states, **bt)
                x = _outproj(h, ret_w_out_b, j, x, ng4, mod5, i, **geo)
            grp["x"] = _ffn(x, ng4, mod5, i, ffn_w_up_b, ffn_conv, ffn_conv_b3, ffn_w_down_b, **geo)

    y_prompt = groups[0]["x"].reshape(bp, tp, D_MODEL)
    y_sample = groups[1]["x"].reshape(bs, ts, D_MODEL)
    new_c, new_n, new_m = ml_states
    return (y_prompt, y_sample, new_c, new_n.reshape(bp, n_a, 2, A_HEADS, A_DK),
            new_m.reshape(bp, n_a, 2, A_HEADS), ret_states)
```

```python
import functools
import math

import jax
import jax.numpy as jnp
from jax import lax
from jax.experimental import pallas as pl
from jax.experimental.pallas import tpu as pltpu

D_MODEL = 1024
DEPTH = 4
GRID_W = 64
CHUNK = 128
N_MIXERS = 2
A_HEADS = 4
A_DV = D_MODEL // A_HEADS
A_DK = A_DV // 2
A_MAIN = 2 * A_HEADS * A_DK + 2 * A_HEADS * A_DV
B_HEADS = 8
B_DK = D_MODEL // B_HEADS
B_DV = 2 * D_MODEL // B_HEADS
B_QK = 2 * B_HEADS * B_DK
ROPE_BASE = 10000.0
D_FF = ((8 * D_MODEL // 3 + 127) // 128) * 128
EPS = 1e-6
LN2 = math.log(2.0)

F32 = jnp.float32
BF16 = jnp.bfloat16

LANES = 128
SUBLANES = 8
MOD_ROWS = 16
VMEM_LIMIT = 48 * 1024 * 1024

TM_PROJ = 512
TN_PROJ = 512
TM_OUT = 1024
TR_OUT = 512
TM_FFN = 512
TF_FFN = 256
FFN_DOWN_BLOCKS = 2
GATE_UNROLL = 16
STATE_UNROLL = 16
OUT_UNROLL = 16
MLSTM_OUT_UNROLL = 16
SCAN_TOKENS = 2048


def _dot(a, b):
    return jnp.dot(a, b, preferred_element_type=F32)


def _dot_nt(a, b):
    return lax.dot_general(a, b, (((1,), (1,)), ((), ())), preferred_element_type=F32)


def _dot_tn(a, b):
    return lax.dot_general(a, b, (((0,), (0,)), ((), ())), preferred_element_type=F32)


def _rms(x):
    return x * lax.rsqrt(jnp.mean(x * x, axis=-1, keepdims=True) + EPS)


def _layer_norm(h):
    d = h - jnp.mean(h, axis=-1, keepdims=True)
    return d * lax.rsqrt(jnp.mean(d * d, axis=-1, keepdims=True) + EPS)


def _params(*sem):
    return pltpu.CompilerParams(dimension_semantics=sem, vmem_limit_bytes=VMEM_LIMIT)


def _resident(shape, index):
    return pl.BlockSpec(shape, lambda *_: index, pipeline_mode=pl.Buffered(1))


def _mod_kernel(cond_ref, w_ref, b_ref, o_ref):
    cnd = cond_ref[...]
    s = cnd * jax.nn.sigmoid(cnd)
    o_ref[...] = _dot(s.astype(BF16), w_ref[...].astype(BF16)) + b_ref[...]


def _modulation(cond, ada_w, ada_b):
    tn = 1024
    n_out = ada_w.shape[-1]
    return pl.pallas_call(
        _mod_kernel,
        grid=(DEPTH, n_out // tn),
        in_specs=[
            pl.BlockSpec((MOD_ROWS, D_MODEL), lambda l, j: (0, 0)),
            pl.BlockSpec((None, D_MODEL, tn), lambda l, j: (l, 0, j)),
            pl.BlockSpec((None, 1, tn), lambda l, j: (l, 0, j)),
        ],
        out_specs=pl.BlockSpec((None, MOD_ROWS, tn), lambda l, j: (l, 0, j)),
        out_shape=jax.ShapeDtypeStruct((DEPTH, MOD_ROWS, n_out), F32),
        compiler_params=_params("parallel", "parallel"),
        name="modulation",
    )(cond, ada_w, ada_b.reshape(DEPTH, 1, n_out))


def _mod_row(sample, seq_len, tm):
    if not sample:
        return lambda i: 0
    tiles_per_seq = seq_len // tm
    return lambda i: 1 + i // tiles_per_seq


def _rope_slab(x, cos, sin):
    return x * cos + pltpu.roll(x, 64, axis=1) * sin


def _inproj_kernel(*refs, n_w, n_q, n_k, k_scale, rope, gates, tn):
    x_ref, g_ref, sh_ref, sc_ref = refs[:4]
    w_refs = refs[4:4 + n_w]
    pos = 4 + n_w
    if gates:
        wg_ref, bg_ref = refs[pos:pos + 2]
        pos += 2
    if rope:
        cos_ref, sin_ref = refs[pos:pos + 2]
        pos += 2
    z_ref = refs[pos]
    pos += 1
    if gates:
        gates_ref = refs[pos]
        pos += 1
    u_sc = refs[pos]

    u = _rms(x_ref[...]) * g_ref[...] * (1.0 + sc_ref[...]) + sh_ref[...]
    u_sc[...] = u.astype(BF16)
    if gates:
        gates_ref[...] = _dot_nt(wg_ref[...], u_sc[...]) + bg_ref[...]

    wb = w_refs[0].shape[1]
    for j in range(z_ref.shape[1] // tn):
        part, off = divmod(j * tn, wb)
        z = _dot(u_sc[...], w_refs[part][:, off:off + tn].astype(BF16))
        scale = k_scale if n_q <= j < n_q + n_k else 1.0
        if rope and j < n_q + n_k:
            for s in range(tn // LANES):
                r = _rope_slab(z[:, s * LANES:(s + 1) * LANES], cos_ref[...], sin_ref[...])
                if scale != 1.0:
                    r = r * scale
                z_ref[:, j * tn + s * LANES:j * tn + (s + 1) * LANES] = r.astype(BF16)
        elif scale != 1.0:
            z_ref[:, j * tn:(j + 1) * tn] = (z * scale).astype(BF16)
        else:
            z_ref[:, j * tn:(j + 1) * tn] = z.astype(BF16)


def _inproj(x, ng4, mod5, layer, w_parts, wb, *, seq_len, sample, n_q, n_k, k_scale, rope=None, gates=None):
    n_tok = x.shape[0]
    tm, tn = TM_PROJ, TN_PROJ
    n_col = wb * len(w_parts)
    row = _mod_row(sample, seq_len, tm)
    in_specs = [
        pl.BlockSpec((tm, D_MODEL), lambda i: (i, 0)),
        pl.BlockSpec((None, None, 1, D_MODEL), lambda i: (layer, 0, 0, 0)),
        pl.BlockSpec((None, None, None, 1, D_MODEL), lambda i: (layer, row(i), 0, 0, 0)),
        pl.BlockSpec((None, None, None, 1, D_MODEL), lambda i: (layer, row(i), 1, 0, 0)),
    ]
    in_specs += [_resident((None, D_MODEL, wb), (jl, 0, blk)) for _, jl, blk in w_parts]
    args = [x, ng4, mod5, mod5] + [w for w, _, _ in w_parts]
    out_specs = [pl.BlockSpec((tm, n_col), lambda i: (i, 0))]
    out_shape = [jax.ShapeDtypeStruct((n_tok, n_col), BF16)]
    if gates is not None:
        wg_t, bg, jg = gates
        n_g = wg_t.shape[1]
        in_specs += [_resident((None, n_g, D_MODEL), (jg, 0, 0)), _resident((None, n_g, 1), (jg, 0, 0))]
        args += [wg_t, bg]
        out_specs += [pl.BlockSpec((n_g, tm), lambda i: (0, i))]
        out_shape += [jax.ShapeDtypeStruct((n_g, n_tok), F32)]
    if rope is not None:
        tiles_per_seq = seq_len // tm
        in_specs += [pl.BlockSpec((tm, LANES), lambda i: (i % tiles_per_seq, 0))] * 2
        args += list(rope)
    kern = functools.partial(_inproj_kernel, n_w=len(w_parts), n_q=n_q, n_k=n_k, k_scale=k_scale,
                             rope=rope is not None, gates=gates is not None, tn=tn)
    return pl.pallas_call(
        kern,
        grid=(n_tok // tm,),
        in_specs=in_specs,
        out_specs=out_specs,
        out_shape=out_shape,
        scratch_shapes=[pltpu.VMEM((tm, D_MODEL), BF16)],
        compiler_params=_params("parallel"),
        name="inproj",
    )(*args)


def _tri_masks():
    li = lax.broadcasted_iota(jnp.int32, (CHUNK, CHUNK), 0)
    si = lax.broadcasted_iota(jnp.int32, (CHUNK, CHUNK), 1)
    return si <= li, si >= li


def _layer_slot(ref, fresh_slot):
    if fresh_slot is None:
        return ref
    for other in range(ref.shape[1]):
        if other != fresh_slot:
            ref[:, other] = jnp.zeros(ref.shape[:1] + ref.shape[2:], ref.dtype)
    return ref.at[:, fresh_slot]


def _seqs_per_step(batch, seq_len, carry_in):
    if carry_in:
        return 1
    nb = max(1, SCAN_TOKENS // seq_len)
    while batch % nb:
        nb -= 1
    return nb


NG = 2 * A_HEADS


def _split_dot(x, mask_b):
    hi = x.astype(BF16)
    r1 = x - hi.astype(F32)
    mid = r1.astype(BF16)
    lo = (r1 - mid.astype(F32)).astype(BF16)
    return _dot(hi, mask_b) + _dot(mid, mask_b) + _dot(lo, mask_b)


def _mlstm_kernel(*refs, n_chunks, nb, carry_in, carry_out, n_alias, fresh_slot):
    q_ref, k_ref, v_ref, o_ref, g_ref, gn_ref = refs[:6]
    pos = 6
    if carry_in:
        c0_ref, n0_ref, m0_ref = refs[pos:pos + 3]
        pos += 3
    pos += n_alias
    h_ref = refs[pos]
    pos += 1
    if carry_out:
        cout_ref, nout_ref, mout_ref = refs[pos:pos + 3]
        pos += 3
    ab_sc, bt_sc, g_sc, bm_sc, mpf_sc, mpb_sc, c_sc, call_sc = refs[pos:pos + 8]

    head = pl.program_id(1)
    masks = _tri_masks()
    ones_b = jnp.ones((CHUNK, LANES), BF16)
    sum_b = jnp.concatenate([masks[1].astype(BF16), ones_b], axis=1)
    lane = lax.broadcasted_iota(jnp.int32, (CHUNK, LANES), 1)
    grow = lax.broadcasted_iota(jnp.int32, (2 * NG, CHUNK), 0)
    gsub = lax.broadcasted_iota(jnp.int32, (NG, LANES), 0)
    zpad = jnp.zeros((LANES - 2 * NG, CHUNK), F32)
    cols = (head, head + A_HEADS)
    mp_sc = (mpf_sc, mpb_sc)
    t_seq = n_chunks * CHUNK

    def pick_col(x, col):
        return jnp.sum(jnp.where(lane == col, x, 0.0), axis=1, keepdims=True)

    def at(s, c):
        return pl.ds(pl.multiple_of(s * t_seq + c * CHUNK, CHUNK), CHUNK)

    def gate_body(c, carry):
        for s in range(nb):
            idx = s * n_chunks + c
            gates = g_ref[:, at(s, c)]
            lf = jnp.minimum(gates, 0.0) - jnp.log1p(jnp.exp(-jnp.abs(gates)))
            lf = jnp.where(grow >= NG, lf, 0.0)
            sums = _split_dot(lf, sum_b)
            a_f = sums[:, :CHUNK]
            tot = sums[:, CHUNK:]
            a_all = jnp.where(grow < NG + A_HEADS, a_f, tot - a_f + lf)[NG:, :]
            b_all = gates[:NG, :] - a_all
            bt_sc[idx] = b_all
            ab_sc[at(s, c), :] = jnp.concatenate([a_all, b_all, zpad], axis=0).T
            g_sc[idx] = tot[NG:, :]
            bm_sc[idx] = jnp.broadcast_to(jnp.max(b_all, axis=1, keepdims=True), (NG, LANES))
        return carry

    lax.fori_loop(0, n_chunks, gate_body, 0, unroll=min(n_chunks, GATE_UNROLL))

    if carry_in:
        m_init = (jnp.broadcast_to(m0_ref[0], (NG, LANES)), jnp.broadcast_to(m0_ref[1], (NG, LANES)))
    else:
        m_init = (jnp.zeros((NG, LANES), F32),) * (2 * nb)

    def m_body(i, carry):
        out = []
        for s in range(nb):
            m_f, m_b = carry[2 * s], carry[2 * s + 1]
            jf = s * n_chunks + i
            jb = s * n_chunks + n_chunks - 1 - i
            mpf_sc[jf] = m_f
            mpb_sc[jb] = m_b
            out.append(g_sc[jf] + jnp.maximum(m_f, bm_sc[jf]))
            out.append(g_sc[jb] + jnp.maximum(m_b, bm_sc[jb]))
        return tuple(out)

    m_last = lax.fori_loop(0, n_chunks, m_body, m_init, unroll=min(n_chunks, STATE_UNROLL))

    if carry_in:
        for dirn in range(2):
            c_sc[dirn, :, :A_DV] = c0_ref[dirn]
            c_sc[dirn, :, A_DV:] = jnp.broadcast_to(n0_ref[dirn], (A_DK, LANES))
    else:
        c_sc[...] = jnp.zeros_like(c_sc)

    def state_body(i, carry):
        for s in range(nb):
            for dirn in range(2):
                c = i if dirn == 0 else n_chunks - 1 - i
                idx = s * n_chunks + c
                mp_row = mp_sc[dirn][idx, pl.ds(cols[dirn], 1), :]
                m_top = jnp.maximum(mp_row, bm_sc[idx, pl.ds(cols[dirn], 1), :])
                ws = jnp.exp(pick_col(ab_sc[at(s, c), :], NG + cols[dirn]) - m_top)
                dec = jnp.exp(mp_row - m_top)
                dec = jnp.concatenate([dec] * (c_sc.shape[2] // LANES), axis=1)
                c_old = c_sc[2 * s + dirn]
                call_sc[2 * s + dirn, c] = c_old.astype(BF16)
                kw = (k_ref[at(s, c), :].astype(F32) * ws).astype(BF16)
                upd = jnp.concatenate([_dot_tn(kw, v_ref[at(s, c), :]), _dot_tn(kw, ones_b)], axis=1)
                c_sc[2 * s + dirn] = dec * c_old + upd
        return carry

    lax.fori_loop(0, n_chunks, state_body, 0, unroll=min(n_chunks, STATE_UNROLL))
    if carry_out:
        outs = [_layer_slot(r, fresh_slot) for r in (cout_ref, nout_ref, mout_ref)]
        for s in range(nb):
            for dirn in range(2):
                outs[0][s, dirn] = c_sc[2 * s + dirn, :, :A_DV]
                outs[1][s, dirn] = c_sc[2 * s + dirn, :, A_DV:].T[0:1, :]
                m_end = jnp.where(gsub == cols[dirn], m_last[2 * s + dirn], 0.0)
                outs[2][s, dirn] = jnp.sum(m_end, axis=0, keepdims=True)[:, 0:1]

    def out_body(c, carry):
        for s in range(nb):
            q = q_ref[at(s, c), :]
            qf = q.astype(F32)
            v_ext = jnp.concatenate([v_ref[at(s, c), :], ones_b], axis=1)
            s_raw = _dot_nt(q, k_ref[at(s, c), :])
            a_chunk = ab_sc[at(s, c), :]
            idx = s * n_chunks + c
            h = None
            for dirn in range(2):
                col = cols[dirn]
                m_prev = mp_sc[dirn][idx, pl.ds(col, 1), :]
                b_vis = jnp.where(masks[dirn], bt_sc[idx, pl.ds(col, 1), :], -jnp.inf)
                m_row = jnp.maximum(m_prev, jnp.max(b_vis, axis=1, keepdims=True))
                sw = (s_raw * jnp.exp(b_vis - m_row)).astype(BF16)
                w_inter = jnp.exp(m_prev - m_row)
                floor = jnp.exp(-(pick_col(a_chunk, col) + m_row))
                qw = (qf * w_inter).astype(BF16)
                nd = _dot(sw, v_ext) + _dot(qw, call_sc[2 * s + dirn, c])
                r = 1.0 / jnp.maximum(jnp.abs(nd[:, A_DV:]), floor)
                hd = nd[:, :A_DV] * jnp.concatenate([r, r], axis=1)
                h = hd if h is None else h + hd
            o = o_ref[at(s, c), :].astype(F32)
            h_ref[at(s, c), :] = (_layer_norm(h) * gn_ref[...] * jax.nn.sigmoid(o)).astype(BF16)
        return carry

    lax.fori_loop(0, n_chunks, out_body, 0, unroll=min(n_chunks, max(1, MLSTM_OUT_UNROLL // nb)))


def _mlstm_scan(z, gates, gn4, j, *, batch, seq_len, n_layers, state=None, prev=None):
    n_tok = z.shape[0]
    n_chunks = seq_len // CHUNK
    carry_in = state is not None
    carry_out = not carry_in
    nb = _seqs_per_step(batch, seq_len, carry_in)
    t = nb * seq_len
    in_specs = [
        pl.BlockSpec((t, A_DK), lambda b, h: (b, h)),
        pl.BlockSpec((t, A_DK), lambda b, h: (b, A_HEADS + h)),
        pl.BlockSpec((t, A_DV), lambda b, h: (b, A_HEADS + h)),
        pl.BlockSpec((t, A_DV), lambda b, h: (b, 2 * A_HEADS + h)),
        pl.BlockSpec((2 * NG, t), lambda b, h: (0, b)),
        pl.BlockSpec((None, None, 1, A_DV), lambda b, h: (j, h, 0, 0)),
    ]
    args = [z, z, z, z, gates, gn4]
    aliases = {}
    if carry_in:
        in_specs += [
            pl.BlockSpec((None, None, 2, None, A_DK, A_DV), lambda b, h: (b, j, 0, h, 0, 0)),
            pl.BlockSpec((None, None, 2, None, A_DK, 1), lambda b, h: (b, j, 0, h, 0, 0)),
            pl.BlockSpec((None, None, 2, None, 1, 1), lambda b, h: (b, j, 0, h, 0, 0)),
        ]
        args += list(state)
    out_specs = [pl.BlockSpec((t, A_DV), lambda b, h: (b, h))]
    out_shape = [jax.ShapeDtypeStruct((n_tok, A_HEADS * A_DV), BF16)]
    fresh_slot = j if carry_out and prev is None else None
    if carry_out:
        lay, jb = (n_layers, 0) if prev is None else (None, j)
        out_specs += [
            pl.BlockSpec((nb, lay, 2, None, A_DK, A_DV), lambda b, h: (b, jb, 0, h, 0, 0)),
            pl.BlockSpec((nb, lay, 2, None, 1, A_DK), lambda b, h: (b, jb, 0, h, 0, 0)),
            pl.BlockSpec((nb, lay, 2, None, 1, 1), lambda b, h: (b, jb, 0, h, 0, 0)),
        ]
        out_shape += [
            jax.ShapeDtypeStruct((batch, n_layers, 2, A_HEADS, A_DK, A_DV), F32),
            jax.ShapeDtypeStruct((batch, n_layers, 2, A_HEADS, 1, A_DK), F32),
            jax.ShapeDtypeStruct((batch, n_layers, 2, A_HEADS, 1, 1), F32),
        ]
        if prev is not None:
            aliases = {len(args) + k: 1 + k for k in range(3)}
            in_specs += [pl.BlockSpec(memory_space=pl.ANY)] * 3
            args += list(prev)
    kern = functools.partial(_mlstm_kernel, n_chunks=n_chunks, nb=nb, carry_in=carry_in,
                             carry_out=carry_out, n_alias=len(aliases), fresh_slot=fresh_slot)
    return pl.pallas_call(
        kern,
        grid=(batch // nb, A_HEADS),
        in_specs=in_specs,
        out_specs=out_specs,
        out_shape=out_shape,
        input_output_aliases=aliases,
        scratch_shapes=[
            pltpu.VMEM((t, LANES), F32),
            pltpu.VMEM((nb * n_chunks, NG, CHUNK), F32),
            pltpu.VMEM((nb * n_chunks, NG, LANES), F32),
            pltpu.VMEM((nb * n_chunks, NG, LANES), F32),
            pltpu.VMEM((nb * n_chunks, NG, LANES), F32),
            pltpu.VMEM((nb * n_chunks, NG, LANES), F32),
            pltpu.VMEM((2 * nb, A_DK, A_DV + LANES), F32),
            pltpu.VMEM((2 * nb, n_chunks, A_DK, A_DV + LANES), BF16),
        ],
        compiler_params=_params("parallel", "parallel"),
        name="mlstm_scan",
    )(*args)


def _ret_kernel(*refs, n_chunks, nb, carry_in, carry_out, n_alias, fresh_slot):
    q_ref, k_ref, v_ref, gate_ref, dec_ref, gn_ref = refs[:6]
    pos = 6
    if carry_in:
        s0_ref = refs[pos]
        pos += 1
    pos += n_alias
    h_ref = refs[pos]
    pos += 1
    if carry_out:
        sout_ref = refs[pos]
        pos += 1
    s_sc, sall_sc, dsum_sc, xi_sc, zeta_sc = refs[pos:pos + 5]
    t_seq = n_chunks * CHUNK

    def at(s, c):
        return pl.ds(pl.multiple_of(s * t_seq + c * CHUNK, CHUNK), CHUNK)

    lg = jnp.log1p(-jnp.exp(-dec_ref[...] * LN2))
    lg_f = lg[0:1, :]
    lg_b = lg[1:2, :]

    @pl.when(pl.program_id(1) == 0)
    def _():
        masks = _tri_masks()
        li = lax.broadcasted_iota(jnp.int32, (CHUNK, B_DV), 0).astype(F32)
        si = lax.broadcasted_iota(jnp.int32, (CHUNK, CHUNK), 1).astype(F32)
        lq = li[:, :CHUNK]
        dsum_sc[...] = (
            jnp.where(masks[0], jnp.exp(jnp.where(masks[0], lq - si, 0.0) * lg_f[:, :CHUNK]), 0.0)
            + jnp.where(masks[1], jnp.exp(jnp.where(masks[1], si - lq, 0.0) * lg_b[:, :CHUNK]), 0.0))
        xi_sc[0] = jnp.exp((lq + 1.0) * lg_f[:, :CHUNK])
        xi_sc[1] = jnp.exp((CHUNK - lq) * lg_b[:, :CHUNK])
        zeta_sc[0] = jnp.exp((CHUNK - 1.0 - lq) * lg_f[:, :CHUNK])
        zeta_sc[1] = jnp.exp(lq * lg_b[:, :CHUNK])

    cdec = (jnp.exp(CHUNK * lg_f), jnp.exp(CHUNK * lg_b))

    if carry_in:
        qr = B_DK // 4
        for dirn in range(2):
            for n, o in enumerate((0, 2, 1, 3)):
                s_sc[dirn, n * qr:(n + 1) * qr, :] = s0_ref[dirn, o * qr:(o + 1) * qr, :]
    else:
        s_sc[...] = jnp.zeros_like(s_sc)

    def state_body(i, carry):
        for s in range(nb):
            for dirn in range(2):
                c = i if dirn == 0 else n_chunks - 1 - i
                s_old = s_sc[2 * s + dirn]
                sall_sc[s, c, dirn * B_DK:(dirn + 1) * B_DK, :] = s_old.astype(BF16)
                kz = (k_ref[at(s, c), :].astype(F32) * zeta_sc[dirn]).astype(BF16)
                s_sc[2 * s + dirn] = cdec[dirn] * s_old + _dot_tn(kz, v_ref[at(s, c), :])
        return carry

    lax.fori_loop(0, n_chunks, state_body, 0, unroll=min(n_chunks, STATE_UNROLL))
    if carry_out:
        s_out = _layer_slot(sout_ref, fresh_slot)
        for s in range(nb):
            for dirn in range(2):
                s_out[s, dirn] = s_sc[2 * s + dirn]

    def out_body(c, carry):
        for s in range(nb):
            q = q_ref[at(s, c), :]
            v = v_ref[at(s, c), :]
            sw = _dot_nt(q, k_ref[at(s, c), :]) * dsum_sc[...]
            qf = q.astype(F32)
            qx = jnp.concatenate([qf * xi_sc[0], qf * xi_sc[1]], axis=1).astype(BF16)
            h = _dot(sw.astype(BF16), v) + _dot(qx, sall_sc[s, c])
            g = gate_ref[at(s, c), :].astype(F32)
            h_ref[at(s, c), :] = (_layer_norm(h) * gn_ref[...] * (g * jax.nn.sigmoid(g))).astype(BF16)
        return carry

    lax.fori_loop(0, n_chunks, out_body, 0, unroll=min(n_chunks, max(1, OUT_UNROLL // nb)))


def _ret_scan(z, dec_rep, gn4, j, *, batch, seq_len, n_layers, state=None, prev=None):
    n_tok = z.shape[0]
    n_chunks = seq_len // CHUNK
    carry_in = state is not None
    carry_out = not carry_in
    nb = _seqs_per_step(batch, seq_len, carry_in)
    t = nb * seq_len
    in_specs = [
        pl.BlockSpec((t, B_DK), lambda h, b: (b, h)),
        pl.BlockSpec((t, B_DK), lambda h, b: (b, B_HEADS + h)),
        pl.BlockSpec((t, B_DV), lambda h, b: (b, B_HEADS + h)),
        pl.BlockSpec((t, B_DV), lambda h, b: (b, 2 * B_HEADS + h)),
        pl.BlockSpec((None, None, 2, B_DV), lambda h, b: (j, h, 0, 0)),
        pl.BlockSpec((None, None, 1, B_DV), lambda h, b: (j, h, 0, 0)),
    ]
    args = [z, z, z, z, dec_rep, gn4]
    aliases = {}
    if carry_in:
        in_specs += [pl.BlockSpec((None, None, 2, None, B_DK, B_DV), lambda h, b: (b, j, 0, h, 0, 0))]
        args += [state]
    out_specs = [pl.BlockSpec((t, B_DV), lambda h, b: (b, h))]
    out_shape = [jax.ShapeDtypeStruct((n_tok, B_HEADS * B_DV), BF16)]
    fresh_slot = j if carry_out and prev is None else None
    if carry_out:
        lay, jb = (n_layers, 0) if prev is None else (None, j)
        out_specs += [pl.BlockSpec((nb, lay, 2, None, B_DK, B_DV), lambda h, b: (b, jb, 0, h, 0, 0))]
        out_shape += [jax.ShapeDtypeStruct((batch, n_layers, 2, B_HEADS, B_DK, B_DV), F32)]
        if prev is not None:
            aliases = {len(args): 1}
            in_specs += [pl.BlockSpec(memory_space=pl.ANY)]
            args += [prev]
    kern = functools.partial(_ret_kernel, n_chunks=n_chunks, nb=nb, carry_in=carry_in,
                             carry_out=carry_out, n_alias=len(aliases), fresh_slot=fresh_slot)
    return pl.pallas_call(
        kern,
        grid=(B_HEADS, batch // nb),
        in_specs=in_specs,
        out_specs=out_specs,
        out_shape=out_shape,
        input_output_aliases=aliases,
        scratch_shapes=[
            pltpu.VMEM((2 * nb, B_DK, B_DV), F32),
            pltpu.VMEM((nb, n_chunks, 2 * B_DK, B_DV), BF16),
            pltpu.VMEM((CHUNK, CHUNK), F32),
            pltpu.VMEM((2, CHUNK, B_DK), F32),
            pltpu.VMEM((2, CHUNK, B_DK), F32),
        ],
        compiler_params=_params("parallel", "arbitrary"),
        name="ret_scan",
    )(*args)


def _outproj_kernel(h_ref, w_ref, x_ref, g_ref, gate_ref, o_ref):
    w = w_ref[...].astype(BF16)
    for r in range(o_ref.shape[0] // TR_OUT):
        rows = slice(r * TR_OUT, (r + 1) * TR_OUT)
        y = _dot(h_ref[rows, :], w)
        o_ref[rows, :] = x_ref[rows, :] + gate_ref[...] * (_rms(y) * g_ref[...])


def _outproj(h, w, j, x, ng4, mod5, layer, *, seq_len, sample):
    n_tok, hv = h.shape
    tm = TM_OUT
    row = _mod_row(sample, seq_len, tm)
    return pl.pallas_call(
        _outproj_kernel,
        grid=(n_tok // tm,),
        in_specs=[
            pl.BlockSpec((tm, hv), lambda i: (i, 0)),
            _resident((None, hv, D_MODEL), (j, 0, 0)),
            pl.BlockSpec((tm, D_MODEL), lambda i: (i, 0)),
            pl.BlockSpec((None, None, 1, D_MODEL), lambda i: (layer, 1, 0, 0)),
            pl.BlockSpec((None, None, None, 1, D_MODEL), lambda i: (layer, row(i), 2, 0, 0)),
        ],
        out_specs=pl.BlockSpec((tm, D_MODEL), lambda i: (i, 0)),
        out_shape=jax.ShapeDtypeStruct((n_tok, D_MODEL), F32),
        compiler_params=_params("parallel"),
        name="outproj",
    )(h, w, x, ng4, mod5)


def _conv3(hs_ref, half, h, cw, cb, seg, n_seg):
    for s in range(n_seg):
        base = SUBLANES + s * (seg + SUBLANES)
        h_seg = h[s * seg:(s + 1) * seg, :]
        hs_ref[2 * half, base + 1:base + 1 + seg, :] = h_seg
        hs_ref[2 * half + 1, base - 1:base - 1 + seg, :] = h_seg
    parts = []
    for s in range(n_seg):
        base = SUBLANES + s * (seg + SUBLANES)
        h_prev = hs_ref[2 * half, base:base + seg, :]
        h_next = hs_ref[2 * half + 1, base:base + seg, :]
        h_mid = h[s * seg:(s + 1) * seg, :]
        parts.append(h_prev * cw[0:1, :] + h_mid * cw[1:2, :] + h_next * cw[2:3, :] + cb)
    return parts


def _ffn_kernel(x_ref, g2_ref, sh_ref, sc_ref, wup_ref, cw_ref, cb_ref, wd_ref, g3_ref, gate_ref,
                o_ref, u_sc, act_sc, hs_sc, *, seg, n_seg, tf):
    u = _rms(x_ref[...]) * g2_ref[...] * (1.0 + sc_ref[...]) + sh_ref[...]
    u_sc[...] = u.astype(BF16)
    zero_rows = jnp.zeros((SUBLANES, tf), F32)
    for s in range(n_seg):
        base = SUBLANES + s * (seg + SUBLANES)
        for half in range(2):
            hs_sc[2 * half, base:base + SUBLANES, :] = zero_rows
            hs_sc[2 * half + 1, base + seg - SUBLANES:base + seg, :] = zero_rows

    for cidx in range(D_FF // tf):
        cg = slice(cidx * tf, (cidx + 1) * tf)
        cu = slice(D_FF + cidx * tf, D_FF + (cidx + 1) * tf)
        hg = _conv3(hs_sc, 0, _dot(u_sc[...], wup_ref[:, cg]), cw_ref[:, cg], cb_ref[:, cg], seg, n_seg)
        hu = _conv3(hs_sc, 1, _dot(u_sc[...], wup_ref[:, cu]), cw_ref[:, cu], cb_ref[:, cu], seg, n_seg)
        for s in range(n_seg):
            act = jax.nn.gelu(hg[s], approximate=True) * hu[s]
            act_sc[s * seg:(s + 1) * seg, cg] = act.astype(BF16)

    tr = o_ref.shape[0] // FFN_DOWN_BLOCKS
    for r in range(FFN_DOWN_BLOCKS):
        rows = slice(r * tr, (r + 1) * tr)
        f = _dot(act_sc[rows, :], wd_ref[...])
        o_ref[rows, :] = x_ref[rows, :] + gate_ref[...] * (_rms(f) * g3_ref[...])


def _ffn(x, ng4, mod5, layer, w_up, conv_w, conv_b, w_down, *, seq_len, sample):
    n_tok = x.shape[0]
    tm, tf = TM_FFN, TF_FFN
    row = _mod_row(sample, seq_len, tm)
    seg = GRID_W if sample else seq_len
    n_seg = tm // seg
    kern = functools.partial(_ffn_kernel, seg=seg, n_seg=n_seg, tf=tf)
    mod_spec = lambda k: pl.BlockSpec((None, None, None, 1, D_MODEL), lambda i: (layer, row(i), k, 0, 0))
    gain_spec = lambda k: pl.BlockSpec((None, None, 1, D_MODEL), lambda i: (layer, k, 0, 0))
    return pl.pallas_call(
        kern,
        grid=(n_tok // tm,),
        in_specs=[
            pl.BlockSpec((tm, D_MODEL), lambda i: (i, 0)),
            gain_spec(2),
            mod_spec(3),
            mod_spec(4),
            _resident((None, D_MODEL, 2 * D_FF), (layer, 0, 0)),
            _resident((None, 3, 2 * D_FF), (layer, 0, 0)),
            _resident((None, 1, 2 * D_FF), (layer, 0, 0)),
            _resident((None, D_FF, D_MODEL), (layer, 0, 0)),
            gain_spec(3),
            mod_spec(5),
        ],
        out_specs=pl.BlockSpec((tm, D_MODEL), lambda i: (i, 0)),
        out_shape=jax.ShapeDtypeStruct((n_tok, D_MODEL), F32),
        scratch_shapes=[
            pltpu.VMEM((tm, D_MODEL), BF16),
            pltpu.VMEM((tm, D_FF), BF16),
            pltpu.VMEM((4, SUBLANES + n_seg * (seg + SUBLANES), tf), F32),
        ],
        compiler_params=_params("parallel"),
        name="convffn",
    )(x, ng4, mod5, mod5, w_up, conv_w, conv_b, w_down, ng4, mod5)


def _rope_tables(seq_len):
    quarter = B_DK // 4
    inv = ROPE_BASE ** (-jnp.arange(quarter, dtype=F32) / quarter)
    t = jnp.arange(seq_len)
    rows = (t // GRID_W).astype(F32)[:, None] * inv
    cols = (t % GRID_W).astype(F32)[:, None] * inv
    cos = jnp.concatenate([jnp.cos(rows), jnp.cos(cols)] * 2, axis=-1)
    sin = jnp.concatenate([-jnp.sin(rows), -jnp.sin(cols), jnp.sin(rows), jnp.sin(cols)], axis=-1)
    return cos, sin


def _rope_qk_weights(w_in):
    n_l = w_in.shape[0]
    quarter = B_DK // 4
    w_qk = w_in[:, :, :B_QK].astype(BF16).reshape(n_l, D_MODEL, 2 * B_HEADS, 2, 2, quarter)
    return jnp.swapaxes(w_qk, 3, 4).reshape(n_l, D_MODEL, B_QK)


def _gate_weights(w_in, b_gate):
    n_l = w_in.shape[0]
    order = jnp.array((0, 2, 1, 3))
    wg = w_in[:, :, A_MAIN:].reshape(n_l, D_MODEL, 4, A_HEADS)[:, :, order, :]
    wg_t = jnp.transpose(wg, (0, 2, 3, 1)).reshape(n_l, 2 * NG, D_MODEL).astype(BF16)
    bg = b_gate[:, order, :].reshape(n_l, 2 * NG, 1)
    return wg_t, bg


def kernel(x_prompt, x_sample, state_mlstm_C, state_mlstm_n, state_mlstm_m, state_ret_S, c, c_ctx,
           norm_gain, ada_w, ada_b, ml_w_in, ml_b_gate, ml_norm, ml_w_out,
           ret_w_in, ret_decay, ret_norm, ret_w_out, ffn_w_up, ffn_conv, ffn_conv_b, ffn_w_down):
    bp, tp, _ = x_prompt.shape
    bs, ts, _ = x_sample.shape
    n_a = ml_w_in.shape[0]
    n_b = ret_w_in.shape[0]

    cond = jnp.concatenate([c_ctx[None, :], c, jnp.zeros((MOD_ROWS - 1 - bs, D_MODEL), F32)], axis=0)
    mod5 = _modulation(cond, ada_w, ada_b).reshape(DEPTH, MOD_ROWS, 6, 1, D_MODEL)
    ng4 = norm_gain.reshape(DEPTH, 4, 1, D_MODEL)
    rope = _rope_tables(ts)

    ml_gates = _gate_weights(ml_w_in, ml_b_gate)
    ret_w_qk_rope = _rope_qk_weights(ret_w_in)
    ffn_w_up_b = ffn_w_up.astype(BF16)
    ffn_w_down_b = ffn_w_down.astype(BF16)
    ffn_conv_b3 = ffn_conv_b.reshape(DEPTH, 1, 2 * D_FF)
    ml_gn4 = ml_norm.reshape(n_a, A_HEADS, 1, A_DV)
    ret_gn4 = ret_norm.reshape(n_b, B_HEADS, 1, B_DV)
    dec_rep = jnp.broadcast_to(jnp.swapaxes(ret_decay, 1, 2)[..., None], (n_b, B_HEADS, 2, B_DV))
    st_c = state_mlstm_C
    st_n = state_mlstm_n.reshape(bs, n_a, 2, A_HEADS, A_DK, 1)
    st_m = state_mlstm_m.reshape(bs, n_a, 2, A_HEADS, 1, 1)

    groups = [
        dict(x=x_prompt.reshape(bp * tp, D_MODEL), batch=bp, seq_len=tp, sample=False),
        dict(x=x_sample.reshape(bs * ts, D_MODEL), batch=bs, seq_len=ts, sample=True),
    ]
    ml_states = None
    ret_states = None
    for i in range(DEPTH):
        j = i // N_MIXERS
        for grp in groups:
            x = grp["x"]
            geo = dict(seq_len=grp["seq_len"], sample=grp["sample"])
            bt = dict(batch=grp["batch"], seq_len=grp["seq_len"])
            if i % N_MIXERS == 0:
                n_qk = A_HEADS * A_DK // TN_PROJ
                z, gates = _inproj(x, ng4, mod5, i, [(ml_w_in, j, 0)], A_MAIN, n_q=n_qk, n_k=n_qk,
                                    k_scale=A_DK ** -0.5, gates=ml_gates + (j,), **geo)
                if grp["sample"]:
                    (h,) = _mlstm_scan(z, gates, ml_gn4, j, n_layers=n_a, state=(st_c, st_n, st_m), **bt)
                else:
                    h, *ml_states = _mlstm_scan(z, gates, ml_gn4, j, n_layers=n_a, prev=ml_states, **bt)
                x = _outproj(h, ml_w_out, j, x, ng4, mod5, i, **geo)
            else:
                n_qk = B_HEADS * B_DK // TN_PROJ
                w_qk = ret_w_qk_rope if grp["sample"] else ret_w_in
                w_parts = [(w_qk, j, 0), (ret_w_in, j, 1), (ret_w_in, j, 2)]
                (z,) = _inproj(x, ng4, mod5, i, w_parts, B_QK, n_q=n_qk, n_k=n_qk, k_scale=B_DK ** -0.5,
                               rope=rope if grp["sample"] else None, **geo)
                if grp["sample"]:
                    (h,) = _ret_scan(z, dec_rep, ret_gn4, j, n_layers=n_b, state=state_ret_S, **bt)
                else:
                    h, ret_states = _ret_scan(z, dec_rep, ret_gn4, j, n_layers=n_b, prev=ret_states, **bt)
                x = _outproj(h, ret_w_out, j, x, ng4, mod5, i, **geo)
            grp["x"] = _ffn(x, ng4, mod5, i, ffn_w_up_b, ffn_conv, ffn_conv_b3, ffn_w_down_b, **geo)

    y_prompt = groups[0]["x"].reshape(bp, tp, D_MODEL)
    y_sample = groups[1]["x"].reshape(bs, ts, D_MODEL)
    new_c, new_n, new_m = ml_states
    return (y_prompt, y_sample, new_c, new_n.reshape(bp, n_a, 2, A_HEADS, A_DK),
            new_m.reshape(bp, n_a, 2, A_HEADS), ret_states)
```

```python
import functools
import math

import jax
import jax.numpy as jnp
from jax import lax
from jax.experimental import pallas as pl
from jax.experimental.pallas import tpu as pltpu

D_MODEL = 1024
DEPTH = 4
GRID_W = 64
CHUNK = 128
N_MIXERS = 2
A_HEADS = 4
A_DV = D_MODEL // A_HEADS
A_DK = A_DV // 2
A_MAIN = 2 * A_HEADS * A_DK + 2 * A_HEADS * A_DV
B_HEADS = 8
B_DK = D_MODEL // B_HEADS
B_DV = 2 * D_MODEL // B_HEADS
B_QK = 2 * B_HEADS * B_DK
ROPE_BASE = 10000.0
D_FF = ((8 * D_MODEL // 3 + 127) // 128) * 128
EPS = 1e-6
LN2 = math.log(2.0)

F32 = jnp.float32
BF16 = jnp.bfloat16

LANES = 128
SUBLANES = 8
MOD_ROWS = 16
VMEM_LIMIT = 48 * 1024 * 1024

TM_PROJ = 512
TN_PROJ = 512
TM_OUT = 1024
TR_OUT = 512
TM_FFN = 512
TF_FFN = 256
FFN_DOWN_BLOCKS = 2
GATE_UNROLL = 16
STATE_UNROLL = 16
OUT_UNROLL = 16
MLSTM_OUT_UNROLL = 16
SCAN_TOKENS = 2048


def _dot(a, b):
    return jnp.dot(a, b, preferred_element_type=F32)


def _dot_nt(a, b):
    return lax.dot_general(a, b, (((1,), (1,)), ((), ())), preferred_element_type=F32)


def _dot_tn(a, b):
    return lax.dot_general(a, b, (((0,), (0,)), ((), ())), preferred_element_type=F32)


def _rms(x):
    return x * lax.rsqrt(jnp.mean(x * x, axis=-1, keepdims=True) + EPS)


def _layer_norm(h):
    d = h - jnp.mean(h, axis=-1, keepdims=True)
    return d * lax.rsqrt(jnp.mean(d * d, axis=-1, keepdims=True) + EPS)


def _params(*sem):
    return pltpu.CompilerParams(dimension_semantics=sem, vmem_limit_bytes=VMEM_LIMIT)


def _resident(shape, index):
    return pl.BlockSpec(shape, lambda *_: index, pipeline_mode=pl.Buffered(1))


def _mod_kernel(cond_ref, w_ref, b_ref, o_ref):
    cnd = cond_ref[...]
    s = cnd * jax.nn.sigmoid(cnd)
    o_ref[...] = _dot(s.astype(BF16), w_ref[...].astype(BF16)) + b_ref[...]


def _modulation(cond, ada_w, ada_b):
    tn = 1024
    n_out = ada_w.shape[-1]
    return pl.pallas_call(
        _mod_kernel,
        grid=(DEPTH, n_out // tn),
        in_specs=[
            pl.BlockSpec((MOD_ROWS, D_MODEL), lambda l, j: (0, 0)),
            pl.BlockSpec((None, D_MODEL, tn), lambda l, j: (l, 0, j)),
            pl.BlockSpec((None, 1, tn), lambda l, j: (l, 0, j)),
        ],
        out_specs=pl.BlockSpec((None, MOD_ROWS, tn), lambda l, j: (l, 0, j)),
        out_shape=jax.ShapeDtypeStruct((DEPTH, MOD_ROWS, n_out), F32),
        compiler_params=_params("parallel", "parallel"),
        name="modulation",
    )(cond, ada_w, ada_b.reshape(DEPTH, 1, n_out))


def _mod_row(sample, seq_len, tm):
    if not sample:
        return lambda i: 0
    tiles_per_seq = seq_len // tm
    return lambda i: 1 + i // tiles_per_seq


def _rope_slab(x, cos, sin):
    return x * cos + pltpu.roll(x, 64, axis=1) * sin


def _inproj_kernel(*refs, n_w, n_q, n_k, k_scale, rope, gates, tn):
    x_ref, g_ref, sh_ref, sc_ref = refs[:4]
    w_refs = refs[4:4 + n_w]
    pos = 4 + n_w
    if gates:
        wg_ref, bg_ref = refs[pos:pos + 2]
        pos += 2
    if rope:
        cos_ref, sin_ref = refs[pos:pos + 2]
        pos += 2
    z_ref = refs[pos]
    pos += 1
    if gates:
        gates_ref = refs[pos]
        pos += 1
    u_sc = refs[pos]

    u = _rms(x_ref[...]) * g_ref[...] * (1.0 + sc_ref[...]) + sh_ref[...]
    u_sc[...] = u.astype(BF16)
    if gates:
        gates_ref[...] = _dot_nt(wg_ref[...], u_sc[...]) + bg_ref[...]

    wb = w_refs[0].shape[1]
    for j in range(z_ref.shape[1] // tn):
        part, off = divmod(j * tn, wb)
        z = _dot(u_sc[...], w_refs[part][:, off:off + tn].astype(BF16))
        scale = k_scale if n_q <= j < n_q + n_k else 1.0
        if rope and j < n_q + n_k:
            for s in range(tn // LANES):
                r = _rope_slab(z[:, s * LANES:(s + 1) * LANES], cos_ref[...], sin_ref[...])
                if scale != 1.0:
                    r = r * scale
                z_ref[:, j * tn + s * LANES:j * tn + (s + 1) * LANES] = r.astype(BF16)
        elif scale != 1.0:
            z_ref[:, j * tn:(j + 1) * tn] = (z * scale).astype(BF16)
        else:
            z_ref[:, j * tn:(j + 1) * tn] = z.astype(BF16)


def _inproj(x, ng4, mod5, layer, w_parts, wb, *, seq_len, sample, n_q, n_k, k_scale, rope=None, gates=None):
    n_tok = x.shape[0]
    tm, tn = TM_PROJ, TN_PROJ
    n_col = wb * len(w_parts)
    row = _mod_row(sample, seq_len, tm)
    in_specs = [
        pl.BlockSpec((tm, D_MODEL), lambda i: (i, 0)),
        pl.BlockSpec((None, None, 1, D_MODEL), lambda i: (layer, 0, 0, 0)),
        pl.BlockSpec((None, None, None, 1, D_MODEL), lambda i: (layer, row(i), 0, 0, 0)),
        pl.BlockSpec((None, None, None, 1, D_MODEL), lambda i: (layer, row(i), 1, 0, 0)),
    ]
    in_specs += [_resident((None, D_MODEL, wb), (jl, 0, blk)) for _, jl, blk in w_parts]
    args = [x, ng4, mod5, mod5] + [w for w, _, _ in w_parts]
    out_specs = [pl.BlockSpec((tm, n_col), lambda i: (i, 0))]
    out_shape = [jax.ShapeDtypeStruct((n_tok, n_col), BF16)]
    if gates is not None:
        wg_t, bg, jg = gates
        n_g = wg_t.shape[1]
        in_specs += [_resident((None, n_g, D_MODEL), (jg, 0, 0)), _resident((None, n_g, 1), (jg, 0, 0))]
        args += [wg_t, bg]
        out_specs += [pl.BlockSpec((n_g, tm), lambda i: (0, i))]
        out_shape += [jax.ShapeDtypeStruct((n_g, n_tok), F32)]
    if rope is not None:
        tiles_per_seq = seq_len // tm
        in_specs += [pl.BlockSpec((tm, LANES), lambda i: (i % tiles_per_seq, 0))] * 2
        args += list(rope)
    kern = functools.partial(_inproj_kernel, n_w=len(w_parts), n_q=n_q, n_k=n_k, k_scale=k_scale,
                             rope=rope is not None, gates=gates is not None, tn=tn)
    return pl.pallas_call(
        kern,
        grid=(n_tok // tm,),
        in_specs=in_specs,
        out_specs=out_specs,
        out_shape=out_shape,
        scratch_shapes=[pltpu.VMEM((tm, D_MODEL), BF16)],
        compiler_params=_params("parallel"),
        name="inproj",
    )(*args)


def _tri_masks():
    li = lax.broadcasted_iota(jnp.int32, (CHUNK, CHUNK), 0)
    si = lax.broadcasted_iota(jnp.int32, (CHUNK, CHUNK), 1)
    return si <= li, si >= li


def _layer_slot(ref, fresh_slot):
    if fresh_slot is None:
        return ref
    for other in range(ref.shape[1]):
        if other != fresh_slot:
            ref[:, other] = jnp.zeros(ref.shape[:1] + ref.shape[2:], ref.dtype)
    return ref.at[:, fresh_slot]


def _seqs_per_step(batch, seq_len, carry_in):
    if carry_in:
        return 1
    nb = max(1, SCAN_TOKENS // seq_len)
    while batch % nb:
        nb -= 1
    return nb


NG = 2 * A_HEADS


def _split_dot(x, mask_b):
    hi = x.astype(BF16)
    r1 = x - hi.astype(F32)
    mid = r1.astype(BF16)
    lo = (r1 - mid.astype(F32)).astype(BF16)
    return _dot(hi, mask_b) + _dot(mid, mask_b) + _dot(lo, mask_b)


def _mlstm_kernel(*refs, n_chunks, nb, carry_in, carry_out, n_alias, fresh_slot):
    q_ref, k_ref, v_ref, o_ref, g_ref, gn_ref = refs[:6]
    pos = 6
    if carry_in:
        c0_ref, n0_ref, m0_ref = refs[pos:pos + 3]
        pos += 3
    pos += n_alias
    h_ref = refs[pos]
    pos += 1
    if carry_out:
        cout_ref, nout_ref, mout_ref = refs[pos:pos + 3]
        pos += 3
    ab_sc, bt_sc, g_sc, bm_sc, mpf_sc, mpb_sc, c_sc, call_sc = refs[pos:pos + 8]

    head = pl.program_id(1)
    masks = _tri_masks()
    ones_b = jnp.ones((CHUNK, LANES), BF16)
    sum_b = jnp.concatenate([masks[1].astype(BF16), ones_b], axis=1)
    lane = lax.broadcasted_iota(jnp.int32, (CHUNK, LANES), 1)
    grow = lax.broadcasted_iota(jnp.int32, (2 * NG, CHUNK), 0)
    gsub = lax.broadcasted_iota(jnp.int32, (NG, LANES), 0)
    zpad = jnp.zeros((LANES - 2 * NG, CHUNK), F32)
    cols = (head, head + A_HEADS)
    mp_sc = (mpf_sc, mpb_sc)
    t_seq = n_chunks * CHUNK

    def pick_col(x, col):
        return jnp.sum(jnp.where(lane == col, x, 0.0), axis=1, keepdims=True)

    def at(s, c):
        return pl.ds(pl.multiple_of(s * t_seq + c * CHUNK, CHUNK), CHUNK)

    def gate_body(c, carry):
        for s in range(nb):
            idx = s * n_chunks + c
            gates = g_ref[:, at(s, c)]
            lf = jnp.minimum(gates, 0.0) - jnp.log1p(jnp.exp(-jnp.abs(gates)))
            lf = jnp.where(grow >= NG, lf, 0.0)
            sums = _split_dot(lf, sum_b)
            a_f = sums[:, :CHUNK]
            tot = sums[:, CHUNK:]
            a_all = jnp.where(grow < NG + A_HEADS, a_f, tot - a_f + lf)[NG:, :]
            b_all = gates[:NG, :] - a_all
            bt_sc[idx] = b_all
            ab_sc[at(s, c), :] = jnp.concatenate([a_all, b_all, zpad], axis=0).T
            g_sc[idx] = tot[NG:, :]
            bm_sc[idx] = jnp.broadcast_to(jnp.max(b_all, axis=1, keepdims=True), (NG, LANES))
        return carry

    lax.fori_loop(0, n_chunks, gate_body, 0, unroll=min(n_chunks, GATE_UNROLL))

    if carry_in:
        m_init = (jnp.broadcast_to(m0_ref[0], (NG, LANES)), jnp.broadcast_to(m0_ref[1], (NG, LANES)))
    else:
        m_init = (jnp.zeros((NG, LANES), F32),) * (2 * nb)

    def m_body(i, carry):
        out = []
        for s in range(nb):
            m_f, m_b = carry[2 * s], carry[2 * s + 1]
            jf = s * n_chunks + i
            jb = s * n_chunks + n_chunks - 1 - i
            mpf_sc[jf] = m_f
            mpb_sc[jb] = m_b
            out.append(g_sc[jf] + jnp.maximum(m_f, bm_sc[jf]))
            out.append(g_sc[jb] + jnp.maximum(m_b, bm_sc[jb]))
        return tuple(out)

    m_last = lax.fori_loop(0, n_chunks, m_body, m_init, unroll=min(n_chunks, STATE_UNROLL))

    if carry_in:
        for dirn in range(2):
            c_sc[dirn, :, :A_DV] = c0_ref[dirn]
            c_sc[dirn, :, A_DV:] = jnp.broadcast_to(n0_ref[dirn], (A_DK, LANES))
    else:
        c_sc[...] = jnp.zeros_like(c_sc)

    def state_body(i, carry):
        for s in range(nb):
            for dirn in range(2):
                c = i if dirn == 0 else n_chunks - 1 - i
                idx = s * n_chunks + c
                mp_row = mp_sc[dirn][idx, pl.ds(cols[dirn], 1), :]
                m_top = jnp.maximum(mp_row, bm_sc[idx, pl.ds(cols[dirn], 1), :])
                ws = jnp.exp(pick_col(ab_sc[at(s, c), :], NG + cols[dirn]) - m_top)
                dec = jnp.exp(mp_row - m_top)
                dec = jnp.concatenate([dec] * (c_sc.shape[2] // LANES), axis=1)
                c_old = c_sc[2 * s + dirn]
                v_ext = jnp.concatenate([v_ref[at(s, c), :], ones_b], axis=1)
                call_sc[2 * s + dirn, c, :CHUNK, :] = v_ext
                call_sc[2 * s + dirn, c, CHUNK:, :] = c_old.astype(BF16)
                kw = (k_ref[at(s, c), :].astype(F32) * ws).astype(BF16)
                c_sc[2 * s + dirn] = dec * c_old + _dot_tn(kw, v_ext)
        return carry

    lax.fori_loop(0, n_chunks, state_body, 0, unroll=min(n_chunks, STATE_UNROLL))
    if carry_out:
        outs = [_layer_slot(r, fresh_slot) for r in (cout_ref, nout_ref, mout_ref)]
        for s in range(nb):
            for dirn in range(2):
                outs[0][s, dirn] = c_sc[2 * s + dirn, :, :A_DV]
                outs[1][s, dirn] = c_sc[2 * s + dirn, :, A_DV:].T[0:1, :]
                m_end = jnp.where(gsub == cols[dirn], m_last[2 * s + dirn], 0.0)
                outs[2][s, dirn] = jnp.sum(m_end, axis=0, keepdims=True)[:, 0:1]

    def out_body(c, carry):
        for s in range(nb):
            q = q_ref[at(s, c), :]
            qf = q.astype(F32)
            s_raw = _dot_nt(q, k_ref[at(s, c), :])
            a_chunk = ab_sc[at(s, c), :]
            idx = s * n_chunks + c
            h = None
            for dirn in range(2):
                col = cols[dirn]
                m_prev = mp_sc[dirn][idx, pl.ds(col, 1), :]
                b_vis = jnp.where(masks[dirn], bt_sc[idx, pl.ds(col, 1), :], -jnp.inf)
                m_row = jnp.maximum(m_prev, jnp.max(b_vis, axis=1, keepdims=True))
                sw = (s_raw * jnp.exp(b_vis - m_row)).astype(BF16)
                w_inter = jnp.exp(m_prev - m_row)
                floor = jnp.exp(-(pick_col(a_chunk, col) + m_row))
                qw = (qf * w_inter).astype(BF16)
                nd = _dot(jnp.concatenate([sw, qw], axis=1), call_sc[2 * s + dirn, c])
                r = 1.0 / jnp.maximum(jnp.abs(nd[:, A_DV:]), floor)
                hd = nd[:, :A_DV] * jnp.concatenate([r, r], axis=1)
                h = hd if h is None else h + hd
            o = o_ref[at(s, c), :].astype(F32)
            h_ref[at(s, c), :] = (_layer_norm(h) * gn_ref[...] * jax.nn.sigmoid(o)).astype(BF16)
        return carry

    lax.fori_loop(0, n_chunks, out_body, 0, unroll=min(n_chunks, max(1, MLSTM_OUT_UNROLL // nb)))


def _mlstm_scan(z, gates, gn4, j, *, batch, seq_len, n_layers, state=None, prev=None):
    n_tok = z.shape[0]
    n_chunks = seq_len // CHUNK
    carry_in = state is not None
    carry_out = not carry_in
    nb = _seqs_per_step(batch, seq_len, carry_in)
    t = nb * seq_len
    in_specs = [
        pl.BlockSpec((t, A_DK), lambda b, h: (b, h)),
        pl.BlockSpec((t, A_DK), lambda b, h: (b, A_HEADS + h)),
        pl.BlockSpec((t, A_DV), lambda b, h: (b, A_HEADS + h)),
        pl.BlockSpec((t, A_DV), lambda b, h: (b, 2 * A_HEADS + h)),
        pl.BlockSpec((2 * NG, t), lambda b, h: (0, b)),
        pl.BlockSpec((None, None, 1, A_DV), lambda b, h: (j, h, 0, 0)),
    ]
    args = [z, z, z, z, gates, gn4]
    aliases = {}
    if carry_in:
        in_specs += [
            pl.BlockSpec((None, None, 2, None, A_DK, A_DV), lambda b, h: (b, j, 0, h, 0, 0)),
            pl.BlockSpec((None, None, 2, None, A_DK, 1), lambda b, h: (b, j, 0, h, 0, 0)),
            pl.BlockSpec((None, None, 2, None, 1, 1), lambda b, h: (b, j, 0, h, 0, 0)),
        ]
        args += list(state)
    out_specs = [pl.BlockSpec((t, A_DV), lambda b, h: (b, h))]
    out_shape = [jax.ShapeDtypeStruct((n_tok, A_HEADS * A_DV), BF16)]
    fresh_slot = j if carry_out and prev is None else None
    if carry_out:
        lay, jb = (n_layers, 0) if prev is None else (None, j)
        out_specs += [
            pl.BlockSpec((nb, lay, 2, None, A_DK, A_DV), lambda b, h: (b, jb, 0, h, 0, 0)),
            pl.BlockSpec((nb, lay, 2, None, 1, A_DK), lambda b, h: (b, jb, 0, h, 0, 0)),
            pl.BlockSpec((nb, lay, 2, None, 1, 1), lambda b, h: (b, jb, 0, h, 0, 0)),
        ]
        out_shape += [
            jax.ShapeDtypeStruct((batch, n_layers, 2, A_HEADS, A_DK, A_DV), F32),
            jax.ShapeDtypeStruct((batch, n_layers, 2, A_HEADS, 1, A_DK), F32),
            jax.ShapeDtypeStruct((batch, n_layers, 2, A_HEADS, 1, 1), F32),
        ]
        if prev is not None:
            aliases = {len(args) + k: 1 + k for k in range(3)}
            in_specs += [pl.BlockSpec(memory_space=pl.ANY)] * 3
            args += list(prev)
    kern = functools.partial(_mlstm_kernel, n_chunks=n_chunks, nb=nb, carry_in=carry_in,
                             carry_out=carry_out, n_alias=len(aliases), fresh_slot=fresh_slot)
    return pl.pallas_call(
        kern,
        grid=(batch // nb, A_HEADS),
        in_specs=in_specs,
        out_specs=out_specs,
        out_shape=out_shape,
        input_output_aliases=aliases,
        scratch_shapes=[
            pltpu.VMEM((t, LANES), F32),
            pltpu.VMEM((nb * n_chunks, NG, CHUNK), F32),
            pltpu.VMEM((nb * n_chunks, NG, LANES), F32),
            pltpu.VMEM((nb * n_chunks, NG, LANES), F32),
            pltpu.VMEM((nb * n_chunks, NG, LANES), F32),
            pltpu.VMEM((nb * n_chunks, NG, LANES), F32),
            pltpu.VMEM((2 * nb, A_DK, A_DV + LANES), F32),
            pltpu.VMEM((2 * nb, n_chunks, CHUNK + A_DK, A_DV + LANES), BF16),
        ],
        compiler_params=_params("parallel", "parallel"),
        name="mlstm_scan",
    )(*args)


def _ret_kernel(*refs, n_chunks, nb, carry_in, carry_out, n_alias, fresh_slot):
    q_ref, k_ref, v_ref, gate_ref, dec_ref, gn_ref = refs[:6]
    pos = 6
    if carry_in:
        s0_ref = refs[pos]
        pos += 1
    pos += n_alias
    h_ref = refs[pos]
    pos += 1
    if carry_out:
        sout_ref = refs[pos]
        pos += 1
    s_sc, sall_sc, dsum_sc, xi_sc, zeta_sc = refs[pos:pos + 5]
    t_seq = n_chunks * CHUNK

    def at(s, c):
        return pl.ds(pl.multiple_of(s * t_seq + c * CHUNK, CHUNK), CHUNK)

    lg = jnp.log1p(-jnp.exp(-dec_ref[...] * LN2))
    lg_f = lg[0:1, :]
    lg_b = lg[1:2, :]

    @pl.when(pl.program_id(1) == 0)
    def _():
        masks = _tri_masks()
        li = lax.broadcasted_iota(jnp.int32, (CHUNK, B_DV), 0).astype(F32)
        si = lax.broadcasted_iota(jnp.int32, (CHUNK, CHUNK), 1).astype(F32)
        lq = li[:, :CHUNK]
        dsum_sc[...] = (
            jnp.where(masks[0], jnp.exp(jnp.where(masks[0], lq - si, 0.0) * lg_f[:, :CHUNK]), 0.0)
            + jnp.where(masks[1], jnp.exp(jnp.where(masks[1], si - lq, 0.0) * lg_b[:, :CHUNK]), 0.0))
        xi_sc[0] = jnp.exp((lq + 1.0) * lg_f[:, :CHUNK])
        xi_sc[1] = jnp.exp((CHUNK - lq) * lg_b[:, :CHUNK])
        zeta_sc[0] = jnp.exp((CHUNK - 1.0 - lq) * lg_f[:, :CHUNK])
        zeta_sc[1] = jnp.exp(lq * lg_b[:, :CHUNK])

    cdec = (jnp.exp(CHUNK * lg_f), jnp.exp(CHUNK * lg_b))

    if carry_in:
        qr = B_DK // 4
        for dirn in range(2):
            for n, o in enumerate((0, 2, 1, 3)):
                s_sc[dirn, n * qr:(n + 1) * qr, :] = s0_ref[dirn, o * qr:(o + 1) * qr, :]
    else:
        s_sc[...] = jnp.zeros_like(s_sc)

    def state_body(i, carry):
        for s in range(nb):
            for dirn in range(2):
                c = i if dirn == 0 else n_chunks - 1 - i
                s_old = s_sc[2 * s + dirn]
                sall_sc[s, c, dirn * B_DK:(dirn + 1) * B_DK, :] = s_old.astype(BF16)
                kz = (k_ref[at(s, c), :].astype(F32) * zeta_sc[dirn]).astype(BF16)
                s_sc[2 * s + dirn] = cdec[dirn] * s_old + _dot_tn(kz, v_ref[at(s, c), :])
        return carry

    lax.fori_loop(0, n_chunks, state_body, 0, unroll=min(n_chunks, STATE_UNROLL))
    if carry_out:
        s_out = _layer_slot(sout_ref, fresh_slot)
        for s in range(nb):
            for dirn in range(2):
                s_out[s, dirn] = s_sc[2 * s + dirn]

    def out_body(c, carry):
        for s in range(nb):
            q = q_ref[at(s, c), :]
            v = v_ref[at(s, c), :]
            sw = _dot_nt(q, k_ref[at(s, c), :]) * dsum_sc[...]
            qf = q.astype(F32)
            qx = jnp.concatenate([qf * xi_sc[0], qf * xi_sc[1]], axis=1).astype(BF16)
            h = _dot(sw.astype(BF16), v) + _dot(qx, sall_sc[s, c])
            g = gate_ref[at(s, c), :].astype(F32)
            h_ref[at(s, c), :] = (_layer_norm(h) * gn_ref[...] * (g * jax.nn.sigmoid(g))).astype(BF16)
        return carry

    lax.fori_loop(0, n_chunks, out_body, 0, unroll=min(n_chunks, max(1, OUT_UNROLL // nb)))


def _ret_scan(z, dec_rep, gn4, j, *, batch, seq_len, n_layers, state=None, prev=None):
    n_tok = z.shape[0]
    n_chunks = seq_len // CHUNK
    carry_in = state is not None
    carry_out = not carry_in
    nb = _seqs_per_step(batch, seq_len, carry_in)
    t = nb * seq_len
    in_specs = [
        pl.BlockSpec((t, B_DK), lambda h, b: (b, h)),
        pl.BlockSpec((t, B_DK), lambda h, b: (b, B_HEADS + h)),
        pl.BlockSpec((t, B_DV), lambda h, b: (b, B_HEADS + h)),
        pl.BlockSpec((t, B_DV), lambda h, b: (b, 2 * B_HEADS + h)),
        pl.BlockSpec((None, None, 2, B_DV), lambda h, b: (j, h, 0, 0)),
        pl.BlockSpec((None, None, 1, B_DV), lambda h, b: (j, h, 0, 0)),
    ]
    args = [z, z, z, z, dec_rep, gn4]
    aliases = {}
    if carry_in:
        in_specs += [pl.BlockSpec((None, None, 2, None, B_DK, B_DV), lambda h, b: (b, j, 0, h, 0, 0))]
        args += [state]
    out_specs = [pl.BlockSpec((t, B_DV), lambda h, b: (b, h))]
    out_shape = [jax.ShapeDtypeStruct((n_tok, B_HEADS * B_DV), BF16)]
    fresh_slot = j if carry_out and prev is None else None
    if carry_out:
        lay, jb = (n_layers, 0) if prev is None else (None, j)
        out_specs += [pl.BlockSpec((nb, lay, 2, None, B_DK, B_DV), lambda h, b: (b, jb, 0, h, 0, 0))]
        out_shape += [jax.ShapeDtypeStruct((batch, n_layers, 2, B_HEADS, B_DK, B_DV), F32)]
        if prev is not None:
            aliases = {len(args): 1}
            in_specs += [pl.BlockSpec(memory_space=pl.ANY)]
            args += [prev]
    kern = functools.partial(_ret_kernel, n_chunks=n_chunks, nb=nb, carry_in=carry_in,
                             carry_out=carry_out, n_alias=len(aliases), fresh_slot=fresh_slot)
    return pl.pallas_call(
        kern,
        grid=(B_HEADS, batch // nb),
        in_specs=in_specs,
        out_specs=out_specs,
        out_shape=out_shape,
        input_output_aliases=aliases,
        scratch_shapes=[
            pltpu.VMEM((2 * nb, B_DK, B_DV), F32),
            pltpu.VMEM((nb, n_chunks, 2 * B_DK, B_DV), BF16),
            pltpu.VMEM((CHUNK, CHUNK), F32),
            pltpu.VMEM((2, CHUNK, B_DK), F32),
            pltpu.VMEM((2, CHUNK, B_DK), F32),
        ],
        compiler_params=_params("parallel", "arbitrary"),
        name="ret_scan",
    )(*args)


def _outproj_kernel(h_ref, w_ref, x_ref, g_ref, gate_ref, o_ref):
    w = w_ref[...].astype(BF16)
    for r in range(o_ref.shape[0] // TR_OUT):
        rows = slice(r * TR_OUT, (r + 1) * TR_OUT)
        y = _dot(h_ref[rows, :], w)
        o_ref[rows, :] = x_ref[rows, :] + gate_ref[...] * (_rms(y) * g_ref[...])


def _outproj(h, w, j, x, ng4, mod5, layer, *, seq_len, sample):
    n_tok, hv = h.shape
    tm = TM_OUT
    row = _mod_row(sample, seq_len, tm)
    return pl.pallas_call(
        _outproj_kernel,
        grid=(n_tok // tm,),
        in_specs=[
            pl.BlockSpec((tm, hv), lambda i: (i, 0)),
            _resident((None, hv, D_MODEL), (j, 0, 0)),
            pl.BlockSpec((tm, D_MODEL), lambda i: (i, 0)),
            pl.BlockSpec((None, None, 1, D_MODEL), lambda i: (layer, 1, 0, 0)),
            pl.BlockSpec((None, None, None, 1, D_MODEL), lambda i: (layer, row(i), 2, 0, 0)),
        ],
        out_specs=pl.BlockSpec((tm, D_MODEL), lambda i: (i, 0)),
        out_shape=jax.ShapeDtypeStruct((n_tok, D_MODEL), F32),
        compiler_params=_params("parallel"),
        name="outproj",
    )(h, w, x, ng4, mod5)


def _conv3(hs_ref, half, h, cw, cb, seg, n_seg):
    for s in range(n_seg):
        base = SUBLANES + s * (seg + SUBLANES)
        h_seg = h[s * seg:(s + 1) * seg, :]
        hs_ref[2 * half, base + 1:base + 1 + seg, :] = h_seg
        hs_ref[2 * half + 1, base - 1:base - 1 + seg, :] = h_seg
    parts = []
    for s in range(n_seg):
        base = SUBLANES + s * (seg + SUBLANES)
        h_prev = hs_ref[2 * half, base:base + seg, :]
        h_next = hs_ref[2 * half + 1, base:base + seg, :]
        h_mid = h[s * seg:(s + 1) * seg, :]
        parts.append(h_prev * cw[0:1, :] + h_mid * cw[1:2, :] + h_next * cw[2:3, :] + cb)
    return parts


def _ffn_kernel(x_ref, g2_ref, sh_ref, sc_ref, wup_ref, cw_ref, cb_ref, wd_ref, g3_ref, gate_ref,
                o_ref, u_sc, act_sc, hs_sc, *, seg, n_seg, tf):
    u = _rms(x_ref[...]) * g2_ref[...] * (1.0 + sc_ref[...]) + sh_ref[...]
    u_sc[...] = u.astype(BF16)
    zero_rows = jnp.zeros((SUBLANES, tf), F32)
    for s in range(n_seg):
        base = SUBLANES + s * (seg + SUBLANES)
        for half in range(2):
            hs_sc[2 * half, base:base + SUBLANES, :] = zero_rows
            hs_sc[2 * half + 1, base + seg - SUBLANES:base + seg, :] = zero_rows

    for cidx in range(D_FF // tf):
        cg = slice(cidx * tf, (cidx + 1) * tf)
        cu = slice(D_FF + cidx * tf, D_FF + (cidx + 1) * tf)
        hg = _conv3(hs_sc, 0, _dot(u_sc[...], wup_ref[:, cg]), cw_ref[:, cg], cb_ref[:, cg], seg, n_seg)
        hu = _conv3(hs_sc, 1, _dot(u_sc[...], wup_ref[:, cu]), cw_ref[:, cu], cb_ref[:, cu], seg, n_seg)
        for s in range(n_seg):
            act = jax.nn.gelu(hg[s], approximate=True) * hu[s]
            act_sc[s * seg:(s + 1) * seg, cg] = act.astype(BF16)

    tr = o_ref.shape[0] // FFN_DOWN_BLOCKS
    for r in range(FFN_DOWN_BLOCKS):
        rows = slice(r * tr, (r + 1) * tr)
        f = _dot(act_sc[rows, :], wd_ref[...])
        o_ref[rows, :] = x_ref[rows, :] + gate_ref[...] * (_rms(f) * g3_ref[...])


def _ffn(x, ng4, mod5, layer, w_up, conv_w, conv_b, w_down, *, seq_len, sample):
    n_tok = x.shape[0]
    tm, tf = TM_FFN, TF_FFN
    row = _mod_row(sample, seq_len, tm)
    seg = GRID_W if sample else seq_len
    n_seg = tm // seg
    kern = functools.partial(_ffn_kernel, seg=seg, n_seg=n_seg, tf=tf)
    mod_spec = lambda k: pl.BlockSpec((None, None, None, 1, D_MODEL), lambda i: (layer, row(i), k, 0, 0))
    gain_spec = lambda k: pl.BlockSpec((None, None, 1, D_MODEL), lambda i: (layer, k, 0, 0))
    return pl.pallas_call(
        kern,
        grid=(n_tok // tm,),
        in_specs=[
            pl.BlockSpec((tm, D_MODEL), lambda i: (i, 0)),
            gain_spec(2),
            mod_spec(3),
            mod_spec(4),
            _resident((None, D_MODEL, 2 * D_FF), (layer, 0, 0)),
            _resident((None, 3, 2 * D_FF), (layer, 0, 0)),
            _resident((None, 1, 2 * D_FF), (layer, 0, 0)),
            _resident((None, D_FF, D_MODEL), (layer, 0, 0)),
            gain_spec(3),
            mod_spec(5),
        ],
        out_specs=pl.BlockSpec((tm, D_MODEL), lambda i: (i, 0)),
        out_shape=jax.ShapeDtypeStruct((n_tok, D_MODEL), F32),
        scratch_shapes=[
            pltpu.VMEM((tm, D_MODEL), BF16),
            pltpu.VMEM((tm, D_FF), BF16),
            pltpu.VMEM((4, SUBLANES + n_seg * (seg + SUBLANES), tf), F32),
        ],
        compiler_params=_params("parallel"),
        name="convffn",
    )(x, ng4, mod5, mod5, w_up, conv_w, conv_b, w_down, ng4, mod5)


def _rope_tables(seq_len):
    quarter = B_DK // 4
    inv = ROPE_BASE ** (-jnp.arange(quarter, dtype=F32) / quarter)
    t = jnp.arange(seq_len)
    rows = (t // GRID_W).astype(F32)[:, None] * inv
    cols = (t % GRID_W).astype(F32)[:, None] * inv
    cos = jnp.concatenate([jnp.cos(rows), jnp.cos(cols)] * 2, axis=-1)
    sin = jnp.concatenate([-jnp.sin(rows), -jnp.sin(cols), jnp.sin(rows), jnp.sin(cols)], axis=-1)
    return cos, sin


def _rope_qk_weights(w_in):
    n_l = w_in.shape[0]
    quarter = B_DK // 4
    w_qk = w_in[:, :, :B_QK].astype(BF16).reshape(n_l, D_MODEL, 2 * B_HEADS, 2, 2, quarter)
    return jnp.swapaxes(w_qk, 3, 4).reshape(n_l, D_MODEL, B_QK)


def _gate_weights(w_in, b_gate):
    n_l = w_in.shape[0]
    order = jnp.array((0, 2, 1, 3))
    wg = w_in[:, :, A_MAIN:].reshape(n_l, D_MODEL, 4, A_HEADS)[:, :, order, :]
    wg_t = jnp.transpose(wg, (0, 2, 3, 1)).reshape(n_l, 2 * NG, D_MODEL).astype(BF16)
    bg = b_gate[:, order, :].reshape(n_l, 2 * NG, 1)
    return wg_t, bg


def kernel(x_prompt, x_sample, state_mlstm_C, state_mlstm_n, state_mlstm_m, state_ret_S, c, c_ctx,
           norm_gain, ada_w, ada_b, ml_w_in, ml_b_gate, ml_norm, ml_w_out,
           ret_w_in, ret_decay, ret_norm, ret_w_out, ffn_w_up, ffn_conv, ffn_conv_b, ffn_w_down):
    bp, tp, _ = x_prompt.shape
    bs, ts, _ = x_sample.shape
    n_a = ml_w_in.shape[0]
    n_b = ret_w_in.shape[0]

    cond = jnp.concatenate([c_ctx[None, :], c, jnp.zeros((MOD_ROWS - 1 - bs, D_MODEL), F32)], axis=0)
    mod5 = _modulation(cond, ada_w, ada_b).reshape(DEPTH, MOD_ROWS, 6, 1, D_MODEL)
    ng4 = norm_gain.reshape(DEPTH, 4, 1, D_MODEL)
    rope = _rope_tables(ts)

    ml_gates = _gate_weights(ml_w_in, ml_b_gate)
    ret_w_qk_rope = _rope_qk_weights(ret_w_in)
    ffn_w_up_b = ffn_w_up.astype(BF16)
    ffn_w_down_b = ffn_w_down.astype(BF16)
    ffn_conv_b3 = ffn_conv_b.reshape(DEPTH, 1, 2 * D_FF)
    ml_gn4 = ml_norm.reshape(n_a, A_HEADS, 1, A_DV)
    ret_gn4 = ret_norm.reshape(n_b, B_HEADS, 1, B_DV)
    dec_rep = jnp.broadcast_to(jnp.swapaxes(ret_decay, 1, 2)[..., None], (n_b, B_HEADS, 2, B_DV))
    st_c = state_mlstm_C
    st_n = state_mlstm_n.reshape(bs, n_a, 2, A_HEADS, A_DK, 1)
    st_m = state_mlstm_m.reshape(bs, n_a, 2, A_HEADS, 1, 1)

    groups = [
        dict(x=x_prompt.reshape(bp * tp, D_MODEL), batch=bp, seq_len=tp, sample=False),
        dict(x=x_sample.reshape(bs * ts, D_MODEL), batch=bs, seq_len=ts, sample=True),
    ]
    ml_states = None
    ret_states = None
    for i in range(DEPTH):
        j = i // N_MIXERS
        for grp in groups:
            x = grp["x"]
            geo = dict(seq_len=grp["seq_len"], sample=grp["sample"])
            bt = dict(batch=grp["batch"], seq_len=grp["seq_len"])
            if i % N_MIXERS == 0:
                n_qk = A_HEADS * A_DK // TN_PROJ
                z, gates = _inproj(x, ng4, mod5, i, [(ml_w_in, j, 0)], A_MAIN, n_q=n_qk, n_k=n_qk,
                                    k_scale=A_DK ** -0.5, gates=ml_gates + (j,), **geo)
                if grp["sample"]:
                    (h,) = _mlstm_scan(z, gates, ml_gn4, j, n_layers=n_a, state=(st_c, st_n, st_m), **bt)
                else:
                    h, *ml_states = _mlstm_scan(z, gates, ml_gn4, j, n_layers=n_a, prev=ml_states, **bt)
                x = _outproj(h, ml_w_out, j, x, ng4, mod5, i, **geo)
            else:
                n_qk = B_HEADS * B_DK // TN_PROJ
                w_qk = ret_w_qk_rope if grp["sample"] else ret_w_in
                w_parts = [(w_qk, j, 0), (ret_w_in, j, 1), (ret_w_in, j, 2)]
                (z,) = _inproj(x, ng4, mod5, i, w_parts, B_QK, n_q=n_qk, n_k=n_qk, k_scale=B_DK ** -0.5,
                               rope=rope if grp["sample"] else None, **geo)
                if grp["sample"]:
                    (h,) = _ret_scan(z, dec_rep, ret_gn4, j, n_layers=n_b, state=state_ret_S, **bt)
                else:
                    h, ret_states = _ret_scan(z, dec_rep, ret_gn4, j, n_layers=n_b, prev=ret_states, **bt)
                x = _outproj(h, ret_w_out, j, x, ng4, mod5, i, **geo)
            grp["x"] = _ffn(x, ng4, mod5, i, ffn_w_up_b, ffn_conv, ffn_conv_b3, ffn_w_down_b, **geo)

    y_prompt = groups[0]["x"].reshape(bp, tp, D_MODEL)
    y_sample = groups[1]["x"].reshape(bs, ts, D_MODEL)
    new_c, new_n, new_m = ml_states
    return (y_prompt, y_sample, new_c, new_n.reshape(bp, n_a, 2, A_HEADS, A_DK),
            new_m.reshape(bp, n_a, 2, A_HEADS), ret_states)
```

```python
import functools
import math

import jax
import jax.numpy as jnp
from jax import lax
from jax.experimental import pallas as pl
from jax.experimental.pallas import tpu as pltpu

D_MODEL = 1024
DEPTH = 4
GRID_W = 64
CHUNK = 128
N_MIXERS = 2
A_HEADS = 4
A_DV = D_MODEL // A_HEADS
A_DK = A_DV // 2
A_MAIN = 2 * A_HEADS * A_DK + 2 * A_HEADS * A_DV
B_HEADS = 8
B_DK = D_MODEL // B_HEADS
B_DV = 2 * D_MODEL // B_HEADS
B_QK = 2 * B_HEADS * B_DK
ROPE_BASE = 10000.0
D_FF = ((8 * D_MODEL // 3 + 127) // 128) * 128
EPS = 1e-6
LN2 = math.log(2.0)

F32 = jnp.float32
BF16 = jnp.bfloat16

LANES = 128
SUBLANES = 8
MOD_ROWS = 16
VMEM_LIMIT = 48 * 1024 * 1024

TM_PROJ = 512
TN_PROJ = 512
TM_OUT = 1024
TR_OUT = 512
TM_FFN = 512
TF_FFN = 256
FFN_DOWN_BLOCKS = 2
GATE_UNROLL = 16
STATE_UNROLL = 16
OUT_UNROLL = 16
MLSTM_OUT_UNROLL = 16
SCAN_TOKENS = 2048


def _dot(a, b):
    return jnp.dot(a, b, preferred_element_type=F32)


def _dot_nt(a, b):
    return lax.dot_general(a, b, (((1,), (1,)), ((), ())), preferred_element_type=F32)


def _dot_tn(a, b):
    return lax.dot_general(a, b, (((0,), (0,)), ((), ())), preferred_element_type=F32)


def _rms(x):
    return x * lax.rsqrt(jnp.mean(x * x, axis=-1, keepdims=True) + EPS)


def _layer_norm(h):
    d = h - jnp.mean(h, axis=-1, keepdims=True)
    return d * lax.rsqrt(jnp.mean(d * d, axis=-1, keepdims=True) + EPS)


def _params(*sem):
    return pltpu.CompilerParams(dimension_semantics=sem, vmem_limit_bytes=VMEM_LIMIT)


def _resident(shape, index):
    return pl.BlockSpec(shape, lambda *_: index, pipeline_mode=pl.Buffered(1))


def _mod_kernel(cond_ref, w_ref, b_ref, o_ref):
    cnd = cond_ref[...]
    s = cnd * jax.nn.sigmoid(cnd)
    o_ref[...] = _dot(s.astype(BF16), w_ref[...].astype(BF16)) + b_ref[...]


def _modulation(cond, ada_w, ada_b):
    tn = 1024
    n_out = ada_w.shape[-1]
    return pl.pallas_call(
        _mod_kernel,
        grid=(DEPTH, n_out // tn),
        in_specs=[
            pl.BlockSpec((MOD_ROWS, D_MODEL), lambda l, j: (0, 0)),
            pl.BlockSpec((None, D_MODEL, tn), lambda l, j: (l, 0, j)),
            pl.BlockSpec((None, 1, tn), lambda l, j: (l, 0, j)),
        ],
        out_specs=pl.BlockSpec((None, MOD_ROWS, tn), lambda l, j: (l, 0, j)),
        out_shape=jax.ShapeDtypeStruct((DEPTH, MOD_ROWS, n_out), F32),
        compiler_params=_params("parallel", "parallel"),
        name="modulation",
    )(cond, ada_w, ada_b.reshape(DEPTH, 1, n_out))


def _mod_row(sample, seq_len, tm):
    if not sample:
        return lambda i: 0
    tiles_per_seq = seq_len // tm
    return lambda i: 1 + i // tiles_per_seq


def _rope_slab(x, cos, sin):
    return x * cos + pltpu.roll(x, 64, axis=1) * sin


def _inproj_kernel(*refs, n_w, n_q, n_k, k_scale, rope, gates, n_cast, tn):
    x_ref, g_ref, sh_ref, sc_ref = refs[:4]
    w_refs = refs[4:4 + n_w]
    pos = 4 + n_w
    if gates:
        wg_ref, bg_ref = refs[pos:pos + 2]
        pos += 2
    if rope:
        cos_ref, sin_ref = refs[pos:pos + 2]
        pos += 2
    cast_in = refs[pos:pos + n_cast]
    pos += n_cast
    z_ref = refs[pos]
    pos += 1
    if gates:
        gates_ref = refs[pos]
        pos += 1
    cast_out = refs[pos:pos + n_cast]
    pos += n_cast
    u_sc = refs[pos]

    for src, dst in zip(cast_in, cast_out):
        dst[...] = src[...].astype(BF16)

    u = _rms(x_ref[...]) * g_ref[...] * (1.0 + sc_ref[...]) + sh_ref[...]
    u_sc[...] = u.astype(BF16)
    if gates:
        gates_ref[...] = _dot_nt(wg_ref[...], u_sc[...]) + bg_ref[...]

    wb = w_refs[0].shape[1]
    for j in range(z_ref.shape[1] // tn):
        part, off = divmod(j * tn, wb)
        z = _dot(u_sc[...], w_refs[part][:, off:off + tn].astype(BF16))
        scale = k_scale if n_q <= j < n_q + n_k else 1.0
        if rope and j < n_q + n_k:
            for s in range(tn // LANES):
                r = _rope_slab(z[:, s * LANES:(s + 1) * LANES], cos_ref[...], sin_ref[...])
                if scale != 1.0:
                    r = r * scale
                z_ref[:, j * tn + s * LANES:j * tn + (s + 1) * LANES] = r.astype(BF16)
        elif scale != 1.0:
            z_ref[:, j * tn:(j + 1) * tn] = (z * scale).astype(BF16)
        else:
            z_ref[:, j * tn:(j + 1) * tn] = z.astype(BF16)


def _inproj(x, ng4, mod5, layer, w_parts, wb, *, seq_len, sample, n_q, n_k, k_scale, rope=None, gates=None,
            cast=()):
    n_tok = x.shape[0]
    tm, tn = TM_PROJ, TN_PROJ
    n_col = wb * len(w_parts)
    row = _mod_row(sample, seq_len, tm)
    in_specs = [
        pl.BlockSpec((tm, D_MODEL), lambda i: (i, 0)),
        pl.BlockSpec((None, None, 1, D_MODEL), lambda i: (layer, 0, 0, 0)),
        pl.BlockSpec((None, None, None, 1, D_MODEL), lambda i: (layer, row(i), 0, 0, 0)),
        pl.BlockSpec((None, None, None, 1, D_MODEL), lambda i: (layer, row(i), 1, 0, 0)),
    ]
    in_specs += [_resident((None, D_MODEL, wb), (jl, 0, blk)) for _, jl, blk in w_parts]
    args = [x, ng4, mod5, mod5] + [w for w, _, _ in w_parts]
    out_specs = [pl.BlockSpec((tm, n_col), lambda i: (i, 0))]
    out_shape = [jax.ShapeDtypeStruct((n_tok, n_col), BF16)]
    if gates is not None:
        wg_t, bg, jg = gates
        n_g = wg_t.shape[1]
        in_specs += [_resident((None, n_g, D_MODEL), (jg, 0, 0)), _resident((None, n_g, 1), (jg, 0, 0))]
        args += [wg_t, bg]
        out_specs += [pl.BlockSpec((n_g, tm), lambda i: (0, i))]
        out_shape += [jax.ShapeDtypeStruct((n_g, n_tok), F32)]
    if rope is not None:
        tiles_per_seq = seq_len // tm
        in_specs += [pl.BlockSpec((tm, LANES), lambda i: (i % tiles_per_seq, 0))] * 2
        args += list(rope)
    n_steps = n_tok // tm
    for w_src, jl in cast:
        _, rows, cols = w_src.shape
        slab = rows // n_steps
        in_specs += [pl.BlockSpec((None, slab, cols), lambda i, jl=jl: (jl, i, 0))]
        args += [w_src]
        out_specs += [pl.BlockSpec((slab, cols), lambda i: (i, 0))]
        out_shape += [jax.ShapeDtypeStruct((rows, cols), BF16)]
    kern = functools.partial(_inproj_kernel, n_w=len(w_parts), n_q=n_q, n_k=n_k, k_scale=k_scale,
                             rope=rope is not None, gates=gates is not None, n_cast=len(cast), tn=tn)
    return pl.pallas_call(
        kern,
        grid=(n_tok // tm,),
        in_specs=in_specs,
        out_specs=out_specs,
        out_shape=out_shape,
        scratch_shapes=[pltpu.VMEM((tm, D_MODEL), BF16)],
        compiler_params=_params("parallel"),
        name="inproj",
    )(*args)


def _tri_masks():
    li = lax.broadcasted_iota(jnp.int32, (CHUNK, CHUNK), 0)
    si = lax.broadcasted_iota(jnp.int32, (CHUNK, CHUNK), 1)
    return si <= li, si >= li


def _layer_slot(ref, fresh_slot):
    if fresh_slot is None:
        return ref
    for other in range(ref.shape[1]):
        if other != fresh_slot:
            ref[:, other] = jnp.zeros(ref.shape[:1] + ref.shape[2:], ref.dtype)
    return ref.at[:, fresh_slot]


def _seqs_per_step(batch, seq_len, carry_in):
    if carry_in:
        return 1
    nb = max(1, SCAN_TOKENS // seq_len)
    while batch % nb:
        nb -= 1
    return nb


NG = 2 * A_HEADS


def _split_dot(x, mask_b):
    hi = x.astype(BF16)
    r1 = x - hi.astype(F32)
    mid = r1.astype(BF16)
    lo = (r1 - mid.astype(F32)).astype(BF16)
    return _dot(hi, mask_b) + _dot(mid, mask_b) + _dot(lo, mask_b)


def _mlstm_kernel(*refs, n_chunks, nb, carry_in, carry_out, n_alias, fresh_slot):
    q_ref, k_ref, v_ref, o_ref, g_ref, gn_ref = refs[:6]
    pos = 6
    if carry_in:
        c0_ref, n0_ref, m0_ref = refs[pos:pos + 3]
        pos += 3
    pos += n_alias
    h_ref = refs[pos]
    pos += 1
    if carry_out:
        cout_ref, nout_ref, mout_ref = refs[pos:pos + 3]
        pos += 3
    ab_sc, bt_sc, g_sc, bm_sc, mpf_sc, mpb_sc, c_sc, call_sc = refs[pos:pos + 8]

    head = pl.program_id(1)
    masks = _tri_masks()
    ones_b = jnp.ones((CHUNK, LANES), BF16)
    sum_b = jnp.concatenate([masks[1].astype(BF16), ones_b], axis=1)
    lane = lax.broadcasted_iota(jnp.int32, (CHUNK, LANES), 1)
    grow = lax.broadcasted_iota(jnp.int32, (2 * NG, CHUNK), 0)
    gsub = lax.broadcasted_iota(jnp.int32, (NG, LANES), 0)
    zpad = jnp.zeros((LANES - 2 * NG, CHUNK), F32)
    cols = (head, head + A_HEADS)
    mp_sc = (mpf_sc, mpb_sc)
    t_seq = n_chunks * CHUNK

    def pick_col(x, col):
        return jnp.sum(jnp.where(lane == col, x, 0.0), axis=1, keepdims=True)

    def at(s, c):
        return pl.ds(pl.multiple_of(s * t_seq + c * CHUNK, CHUNK), CHUNK)

    def gate_body(c, carry):
        for s in range(nb):
            idx = s * n_chunks + c
            gates = g_ref[:, at(s, c)]
            lf = jnp.minimum(gates, 0.0) - jnp.log1p(jnp.exp(-jnp.abs(gates)))
            lf = jnp.where(grow >= NG, lf, 0.0)
            sums = _split_dot(lf, sum_b)
            a_f = sums[:, :CHUNK]
            tot = sums[:, CHUNK:]
            a_all = jnp.where(grow < NG + A_HEADS, a_f, tot - a_f + lf)[NG:, :]
            b_all = gates[:NG, :] - a_all
            bt_sc[idx] = b_all
            ab_sc[at(s, c), :] = jnp.concatenate([a_all, b_all, zpad], axis=0).T
            g_sc[idx] = tot[NG:, :]
            bm_sc[idx] = jnp.broadcast_to(jnp.max(b_all, axis=1, keepdims=True), (NG, LANES))
        return carry

    lax.fori_loop(0, n_chunks, gate_body, 0, unroll=min(n_chunks, GATE_UNROLL))

    if carry_in:
        m_init = (jnp.broadcast_to(m0_ref[0], (NG, LANES)), jnp.broadcast_to(m0_ref[1], (NG, LANES)))
    else:
        m_init = (jnp.zeros((NG, LANES), F32),) * (2 * nb)

    def m_body(i, carry):
        out = []
        for s in range(nb):
            m_f, m_b = carry[2 * s], carry[2 * s + 1]
            jf = s * n_chunks + i
            jb = s * n_chunks + n_chunks - 1 - i
            mpf_sc[jf] = m_f
            mpb_sc[jb] = m_b
            out.append(g_sc[jf] + jnp.maximum(m_f, bm_sc[jf]))
            out.append(g_sc[jb] + jnp.maximum(m_b, bm_sc[jb]))
        return tuple(out)

    m_last = lax.fori_loop(0, n_chunks, m_body, m_init, unroll=min(n_chunks, STATE_UNROLL))

    if carry_in:
        for dirn in range(2):
            c_sc[dirn, :, :A_DV] = c0_ref[dirn]
            c_sc[dirn, :, A_DV:] = jnp.broadcast_to(n0_ref[dirn], (A_DK, LANES))
    else:
        c_sc[...] = jnp.zeros_like(c_sc)

    def state_body(i, carry):
        for s in range(nb):
            for dirn in range(2):
                c = i if dirn == 0 else n_chunks - 1 - i
                idx = s * n_chunks + c
                mp_row = mp_sc[dirn][idx, pl.ds(cols[dirn], 1), :]
                m_top = jnp.maximum(mp_row, bm_sc[idx, pl.ds(cols[dirn], 1), :])
                ws = jnp.exp(pick_col(ab_sc[at(s, c), :], NG + cols[dirn]) - m_top)
                dec = jnp.exp(mp_row - m_top)
                dec = jnp.concatenate([dec] * (c_sc.shape[2] // LANES), axis=1)
                c_old = c_sc[2 * s + dirn]
                call_sc[2 * s + dirn, c] = c_old.astype(BF16)
                kw = (k_ref[at(s, c), :].astype(F32) * ws).astype(BF16)
                upd = jnp.concatenate([_dot_tn(kw, v_ref[at(s, c), :]), _dot_tn(kw, ones_b)], axis=1)
                c_sc[2 * s + dirn] = dec * c_old + upd
        return carry

    lax.fori_loop(0, n_chunks, state_body, 0, unroll=min(n_chunks, STATE_UNROLL))
    if carry_out:
        outs = [_layer_slot(r, fresh_slot) for r in (cout_ref, nout_ref, mout_ref)]
        for s in range(nb):
            for dirn in range(2):
                outs[0][s, dirn] = c_sc[2 * s + dirn, :, :A_DV]
                outs[1][s, dirn] = c_sc[2 * s + dirn, :, A_DV:].T[0:1, :]
                m_end = jnp.where(gsub == cols[dirn], m_last[2 * s + dirn], 0.0)
                outs[2][s, dirn] = jnp.sum(m_end, axis=0, keepdims=True)[:, 0:1]

    def out_body(c, carry):
        for s in range(nb):
            q = q_ref[at(s, c), :]
            qf = q.astype(F32)
            v_ext = jnp.concatenate([v_ref[at(s, c), :], ones_b], axis=1)
            s_raw = _dot_nt(q, k_ref[at(s, c), :])
            a_chunk = ab_sc[at(s, c), :]
            idx = s * n_chunks + c
            h = None
            for dirn in range(2):
                col = cols[dirn]
                m_prev = mp_sc[dirn][idx, pl.ds(col, 1), :]
                b_vis = jnp.where(masks[dirn], bt_sc[idx, pl.ds(col, 1), :], -jnp.inf)
                m_row = jnp.maximum(m_prev, jnp.max(b_vis, axis=1, keepdims=True))
                sw = (s_raw * jnp.exp(b_vis - m_row)).astype(BF16)
                w_inter = jnp.exp(m_prev - m_row)
                floor = jnp.exp(-(pick_col(a_chunk, col) + m_row))
                qw = (qf * w_inter).astype(BF16)
                nd = _dot(sw, v_ext) + _dot(qw, call_sc[2 * s + dirn, c])
                r = 1.0 / jnp.maximum(jnp.abs(nd[:, A_DV:]), floor)
                hd = nd[:, :A_DV] * jnp.concatenate([r, r], axis=1)
                h = hd if h is None else h + hd
            o = o_ref[at(s, c), :].astype(F32)
            h_ref[at(s, c), :] = (_layer_norm(h) * gn_ref[...] * jax.nn.sigmoid(o)).astype(BF16)
        return carry

    lax.fori_loop(0, n_chunks, out_body, 0, unroll=min(n_chunks, max(1, MLSTM_OUT_UNROLL // nb)))


def _mlstm_scan(z, gates, gn4, j, *, batch, seq_len, n_layers, state=None, prev=None):
    n_tok = z.shape[0]
    n_chunks = seq_len // CHUNK
    carry_in = state is not None
    carry_out = not carry_in
    nb = _seqs_per_step(batch, seq_len, carry_in)
    t = nb * seq_len
    in_specs = [
        pl.BlockSpec((t, A_DK), lambda b, h: (b, h)),
        pl.BlockSpec((t, A_DK), lambda b, h: (b, A_HEADS + h)),
        pl.BlockSpec((t, A_DV), lambda b, h: (b, A_HEADS + h)),
        pl.BlockSpec((t, A_DV), lambda b, h: (b, 2 * A_HEADS + h)),
        pl.BlockSpec((2 * NG, t), lambda b, h: (0, b)),
        pl.BlockSpec((None, None, 1, A_DV), lambda b, h: (j, h, 0, 0)),
    ]
    args = [z, z, z, z, gates, gn4]
    aliases = {}
    if carry_in:
        in_specs += [
            pl.BlockSpec((None, None, 2, None, A_DK, A_DV), lambda b, h: (b, j, 0, h, 0, 0)),
            pl.BlockSpec((None, None, 2, None, A_DK, 1), lambda b, h: (b, j, 0, h, 0, 0)),
            pl.BlockSpec((None, None, 2, None, 1, 1), lambda b, h: (b, j, 0, h, 0, 0)),
        ]
        args += list(state)
    out_specs = [pl.BlockSpec((t, A_DV), lambda b, h: (b, h))]
    out_shape = [jax.ShapeDtypeStruct((n_tok, A_HEADS * A_DV), BF16)]
    fresh_slot = j if carry_out and prev is None else None
    if carry_out:
        lay, jb = (n_layers, 0) if prev is None else (None, j)
        out_specs += [
            pl.BlockSpec((nb, lay, 2, None, A_DK, A_DV), lambda b, h: (b, jb, 0, h, 0, 0)),
            pl.BlockSpec((nb, lay, 2, None, 1, A_DK), lambda b, h: (b, jb, 0, h, 0, 0)),
            pl.BlockSpec((nb, lay, 2, None, 1, 1), lambda b, h: (b, jb, 0, h, 0, 0)),
        ]
        out_shape += [
            jax.ShapeDtypeStruct((batch, n_layers, 2, A_HEADS, A_DK, A_DV), F32),
            jax.ShapeDtypeStruct((batch, n_layers, 2, A_HEADS, 1, A_DK), F32),
            jax.ShapeDtypeStruct((batch, n_layers, 2, A_HEADS, 1, 1), F32),
        ]
        if prev is not None:
            aliases = {len(args) + k: 1 + k for k in range(3)}
            in_specs += [pl.BlockSpec(memory_space=pl.ANY)] * 3
            args += list(prev)
    kern = functools.partial(_mlstm_kernel, n_chunks=n_chunks, nb=nb, carry_in=carry_in,
                             carry_out=carry_out, n_alias=len(aliases), fresh_slot=fresh_slot)
    return pl.pallas_call(
        kern,
        grid=(batch // nb, A_HEADS),
        in_specs=in_specs,
        out_specs=out_specs,
        out_shape=out_shape,
        input_output_aliases=aliases,
        scratch_shapes=[
            pltpu.VMEM((t, LANES), F32),
            pltpu.VMEM((nb * n_chunks, NG, CHUNK), F32),
            pltpu.VMEM((nb * n_chunks, NG, LANES), F32),
            pltpu.VMEM((nb * n_chunks, NG, LANES), F32),
            pltpu.VMEM((nb * n_chunks, NG, LANES), F32),
            pltpu.VMEM((nb * n_chunks, NG, LANES), F32),
            pltpu.VMEM((2 * nb, A_DK, A_DV + LANES), F32),
            pltpu.VMEM((2 * nb, n_chunks, A_DK, A_DV + LANES), BF16),
        ],
        compiler_params=_params("parallel", "parallel"),
        name="mlstm_scan",
    )(*args)


def _ret_kernel(*refs, n_chunks, nb, carry_in, carry_out, n_alias, fresh_slot):
    q_ref, k_ref, v_ref, gate_ref, dec_ref, gn_ref = refs[:6]
    pos = 6
    if carry_in:
        s0_ref = refs[pos]
        pos += 1
    pos += n_alias
    h_ref = refs[pos]
    pos += 1
    if carry_out:
        sout_ref = refs[pos]
        pos += 1
    s_sc, sall_sc, dsum_sc, xi_sc, zeta_sc = refs[pos:pos + 5]
    t_seq = n_chunks * CHUNK

    def at(s, c):
        return pl.ds(pl.multiple_of(s * t_seq + c * CHUNK, CHUNK), CHUNK)

    lg = jnp.log1p(-jnp.exp(-dec_ref[...] * LN2))
    lg_f = lg[0:1, :]
    lg_b = lg[1:2, :]

    @pl.when(pl.program_id(1) == 0)
    def _():
        masks = _tri_masks()
        li = lax.broadcasted_iota(jnp.int32, (CHUNK, B_DV), 0).astype(F32)
        si = lax.broadcasted_iota(jnp.int32, (CHUNK, CHUNK), 1).astype(F32)
        lq = li[:, :CHUNK]
        dsum_sc[...] = (
            jnp.where(masks[0], jnp.exp(jnp.where(masks[0], lq - si, 0.0) * lg_f[:, :CHUNK]), 0.0)
            + jnp.where(masks[1], jnp.exp(jnp.where(masks[1], si - lq, 0.0) * lg_b[:, :CHUNK]), 0.0))
        xi_sc[0] = jnp.exp((lq + 1.0) * lg_f[:, :CHUNK])
        xi_sc[1] = jnp.exp((CHUNK - lq) * lg_b[:, :CHUNK])
        zeta_sc[0] = jnp.exp((CHUNK - 1.0 - lq) * lg_f[:, :CHUNK])
        zeta_sc[1] = jnp.exp(lq * lg_b[:, :CHUNK])

    cdec = (jnp.exp(CHUNK * lg_f), jnp.exp(CHUNK * lg_b))

    if carry_in:
        qr = B_DK // 4
        for dirn in range(2):
            for n, o in enumerate((0, 2, 1, 3)):
                s_sc[dirn, n * qr:(n + 1) * qr, :] = s0_ref[dirn, o * qr:(o + 1) * qr, :]
    else:
        s_sc[...] = jnp.zeros_like(s_sc)

    def state_body(i, carry):
        for s in range(nb):
            for dirn in range(2):
                c = i if dirn == 0 else n_chunks - 1 - i
                s_old = s_sc[2 * s + dirn]
                sall_sc[s, c, dirn * B_DK:(dirn + 1) * B_DK, :] = s_old.astype(BF16)
                kz = (k_ref[at(s, c), :].astype(F32) * zeta_sc[dirn]).astype(BF16)
                s_sc[2 * s + dirn] = cdec[dirn] * s_old + _dot_tn(kz, v_ref[at(s, c), :])
        return carry

    lax.fori_loop(0, n_chunks, state_body, 0, unroll=min(n_chunks, STATE_UNROLL))
    if carry_out:
        s_out = _layer_slot(sout_ref, fresh_slot)
        for s in range(nb):
            for dirn in range(2):
                s_out[s, dirn] = s_sc[2 * s + dirn]

    def out_body(c, carry):
        for s in range(nb):
            q = q_ref[at(s, c), :]
            v = v_ref[at(s, c), :]
            sw = _dot_nt(q, k_ref[at(s, c), :]) * dsum_sc[...]
            qf = q.astype(F32)
            qx = jnp.concatenate([qf * xi_sc[0], qf * xi_sc[1]], axis=1).astype(BF16)
            h = _dot(sw.astype(BF16), v) + _dot(qx, sall_sc[s, c])
            g = gate_ref[at(s, c), :].astype(F32)
            h_ref[at(s, c), :] = (_layer_norm(h) * gn_ref[...] * (g * jax.nn.sigmoid(g))).astype(BF16)
        return carry

    lax.fori_loop(0, n_chunks, out_body, 0, unroll=min(n_chunks, max(1, OUT_UNROLL // nb)))


def _ret_scan(z, dec_rep, gn4, j, *, batch, seq_len, n_layers, state=None, prev=None):
    n_tok = z.shape[0]
    n_chunks = seq_len // CHUNK
    carry_in = state is not None
    carry_out = not carry_in
    nb = _seqs_per_step(batch, seq_len, carry_in)
    t = nb * seq_len
    in_specs = [
        pl.BlockSpec((t, B_DK), lambda h, b: (b, h)),
        pl.BlockSpec((t, B_DK), lambda h, b: (b, B_HEADS + h)),
        pl.BlockSpec((t, B_DV), lambda h, b: (b, B_HEADS + h)),
        pl.BlockSpec((t, B_DV), lambda h, b: (b, 2 * B_HEADS + h)),
        pl.BlockSpec((None, None, 2, B_DV), lambda h, b: (j, h, 0, 0)),
        pl.BlockSpec((None, None, 1, B_DV), lambda h, b: (j, h, 0, 0)),
    ]
    args = [z, z, z, z, dec_rep, gn4]
    aliases = {}
    if carry_in:
        in_specs += [pl.BlockSpec((None, None, 2, None, B_DK, B_DV), lambda h, b: (b, j, 0, h, 0, 0))]
        args += [state]
    out_specs = [pl.BlockSpec((t, B_DV), lambda h, b: (b, h))]
    out_shape = [jax.ShapeDtypeStruct((n_tok, B_HEADS * B_DV), BF16)]
    fresh_slot = j if carry_out and prev is None else None
    if carry_out:
        lay, jb = (n_layers, 0) if prev is None else (None, j)
        out_specs += [pl.BlockSpec((nb, lay, 2, None, B_DK, B_DV), lambda h, b: (b, jb, 0, h, 0, 0))]
        out_shape += [jax.ShapeDtypeStruct((batch, n_layers, 2, B_HEADS, B_DK, B_DV), F32)]
        if prev is not None:
            aliases = {len(args): 1}
            in_specs += [pl.BlockSpec(memory_space=pl.ANY)]
            args += [prev]
    kern = functools.partial(_ret_kernel, n_chunks=n_chunks, nb=nb, carry_in=carry_in,
                             carry_out=carry_out, n_alias=len(aliases), fresh_slot=fresh_slot)
    return pl.pallas_call(
        kern,
        grid=(B_HEADS, batch // nb),
        in_specs=in_specs,
        out_specs=out_specs,
        out_shape=out_shape,
        input_output_aliases=aliases,
        scratch_shapes=[
            pltpu.VMEM((2 * nb, B_DK, B_DV), F32),
            pltpu.VMEM((nb, n_chunks, 2 * B_DK, B_DV), BF16),
            pltpu.VMEM((CHUNK, CHUNK), F32),
            pltpu.VMEM((2, CHUNK, B_DK), F32),
            pltpu.VMEM((2, CHUNK, B_DK), F32),
        ],
        compiler_params=_params("parallel", "arbitrary"),
        name="ret_scan",
    )(*args)


def _outproj_kernel(h_ref, w_ref, x_ref, g_ref, gate_ref, o_ref):
    w = w_ref[...].astype(BF16)
    for r in range(o_ref.shape[0] // TR_OUT):
        rows = slice(r * TR_OUT, (r + 1) * TR_OUT)
        y = _dot(h_ref[rows, :], w)
        o_ref[rows, :] = x_ref[rows, :] + gate_ref[...] * (_rms(y) * g_ref[...])


def _outproj(h, w, j, x, ng4, mod5, layer, *, seq_len, sample):
    n_tok, hv = h.shape
    tm = TM_OUT
    row = _mod_row(sample, seq_len, tm)
    return pl.pallas_call(
        _outproj_kernel,
        grid=(n_tok // tm,),
        in_specs=[
            pl.BlockSpec((tm, hv), lambda i: (i, 0)),
            _resident((None, hv, D_MODEL), (j, 0, 0)),
            pl.BlockSpec((tm, D_MODEL), lambda i: (i, 0)),
            pl.BlockSpec((None, None, 1, D_MODEL), lambda i: (layer, 1, 0, 0)),
            pl.BlockSpec((None, None, None, 1, D_MODEL), lambda i: (layer, row(i), 2, 0, 0)),
        ],
        out_specs=pl.BlockSpec((tm, D_MODEL), lambda i: (i, 0)),
        out_shape=jax.ShapeDtypeStruct((n_tok, D_MODEL), F32),
        compiler_params=_params("parallel"),
        name="outproj",
    )(h, w, x, ng4, mod5)


def _conv3(hs_ref, half, h, cw, cb, seg, n_seg):
    for s in range(n_seg):
        base = SUBLANES + s * (seg + SUBLANES)
        h_seg = h[s * seg:(s + 1) * seg, :]
        hs_ref[2 * half, base + 1:base + 1 + seg, :] = h_seg
        hs_ref[2 * half + 1, base - 1:base - 1 + seg, :] = h_seg
    parts = []
    for s in range(n_seg):
        base = SUBLANES + s * (seg + SUBLANES)
        h_prev = hs_ref[2 * half, base:base + seg, :]
        h_next = hs_ref[2 * half + 1, base:base + seg, :]
        h_mid = h[s * seg:(s + 1) * seg, :]
        parts.append(h_prev * cw[0:1, :] + h_mid * cw[1:2, :] + h_next * cw[2:3, :] + cb)
    return parts


def _ffn_kernel(x_ref, g2_ref, sh_ref, sc_ref, wup_ref, cw_ref, cb_ref, wd_ref, g3_ref, gate_ref,
                o_ref, u_sc, act_sc, hs_sc, *, seg, n_seg, tf):
    u = _rms(x_ref[...]) * g2_ref[...] * (1.0 + sc_ref[...]) + sh_ref[...]
    u_sc[...] = u.astype(BF16)
    zero_rows = jnp.zeros((SUBLANES, tf), F32)
    for s in range(n_seg):
        base = SUBLANES + s * (seg + SUBLANES)
        for half in range(2):
            hs_sc[2 * half, base:base + SUBLANES, :] = zero_rows
            hs_sc[2 * half + 1, base + seg - SUBLANES:base + seg, :] = zero_rows

    for cidx in range(D_FF // tf):
        cg = slice(cidx * tf, (cidx + 1) * tf)
        cu = slice(D_FF + cidx * tf, D_FF + (cidx + 1) * tf)
        hg = _conv3(hs_sc, 0, _dot(u_sc[...], wup_ref[:, cg]), cw_ref[:, cg], cb_ref[:, cg], seg, n_seg)
        hu = _conv3(hs_sc, 1, _dot(u_sc[...], wup_ref[:, cu]), cw_ref[:, cu], cb_ref[:, cu], seg, n_seg)
        for s in range(n_seg):
            act = jax.nn.gelu(hg[s], approximate=True) * hu[s]
            act_sc[s * seg:(s + 1) * seg, cg] = act.astype(BF16)

    tr = o_ref.shape[0] // FFN_DOWN_BLOCKS
    for r in range(FFN_DOWN_BLOCKS):
        rows = slice(r * tr, (r + 1) * tr)
        f = _dot(act_sc[rows, :], wd_ref[...])
        o_ref[rows, :] = x_ref[rows, :] + gate_ref[...] * (_rms(f) * g3_ref[...])


def _ffn(x, ng4, mod5, layer, w_up, conv_w, conv_b, w_down, *, seq_len, sample):
    n_tok = x.shape[0]
    tm, tf = TM_FFN, TF_FFN
    row = _mod_row(sample, seq_len, tm)
    seg = GRID_W if sample else seq_len
    n_seg = tm // seg
    kern = functools.partial(_ffn_kernel, seg=seg, n_seg=n_seg, tf=tf)
    mod_spec = lambda k: pl.BlockSpec((None, None, None, 1, D_MODEL), lambda i: (layer, row(i), k, 0, 0))
    gain_spec = lambda k: pl.BlockSpec((None, None, 1, D_MODEL), lambda i: (layer, k, 0, 0))
    return pl.pallas_call(
        kern,
        grid=(n_tok // tm,),
        in_specs=[
            pl.BlockSpec((tm, D_MODEL), lambda i: (i, 0)),
            gain_spec(2),
            mod_spec(3),
            mod_spec(4),
            _resident((D_MODEL, 2 * D_FF), (0, 0)),
            _resident((None, 3, 2 * D_FF), (layer, 0, 0)),
            _resident((None, 1, 2 * D_FF), (layer, 0, 0)),
            _resident((D_FF, D_MODEL), (0, 0)),
            gain_spec(3),
            mod_spec(5),
        ],
        out_specs=pl.BlockSpec((tm, D_MODEL), lambda i: (i, 0)),
        out_shape=jax.ShapeDtypeStruct((n_tok, D_MODEL), F32),
        scratch_shapes=[
            pltpu.VMEM((tm, D_MODEL), BF16),
            pltpu.VMEM((tm, D_FF), BF16),
            pltpu.VMEM((4, SUBLANES + n_seg * (seg + SUBLANES), tf), F32),
        ],
        compiler_params=_params("parallel"),
        name="convffn",
    )(x, ng4, mod5, mod5, w_up, conv_w, conv_b, w_down, ng4, mod5)


def _rope_tables(seq_len):
    quarter = B_DK // 4
    inv = ROPE_BASE ** (-jnp.arange(quarter, dtype=F32) / quarter)
    t = jnp.arange(seq_len)
    rows = (t // GRID_W).astype(F32)[:, None] * inv
    cols = (t % GRID_W).astype(F32)[:, None] * inv
    cos = jnp.concatenate([jnp.cos(rows), jnp.cos(cols)] * 2, axis=-1)
    sin = jnp.concatenate([-jnp.sin(rows), -jnp.sin(cols), jnp.sin(rows), jnp.sin(cols)], axis=-1)
    return cos, sin


def _rope_qk_weights(w_in):
    n_l = w_in.shape[0]
    quarter = B_DK // 4
    w_qk = w_in[:, :, :B_QK].astype(BF16).reshape(n_l, D_MODEL, 2 * B_HEADS, 2, 2, quarter)
    return jnp.swapaxes(w_qk, 3, 4).reshape(n_l, D_MODEL, B_QK)


def _gate_weights(w_in, b_gate):
    n_l = w_in.shape[0]
    order = jnp.array((0, 2, 1, 3))
    wg = w_in[:, :, A_MAIN:].reshape(n_l, D_MODEL, 4, A_HEADS)[:, :, order, :]
    wg_t = jnp.transpose(wg, (0, 2, 3, 1)).reshape(n_l, 2 * NG, D_MODEL).astype(BF16)
    bg = b_gate[:, order, :].reshape(n_l, 2 * NG, 1)
    return wg_t, bg


def kernel(x_prompt, x_sample, state_mlstm_C, state_mlstm_n, state_mlstm_m, state_ret_S, c, c_ctx,
           norm_gain, ada_w, ada_b, ml_w_in, ml_b_gate, ml_norm, ml_w_out,
           ret_w_in, ret_decay, ret_norm, ret_w_out, ffn_w_up, ffn_conv, ffn_conv_b, ffn_w_down):
    bp, tp, _ = x_prompt.shape
    bs, ts, _ = x_sample.shape
    n_a = ml_w_in.shape[0]
    n_b = ret_w_in.shape[0]

    cond = jnp.concatenate([c_ctx[None, :], c, jnp.zeros((MOD_ROWS - 1 - bs, D_MODEL), F32)], axis=0)
    mod5 = _modulation(cond, ada_w, ada_b).reshape(DEPTH, MOD_ROWS, 6, 1, D_MODEL)
    ng4 = norm_gain.reshape(DEPTH, 4, 1, D_MODEL)
    rope = _rope_tables(ts)

    ml_gates = _gate_weights(ml_w_in, ml_b_gate)
    ret_w_qk_rope = _rope_qk_weights(ret_w_in)
    ffn_conv_b3 = ffn_conv_b.reshape(DEPTH, 1, 2 * D_FF)
    ml_gn4 = ml_norm.reshape(n_a, A_HEADS, 1, A_DV)
    ret_gn4 = ret_norm.reshape(n_b, B_HEADS, 1, B_DV)
    dec_rep = jnp.broadcast_to(jnp.swapaxes(ret_decay, 1, 2)[..., None], (n_b, B_HEADS, 2, B_DV))
    st_c = state_mlstm_C
    st_n = state_mlstm_n.reshape(bs, n_a, 2, A_HEADS, A_DK, 1)
    st_m = state_mlstm_m.reshape(bs, n_a, 2, A_HEADS, 1, 1)

    groups = [
        dict(x=x_prompt.reshape(bp * tp, D_MODEL), batch=bp, seq_len=tp, sample=False),
        dict(x=x_sample.reshape(bs * ts, D_MODEL), batch=bs, seq_len=ts, sample=True),
    ]
    ml_states = None
    ret_states = None
    for i in range(DEPTH):
        j = i // N_MIXERS
        for grp in groups:
            x = grp["x"]
            geo = dict(seq_len=grp["seq_len"], sample=grp["sample"])
            bt = dict(batch=grp["batch"], seq_len=grp["seq_len"])
            cast = () if grp["sample"] else ((ffn_w_up, i), (ffn_w_down, i))
            if i % N_MIXERS == 0:
                n_qk = A_HEADS * A_DK // TN_PROJ
                z, gates, *ffn_w = _inproj(x, ng4, mod5, i, [(ml_w_in, j, 0)], A_MAIN, n_q=n_qk, n_k=n_qk,
                                           k_scale=A_DK ** -0.5, gates=ml_gates + (j,), cast=cast, **geo)
                if grp["sample"]:
                    (h,) = _mlstm_scan(z, gates, ml_gn4, j, n_layers=n_a, state=(st_c, st_n, st_m), **bt)
                else:
                    h, *ml_states = _mlstm_scan(z, gates, ml_gn4, j, n_layers=n_a, prev=ml_states, **bt)
                x = _outproj(h, ml_w_out, j, x, ng4, mod5, i, **geo)
            else:
                n_qk = B_HEADS * B_DK // TN_PROJ
                w_qk = ret_w_qk_rope if grp["sample"] else ret_w_in
                w_parts = [(w_qk, j, 0), (ret_w_in, j, 1), (ret_w_in, j, 2)]
                z, *ffn_w = _inproj(x, ng4, mod5, i, w_parts, B_QK, n_q=n_qk, n_k=n_qk, k_scale=B_DK ** -0.5,
                                    rope=rope if grp["sample"] else None, cast=cast, **geo)
                if grp["sample"]:
                    (h,) = _ret_scan(z, dec_rep, ret_gn4, j, n_layers=n_b, state=state_ret_S, **bt)
                else:
                    h, ret_states = _ret_scan(z, dec_rep, ret_gn4, j, n_layers=n_b, prev=ret_states, **bt)
                x = _outproj(h, ret_w_out, j, x, ng4, mod5, i, **geo)
            if ffn_w:
                ffn_w_up_b, ffn_w_down_b = ffn_w
            grp["x"] = _ffn(x, ng4, mod5, i, ffn_w_up_b, ffn_conv, ffn_conv_b3, ffn_w_down_b, **geo)

    y_prompt = groups[0]["x"].reshape(bp, tp, D_MODEL)
    y_sample = groups[1]["x"].reshape(bs, ts, D_MODEL)
    new_c, new_n, new_m = ml_states
    return (y_prompt, y_sample, new_c, new_n.reshape(bp, n_a, 2, A_HEADS, A_DK),
            new_m.reshape(bp, n_a, 2, A_HEADS), ret_states)
```

```python
import functools
import math

import jax
import jax.numpy as jnp
from jax import lax
from jax.experimental import pallas as pl
from jax.experimental.pallas import tpu as pltpu

D_MODEL = 1024
DEPTH = 4
GRID_W = 64
CHUNK = 128
N_MIXERS = 2
A_HEADS = 4
A_DV = D_MODEL // A_HEADS
A_DK = A_DV // 2
A_MAIN = 2 * A_HEADS * A_DK + 2 * A_HEADS * A_DV
B_HEADS = 8
B_DK = D_MODEL // B_HEADS
B_DV = 2 * D_MODEL // B_HEADS
B_QK = 2 * B_HEADS * B_DK
ROPE_BASE = 10000.0
D_FF = ((8 * D_MODEL // 3 + 127) // 128) * 128
EPS = 1e-6
LN2 = math.log(2.0)

F32 = jnp.float32
BF16 = jnp.bfloat16

LANES = 128
SUBLANES = 8
MOD_ROWS = 16
VMEM_LIMIT = 48 * 1024 * 1024

TM_PROJ = 512
TM_PROJ_MLSTM = 1024
TN_PROJ = 512
TM_OUT = 1024
TR_OUT = 512
TM_FFN = 512
TF_FFN = 256
FFN_DOWN_BLOCKS = 2
GATE_UNROLL = 16
STATE_UNROLL = 16
OUT_UNROLL = 16
MLSTM_OUT_UNROLL = 16
SCAN_TOKENS = 2048


def _dot(a, b):
    return jnp.dot(a, b, preferred_element_type=F32)


def _dot_nt(a, b):
    return lax.dot_general(a, b, (((1,), (1,)), ((), ())), preferred_element_type=F32)


def _dot_tn(a, b):
    return lax.dot_general(a, b, (((0,), (0,)), ((), ())), preferred_element_type=F32)


def _rms(x):
    return x * lax.rsqrt(jnp.mean(x * x, axis=-1, keepdims=True) + EPS)


def _layer_norm(h):
    d = h - jnp.mean(h, axis=-1, keepdims=True)
    return d * lax.rsqrt(jnp.mean(d * d, axis=-1, keepdims=True) + EPS)


def _params(*sem):
    return pltpu.CompilerParams(dimension_semantics=sem, vmem_limit_bytes=VMEM_LIMIT)


def _resident(shape, index):
    return pl.BlockSpec(shape, lambda *_: index, pipeline_mode=pl.Buffered(1))


def _mod_kernel(cond_ref, w_ref, b_ref, o_ref):
    cnd = cond_ref[...]
    s = cnd * jax.nn.sigmoid(cnd)
    o_ref[...] = _dot(s.astype(BF16), w_ref[...].astype(BF16)) + b_ref[...]


def _modulation(cond, ada_w, ada_b):
    tn = 1024
    n_out = ada_w.shape[-1]
    return pl.pallas_call(
        _mod_kernel,
        grid=(DEPTH, n_out // tn),
        in_specs=[
            pl.BlockSpec((MOD_ROWS, D_MODEL), lambda l, j: (0, 0)),
            pl.BlockSpec((None, D_MODEL, tn), lambda l, j: (l, 0, j)),
            pl.BlockSpec((None, 1, tn), lambda l, j: (l, 0, j)),
        ],
        out_specs=pl.BlockSpec((None, MOD_ROWS, tn), lambda l, j: (l, 0, j)),
        out_shape=jax.ShapeDtypeStruct((DEPTH, MOD_ROWS, n_out), F32),
        compiler_params=_params("parallel", "parallel"),
        name="modulation",
    )(cond, ada_w, ada_b.reshape(DEPTH, 1, n_out))


def _mod_row(sample, seq_len, tm):
    if not sample:
        return lambda i: 0
    tiles_per_seq = seq_len // tm
    return lambda i: 1 + i // tiles_per_seq


def _rope_slab(x, cos, sin):
    return x * cos + pltpu.roll(x, 64, axis=1) * sin


def _inproj_kernel(*refs, n_w, n_q, n_k, k_scale, rope, gates, n_cast, tn):
    x_ref, g_ref, sh_ref, sc_ref = refs[:4]
    w_refs = refs[4:4 + n_w]
    pos = 4 + n_w
    if gates:
        wg_ref, bg_ref = refs[pos:pos + 2]
        pos += 2
    if rope:
        cos_ref, sin_ref = refs[pos:pos + 2]
        pos += 2
    cast_in = refs[pos:pos + n_cast]
    pos += n_cast
    z_ref = refs[pos]
    pos += 1
    if gates:
        gates_ref = refs[pos]
        pos += 1
    cast_out = refs[pos:pos + n_cast]
    pos += n_cast
    u_sc = refs[pos]

    for src, dst in zip(cast_in, cast_out):
        dst[...] = src[...].astype(BF16)

    u = _rms(x_ref[...]) * g_ref[...] * (1.0 + sc_ref[...]) + sh_ref[...]
    u_sc[...] = u.astype(BF16)
    if gates:
        gates_ref[...] = _dot_nt(wg_ref[...], u_sc[...]) + bg_ref[...]

    wb = w_refs[0].shape[1]
    for j in range(z_ref.shape[1] // tn):
        part, off = divmod(j * tn, wb)
        z = _dot(u_sc[...], w_refs[part][:, off:off + tn].astype(BF16))
        scale = k_scale if n_q <= j < n_q + n_k else 1.0
        if rope and j < n_q + n_k:
            for s in range(tn // LANES):
                r = _rope_slab(z[:, s * LANES:(s + 1) * LANES], cos_ref[...], sin_ref[...])
                if scale != 1.0:
                    r = r * scale
                z_ref[:, j * tn + s * LANES:j * tn + (s + 1) * LANES] = r.astype(BF16)
        elif scale != 1.0:
            z_ref[:, j * tn:(j + 1) * tn] = (z * scale).astype(BF16)
        else:
            z_ref[:, j * tn:(j + 1) * tn] = z.astype(BF16)


def _inproj(x, ng4, mod5, layer, w_parts, wb, *, tm, seq_len, sample, n_q, n_k, k_scale, rope=None, gates=None,
            cast=()):
    n_tok = x.shape[0]
    tn = TN_PROJ
    n_col = wb * len(w_parts)
    row = _mod_row(sample, seq_len, tm)
    in_specs = [
        pl.BlockSpec((tm, D_MODEL), lambda i: (i, 0)),
        pl.BlockSpec((None, None, 1, D_MODEL), lambda i: (layer, 0, 0, 0)),
        pl.BlockSpec((None, None, None, 1, D_MODEL), lambda i: (layer, row(i), 0, 0, 0)),
        pl.BlockSpec((None, None, None, 1, D_MODEL), lambda i: (layer, row(i), 1, 0, 0)),
    ]
    in_specs += [_resident((None, D_MODEL, wb), (jl, 0, blk)) for _, jl, blk in w_parts]
    args = [x, ng4, mod5, mod5] + [w for w, _, _ in w_parts]
    out_specs = [pl.BlockSpec((tm, n_col), lambda i: (i, 0))]
    out_shape = [jax.ShapeDtypeStruct((n_tok, n_col), BF16)]
    if gates is not None:
        wg_t, bg, jg = gates
        n_g = wg_t.shape[1]
        in_specs += [_resident((None, n_g, D_MODEL), (jg, 0, 0)), _resident((None, n_g, 1), (jg, 0, 0))]
        args += [wg_t, bg]
        out_specs += [pl.BlockSpec((n_g, tm), lambda i: (0, i))]
        out_shape += [jax.ShapeDtypeStruct((n_g, n_tok), F32)]
    if rope is not None:
        tiles_per_seq = seq_len // tm
        in_specs += [pl.BlockSpec((tm, LANES), lambda i: (i % tiles_per_seq, 0))] * 2
        args += list(rope)
    n_steps = n_tok // tm
    for w_src, jl in cast:
        _, rows, cols = w_src.shape
        slab = rows // n_steps
        in_specs += [pl.BlockSpec((None, slab, cols), lambda i, jl=jl: (jl, i, 0))]
        args += [w_src]
        out_specs += [pl.BlockSpec((slab, cols), lambda i: (i, 0))]
        out_shape += [jax.ShapeDtypeStruct((rows, cols), BF16)]
    kern = functools.partial(_inproj_kernel, n_w=len(w_parts), n_q=n_q, n_k=n_k, k_scale=k_scale,
                             rope=rope is not None, gates=gates is not None, n_cast=len(cast), tn=tn)
    return pl.pallas_call(
        kern,
        grid=(n_tok // tm,),
        in_specs=in_specs,
        out_specs=out_specs,
        out_shape=out_shape,
        scratch_shapes=[pltpu.VMEM((tm, D_MODEL), BF16)],
        compiler_params=_params("parallel"),
        name="inproj",
    )(*args)


def _tri_masks():
    li = lax.broadcasted_iota(jnp.int32, (CHUNK, CHUNK), 0)
    si = lax.broadcasted_iota(jnp.int32, (CHUNK, CHUNK), 1)
    return si <= li, si >= li


def _layer_slot(ref, fresh_slot):
    if fresh_slot is None:
        return ref
    for other in range(ref.shape[1]):
        if other != fresh_slot:
            ref[:, other] = jnp.zeros(ref.shape[:1] + ref.shape[2:], ref.dtype)
    return ref.at[:, fresh_slot]


def _seqs_per_step(batch, seq_len, carry_in):
    if carry_in:
        return 1
    nb = max(1, SCAN_TOKENS // seq_len)
    while batch % nb:
        nb -= 1
    return nb


NG = 2 * A_HEADS


def _split_dot(x, mask_b):
    hi = x.astype(BF16)
    r1 = x - hi.astype(F32)
    mid = r1.astype(BF16)
    lo = (r1 - mid.astype(F32)).astype(BF16)
    return _dot(hi, mask_b) + _dot(mid, mask_b) + _dot(lo, mask_b)


def _mlstm_kernel(*refs, n_chunks, nb, carry_in, carry_out, n_alias, fresh_slot):
    q_ref, k_ref, v_ref, o_ref, g_ref, gn_ref = refs[:6]
    pos = 6
    if carry_in:
        c0_ref, n0_ref, m0_ref = refs[pos:pos + 3]
        pos += 3
    pos += n_alias
    h_ref = refs[pos]
    pos += 1
    if carry_out:
        cout_ref, nout_ref, mout_ref = refs[pos:pos + 3]
        pos += 3
    ab_sc, bt_sc, g_sc, bm_sc, mpf_sc, mpb_sc, c_sc, call_sc = refs[pos:pos + 8]

    head = pl.program_id(1)
    masks = _tri_masks()
    ones_b = jnp.ones((CHUNK, LANES), BF16)
    sum_b = jnp.concatenate([masks[1].astype(BF16), ones_b], axis=1)
    lane = lax.broadcasted_iota(jnp.int32, (CHUNK, LANES), 1)
    grow = lax.broadcasted_iota(jnp.int32, (2 * NG, CHUNK), 0)
    gsub = lax.broadcasted_iota(jnp.int32, (NG, LANES), 0)
    zpad = jnp.zeros((LANES - 2 * NG, CHUNK), F32)
    cols = (head, head + A_HEADS)
    mp_sc = (mpf_sc, mpb_sc)
    t_seq = n_chunks * CHUNK

    def pick_col(x, col):
        return jnp.sum(jnp.where(lane == col, x, 0.0), axis=1, keepdims=True)

    def at(s, c):
        return pl.ds(pl.multiple_of(s * t_seq + c * CHUNK, CHUNK), CHUNK)

    def gate_body(c, carry):
        for s in range(nb):
            idx = s * n_chunks + c
            gates = g_ref[:, at(s, c)]
            lf = jnp.minimum(gates, 0.0) - jnp.log1p(jnp.exp(-jnp.abs(gates)))
            lf = jnp.where(grow >= NG, lf, 0.0)
            sums = _split_dot(lf, sum_b)
            a_f = sums[:, :CHUNK]
            tot = sums[:, CHUNK:]
            a_all = jnp.where(grow < NG + A_HEADS, a_f, tot - a_f + lf)[NG:, :]
            b_all = gates[:NG, :] - a_all
            bt_sc[idx] = b_all
            ab_sc[at(s, c), :] = jnp.concatenate([a_all, b_all, zpad], axis=0).T
            g_sc[idx] = tot[NG:, :]
            bm_sc[idx] = jnp.broadcast_to(jnp.max(b_all, axis=1, keepdims=True), (NG, LANES))
        return carry

    lax.fori_loop(0, n_chunks, gate_body, 0, unroll=min(n_chunks, GATE_UNROLL))

    if carry_in:
        m_init = (jnp.broadcast_to(m0_ref[0], (NG, LANES)), jnp.broadcast_to(m0_ref[1], (NG, LANES)))
    else:
        m_init = (jnp.zeros((NG, LANES), F32),) * (2 * nb)

    def m_body(i, carry):
        out = []
        for s in range(nb):
            m_f, m_b = carry[2 * s], carry[2 * s + 1]
            jf = s * n_chunks + i
            jb = s * n_chunks + n_chunks - 1 - i
            mpf_sc[jf] = m_f
            mpb_sc[jb] = m_b
            out.append(g_sc[jf] + jnp.maximum(m_f, bm_sc[jf]))
            out.append(g_sc[jb] + jnp.maximum(m_b, bm_sc[jb]))
        return tuple(out)

    m_last = lax.fori_loop(0, n_chunks, m_body, m_init, unroll=min(n_chunks, STATE_UNROLL))

    if carry_in:
        for dirn in range(2):
            c_sc[dirn, :, :A_DV] = c0_ref[dirn]
            c_sc[dirn, :, A_DV:] = jnp.broadcast_to(n0_ref[dirn], (A_DK, LANES))
    else:
        c_sc[...] = jnp.zeros_like(c_sc)

    def state_body(i, carry):
        for s in range(nb):
            for dirn in range(2):
                c = i if dirn == 0 else n_chunks - 1 - i
                idx = s * n_chunks + c
                mp_row = mp_sc[dirn][idx, pl.ds(cols[dirn], 1), :]
                m_top = jnp.maximum(mp_row, bm_sc[idx, pl.ds(cols[dirn], 1), :])
                ws = jnp.exp(pick_col(ab_sc[at(s, c), :], NG + cols[dirn]) - m_top)
                dec = jnp.exp(mp_row - m_top)
                dec = jnp.concatenate([dec] * (c_sc.shape[2] // LANES), axis=1)
                c_old = c_sc[2 * s + dirn]
                call_sc[2 * s + dirn, c] = c_old.astype(BF16)
                kw = (k_ref[at(s, c), :].astype(F32) * ws).astype(BF16)
                upd = jnp.concatenate([_dot_tn(kw, v_ref[at(s, c), :]), _dot_tn(kw, ones_b)], axis=1)
                c_sc[2 * s + dirn] = dec * c_old + upd
        return carry

    lax.fori_loop(0, n_chunks, state_body, 0, unroll=min(n_chunks, STATE_UNROLL))
    if carry_out:
        outs = [_layer_slot(r, fresh_slot) for r in (cout_ref, nout_ref, mout_ref)]
        for s in range(nb):
            for dirn in range(2):
                outs[0][s, dirn] = c_sc[2 * s + dirn, :, :A_DV]
                outs[1][s, dirn] = c_sc[2 * s + dirn, :, A_DV:].T[0:1, :]
                m_end = jnp.where(gsub == cols[dirn], m_last[2 * s + dirn], 0.0)
                outs[2][s, dirn] = jnp.sum(m_end, axis=0, keepdims=True)[:, 0:1]

    def out_body(c, carry):
        for s in range(nb):
            q = q_ref[at(s, c), :]
            qf = q.astype(F32)
            v_ext = jnp.concatenate([v_ref[at(s, c), :], ones_b], axis=1)
            s_raw = _dot_nt(q, k_ref[at(s, c), :])
            a_chunk = ab_sc[at(s, c), :]
            idx = s * n_chunks + c
            h = None
            for dirn in range(2):
                col = cols[dirn]
                m_prev = mp_sc[dirn][idx, pl.ds(col, 1), :]
                b_vis = jnp.where(masks[dirn], bt_sc[idx, pl.ds(col, 1), :], -jnp.inf)
                m_row = jnp.maximum(m_prev, jnp.max(b_vis, axis=1, keepdims=True))
                sw = (s_raw * jnp.exp(b_vis - m_row)).astype(BF16)
                w_inter = jnp.exp(m_prev - m_row)
                floor = jnp.exp(-(pick_col(a_chunk, col) + m_row))
                qw = (qf * w_inter).astype(BF16)
                nd = _dot(sw, v_ext) + _dot(qw, call_sc[2 * s + dirn, c])
                r = 1.0 / jnp.maximum(jnp.abs(nd[:, A_DV:]), floor)
                hd = nd[:, :A_DV] * jnp.concatenate([r, r], axis=1)
                h = hd if h is None else h + hd
            o = o_ref[at(s, c), :].astype(F32)
            h_ref[at(s, c), :] = (_layer_norm(h) * gn_ref[...] * jax.nn.sigmoid(o)).astype(BF16)
        return carry

    lax.fori_loop(0, n_chunks, out_body, 0, unroll=min(n_chunks, max(1, MLSTM_OUT_UNROLL // nb)))


def _mlstm_scan(z, gates, gn4, j, *, batch, seq_len, n_layers, state=None, prev=None):
    n_tok = z.shape[0]
    n_chunks = seq_len // CHUNK
    carry_in = state is not None
    carry_out = not carry_in
    nb = _seqs_per_step(batch, seq_len, carry_in)
    t = nb * seq_len
    in_specs = [
        pl.BlockSpec((t, A_DK), lambda b, h: (b, h)),
        pl.BlockSpec((t, A_DK), lambda b, h: (b, A_HEADS + h)),
        pl.BlockSpec((t, A_DV), lambda b, h: (b, A_HEADS + h)),
        pl.BlockSpec((t, A_DV), lambda b, h: (b, 2 * A_HEADS + h)),
        pl.BlockSpec((2 * NG, t), lambda b, h: (0, b)),
        pl.BlockSpec((None, None, 1, A_DV), lambda b, h: (j, h, 0, 0)),
    ]
    args = [z, z, z, z, gates, gn4]
    aliases = {}
    if carry_in:
        in_specs += [
            pl.BlockSpec((None, None, 2, None, A_DK, A_DV), lambda b, h: (b, j, 0, h, 0, 0)),
            pl.BlockSpec((None, None, 2, None, A_DK, 1), lambda b, h: (b, j, 0, h, 0, 0)),
            pl.BlockSpec((None, None, 2, None, 1, 1), lambda b, h: (b, j, 0, h, 0, 0)),
        ]
        args += list(state)
    out_specs = [pl.BlockSpec((t, A_DV), lambda b, h: (b, h))]
    out_shape = [jax.ShapeDtypeStruct((n_tok, A_HEADS * A_DV), BF16)]
    fresh_slot = j if carry_out and prev is None else None
    if carry_out:
        lay, jb = (n_layers, 0) if prev is None else (None, j)
        out_specs += [
            pl.BlockSpec((nb, lay, 2, None, A_DK, A_DV), lambda b, h: (b, jb, 0, h, 0, 0)),
            pl.BlockSpec((nb, lay, 2, None, 1, A_DK), lambda b, h: (b, jb, 0, h, 0, 0)),
            pl.BlockSpec((nb, lay, 2, None, 1, 1), lambda b, h: (b, jb, 0, h, 0, 0)),
        ]
        out_shape += [
            jax.ShapeDtypeStruct((batch, n_layers, 2, A_HEADS, A_DK, A_DV), F32),
            jax.ShapeDtypeStruct((batch, n_layers, 2, A_HEADS, 1, A_DK), F32),
            jax.ShapeDtypeStruct((batch, n_layers, 2, A_HEADS, 1, 1), F32),
        ]
        if prev is not None:
            aliases = {len(args) + k: 1 + k for k in range(3)}
            in_specs += [pl.BlockSpec(memory_space=pl.ANY)] * 3
            args += list(prev)
    kern = functools.partial(_mlstm_kernel, n_chunks=n_chunks, nb=nb, carry_in=carry_in,
                             carry_out=carry_out, n_alias=len(aliases), fresh_slot=fresh_slot)
    return pl.pallas_call(
        kern,
        grid=(batch // nb, A_HEADS),
        in_specs=in_specs,
        out_specs=out_specs,
        out_shape=out_shape,
        input_output_aliases=aliases,
        scratch_shapes=[
            pltpu.VMEM((t, LANES), F32),
            pltpu.VMEM((nb * n_chunks, NG, CHUNK), F32),
            pltpu.VMEM((nb * n_chunks, NG, LANES), F32),
            pltpu.VMEM((nb * n_chunks, NG, LANES), F32),
            pltpu.VMEM((nb * n_chunks, NG, LANES), F32),
            pltpu.VMEM((nb * n_chunks, NG, LANES), F32),
            pltpu.VMEM((2 * nb, A_DK, A_DV + LANES), F32),
            pltpu.VMEM((2 * nb, n_chunks, A_DK, A_DV + LANES), BF16),
        ],
        compiler_params=_params("parallel", "parallel"),
        name="mlstm_scan",
    )(*args)


def _ret_kernel(*refs, n_chunks, nb, carry_in, carry_out, n_alias, fresh_slot):
    q_ref, k_ref, v_ref, gate_ref, dec_ref, gn_ref = refs[:6]
    pos = 6
    if carry_in:
        s0_ref = refs[pos]
        pos += 1
    pos += n_alias
    h_ref = refs[pos]
    pos += 1
    if carry_out:
        sout_ref = refs[pos]
        pos += 1
    s_sc, sall_sc, dsum_sc, xi_sc, zeta_sc = refs[pos:pos + 5]
    t_seq = n_chunks * CHUNK

    def at(s, c):
        return pl.ds(pl.multiple_of(s * t_seq + c * CHUNK, CHUNK), CHUNK)

    lg = jnp.log1p(-jnp.exp(-dec_ref[...] * LN2))
    lg_f = lg[0:1, :]
    lg_b = lg[1:2, :]

    @pl.when(pl.program_id(1) == 0)
    def _():
        masks = _tri_masks()
        li = lax.broadcasted_iota(jnp.int32, (CHUNK, B_DV), 0).astype(F32)
        si = lax.broadcasted_iota(jnp.int32, (CHUNK, CHUNK), 1).astype(F32)
        lq = li[:, :CHUNK]
        dsum_sc[...] = (
            jnp.where(masks[0], jnp.exp(jnp.where(masks[0], lq - si, 0.0) * lg_f[:, :CHUNK]), 0.0)
            + jnp.where(masks[1], jnp.exp(jnp.where(masks[1], si - lq, 0.0) * lg_b[:, :CHUNK]), 0.0))
        xi_sc[0] = jnp.exp((lq + 1.0) * lg_f[:, :CHUNK])
        xi_sc[1] = jnp.exp((CHUNK - lq) * lg_b[:, :CHUNK])
        zeta_sc[0] = jnp.exp((CHUNK - 1.0 - lq) * lg_f[:, :CHUNK])
        zeta_sc[1] = jnp.exp(lq * lg_b[:, :CHUNK])

    cdec = (jnp.exp(CHUNK * lg_f), jnp.exp(CHUNK * lg_b))

    if carry_in:
        qr = B_DK // 4
        for dirn in range(2):
            for n, o in enumerate((0, 2, 1, 3)):
                s_sc[dirn, n * qr:(n + 1) * qr, :] = s0_ref[dirn, o * qr:(o + 1) * qr, :]
    else:
        s_sc[...] = jnp.zeros_like(s_sc)

    def state_body(i, carry):
        for s in range(nb):
            for dirn in range(2):
                c = i if dirn == 0 else n_chunks - 1 - i
                s_old = s_sc[2 * s + dirn]
                sall_sc[s, c, dirn * B_DK:(dirn + 1) * B_DK, :] = s_old.astype(BF16)
                kz = (k_ref[at(s, c), :].astype(F32) * zeta_sc[dirn]).astype(BF16)
                s_sc[2 * s + dirn] = cdec[dirn] * s_old + _dot_tn(kz, v_ref[at(s, c), :])
        return carry

    lax.fori_loop(0, n_chunks, state_body, 0, unroll=min(n_chunks, STATE_UNROLL))
    if carry_out:
        s_out = _layer_slot(sout_ref, fresh_slot)
        for s in range(nb):
            for dirn in range(2):
                s_out[s, dirn] = s_sc[2 * s + dirn]

    def out_body(c, carry):
        for s in range(nb):
            q = q_ref[at(s, c), :]
            v = v_ref[at(s, c), :]
            sw = _dot_nt(q, k_ref[at(s, c), :]) * dsum_sc[...]
            qf = q.astype(F32)
            qx = jnp.concatenate([qf * xi_sc[0], qf * xi_sc[1]], axis=1).astype(BF16)
            h = _dot(sw.astype(BF16), v) + _dot(qx, sall_sc[s, c])
            g = gate_ref[at(s, c), :].astype(F32)
            h_ref[at(s, c), :] = (_layer_norm(h) * gn_ref[...] * (g * jax.nn.sigmoid(g))).astype(BF16)
        return carry

    lax.fori_loop(0, n_chunks, out_body, 0, unroll=min(n_chunks, max(1, OUT_UNROLL // nb)))


def _ret_scan(z, dec_rep, gn4, j, *, batch, seq_len, n_layers, state=None, prev=None):
    n_tok = z.shape[0]
    n_chunks = seq_len // CHUNK
    carry_in = state is not None
    carry_out = not carry_in
    nb = _seqs_per_step(batch, seq_len, carry_in)
    t = nb * seq_len
    in_specs = [
        pl.BlockSpec((t, B_DK), lambda h, b: (b, h)),
        pl.BlockSpec((t, B_DK), lambda h, b: (b, B_HEADS + h)),
        pl.BlockSpec((t, B_DV), lambda h, b: (b, B_HEADS + h)),
        pl.BlockSpec((t, B_DV), lambda h, b: (b, 2 * B_HEADS + h)),
        pl.BlockSpec((None, None, 2, B_DV), lambda h, b: (j, h, 0, 0)),
        pl.BlockSpec((None, None, 1, B_DV), lambda h, b: (j, h, 0, 0)),
    ]
    args = [z, z, z, z, dec_rep, gn4]
    aliases = {}
    if carry_in:
        in_specs += [pl.BlockSpec((None, None, 2, None, B_DK, B_DV), lambda h, b: (b, j, 0, h, 0, 0))]
        args += [state]
    out_specs = [pl.BlockSpec((t, B_DV), lambda h, b: (b, h))]
    out_shape = [jax.ShapeDtypeStruct((n_tok, B_HEADS * B_DV), BF16)]
    fresh_slot = j if carry_out and prev is None else None
    if carry_out:
        lay, jb = (n_layers, 0) if prev is None else (None, j)
        out_specs += [pl.BlockSpec((nb, lay, 2, None, B_DK, B_DV), lambda h, b: (b, jb, 0, h, 0, 0))]
        out_shape += [jax.ShapeDtypeStruct((batch, n_layers, 2, B_HEADS, B_DK, B_DV), F32)]
        if prev is not None:
            aliases = {len(args): 1}
            in_specs += [pl.BlockSpec(memory_space=pl.ANY)]
            args += [prev]
    kern = functools.partial(_ret_kernel, n_chunks=n_chunks, nb=nb, carry_in=carry_in,
                             carry_out=carry_out, n_alias=len(aliases), fresh_slot=fresh_slot)
    return pl.pallas_call(
        kern,
        grid=(B_HEADS, batch // nb),
        in_specs=in_specs,
        out_specs=out_specs,
        out_shape=out_shape,
        input_output_aliases=aliases,
        scratch_shapes=[
            pltpu.VMEM((2 * nb, B_DK, B_DV), F32),
            pltpu.VMEM((nb, n_chunks, 2 * B_DK, B_DV), BF16),
            pltpu.VMEM((CHUNK, CHUNK), F32),
            pltpu.VMEM((2, CHUNK, B_DK), F32),
            pltpu.VMEM((2, CHUNK, B_DK), F32),
        ],
        compiler_params=_params("parallel", "arbitrary"),
        name="ret_scan",
    )(*args)


def _outproj_kernel(h_ref, w_ref, x_ref, g_ref, gate_ref, o_ref):
    w = w_ref[...].astype(BF16)
    for r in range(o_ref.shape[0] // TR_OUT):
        rows = slice(r * TR_OUT, (r + 1) * TR_OUT)
        y = _dot(h_ref[rows, :], w)
        o_ref[rows, :] = x_ref[rows, :] + gate_ref[...] * (_rms(y) * g_ref[...])


def _outproj(h, w, j, x, ng4, mod5, layer, *, seq_len, sample):
    n_tok, hv = h.shape
    tm = TM_OUT
    row = _mod_row(sample, seq_len, tm)
    return pl.pallas_call(
        _outproj_kernel,
        grid=(n_tok // tm,),
        in_specs=[
            pl.BlockSpec((tm, hv), lambda i: (i, 0)),
            _resident((None, hv, D_MODEL), (j, 0, 0)),
            pl.BlockSpec((tm, D_MODEL), lambda i: (i, 0)),
            pl.BlockSpec((None, None, 1, D_MODEL), lambda i: (layer, 1, 0, 0)),
            pl.BlockSpec((None, None, None, 1, D_MODEL), lambda i: (layer, row(i), 2, 0, 0)),
        ],
        out_specs=pl.BlockSpec((tm, D_MODEL), lambda i: (i, 0)),
        out_shape=jax.ShapeDtypeStruct((n_tok, D_MODEL), F32),
        compiler_params=_params("parallel"),
        name="outproj",
    )(h, w, x, ng4, mod5)


def _conv3(hs_ref, half, h, cw, cb, seg, n_seg):
    for s in range(n_seg):
        base = SUBLANES + s * (seg + SUBLANES)
        h_seg = h[s * seg:(s + 1) * seg, :]
        hs_ref[2 * half, base + 1:base + 1 + seg, :] = h_seg
        hs_ref[2 * half + 1, base - 1:base - 1 + seg, :] = h_seg
    parts = []
    for s in range(n_seg):
        base = SUBLANES + s * (seg + SUBLANES)
        h_prev = hs_ref[2 * half, base:base + seg, :]
        h_next = hs_ref[2 * half + 1, base:base + seg, :]
        h_mid = h[s * seg:(s + 1) * seg, :]
        parts.append(h_prev * cw[0:1, :] + h_mid * cw[1:2, :] + h_next * cw[2:3, :] + cb)
    return parts


def _ffn_kernel(x_ref, g2_ref, sh_ref, sc_ref, wup_ref, cw_ref, cb_ref, wd_ref, g3_ref, gate_ref,
                o_ref, u_sc, act_sc, hs_sc, *, seg, n_seg, tf):
    u = _rms(x_ref[...]) * g2_ref[...] * (1.0 + sc_ref[...]) + sh_ref[...]
    u_sc[...] = u.astype(BF16)
    zero_rows = jnp.zeros((SUBLANES, tf), F32)
    for s in range(n_seg):
        base = SUBLANES + s * (seg + SUBLANES)
        for half in range(2):
            hs_sc[2 * half, base:base + SUBLANES, :] = zero_rows
            hs_sc[2 * half + 1, base + seg - SUBLANES:base + seg, :] = zero_rows

    for cidx in range(D_FF // tf):
        cg = slice(cidx * tf, (cidx + 1) * tf)
        cu = slice(D_FF + cidx * tf, D_FF + (cidx + 1) * tf)
        hg = _conv3(hs_sc, 0, _dot(u_sc[...], wup_ref[:, cg]), cw_ref[:, cg], cb_ref[:, cg], seg, n_seg)
        hu = _conv3(hs_sc, 1, _dot(u_sc[...], wup_ref[:, cu]), cw_ref[:, cu], cb_ref[:, cu], seg, n_seg)
        for s in range(n_seg):
            act = jax.nn.gelu(hg[s], approximate=True) * hu[s]
            act_sc[s * seg:(s + 1) * seg, cg] = act.astype(BF16)

    tr = o_ref.shape[0] // FFN_DOWN_BLOCKS
    for r in range(FFN_DOWN_BLOCKS):
        rows = slice(r * tr, (r + 1) * tr)
        f = _dot(act_sc[rows, :], wd_ref[...])
        o_ref[rows, :] = x_ref[rows, :] + gate_ref[...] * (_rms(f) * g3_ref[...])


def _ffn(x, ng4, mod5, layer, w_up, conv_w, conv_b, w_down, *, seq_len, sample):
    n_tok = x.shape[0]
    tm, tf = TM_FFN, TF_FFN
    row = _mod_row(sample, seq_len, tm)
    seg = GRID_W if sample else seq_len
    n_seg = tm // seg
    kern = functools.partial(_ffn_kernel, seg=seg, n_seg=n_seg, tf=tf)
    mod_spec = lambda k: pl.BlockSpec((None, None, None, 1, D_MODEL), lambda i: (layer, row(i), k, 0, 0))
    gain_spec = lambda k: pl.BlockSpec((None, None, 1, D_MODEL), lambda i: (layer, k, 0, 0))
    return pl.pallas_call(
        kern,
        grid=(n_tok // tm,),
        in_specs=[
            pl.BlockSpec((tm, D_MODEL), lambda i: (i, 0)),
            gain_spec(2),
            mod_spec(3),
            mod_spec(4),
            _resident((D_MODEL, 2 * D_FF), (0, 0)),
            _resident((None, 3, 2 * D_FF), (layer, 0, 0)),
            _resident((None, 1, 2 * D_FF), (layer, 0, 0)),
            _resident((D_FF, D_MODEL), (0, 0)),
            gain_spec(3),
            mod_spec(5),
        ],
        out_specs=pl.BlockSpec((tm, D_MODEL), lambda i: (i, 0)),
        out_shape=jax.ShapeDtypeStruct((n_tok, D_MODEL), F32),
        scratch_shapes=[
            pltpu.VMEM((tm, D_MODEL), BF16),
            pltpu.VMEM((tm, D_FF), BF16),
            pltpu.VMEM((4, SUBLANES + n_seg * (seg + SUBLANES), tf), F32),
        ],
        compiler_params=_params("parallel"),
        name="convffn",
    )(x, ng4, mod5, mod5, w_up, conv_w, conv_b, w_down, ng4, mod5)


def _rope_tables(seq_len):
    quarter = B_DK // 4
    inv = ROPE_BASE ** (-jnp.arange(quarter, dtype=F32) / quarter)
    t = jnp.arange(seq_len)
    rows = (t // GRID_W).astype(F32)[:, None] * inv
    cols = (t % GRID_W).astype(F32)[:, None] * inv
    cos = jnp.concatenate([jnp.cos(rows), jnp.cos(cols)] * 2, axis=-1)
    sin = jnp.concatenate([-jnp.sin(rows), -jnp.sin(cols), jnp.sin(rows), jnp.sin(cols)], axis=-1)
    return cos, sin


def _rope_qk_weights(w_in):
    n_l = w_in.shape[0]
    quarter = B_DK // 4
    w_qk = w_in[:, :, :B_QK].astype(BF16).reshape(n_l, D_MODEL, 2 * B_HEADS, 2, 2, quarter)
    return jnp.swapaxes(w_qk, 3, 4).reshape(n_l, D_MODEL, B_QK)


def _gate_weights(w_in, b_gate):
    n_l = w_in.shape[0]
    order = jnp.array((0, 2, 1, 3))
    wg = w_in[:, :, A_MAIN:].reshape(n_l, D_MODEL, 4, A_HEADS)[:, :, order, :]
    wg_t = jnp.transpose(wg, (0, 2, 3, 1)).reshape(n_l, 2 * NG, D_MODEL).astype(BF16)
    bg = b_gate[:, order, :].reshape(n_l, 2 * NG, 1)
    return wg_t, bg


def kernel(x_prompt, x_sample, state_mlstm_C, state_mlstm_n, state_mlstm_m, state_ret_S, c, c_ctx,
           norm_gain, ada_w, ada_b, ml_w_in, ml_b_gate, ml_norm, ml_w_out,
           ret_w_in, ret_decay, ret_norm, ret_w_out, ffn_w_up, ffn_conv, ffn_conv_b, ffn_w_down):
    bp, tp, _ = x_prompt.shape
    bs, ts, _ = x_sample.shape
    n_a = ml_w_in.shape[0]
    n_b = ret_w_in.shape[0]

    cond = jnp.concatenate([c_ctx[None, :], c, jnp.zeros((MOD_ROWS - 1 - bs, D_MODEL), F32)], axis=0)
    mod5 = _modulation(cond, ada_w, ada_b).reshape(DEPTH, MOD_ROWS, 6, 1, D_MODEL)
    ng4 = norm_gain.reshape(DEPTH, 4, 1, D_MODEL)
    rope = _rope_tables(ts)

    ml_gates = _gate_weights(ml_w_in, ml_b_gate)
    ret_w_qk_rope = _rope_qk_weights(ret_w_in)
    ffn_conv_b3 = ffn_conv_b.reshape(DEPTH, 1, 2 * D_FF)
    ml_gn4 = ml_norm.reshape(n_a, A_HEADS, 1, A_DV)
    ret_gn4 = ret_norm.reshape(n_b, B_HEADS, 1, B_DV)
    dec_rep = jnp.broadcast_to(jnp.swapaxes(ret_decay, 1, 2)[..., None], (n_b, B_HEADS, 2, B_DV))
    st_c = state_mlstm_C
    st_n = state_mlstm_n.reshape(bs, n_a, 2, A_HEADS, A_DK, 1)
    st_m = state_mlstm_m.reshape(bs, n_a, 2, A_HEADS, 1, 1)

    groups = [
        dict(x=x_prompt.reshape(bp * tp, D_MODEL), batch=bp, seq_len=tp, sample=False),
        dict(x=x_sample.reshape(bs * ts, D_MODEL), batch=bs, seq_len=ts, sample=True),
    ]
    ml_states = None
    ret_states = None
    for i in range(DEPTH):
        j = i // N_MIXERS
        for grp in groups:
            x = grp["x"]
            geo = dict(seq_len=grp["seq_len"], sample=grp["sample"])
            bt = dict(batch=grp["batch"], seq_len=grp["seq_len"])
            cast = () if grp["sample"] else ((ffn_w_up, i), (ffn_w_down, i))
            if i % N_MIXERS == 0:
                n_qk = A_HEADS * A_DK // TN_PROJ
                z, gates, *ffn_w = _inproj(x, ng4, mod5, i, [(ml_w_in, j, 0)], A_MAIN, n_q=n_qk, n_k=n_qk,
                                           k_scale=A_DK ** -0.5, gates=ml_gates + (j,), cast=cast,
                                           tm=TM_PROJ_MLSTM, **geo)
                if grp["sample"]:
                    (h,) = _mlstm_scan(z, gates, ml_gn4, j, n_layers=n_a, state=(st_c, st_n, st_m), **bt)
                else:
                    h, *ml_states = _mlstm_scan(z, gates, ml_gn4, j, n_layers=n_a, prev=ml_states, **bt)
                x = _outproj(h, ml_w_out, j, x, ng4, mod5, i, **geo)
            else:
                n_qk = B_HEADS * B_DK // TN_PROJ
                w_qk = ret_w_qk_rope if grp["sample"] else ret_w_in
                w_parts = [(w_qk, j, 0), (ret_w_in, j, 1), (ret_w_in, j, 2)]
                z, *ffn_w = _inproj(x, ng4, mod5, i, w_parts, B_QK, n_q=n_qk, n_k=n_qk, k_scale=B_DK ** -0.5,
                                    rope=rope if grp["sample"] else None, cast=cast, tm=TM_PROJ, **geo)
                if grp["sample"]:
                    (h,) = _ret_scan(z, dec_rep, ret_gn4, j, n_layers=n_b, state=state_ret_S, **bt)
                else:
                    h, ret_states = _ret_scan(z, dec_rep, ret_gn4, j, n_layers=n_b, prev=ret_states, **bt)
                x = _outproj(h, ret_w_out, j, x, ng4, mod5, i, **geo)
            if ffn_w:
                ffn_w_up_b, ffn_w_down_b = ffn_w
            grp["x"] = _ffn(x, ng4, mod5, i, ffn_w_up_b, ffn_conv, ffn_conv_b3, ffn_w_down_b, **geo)

    y_prompt = groups[0]["x"].reshape(bp, tp, D_MODEL)
    y_sample = groups[1]["x"].reshape(bs, ts, D_MODEL)
    new_c, new_n, new_m = ml_states
    return (y_prompt, y_sample, new_c, new_n.reshape(bp, n_a, 2, A_HEADS, A_DK),
            new_m.reshape(bp, n_a, 2, A_HEADS), ret_states)
```

```python
import functools
import math

import jax
import jax.numpy as jnp
from jax import lax
from jax.experimental import pallas as pl
from jax.experimental.pallas import tpu as pltpu

D_MODEL = 1024
DEPTH = 4
GRID_W = 64
CHUNK = 128
N_MIXERS = 2
A_HEADS = 4
A_DV = D_MODEL // A_HEADS
A_DK = A_DV // 2
A_MAIN = 2 * A_HEADS * A_DK + 2 * A_HEADS * A_DV
B_HEADS = 8
B_DK = D_MODEL // B_HEADS
B_DV = 2 * D_MODEL // B_HEADS
B_QK = 2 * B_HEADS * B_DK
ROPE_BASE = 10000.0
D_FF = ((8 * D_MODEL // 3 + 127) // 128) * 128
EPS = 1e-6
LN2 = math.log(2.0)

F32 = jnp.float32
BF16 = jnp.bfloat16

LANES = 128
SUBLANES = 8
MOD_ROWS = 16
VMEM_LIMIT = 48 * 1024 * 1024

TM_PROJ = 512
TN_PROJ = 512
TM_OUT = 1024
TR_OUT = 512
TM_FFN = 512
TF_FFN = 256
FFN_DOWN_BLOCKS = 2
GATE_UNROLL = 16
STATE_UNROLL = 16
OUT_UNROLL = 16
MLSTM_OUT_UNROLL = 16
SCAN_TOKENS = 2048


def _dot(a, b):
    return jnp.dot(a, b, preferred_element_type=F32)


def _dot_nt(a, b):
    return lax.dot_general(a, b, (((1,), (1,)), ((), ())), preferred_element_type=F32)


def _dot_tn(a, b):
    return lax.dot_general(a, b, (((0,), (0,)), ((), ())), preferred_element_type=F32)


def _rms(x):
    return x * lax.rsqrt(jnp.mean(x * x, axis=-1, keepdims=True) + EPS)


def _layer_norm(h):
    d = h - jnp.mean(h, axis=-1, keepdims=True)
    return d * lax.rsqrt(jnp.mean(d * d, axis=-1, keepdims=True) + EPS)


def _params(*sem):
    return pltpu.CompilerParams(dimension_semantics=sem, vmem_limit_bytes=VMEM_LIMIT)


def _resident(shape, index):
    return pl.BlockSpec(shape, lambda *_: index, pipeline_mode=pl.Buffered(1))


def _mod_kernel(cond_ref, w_ref, b_ref, o_ref):
    cnd = cond_ref[...]
    s = cnd * jax.nn.sigmoid(cnd)
    o_ref[...] = _dot(s.astype(BF16), w_ref[...].astype(BF16)) + b_ref[...]


def _modulation(cond, ada_w, ada_b):
    tn = 1024
    n_out = ada_w.shape[-1]
    return pl.pallas_call(
        _mod_kernel,
        grid=(DEPTH, n_out // tn),
        in_specs=[
            pl.BlockSpec((MOD_ROWS, D_MODEL), lambda l, j: (0, 0)),
            pl.BlockSpec((None, D_MODEL, tn), lambda l, j: (l, 0, j)),
            pl.BlockSpec((None, 1, tn), lambda l, j: (l, 0, j)),
        ],
        out_specs=pl.BlockSpec((None, MOD_ROWS, tn), lambda l, j: (l, 0, j)),
        out_shape=jax.ShapeDtypeStruct((DEPTH, MOD_ROWS, n_out), F32),
        compiler_params=_params("parallel", "parallel"),
        name="modulation",
    )(cond, ada_w, ada_b.reshape(DEPTH, 1, n_out))


def _mod_row(sample, seq_len, tm):
    if not sample:
        return lambda i: 0
    tiles_per_seq = seq_len // tm
    return lambda i: 1 + i // tiles_per_seq


def _rope_slab(x, cos, sin):
    return x * cos + pltpu.roll(x, 64, axis=1) * sin


def _inproj_kernel(*refs, n_w, n_q, n_k, k_scale, rope, gates, n_cast, tn):
    x_ref, g_ref, sh_ref, sc_ref = refs[:4]
    w_refs = refs[4:4 + n_w]
    pos = 4 + n_w
    if gates:
        wg_ref, bg_ref = refs[pos:pos + 2]
        pos += 2
    if rope:
        cos_ref, sin_ref = refs[pos:pos + 2]
        pos += 2
    cast_in = refs[pos:pos + n_cast]
    pos += n_cast
    z_ref = refs[pos]
    pos += 1
    if gates:
        gates_ref = refs[pos]
        pos += 1
    cast_out = refs[pos:pos + n_cast]
    pos += n_cast
    u_sc = refs[pos]

    for src, dst in zip(cast_in, cast_out):
        dst[...] = src[...].astype(BF16)

    u = _rms(x_ref[...]) * g_ref[...] * (1.0 + sc_ref[...]) + sh_ref[...]
    u_sc[...] = u.astype(BF16)
    if gates:
        gates_ref[...] = _dot_nt(wg_ref[...], u_sc[...]) + bg_ref[...]

    wb = w_refs[0].shape[1]
    for j in range(z_ref.shape[1] // tn):
        part, off = divmod(j * tn, wb)
        z = _dot(u_sc[...], w_refs[part][:, off:off + tn].astype(BF16))
        scale = k_scale if n_q <= j < n_q + n_k else 1.0
        if rope and j < n_q + n_k:
            for s in range(tn // LANES):
                r = _rope_slab(z[:, s * LANES:(s + 1) * LANES], cos_ref[...], sin_ref[...])
                if scale != 1.0:
                    r = r * scale
                z_ref[:, j * tn + s * LANES:j * tn + (s + 1) * LANES] = r.astype(BF16)
        elif scale != 1.0:
            z_ref[:, j * tn:(j + 1) * tn] = (z * scale).astype(BF16)
        else:
            z_ref[:, j * tn:(j + 1) * tn] = z.astype(BF16)


def _inproj(x, ng4, mod5, layer, w_parts, wb, *, seq_len, sample, n_q, n_k, k_scale, rope=None, gates=None,
            cast=()):
    n_tok = x.shape[0]
    tm, tn = TM_PROJ, TN_PROJ
    n_col = wb * len(w_parts)
    row = _mod_row(sample, seq_len, tm)
    in_specs = [
        pl.BlockSpec((tm, D_MODEL), lambda i: (i, 0)),
        pl.BlockSpec((None, None, 1, D_MODEL), lambda i: (layer, 0, 0, 0)),
        pl.BlockSpec((None, None, None, 1, D_MODEL), lambda i: (layer, row(i), 0, 0, 0)),
        pl.BlockSpec((None, None, None, 1, D_MODEL), lambda i: (layer, row(i), 1, 0, 0)),
    ]
    in_specs += [_resident((None, D_MODEL, wb), (jl, 0, blk)) for _, jl, blk in w_parts]
    args = [x, ng4, mod5, mod5] + [w for w, _, _ in w_parts]
    out_specs = [pl.BlockSpec((tm, n_col), lambda i: (i, 0))]
    out_shape = [jax.ShapeDtypeStruct((n_tok, n_col), BF16)]
    if gates is not None:
        wg_t, bg, jg = gates
        n_g = wg_t.shape[1]
        in_specs += [_resident((None, n_g, D_MODEL), (jg, 0, 0)), _resident((None, n_g, 1), (jg, 0, 0))]
        args += [wg_t, bg]
        out_specs += [pl.BlockSpec((n_g, tm), lambda i: (0, i))]
        out_shape += [jax.ShapeDtypeStruct((n_g, n_tok), F32)]
    if rope is not None:
        tiles_per_seq = seq_len // tm
        in_specs += [pl.BlockSpec((tm, LANES), lambda i: (i % tiles_per_seq, 0))] * 2
        args += list(rope)
    n_steps = n_tok // tm
    for w_src, jl in cast:
        _, rows, cols = w_src.shape
        slab = rows // n_steps
        in_specs += [pl.BlockSpec((None, slab, cols), lambda i, jl=jl: (jl, i, 0))]
        args += [w_src]
        out_specs += [pl.BlockSpec((slab, cols), lambda i: (i, 0))]
        out_shape += [jax.ShapeDtypeStruct((rows, cols), BF16)]
    kern = functools.partial(_inproj_kernel, n_w=len(w_parts), n_q=n_q, n_k=n_k, k_scale=k_scale,
                             rope=rope is not None, gates=gates is not None, n_cast=len(cast), tn=tn)
    return pl.pallas_call(
        kern,
        grid=(n_tok // tm,),
        in_specs=in_specs,
        out_specs=out_specs,
        out_shape=out_shape,
        scratch_shapes=[pltpu.VMEM((tm, D_MODEL), BF16)],
        compiler_params=_params("parallel"),
        name="inproj",
    )(*args)


def _tri_masks():
    li = lax.broadcasted_iota(jnp.int32, (CHUNK, CHUNK), 0)
    si = lax.broadcasted_iota(jnp.int32, (CHUNK, CHUNK), 1)
    return si <= li, si >= li


def _layer_slot(ref, fresh_slot):
    if fresh_slot is None:
        return ref
    for other in range(ref.shape[1]):
        if other != fresh_slot:
            ref[:, other] = jnp.zeros(ref.shape[:1] + ref.shape[2:], ref.dtype)
    return ref.at[:, fresh_slot]


def _seqs_per_step(batch, seq_len, carry_in):
    if carry_in:
        return 1
    nb = max(1, SCAN_TOKENS // seq_len)
    while batch % nb:
        nb -= 1
    return nb


NG = 2 * A_HEADS


def _split_dot(x, mask_b):
    hi = x.astype(BF16)
    r1 = x - hi.astype(F32)
    mid = r1.astype(BF16)
    lo = (r1 - mid.astype(F32)).astype(BF16)
    return _dot(hi, mask_b) + _dot(mid, mask_b) + _dot(lo, mask_b)


def _mlstm_kernel(*refs, n_chunks, nb, carry_in, carry_out, n_alias, fresh_slot):
    q_ref, k_ref, v_ref, o_ref, g_ref, gn_ref = refs[:6]
    pos = 6
    if carry_in:
        c0_ref, n0_ref, m0_ref = refs[pos:pos + 3]
        pos += 3
    pos += n_alias
    h_ref = refs[pos]
    pos += 1
    if carry_out:
        cout_ref, nout_ref, mout_ref = refs[pos:pos + 3]
        pos += 3
    ab_sc, bt_sc, g_sc, bm_sc, mpf_sc, mpb_sc, c_sc, call_sc = refs[pos:pos + 8]

    head = pl.program_id(1)
    masks = _tri_masks()
    ones_b = jnp.ones((CHUNK, LANES), BF16)
    sum_b = jnp.concatenate([masks[1].astype(BF16), masks[0].astype(BF16), ones_b], axis=1)
    lane = lax.broadcasted_iota(jnp.int32, (CHUNK, LANES), 1)
    grow = lax.broadcasted_iota(jnp.int32, (2 * NG, CHUNK), 0)
    gsub = lax.broadcasted_iota(jnp.int32, (NG, LANES), 0)
    zpad = jnp.zeros((LANES - 2 * NG, CHUNK), F32)
    cols = (head, head + A_HEADS)
    mp_sc = (mpf_sc, mpb_sc)
    t_seq = n_chunks * CHUNK

    def pick_col(x, col):
        return jnp.sum(jnp.where(lane == col, x, 0.0), axis=1, keepdims=True)

    def at(s, c):
        return pl.ds(pl.multiple_of(s * t_seq + c * CHUNK, CHUNK), CHUNK)

    def gate_body(c, carry):
        for s in range(nb):
            idx = s * n_chunks + c
            gates = g_ref[:, at(s, c)]
            lf = jnp.minimum(gates, 0.0) - jnp.log1p(jnp.exp(-jnp.abs(gates)))
            lf = jnp.where(grow >= NG, lf, 0.0)
            sums = _split_dot(lf, sum_b)
            tot = sums[:, 2 * CHUNK:]
            a_all = jnp.where(grow < NG + A_HEADS, sums[:, :CHUNK], sums[:, CHUNK:2 * CHUNK])[NG:, :]
            b_all = gates[:NG, :] - a_all
            bt_sc[idx] = b_all
            ab_sc[at(s, c), :] = jnp.concatenate([a_all, b_all, zpad], axis=0).T
            g_sc[idx] = tot[NG:, :]
            bm_sc[idx] = jnp.broadcast_to(jnp.max(b_all, axis=1, keepdims=True), (NG, LANES))
        return carry

    lax.fori_loop(0, n_chunks, gate_body, 0, unroll=min(n_chunks, GATE_UNROLL))

    if carry_in:
        m_init = (jnp.broadcast_to(m0_ref[0], (NG, LANES)), jnp.broadcast_to(m0_ref[1], (NG, LANES)))
    else:
        m_init = (jnp.zeros((NG, LANES), F32),) * (2 * nb)

    def m_body(i, carry):
        out = []
        for s in range(nb):
            m_f, m_b = carry[2 * s], carry[2 * s + 1]
            jf = s * n_chunks + i
            jb = s * n_chunks + n_chunks - 1 - i
            mpf_sc[jf] = m_f
            mpb_sc[jb] = m_b
            out.append(g_sc[jf] + jnp.maximum(m_f, bm_sc[jf]))
            out.append(g_sc[jb] + jnp.maximum(m_b, bm_sc[jb]))
        return tuple(out)

    m_last = lax.fori_loop(0, n_chunks, m_body, m_init, unroll=min(n_chunks, STATE_UNROLL))

    if carry_in:
        for dirn in range(2):
            c_sc[dirn, :, :A_DV] = c0_ref[dirn]
            c_sc[dirn, :, A_DV:] = jnp.broadcast_to(n0_ref[dirn], (A_DK, LANES))
    else:
        c_sc[...] = jnp.zeros_like(c_sc)

    def state_body(i, carry):
        for s in range(nb):
            for dirn in range(2):
                c = i if dirn == 0 else n_chunks - 1 - i
                idx = s * n_chunks + c
                mp_row = mp_sc[dirn][idx, pl.ds(cols[dirn], 1), :]
                m_top = jnp.maximum(mp_row, bm_sc[idx, pl.ds(cols[dirn], 1), :])
                ws = jnp.exp(pick_col(ab_sc[at(s, c), :], NG + cols[dirn]) - m_top)
                dec = jnp.exp(mp_row - m_top)
                dec = jnp.concatenate([dec] * (c_sc.shape[2] // LANES), axis=1)
                c_old = c_sc[2 * s + dirn]
                call_sc[2 * s + dirn, c] = c_old.astype(BF16)
                kw = (k_ref[at(s, c), :].astype(F32) * ws).astype(BF16)
                upd = jnp.concatenate([_dot_tn(kw, v_ref[at(s, c), :]), _dot_tn(kw, ones_b)], axis=1)
                c_sc[2 * s + dirn] = dec * c_old + upd
        return carry

    lax.fori_loop(0, n_chunks, state_body, 0, unroll=min(n_chunks, STATE_UNROLL))
    if carry_out:
        outs = [_layer_slot(r, fresh_slot) for r in (cout_ref, nout_ref, mout_ref)]
        for s in range(nb):
            for dirn in range(2):
                outs[0][s, dirn] = c_sc[2 * s + dirn, :, :A_DV]
                outs[1][s, dirn] = c_sc[2 * s + dirn, :, A_DV:].T[0:1, :]
                m_end = jnp.where(gsub == cols[dirn], m_last[2 * s + dirn], 0.0)
                outs[2][s, dirn] = jnp.sum(m_end, axis=0, keepdims=True)[:, 0:1]

    def out_body(c, carry):
        for s in range(nb):
            q = q_ref[at(s, c), :]
            qf = q.astype(F32)
            v_ext = jnp.concatenate([v_ref[at(s, c), :], ones_b], axis=1)
            s_raw = _dot_nt(q, k_ref[at(s, c), :])
            a_chunk = ab_sc[at(s, c), :]
            idx = s * n_chunks + c
            h = None
            for dirn in range(2):
                col = cols[dirn]
                m_prev = mp_sc[dirn][idx, pl.ds(col, 1), :]
                b_vis = jnp.where(masks[dirn], bt_sc[idx, pl.ds(col, 1), :], -jnp.inf)
                m_row = jnp.maximum(m_prev, jnp.max(b_vis, axis=1, keepdims=True))
                sw = (s_raw * jnp.exp(b_vis - m_row)).astype(BF16)
                w_inter = jnp.exp(m_prev - m_row)
                floor = jnp.exp(-(pick_col(a_chunk, col) + m_row))
                qw = (qf * w_inter).astype(BF16)
                nd = _dot(sw, v_ext) + _dot(qw, call_sc[2 * s + dirn, c])
                r = 1.0 / jnp.maximum(jnp.abs(nd[:, A_DV:]), floor)
                hd = nd[:, :A_DV] * jnp.concatenate([r, r], axis=1)
                h = hd if h is None else h + hd
            o = o_ref[at(s, c), :].astype(F32)
            h_ref[at(s, c), :] = (_layer_norm(h) * gn_ref[...] * jax.nn.sigmoid(o)).astype(BF16)
        return carry

    lax.fori_loop(0, n_chunks, out_body, 0, unroll=min(n_chunks, max(1, MLSTM_OUT_UNROLL // nb)))


def _mlstm_scan(z, gates, gn4, j, *, batch, seq_len, n_layers, state=None, prev=None):
    n_tok = z.shape[0]
    n_chunks = seq_len // CHUNK
    carry_in = state is not None
    carry_out = not carry_in
    nb = _seqs_per_step(batch, seq_len, carry_in)
    t = nb * seq_len
    in_specs = [
        pl.BlockSpec((t, A_DK), lambda b, h: (b, h)),
        pl.BlockSpec((t, A_DK), lambda b, h: (b, A_HEADS + h)),
        pl.BlockSpec((t, A_DV), lambda b, h: (b, A_HEADS + h)),
        pl.BlockSpec((t, A_DV), lambda b, h: (b, 2 * A_HEADS + h)),
        pl.BlockSpec((2 * NG, t), lambda b, h: (0, b)),
        pl.BlockSpec((None, None, 1, A_DV), lambda b, h: (j, h, 0, 0)),
    ]
    args = [z, z, z, z, gates, gn4]
    aliases = {}
    if carry_in:
        in_specs += [
            pl.BlockSpec((None, None, 2, None, A_DK, A_DV), lambda b, h: (b, j, 0, h, 0, 0)),
            pl.BlockSpec((None, None, 2, None, A_DK, 1), lambda b, h: (b, j, 0, h, 0, 0)),
            pl.BlockSpec((None, None, 2, None, 1, 1), lambda b, h: (b, j, 0, h, 0, 0)),
        ]
        args += list(state)
    out_specs = [pl.BlockSpec((t, A_DV), lambda b, h: (b, h))]
    out_shape = [jax.ShapeDtypeStruct((n_tok, A_HEADS * A_DV), BF16)]
    fresh_slot = j if carry_out and prev is None else None
    if carry_out:
        lay, jb = (n_layers, 0) if prev is None else (None, j)
        out_specs += [
            pl.BlockSpec((nb, lay, 2, None, A_DK, A_DV), lambda b, h: (b, jb, 0, h, 0, 0)),
            pl.BlockSpec((nb, lay, 2, None, 1, A_DK), lambda b, h: (b, jb, 0, h, 0, 0)),
            pl.BlockSpec((nb, lay, 2, None, 1, 1), lambda b, h: (b, jb, 0, h, 0, 0)),
        ]
        out_shape += [
            jax.ShapeDtypeStruct((batch, n_layers, 2, A_HEADS, A_DK, A_DV), F32),
            jax.ShapeDtypeStruct((batch, n_layers, 2, A_HEADS, 1, A_DK), F32),
            jax.ShapeDtypeStruct((batch, n_layers, 2, A_HEADS, 1, 1), F32),
        ]
        if prev is not None:
            aliases = {len(args) + k: 1 + k for k in range(3)}
            in_specs += [pl.BlockSpec(memory_space=pl.ANY)] * 3
            args += list(prev)
    kern = functools.partial(_mlstm_kernel, n_chunks=n_chunks, nb=nb, carry_in=carry_in,
                             carry_out=carry_out, n_alias=len(aliases), fresh_slot=fresh_slot)
    return pl.pallas_call(
        kern,
        grid=(batch // nb, A_HEADS),
        in_specs=in_specs,
        out_specs=out_specs,
        out_shape=out_shape,
        input_output_aliases=aliases,
        scratch_shapes=[
            pltpu.VMEM((t, LANES), F32),
            pltpu.VMEM((nb * n_chunks, NG, CHUNK), F32),
            pltpu.VMEM((nb * n_chunks, NG, LANES), F32),
            pltpu.VMEM((nb * n_chunks, NG, LANES), F32),
            pltpu.VMEM((nb * n_chunks, NG, LANES), F32),
            pltpu.VMEM((nb * n_chunks, NG, LANES), F32),
            pltpu.VMEM((2 * nb, A_DK, A_DV + LANES), F32),
            pltpu.VMEM((2 * nb, n_chunks, A_DK, A_DV + LANES), BF16),
        ],
        compiler_params=_params("parallel", "parallel"),
        name="mlstm_scan",
    )(*args)


def _ret_kernel(*refs, n_chunks, nb, carry_in, carry_out, n_alias, fresh_slot):
    q_ref, k_ref, v_ref, gate_ref, dec_ref, gn_ref = refs[:6]
    pos = 6
    if carry_in:
        s0_ref = refs[pos]
        pos += 1
    pos += n_alias
    h_ref = refs[pos]
    pos += 1
    if carry_out:
        sout_ref = refs[pos]
        pos += 1
    s_sc, sall_sc, dsum_sc, xi_sc, zeta_sc = refs[pos:pos + 5]
    t_seq = n_chunks * CHUNK

    def at(s, c):
        return pl.ds(pl.multiple_of(s * t_seq + c * CHUNK, CHUNK), CHUNK)

    lg = jnp.log1p(-jnp.exp(-dec_ref[...] * LN2))
    lg_f = lg[0:1, :]
    lg_b = lg[1:2, :]

    @pl.when(pl.program_id(1) == 0)
    def _():
        masks = _tri_masks()
        li = lax.broadcasted_iota(jnp.int32, (CHUNK, B_DV), 0).astype(F32)
        si = lax.broadcasted_iota(jnp.int32, (CHUNK, CHUNK), 1).astype(F32)
        lq = li[:, :CHUNK]
        dsum_sc[...] = (
            jnp.where(masks[0], jnp.exp(jnp.where(masks[0], lq - si, 0.0) * lg_f[:, :CHUNK]), 0.0)
            + jnp.where(masks[1], jnp.exp(jnp.where(masks[1], si - lq, 0.0) * lg_b[:, :CHUNK]), 0.0))
        xi_sc[0] = jnp.exp((lq + 1.0) * lg_f[:, :CHUNK])
        xi_sc[1] = jnp.exp((CHUNK - lq) * lg_b[:, :CHUNK])
        zeta_sc[0] = jnp.exp((CHUNK - 1.0 - lq) * lg_f[:, :CHUNK])
        zeta_sc[1] = jnp.exp(lq * lg_b[:, :CHUNK])

    cdec = (jnp.exp(CHUNK * lg_f), jnp.exp(CHUNK * lg_b))

    if carry_in:
        qr = B_DK // 4
        for dirn in range(2):
            for n, o in enumerate((0, 2, 1, 3)):
                s_sc[dirn, n * qr:(n + 1) * qr, :] = s0_ref[dirn, o * qr:(o + 1) * qr, :]
    else:
        s_sc[...] = jnp.zeros_like(s_sc)

    def state_body(i, carry):
        for s in range(nb):
            for dirn in range(2):
                c = i if dirn == 0 else n_chunks - 1 - i
                s_old = s_sc[2 * s + dirn]
                sall_sc[s, c, dirn * B_DK:(dirn + 1) * B_DK, :] = s_old.astype(BF16)
                kz = (k_ref[at(s, c), :].astype(F32) * zeta_sc[dirn]).astype(BF16)
                s_sc[2 * s + dirn] = cdec[dirn] * s_old + _dot_tn(kz, v_ref[at(s, c), :])
        return carry

    lax.fori_loop(0, n_chunks, state_body, 0, unroll=min(n_chunks, STATE_UNROLL))
    if carry_out:
        s_out = _layer_slot(sout_ref, fresh_slot)
        for s in range(nb):
            for dirn in range(2):
                s_out[s, dirn] = s_sc[2 * s + dirn]

    def out_body(c, carry):
        for s in range(nb):
            q = q_ref[at(s, c), :]
            v = v_ref[at(s, c), :]
            sw = _dot_nt(q, k_ref[at(s, c), :]) * dsum_sc[...]
            qf = q.astype(F32)
            qx = jnp.concatenate([qf * xi_sc[0], qf * xi_sc[1]], axis=1).astype(BF16)
            h = _dot(sw.astype(BF16), v) + _dot(qx, sall_sc[s, c])
            g = gate_ref[at(s, c), :].astype(F32)
            h_ref[at(s, c), :] = (_layer_norm(h) * gn_ref[...] * (g * jax.nn.sigmoid(g))).astype(BF16)
        return carry

    lax.fori_loop(0, n_chunks, out_body, 0, unroll=min(n_chunks, max(1, OUT_UNROLL // nb)))


def _ret_scan(z, dec_rep, gn4, j, *, batch, seq_len, n_layers, state=None, prev=None):
    n_tok = z.shape[0]
    n_chunks = seq_len // CHUNK
    carry_in = state is not None
    carry_out = not carry_in
    nb = _seqs_per_step(batch, seq_len, carry_in)
    t = nb * seq_len
    in_specs = [
        pl.BlockSpec((t, B_DK), lambda h, b: (b, h)),
        pl.BlockSpec((t, B_DK), lambda h, b: (b, B_HEADS + h)),
        pl.BlockSpec((t, B_DV), lambda h, b: (b, B_HEADS + h)),
        pl.BlockSpec((t, B_DV), lambda h, b: (b, 2 * B_HEADS + h)),
        pl.BlockSpec((None, None, 2, B_DV), lambda h, b: (j, h, 0, 0)),
        pl.BlockSpec((None, None, 1, B_DV), lambda h, b: (j, h, 0, 0)),
    ]
    args = [z, z, z, z, dec_rep, gn4]
    aliases = {}
    if carry_in:
        in_specs += [pl.BlockSpec((None, None, 2, None, B_DK, B_DV), lambda h, b: (b, j, 0, h, 0, 0))]
        args += [state]
    out_specs = [pl.BlockSpec((t, B_DV), lambda h, b: (b, h))]
    out_shape = [jax.ShapeDtypeStruct((n_tok, B_HEADS * B_DV), BF16)]
    fresh_slot = j if carry_out and prev is None else None
    if carry_out:
        lay, jb = (n_layers, 0) if prev is None else (None, j)
        out_specs += [pl.BlockSpec((nb, lay, 2, None, B_DK, B_DV), lambda h, b: (b, jb, 0, h, 0, 0))]
        out_shape += [jax.ShapeDtypeStruct((batch, n_layers, 2, B_HEADS, B_DK, B_DV), F32)]
        if prev is not None:
            aliases = {len(args): 1}
            in_specs += [pl.BlockSpec(memory_space=pl.ANY)]
            args += [prev]
    kern = functools.partial(_ret_kernel, n_chunks=n_chunks, nb=nb, carry_in=carry_in,
                             carry_out=carry_out, n_alias=len(aliases), fresh_slot=fresh_slot)
    return pl.pallas_call(
        kern,
        grid=(B_HEADS, batch // nb),
        in_specs=in_specs,
        out_specs=out_specs,
        out_shape=out_shape,
        input_output_aliases=aliases,
        scratch_shapes=[
            pltpu.VMEM((2 * nb, B_DK, B_DV), F32),
            pltpu.VMEM((nb, n_chunks, 2 * B_DK, B_DV), BF16),
            pltpu.VMEM((CHUNK, CHUNK), F32),
            pltpu.VMEM((2, CHUNK, B_DK), F32),
            pltpu.VMEM((2, CHUNK, B_DK), F32),
        ],
        compiler_params=_params("parallel", "arbitrary"),
        name="ret_scan",
    )(*args)


def _outproj_kernel(h_ref, w_ref, x_ref, g_ref, gate_ref, o_ref):
    w = w_ref[...].astype(BF16)
    for r in range(o_ref.shape[0] // TR_OUT):
        rows = slice(r * TR_OUT, (r + 1) * TR_OUT)
        y = _dot(h_ref[rows, :], w)
        o_ref[rows, :] = x_ref[rows, :] + gate_ref[...] * (_rms(y) * g_ref[...])


def _outproj(h, w, j, x, ng4, mod5, layer, *, seq_len, sample):
    n_tok, hv = h.shape
    tm = TM_OUT
    row = _mod_row(sample, seq_len, tm)
    return pl.pallas_call(
        _outproj_kernel,
        grid=(n_tok // tm,),
        in_specs=[
            pl.BlockSpec((tm, hv), lambda i: (i, 0)),
            _resident((None, hv, D_MODEL), (j, 0, 0)),
            pl.BlockSpec((tm, D_MODEL), lambda i: (i, 0)),
            pl.BlockSpec((None, None, 1, D_MODEL), lambda i: (layer, 1, 0, 0)),
            pl.BlockSpec((None, None, None, 1, D_MODEL), lambda i: (layer, row(i), 2, 0, 0)),
        ],
        out_specs=pl.BlockSpec((tm, D_MODEL), lambda i: (i, 0)),
        out_shape=jax.ShapeDtypeStruct((n_tok, D_MODEL), F32),
        compiler_params=_params("parallel"),
        name="outproj",
    )(h, w, x, ng4, mod5)


def _conv3(hs_ref, half, h, cw, cb, seg, n_seg):
    for s in range(n_seg):
        base = SUBLANES + s * (seg + SUBLANES)
        h_seg = h[s * seg:(s + 1) * seg, :]
        hs_ref[2 * half, base + 1:base + 1 + seg, :] = h_seg
        hs_ref[2 * half + 1, base - 1:base - 1 + seg, :] = h_seg
    parts = []
    for s in range(n_seg):
        base = SUBLANES + s * (seg + SUBLANES)
        h_prev = hs_ref[2 * half, base:base + seg, :]
        h_next = hs_ref[2 * half + 1, base:base + seg, :]
        h_mid = h[s * seg:(s + 1) * seg, :]
        parts.append(h_prev * cw[0:1, :] + h_mid * cw[1:2, :] + h_next * cw[2:3, :] + cb)
    return parts


def _ffn_kernel(x_ref, g2_ref, sh_ref, sc_ref, wup_ref, cw_ref, cb_ref, wd_ref, g3_ref, gate_ref,
                o_ref, u_sc, act_sc, hs_sc, *, seg, n_seg, tf):
    u = _rms(x_ref[...]) * g2_ref[...] * (1.0 + sc_ref[...]) + sh_ref[...]
    u_sc[...] = u.astype(BF16)
    zero_rows = jnp.zeros((SUBLANES, tf), F32)
    for s in range(n_seg):
        base = SUBLANES + s * (seg + SUBLANES)
        for half in range(2):
            hs_sc[2 * half, base:base + SUBLANES, :] = zero_rows
            hs_sc[2 * half + 1, base + seg - SUBLANES:base + seg, :] = zero_rows

    for cidx in range(D_FF // tf):
        cg = slice(cidx * tf, (cidx + 1) * tf)
        cu = slice(D_FF + cidx * tf, D_FF + (cidx + 1) * tf)
        hg = _conv3(hs_sc, 0, _dot(u_sc[...], wup_ref[:, cg]), cw_ref[:, cg], cb_ref[:, cg], seg, n_seg)
        hu = _conv3(hs_sc, 1, _dot(u_sc[...], wup_ref[:, cu]), cw_ref[:, cu], cb_ref[:, cu], seg, n_seg)
        for s in range(n_seg):
            act = jax.nn.gelu(hg[s], approximate=True) * hu[s]
            act_sc[s * seg:(s + 1) * seg, cg] = act.astype(BF16)

    tr = o_ref.shape[0] // FFN_DOWN_BLOCKS
    for r in range(FFN_DOWN_BLOCKS):
        rows = slice(r * tr, (r + 1) * tr)
        f = _dot(act_sc[rows, :], wd_ref[...])
        o_ref[rows, :] = x_ref[rows, :] + gate_ref[...] * (_rms(f) * g3_ref[...])


def _ffn(x, ng4, mod5, layer, w_up, conv_w, conv_b, w_down, *, seq_len, sample):
    n_tok = x.shape[0]
    tm, tf = TM_FFN, TF_FFN
    row = _mod_row(sample, seq_len, tm)
    seg = GRID_W if sample else seq_len
    n_seg = tm // seg
    kern = functools.partial(_ffn_kernel, seg=seg, n_seg=n_seg, tf=tf)
    mod_spec = lambda k: pl.BlockSpec((None, None, None, 1, D_MODEL), lambda i: (layer, row(i), k, 0, 0))
    gain_spec = lambda k: pl.BlockSpec((None, None, 1, D_MODEL), lambda i: (layer, k, 0, 0))
    return pl.pallas_call(
        kern,
        grid=(n_tok // tm,),
        in_specs=[
            pl.BlockSpec((tm, D_MODEL), lambda i: (i, 0)),
            gain_spec(2),
            mod_spec(3),
            mod_spec(4),
            _resident((D_MODEL, 2 * D_FF), (0, 0)),
            _resident((None, 3, 2 * D_FF), (layer, 0, 0)),
            _resident((None, 1, 2 * D_FF), (layer, 0, 0)),
            _resident((D_FF, D_MODEL), (0, 0)),
            gain_spec(3),
            mod_spec(5),
        ],
        out_specs=pl.BlockSpec((tm, D_MODEL), lambda i: (i, 0)),
        out_shape=jax.ShapeDtypeStruct((n_tok, D_MODEL), F32),
        scratch_shapes=[
            pltpu.VMEM((tm, D_MODEL), BF16),
            pltpu.VMEM((tm, D_FF), BF16),
            pltpu.VMEM((4, SUBLANES + n_seg * (seg + SUBLANES), tf), F32),
        ],
        compiler_params=_params("parallel"),
        name="convffn",
    )(x, ng4, mod5, mod5, w_up, conv_w, conv_b, w_down, ng4, mod5)


def _rope_tables(seq_len):
    quarter = B_DK // 4
    inv = ROPE_BASE ** (-jnp.arange(quarter, dtype=F32) / quarter)
    t = jnp.arange(seq_len)
    rows = (t // GRID_W).astype(F32)[:, None] * inv
    cols = (t % GRID_W).astype(F32)[:, None] * inv
    cos = jnp.concatenate([jnp.cos(rows), jnp.cos(cols)] * 2, axis=-1)
    sin = jnp.concatenate([-jnp.sin(rows), -jnp.sin(cols), jnp.sin(rows), jnp.sin(cols)], axis=-1)
    return cos, sin


def _rope_qk_weights(w_in):
    n_l = w_in.shape[0]
    quarter = B_DK // 4
    w_qk = w_in[:, :, :B_QK].astype(BF16).reshape(n_l, D_MODEL, 2 * B_HEADS, 2, 2, quarter)
    return jnp.swapaxes(w_qk, 3, 4).reshape(n_l, D_MODEL, B_QK)


def _gate_weights(w_in, b_gate):
    n_l = w_in.shape[0]
    order = jnp.array((0, 2, 1, 3))
    wg = w_in[:, :, A_MAIN:].reshape(n_l, D_MODEL, 4, A_HEADS)[:, :, order, :]
    wg_t = jnp.transpose(wg, (0, 2, 3, 1)).reshape(n_l, 2 * NG, D_MODEL).astype(BF16)
    bg = b_gate[:, order, :].reshape(n_l, 2 * NG, 1)
    return wg_t, bg


def kernel(x_prompt, x_sample, state_mlstm_C, state_mlstm_n, state_mlstm_m, state_ret_S, c, c_ctx,
           norm_gain, ada_w, ada_b, ml_w_in, ml_b_gate, ml_norm, ml_w_out,
           ret_w_in, ret_decay, ret_norm, ret_w_out, ffn_w_up, ffn_conv, ffn_conv_b, ffn_w_down):
    bp, tp, _ = x_prompt.shape
    bs, ts, _ = x_sample.shape
    n_a = ml_w_in.shape[0]
    n_b = ret_w_in.shape[0]

    cond = jnp.concatenate([c_ctx[None, :], c, jnp.zeros((MOD_ROWS - 1 - bs, D_MODEL), F32)], axis=0)
    mod5 = _modulation(cond, ada_w, ada_b).reshape(DEPTH, MOD_ROWS, 6, 1, D_MODEL)
    ng4 = norm_gain.reshape(DEPTH, 4, 1, D_MODEL)
    rope = _rope_tables(ts)

    ml_gates = _gate_weights(ml_w_in, ml_b_gate)
    ret_w_qk_rope = _rope_qk_weights(ret_w_in)
    ffn_conv_b3 = ffn_conv_b.reshape(DEPTH, 1, 2 * D_FF)
    ml_gn4 = ml_norm.reshape(n_a, A_HEADS, 1, A_DV)
    ret_gn4 = ret_norm.reshape(n_b, B_HEADS, 1, B_DV)
    dec_rep = jnp.broadcast_to(jnp.swapaxes(ret_decay, 1, 2)[..., None], (n_b, B_HEADS, 2, B_DV))
    st_c = state_mlstm_C
    st_n = state_mlstm_n.reshape(bs, n_a, 2, A_HEADS, A_DK, 1)
    st_m = state_mlstm_m.reshape(bs, n_a, 2, A_HEADS, 1, 1)

    groups = [
        dict(x=x_prompt.reshape(bp * tp, D_MODEL), batch=bp, seq_len=tp, sample=False),
        dict(x=x_sample.reshape(bs * ts, D_MODEL), batch=bs, seq_len=ts, sample=True),
    ]
    ml_states = None
    ret_states = None
    for i in range(DEPTH):
        j = i // N_MIXERS
        for grp in groups:
            x = grp["x"]
            geo = dict(seq_len=grp["seq_len"], sample=grp["sample"])
            bt = dict(batch=grp["batch"], seq_len=grp["seq_len"])
            cast = () if grp["sample"] else ((ffn_w_up, i), (ffn_w_down, i))
            if i % N_MIXERS == 0:
                n_qk = A_HEADS * A_DK // TN_PROJ
                z, gates, *ffn_w = _inproj(x, ng4, mod5, i, [(ml_w_in, j, 0)], A_MAIN, n_q=n_qk, n_k=n_qk,
                                           k_scale=A_DK ** -0.5, gates=ml_gates + (j,), cast=cast, **geo)
                if grp["sample"]:
                    (h,) = _mlstm_scan(z, gates, ml_gn4, j, n_layers=n_a, state=(st_c, st_n, st_m), **bt)
                else:
                    h, *ml_states = _mlstm_scan(z, gates, ml_gn4, j, n_layers=n_a, prev=ml_states, **bt)
                x = _outproj(h, ml_w_out, j, x, ng4, mod5, i, **geo)
            else:
                n_qk = B_HEADS * B_DK // TN_PROJ
                w_qk = ret_w_qk_rope if grp["sample"] else ret_w_in
                w_parts = [(w_qk, j, 0), (ret_w_in, j, 1), (ret_w_in, j, 2)]
                z, *ffn_w = _inproj(x, ng4, mod5, i, w_parts, B_QK, n_q=n_qk, n_k=n_qk, k_scale=B_DK ** -0.5,
                                    rope=rope if grp["sample"] else None, cast=cast, **geo)
                if grp["sample"]:
                    (h,) = _ret_scan(z, dec_rep, ret_gn4, j, n_layers=n_b, state=state_ret_S, **bt)
                else:
                    h, ret_states = _ret_scan(z, dec_rep, ret_gn4, j, n_layers=n_b, prev=ret_states, **bt)
                x = _outproj(h, ret_w_out, j, x, ng4, mod5, i, **geo)
            if ffn_w:
                ffn_w_up_b, ffn_w_down_b = ffn_w
            grp["x"] = _ffn(x, ng4, mod5, i, ffn_w_up_b, ffn_conv, ffn_conv_b3, ffn_w_down_b, **geo)

    y_prompt = groups[0]["x"].reshape(bp, tp, D_MODEL)
    y_sample = groups[1]["x"].reshape(bs, ts, D_MODEL)
    new_c, new_n, new_m = ml_states
    return (y_prompt, y_sample, new_c, new_n.reshape(bp, n_a, 2, A_HEADS, A_DK),
            new_m.reshape(bp, n_a, 2, A_HEADS), ret_states)
```

```python
import functools
import math

import jax
import jax.numpy as jnp
from jax import lax
from jax.experimental import pallas as pl
from jax.experimental.pallas import tpu as pltpu

D_MODEL = 1024
DEPTH = 4
GRID_W = 64
CHUNK = 128
N_MIXERS = 2
A_HEADS = 4
A_DV = D_MODEL // A_HEADS
A_DK = A_DV // 2
A_MAIN = 2 * A_HEADS * A_DK + 2 * A_HEADS * A_DV
B_HEADS = 8
B_DK = D_MODEL // B_HEADS
B_DV = 2 * D_MODEL // B_HEADS
B_QK = 2 * B_HEADS * B_DK
ROPE_BASE = 10000.0
D_FF = ((8 * D_MODEL // 3 + 127) // 128) * 128
EPS = 1e-6
LN2 = math.log(2.0)

F32 = jnp.float32
BF16 = jnp.bfloat16

LANES = 128
SUBLANES = 8
MOD_ROWS = 16
VMEM_LIMIT = 48 * 1024 * 1024

TM_PROJ = 512
TN_PROJ = 512
TM_OUT = 1024
TR_OUT = 512
TM_FFN = 512
TF_FFN = 256
FFN_DOWN_BLOCKS = 2
GATE_UNROLL = 16
STATE_UNROLL = 16
OUT_UNROLL = 16
MLSTM_OUT_UNROLL = 16
SCAN_TOKENS = 2048


def _dot(a, b):
    return jnp.dot(a, b, preferred_element_type=F32)


def _dot_nt(a, b):
    return lax.dot_general(a, b, (((1,), (1,)), ((), ())), preferred_element_type=F32)


def _dot_tn(a, b):
    return lax.dot_general(a, b, (((0,), (0,)), ((), ())), preferred_element_type=F32)


def _rms(x):
    return x * lax.rsqrt(jnp.mean(x * x, axis=-1, keepdims=True) + EPS)


def _layer_norm(h):
    d = h - jnp.mean(h, axis=-1, keepdims=True)
    return d * lax.rsqrt(jnp.mean(d * d, axis=-1, keepdims=True) + EPS)


def _params(*sem):
    return pltpu.CompilerParams(dimension_semantics=sem, vmem_limit_bytes=VMEM_LIMIT)


def _resident(shape, index):
    return pl.BlockSpec(shape, lambda *_: index, pipeline_mode=pl.Buffered(1))


def _mod_kernel(cond_ref, w_ref, b_ref, o_ref):
    cnd = cond_ref[...]
    s = cnd * jax.nn.sigmoid(cnd)
    o_ref[...] = _dot(s.astype(BF16), w_ref[...].astype(BF16)) + b_ref[...]


def _modulation(cond, ada_w, ada_b):
    tn = 1024
    n_out = ada_w.shape[-1]
    return pl.pallas_call(
        _mod_kernel,
        grid=(DEPTH, n_out // tn),
        in_specs=[
            pl.BlockSpec((MOD_ROWS, D_MODEL), lambda l, j: (0, 0)),
            pl.BlockSpec((None, D_MODEL, tn), lambda l, j: (l, 0, j)),
            pl.BlockSpec((None, 1, tn), lambda l, j: (l, 0, j)),
        ],
        out_specs=pl.BlockSpec((None, MOD_ROWS, tn), lambda l, j: (l, 0, j)),
        out_shape=jax.ShapeDtypeStruct((DEPTH, MOD_ROWS, n_out), F32),
        compiler_params=_params("parallel", "parallel"),
        name="modulation",
    )(cond, ada_w, ada_b.reshape(DEPTH, 1, n_out))


def _mod_row(sample, seq_len, tm):
    if not sample:
        return lambda i: 0
    tiles_per_seq = seq_len // tm
    return lambda i: 1 + i // tiles_per_seq


def _rope_slab(x, cos, sin):
    return x * cos + pltpu.roll(x, 64, axis=1) * sin


def _inproj_kernel(*refs, n_w, n_q, n_k, k_scale, rope, gates, n_cast, tn):
    x_ref, g_ref, sh_ref, sc_ref = refs[:4]
    w_refs = refs[4:4 + n_w]
    pos = 4 + n_w
    if gates:
        wg_ref, bg_ref = refs[pos:pos + 2]
        pos += 2
    if rope:
        cos_ref, sin_ref = refs[pos:pos + 2]
        pos += 2
    cast_in = refs[pos:pos + n_cast]
    pos += n_cast
    z_ref = refs[pos]
    pos += 1
    if gates:
        gates_ref = refs[pos]
        pos += 1
    cast_out = refs[pos:pos + n_cast]
    pos += n_cast
    u_sc = refs[pos]

    for src, dst in zip(cast_in, cast_out):
        dst[...] = src[...].astype(BF16)

    u = _rms(x_ref[...]) * g_ref[...] * (1.0 + sc_ref[...]) + sh_ref[...]
    u_sc[...] = u.astype(BF16)
    if gates:
        gates_ref[...] = _dot_nt(wg_ref[...], u_sc[...]) + bg_ref[...]

    wb = w_refs[0].shape[1]
    for j in range(z_ref.shape[1] // tn):
        part, off = divmod(j * tn, wb)
        z = _dot(u_sc[...], w_refs[part][:, off:off + tn].astype(BF16))
        scale = k_scale if n_q <= j < n_q + n_k else 1.0
        if rope and j < n_q + n_k:
            for s in range(tn // LANES):
                r = _rope_slab(z[:, s * LANES:(s + 1) * LANES], cos_ref[...], sin_ref[...])
                if scale != 1.0:
                    r = r * scale
                z_ref[:, j * tn + s * LANES:j * tn + (s + 1) * LANES] = r.astype(BF16)
        elif scale != 1.0:
            z_ref[:, j * tn:(j + 1) * tn] = (z * scale).astype(BF16)
        else:
            z_ref[:, j * tn:(j + 1) * tn] = z.astype(BF16)


def _inproj(x, ng4, mod5, layer, w_parts, wb, *, seq_len, sample, n_q, n_k, k_scale, rope=None, gates=None,
            cast=()):
    n_tok = x.shape[0]
    tm, tn = TM_PROJ, TN_PROJ
    n_col = wb * len(w_parts)
    row = _mod_row(sample, seq_len, tm)
    in_specs = [
        pl.BlockSpec((tm, D_MODEL), lambda i: (i, 0)),
        pl.BlockSpec((None, None, 1, D_MODEL), lambda i: (layer, 0, 0, 0)),
        pl.BlockSpec((None, None, None, 1, D_MODEL), lambda i: (layer, row(i), 0, 0, 0)),
        pl.BlockSpec((None, None, None, 1, D_MODEL), lambda i: (layer, row(i), 1, 0, 0)),
    ]
    in_specs += [_resident((None, D_MODEL, wb), (jl, 0, blk)) for _, jl, blk in w_parts]
    args = [x, ng4, mod5, mod5] + [w for w, _, _ in w_parts]
    out_specs = [pl.BlockSpec((tm, n_col), lambda i: (i, 0))]
    out_shape = [jax.ShapeDtypeStruct((n_tok, n_col), BF16)]
    if gates is not None:
        wg_t, bg, jg = gates
        n_g = wg_t.shape[1]
        in_specs += [_resident((None, n_g, D_MODEL), (jg, 0, 0)), _resident((None, n_g, 1), (jg, 0, 0))]
        args += [wg_t, bg]
        out_specs += [pl.BlockSpec((n_g, tm), lambda i: (0, i))]
        out_shape += [jax.ShapeDtypeStruct((n_g, n_tok), F32)]
    if rope is not None:
        tiles_per_seq = seq_len // tm
        in_specs += [pl.BlockSpec((tm, LANES), lambda i: (i % tiles_per_seq, 0))] * 2
        args += list(rope)
    n_steps = n_tok // tm
    for w_src, jl in cast:
        _, rows, cols = w_src.shape
        slab = rows // n_steps
        in_specs += [pl.BlockSpec((None, slab, cols), lambda i, jl=jl: (jl, i, 0))]
        args += [w_src]
        out_specs += [pl.BlockSpec((slab, cols), lambda i: (i, 0))]
        out_shape += [jax.ShapeDtypeStruct((rows, cols), BF16)]
    kern = functools.partial(_inproj_kernel, n_w=len(w_parts), n_q=n_q, n_k=n_k, k_scale=k_scale,
                             rope=rope is not None, gates=gates is not None, n_cast=len(cast), tn=tn)
    return pl.pallas_call(
        kern,
        grid=(n_tok // tm,),
        in_specs=in_specs,
        out_specs=out_specs,
        out_shape=out_shape,
        scratch_shapes=[pltpu.VMEM((tm, D_MODEL), BF16)],
        compiler_params=_params("parallel"),
        name="inproj",
    )(*args)


def _tri_masks():
    li = lax.broadcasted_iota(jnp.int32, (CHUNK, CHUNK), 0)
    si = lax.broadcasted_iota(jnp.int32, (CHUNK, CHUNK), 1)
    return si <= li, si >= li


def _layer_slot(ref, fresh_slot):
    if fresh_slot is None:
        return ref
    for other in range(ref.shape[1]):
        if other != fresh_slot:
            ref[:, other] = jnp.zeros(ref.shape[:1] + ref.shape[2:], ref.dtype)
    return ref.at[:, fresh_slot]


def _seqs_per_step(batch, seq_len, carry_in):
    if carry_in:
        return 1
    nb = max(1, SCAN_TOKENS // seq_len)
    while batch % nb:
        nb -= 1
    return nb


NG = 2 * A_HEADS


def _split_dot(x, mask_b):
    hi = x.astype(BF16)
    r1 = x - hi.astype(F32)
    mid = r1.astype(BF16)
    lo = (r1 - mid.astype(F32)).astype(BF16)
    return _dot(hi, mask_b) + _dot(mid, mask_b) + _dot(lo, mask_b)


def _mlstm_kernel(*refs, n_chunks, nb, carry_in, carry_out, n_alias, fresh_slot):
    q_ref, k_ref, v_ref, o_ref, g_ref, gn_ref = refs[:6]
    pos = 6
    if carry_in:
        c0_ref, n0_ref, m0_ref = refs[pos:pos + 3]
        pos += 3
    pos += n_alias
    h_ref = refs[pos]
    pos += 1
    if carry_out:
        cout_ref, nout_ref, mout_ref = refs[pos:pos + 3]
        pos += 3
    ab_sc, bt_sc, g_sc, bm_sc, mpf_sc, mpb_sc, c_sc, call_sc = refs[pos:pos + 8]

    head = pl.program_id(1)
    masks = _tri_masks()
    ones_b = jnp.ones((CHUNK, LANES), BF16)
    sum_b = jnp.concatenate([masks[1].astype(BF16), ones_b], axis=1)
    lane = lax.broadcasted_iota(jnp.int32, (CHUNK, LANES), 1)
    grow = lax.broadcasted_iota(jnp.int32, (2 * NG, CHUNK), 0)
    gsub = lax.broadcasted_iota(jnp.int32, (NG, LANES), 0)
    zpad = jnp.zeros((LANES - 2 * NG, CHUNK), F32)
    cols = (head, head + A_HEADS)
    mp_sc = (mpf_sc, mpb_sc)
    t_seq = n_chunks * CHUNK

    def pick_col(x, col):
        return jnp.sum(jnp.where(lane == col, x, 0.0), axis=1, keepdims=True)

    def at(s, c):
        return pl.ds(pl.multiple_of(s * t_seq + c * CHUNK, CHUNK), CHUNK)

    def gate_body(c, carry):
        for s in range(nb):
            idx = s * n_chunks + c
            gates = g_ref[:, at(s, c)]
            lf = jnp.minimum(gates, 0.0) - jnp.log1p(jnp.exp(-jnp.abs(gates)))
            lf = jnp.where(grow >= NG, lf, 0.0)
            sums = _split_dot(lf, sum_b)
            a_f = sums[:, :CHUNK]
            tot = sums[:, CHUNK:]
            a_all = jnp.where(grow < NG + A_HEADS, a_f, tot - a_f + lf)[NG:, :]
            b_all = gates[:NG, :] - a_all
            bt_sc[idx] = b_all
            ab_sc[at(s, c), :] = jnp.concatenate([a_all, b_all, zpad], axis=0).T
            g_sc[idx] = tot[NG:, :]
            bm_sc[idx] = jnp.broadcast_to(jnp.max(b_all, axis=1, keepdims=True), (NG, LANES))
        return carry

    lax.fori_loop(0, n_chunks, gate_body, 0, unroll=min(n_chunks, GATE_UNROLL))

    if carry_in:
        m_init = (jnp.broadcast_to(m0_ref[0], (NG, LANES)), jnp.broadcast_to(m0_ref[1], (NG, LANES)))
    else:
        m_init = (jnp.zeros((NG, LANES), F32),) * (2 * nb)

    def m_body(i, carry):
        out = []
        for s in range(nb):
            m_f, m_b = carry[2 * s], carry[2 * s + 1]
            jf = s * n_chunks + i
            jb = s * n_chunks + n_chunks - 1 - i
            mpf_sc[jf] = m_f
            mpb_sc[jb] = m_b
            out.append(g_sc[jf] + jnp.maximum(m_f, bm_sc[jf]))
            out.append(g_sc[jb] + jnp.maximum(m_b, bm_sc[jb]))
        return tuple(out)

    m_last = lax.fori_loop(0, n_chunks, m_body, m_init, unroll=min(n_chunks, STATE_UNROLL))

    if carry_in:
        for dirn in range(2):
            c_sc[dirn, :, :A_DV] = c0_ref[dirn]
            c_sc[dirn, :, A_DV:] = jnp.broadcast_to(n0_ref[dirn], (A_DK, LANES))
    else:
        c_sc[...] = jnp.zeros_like(c_sc)

    def state_body(i, carry):
        for s in range(nb):
            for dirn in range(2):
                c = i if dirn == 0 else n_chunks - 1 - i
                idx = s * n_chunks + c
                mp_row = mp_sc[dirn][idx, pl.ds(cols[dirn], 1), :]
                m_top = jnp.maximum(mp_row, bm_sc[idx, pl.ds(cols[dirn], 1), :])
                ws = jnp.exp(pick_col(ab_sc[at(s, c), :], NG + cols[dirn]) - m_top)
                dec = jnp.exp(mp_row - m_top)
                dec = jnp.concatenate([dec] * (c_sc.shape[2] // LANES), axis=1)
                c_old = c_sc[2 * s + dirn]
                call_sc[2 * s + dirn, c] = c_old.astype(BF16)
                kw = (k_ref[at(s, c), :].astype(F32) * ws).astype(BF16)
                upd = jnp.concatenate([_dot_tn(kw, v_ref[at(s, c), :]), _dot_tn(kw, ones_b)], axis=1)
                c_sc[2 * s + dirn] = dec * c_old + upd
        return carry

    lax.fori_loop(0, n_chunks, state_body, 0, unroll=min(n_chunks, STATE_UNROLL))
    if carry_out:
        outs = [_layer_slot(r, fresh_slot) for r in (cout_ref, nout_ref, mout_ref)]
        for s in range(nb):
            for dirn in range(2):
                outs[0][s, dirn] = c_sc[2 * s + dirn, :, :A_DV]
                outs[1][s, dirn] = c_sc[2 * s + dirn, :, A_DV:].T[0:1, :]
                m_end = jnp.where(gsub == cols[dirn], m_last[2 * s + dirn], 0.0)
                outs[2][s, dirn] = jnp.sum(m_end, axis=0, keepdims=True)[:, 0:1]

    def out_body(c, carry):
        for s in range(nb):
            q = q_ref[at(s, c), :]
            qf = q.astype(F32)
            v_ext = jnp.concatenate([v_ref[at(s, c), :], ones_b], axis=1)
            s_raw = _dot_nt(q, k_ref[at(s, c), :])
            a_chunk = ab_sc[at(s, c), :]
            idx = s * n_chunks + c
            h = None
            for dirn in range(2):
                col = cols[dirn]
                m_prev = mp_sc[dirn][idx, pl.ds(col, 1), :]
                b_vis = jnp.where(masks[dirn], bt_sc[idx, pl.ds(col, 1), :], -jnp.inf)
                m_row = jnp.maximum(m_prev, jnp.max(b_vis, axis=1, keepdims=True))
                sw = (s_raw * jnp.exp(b_vis - m_row)).astype(BF16)
                w_inter = jnp.exp(m_prev - m_row)
                floor = jnp.exp(-(pick_col(a_chunk, col) + m_row))
                qw = (qf * w_inter).astype(BF16)
                nd = _dot(sw, v_ext) + _dot(qw, call_sc[2 * s + dirn, c])
                r = 1.0 / jnp.maximum(jnp.abs(nd[:, A_DV:]), floor)
                hd = nd[:, :A_DV] * jnp.concatenate([r, r], axis=1)
                h = hd if h is None else h + hd
            o = o_ref[at(s, c), :].astype(F32)
            h_ref[at(s, c), :] = (_layer_norm(h) * gn_ref[...] * jax.nn.sigmoid(o)).astype(BF16)
        return carry

    lax.fori_loop(0, n_chunks, out_body, 0, unroll=min(n_chunks, max(1, MLSTM_OUT_UNROLL // nb)))


def _mlstm_scan(z, gates, gn4, j, *, batch, seq_len, n_layers, state=None, prev=None):
    n_tok = z.shape[0]
    n_chunks = seq_len // CHUNK
    carry_in = state is not None
    carry_out = not carry_in
    nb = _seqs_per_step(batch, seq_len, carry_in)
    t = nb * seq_len
    in_specs = [
        pl.BlockSpec((t, A_DK), lambda b, h: (b, h)),
        pl.BlockSpec((t, A_DK), lambda b, h: (b, A_HEADS + h)),
        pl.BlockSpec((t, A_DV), lambda b, h: (b, A_HEADS + h)),
        pl.BlockSpec((t, A_DV), lambda b, h: (b, 2 * A_HEADS + h)),
        pl.BlockSpec((2 * NG, t), lambda b, h: (0, b)),
        pl.BlockSpec((None, None, 1, A_DV), lambda b, h: (j, h, 0, 0)),
    ]
    args = [z, z, z, z, gates, gn4]
    aliases = {}
    if carry_in:
        in_specs += [
            pl.BlockSpec((None, None, 2, None, A_DK, A_DV), lambda b, h: (b, j, 0, h, 0, 0)),
            pl.BlockSpec((None, None, 2, None, A_DK, 1), lambda b, h: (b, j, 0, h, 0, 0)),
            pl.BlockSpec((None, None, 2, None, 1, 1), lambda b, h: (b, j, 0, h, 0, 0)),
        ]
        args += list(state)
    out_specs = [pl.BlockSpec((t, A_DV), lambda b, h: (b, h))]
    out_shape = [jax.ShapeDtypeStruct((n_tok, A_HEADS * A_DV), BF16)]
    fresh_slot = j if carry_out and prev is None else None
    if carry_out:
        lay, jb = (n_layers, 0) if prev is None else (None, j)
        out_specs += [
            pl.BlockSpec((nb, lay, 2, None, A_DK, A_DV), lambda b, h: (b, jb, 0, h, 0, 0)),
            pl.BlockSpec((nb, lay, 2, None, 1, A_DK), lambda b, h: (b, jb, 0, h, 0, 0)),
            pl.BlockSpec((nb, lay, 2, None, 1, 1), lambda b, h: (b, jb, 0, h, 0, 0)),
        ]
        out_shape += [
            jax.ShapeDtypeStruct((batch, n_layers, 2, A_HEADS, A_DK, A_DV), F32),
            jax.ShapeDtypeStruct((batch, n_layers, 2, A_HEADS, 1, A_DK), F32),
            jax.ShapeDtypeStruct((batch, n_layers, 2, A_HEADS, 1, 1), F32),
        ]
        if prev is not None:
            aliases = {len(args) + k: 1 + k for k in range(3)}
            in_specs += [pl.BlockSpec(memory_space=pl.ANY)] * 3
            args += list(prev)
    kern = functools.partial(_mlstm_kernel, n_chunks=n_chunks, nb=nb, carry_in=carry_in,
                             carry_out=carry_out, n_alias=len(aliases), fresh_slot=fresh_slot)
    return pl.pallas_call(
        kern,
        grid=(batch // nb, A_HEADS),
        in_specs=in_specs,
        out_specs=out_specs,
        out_shape=out_shape,
        input_output_aliases=aliases,
        scratch_shapes=[
            pltpu.VMEM((t, LANES), F32),
            pltpu.VMEM((nb * n_chunks, NG, CHUNK), F32),
            pltpu.VMEM((nb * n_chunks, NG, LANES), F32),
            pltpu.VMEM((nb * n_chunks, NG, LANES), F32),
            pltpu.VMEM((nb * n_chunks, NG, LANES), F32),
            pltpu.VMEM((nb * n_chunks, NG, LANES), F32),
            pltpu.VMEM((2 * nb, A_DK, A_DV + LANES), F32),
            pltpu.VMEM((2 * nb, n_chunks, A_DK, A_DV + LANES), BF16),
        ],
        compiler_params=_params("parallel", "parallel"),
        name="mlstm_scan",
    )(*args)


def _ret_kernel(*refs, n_chunks, nb, carry_in, carry_out, n_alias, fresh_slot):
    q_ref, k_ref, v_ref, gate_ref, dec_ref, gn_ref = refs[:6]
    pos = 6
    if carry_in:
        s0_ref = refs[pos]
        pos += 1
    pos += n_alias
    h_ref = refs[pos]
    pos += 1
    if carry_out:
        sout_ref = refs[pos]
        pos += 1
    s_sc, sall_sc, dsum_sc, xi_sc, zeta_sc = refs[pos:pos + 5]
    t_seq = n_chunks * CHUNK

    def at(s, c):
        return pl.ds(pl.multiple_of(s * t_seq + c * CHUNK, CHUNK), CHUNK)

    lg = jnp.log1p(-jnp.exp(-dec_ref[...] * LN2))
    lg_f = lg[0:1, :]
    lg_b = lg[1:2, :]

    @pl.when(pl.program_id(1) == 0)
    def _():
        masks = _tri_masks()
        li = lax.broadcasted_iota(jnp.int32, (CHUNK, B_DV), 0).astype(F32)
        si = lax.broadcasted_iota(jnp.int32, (CHUNK, CHUNK), 1).astype(F32)
        lq = li[:, :CHUNK]
        dsum_sc[...] = (
            jnp.where(masks[0], jnp.exp(jnp.where(masks[0], lq - si, 0.0) * lg_f[:, :CHUNK]), 0.0)
            + jnp.where(masks[1], jnp.exp(jnp.where(masks[1], si - lq, 0.0) * lg_b[:, :CHUNK]), 0.0))
        xi_sc[0] = jnp.exp((lq + 1.0) * lg_f[:, :CHUNK])
        xi_sc[1] = jnp.exp((CHUNK - lq) * lg_b[:, :CHUNK])
        zeta_sc[0] = jnp.exp((CHUNK - 1.0 - lq) * lg_f[:, :CHUNK])
        zeta_sc[1] = jnp.exp(lq * lg_b[:, :CHUNK])

    cdec = (jnp.exp(CHUNK * lg_f), jnp.exp(CHUNK * lg_b))

    if carry_in:
        qr = B_DK // 4
        for dirn in range(2):
            for n, o in enumerate((0, 2, 1, 3)):
                s_sc[dirn, n * qr:(n + 1) * qr, :] = s0_ref[dirn, o * qr:(o + 1) * qr, :]
    else:
        s_sc[...] = jnp.zeros_like(s_sc)

    def state_body(i, carry):
        for s in range(nb):
            for dirn in range(2):
                c = i if dirn == 0 else n_chunks - 1 - i
                s_old = s_sc[2 * s + dirn]
                sall_sc[s, c, dirn * B_DK:(dirn + 1) * B_DK, :] = s_old.astype(BF16)
                kz = (k_ref[at(s, c), :].astype(F32) * zeta_sc[dirn]).astype(BF16)
                s_sc[2 * s + dirn] = cdec[dirn] * s_old + _dot_tn(kz, v_ref[at(s, c), :])
        return carry

    lax.fori_loop(0, n_chunks, state_body, 0, unroll=min(n_chunks, STATE_UNROLL))
    if carry_out:
        s_out = _layer_slot(sout_ref, fresh_slot)
        for s in range(nb):
            for dirn in range(2):
                s_out[s, dirn] = s_sc[2 * s + dirn]

    def out_body(c, carry):
        for s in range(nb):
            q = q_ref[at(s, c), :]
            v = v_ref[at(s, c), :]
            sw = _dot_nt(q, k_ref[at(s, c), :]) * dsum_sc[...]
            qf = q.astype(F32)
            qx = jnp.concatenate([qf * xi_sc[0], qf * xi_sc[1]], axis=1).astype(BF16)
            h = _dot(sw.astype(BF16), v) + _dot(qx, sall_sc[s, c])
            g = gate_ref[at(s, c), :].astype(F32)
            h_ref[at(s, c), :] = (_layer_norm(h) * gn_ref[...] * (g * jax.nn.sigmoid(g))).astype(BF16)
        return carry

    lax.fori_loop(0, n_chunks, out_body, 0, unroll=min(n_chunks, max(1, OUT_UNROLL // nb)))


def _ret_scan(z, dec_rep, gn4, j, *, batch, seq_len, n_layers, state=None, prev=None):
    n_tok = z.shape[0]
    n_chunks = seq_len // CHUNK
    carry_in = state is not None
    carry_out = not carry_in
    nb = _seqs_per_step(batch, seq_len, carry_in)
    t = nb * seq_len
    in_specs = [
        pl.BlockSpec((t, B_DK), lambda h, b: (b, h)),
        pl.BlockSpec((t, B_DK), lambda h, b: (b, B_HEADS + h)),
        pl.BlockSpec((t, B_DV), lambda h, b: (b, B_HEADS + h)),
        pl.BlockSpec((t, B_DV), lambda h, b: (b, 2 * B_HEADS + h)),
        pl.BlockSpec((None, None, 2, B_DV), lambda h, b: (j, h, 0, 0)),
        pl.BlockSpec((None, None, 1, B_DV), lambda h, b: (j, h, 0, 0)),
    ]
    args = [z, z, z, z, dec_rep, gn4]
    aliases = {}
    if carry_in:
        in_specs += [pl.BlockSpec((None, None, 2, None, B_DK, B_DV), lambda h, b: (b, j, 0, h, 0, 0))]
        args += [state]
    out_specs = [pl.BlockSpec((t, B_DV), lambda h, b: (b, h))]
    out_shape = [jax.ShapeDtypeStruct((n_tok, B_HEADS * B_DV), BF16)]
    fresh_slot = j if carry_out and prev is None else None
    if carry_out:
        lay, jb = (n_layers, 0) if prev is None else (None, j)
        out_specs += [pl.BlockSpec((nb, lay, 2, None, B_DK, B_DV), lambda h, b: (b, jb, 0, h, 0, 0))]
        out_shape += [jax.ShapeDtypeStruct((batch, n_layers, 2, B_HEADS, B_DK, B_DV), F32)]
        if prev is not None:
            aliases = {len(args): 1}
            in_specs += [pl.BlockSpec(memory_space=pl.ANY)]
            args += [prev]
    kern = functools.partial(_ret_kernel, n_chunks=n_chunks, nb=nb, carry_in=carry_in,
                             carry_out=carry_out, n_alias=len(aliases), fresh_slot=fresh_slot)
    return pl.pallas_call(
        kern,
        grid=(B_HEADS, batch // nb),
        in_specs=in_specs,
        out_specs=out_specs,
        out_shape=out_shape,
        input_output_aliases=aliases,
        scratch_shapes=[
            pltpu.VMEM((2 * nb, B_DK, B_DV), F32),
            pltpu.VMEM((nb, n_chunks, 2 * B_DK, B_DV), BF16),
            pltpu.VMEM((CHUNK, CHUNK), F32),
            pltpu.VMEM((2, CHUNK, B_DK), F32),
            pltpu.VMEM((2, CHUNK, B_DK), F32),
        ],
        compiler_params=_params("parallel", "arbitrary"),
        name="ret_scan",
    )(*args)


def _outproj_kernel(h_ref, w_ref, x_ref, g_ref, gate_ref, o_ref):
    w = w_ref[...].astype(BF16)
    for r in range(o_ref.shape[0] // TR_OUT):
        rows = slice(r * TR_OUT, (r + 1) * TR_OUT)
        y = _dot(h_ref[rows, :], w)
        o_ref[rows, :] = x_ref[rows, :] + gate_ref[...] * (_rms(y) * g_ref[...])


def _outproj(h, w, j, x, ng4, mod5, layer, *, seq_len, sample):
    n_tok, hv = h.shape
    tm = TM_OUT
    row = _mod_row(sample, seq_len, tm)
    return pl.pallas_call(
        _outproj_kernel,
        grid=(n_tok // tm,),
        in_specs=[
            pl.BlockSpec((tm, hv), lambda i: (i, 0)),
            _resident((None, hv, D_MODEL), (j, 0, 0)),
            pl.BlockSpec((tm, D_MODEL), lambda i: (i, 0)),
            pl.BlockSpec((None, None, 1, D_MODEL), lambda i: (layer, 1, 0, 0)),
            pl.BlockSpec((None, None, None, 1, D_MODEL), lambda i: (layer, row(i), 2, 0, 0)),
        ],
        out_specs=pl.BlockSpec((tm, D_MODEL), lambda i: (i, 0)),
        out_shape=jax.ShapeDtypeStruct((n_tok, D_MODEL), F32),
        compiler_params=_params("parallel"),
        name="outproj",
    )(h, w, x, ng4, mod5)


def _conv3(hs_ref, half, h, cw, cb, seg, n_seg):
    for s in range(n_seg):
        base = SUBLANES + s * (seg + SUBLANES)
        h_seg = h[s * seg:(s + 1) * seg, :]
        hs_ref[2 * half, base + 1:base + 1 + seg, :] = h_seg
        hs_ref[2 * half + 1, base - 1:base - 1 + seg, :] = h_seg
    parts = []
    for s in range(n_seg):
        base = SUBLANES + s * (seg + SUBLANES)
        h_prev = hs_ref[2 * half, base:base + seg, :]
        h_next = hs_ref[2 * half + 1, base:base + seg, :]
        h_mid = h[s * seg:(s + 1) * seg, :]
        parts.append(h_prev * cw[0:1, :] + h_mid * cw[1:2, :] + h_next * cw[2:3, :] + cb)
    return parts


def _ffn_kernel(x_ref, g2_ref, sh_ref, sc_ref, wup_ref, cw_ref, cb_ref, wd_ref, g3_ref, gate_ref,
                o_ref, u_sc, act_sc, hs_sc, *, seg, n_seg, tf):
    u = _rms(x_ref[...]) * g2_ref[...] * (1.0 + sc_ref[...]) + sh_ref[...]
    u_sc[...] = u.astype(BF16)
    zero_rows = jnp.zeros((SUBLANES, tf), F32)
    for s in range(n_seg):
        base = SUBLANES + s * (seg + SUBLANES)
        for half in range(hs_sc.shape[0] // 2):
            hs_sc[2 * half, base:base + SUBLANES, :] = zero_rows
            hs_sc[2 * half + 1, base + seg - SUBLANES:base + seg, :] = zero_rows

    for cidx in range(D_FF // tf):
        cg = slice(cidx * tf, (cidx + 1) * tf)
        cu = slice(D_FF + cidx * tf, D_FF + (cidx + 1) * tf)
        hb = 2 * (cidx % 2)
        hg = _conv3(hs_sc, hb, _dot(u_sc[...], wup_ref[:, cg]), cw_ref[:, cg], cb_ref[:, cg], seg, n_seg)
        hu = _conv3(hs_sc, hb + 1, _dot(u_sc[...], wup_ref[:, cu]), cw_ref[:, cu], cb_ref[:, cu], seg, n_seg)
        for s in range(n_seg):
            act = jax.nn.gelu(hg[s], approximate=True) * hu[s]
            act_sc[s * seg:(s + 1) * seg, cg] = act.astype(BF16)

    tr = o_ref.shape[0] // FFN_DOWN_BLOCKS
    for r in range(FFN_DOWN_BLOCKS):
        rows = slice(r * tr, (r + 1) * tr)
        f = _dot(act_sc[rows, :], wd_ref[...])
        o_ref[rows, :] = x_ref[rows, :] + gate_ref[...] * (_rms(f) * g3_ref[...])


def _ffn(x, ng4, mod5, layer, w_up, conv_w, conv_b, w_down, *, seq_len, sample):
    n_tok = x.shape[0]
    tm, tf = TM_FFN, TF_FFN
    row = _mod_row(sample, seq_len, tm)
    seg = GRID_W if sample else seq_len
    n_seg = tm // seg
    kern = functools.partial(_ffn_kernel, seg=seg, n_seg=n_seg, tf=tf)
    mod_spec = lambda k: pl.BlockSpec((None, None, None, 1, D_MODEL), lambda i: (layer, row(i), k, 0, 0))
    gain_spec = lambda k: pl.BlockSpec((None, None, 1, D_MODEL), lambda i: (layer, k, 0, 0))
    return pl.pallas_call(
        kern,
        grid=(n_tok // tm,),
        in_specs=[
            pl.BlockSpec((tm, D_MODEL), lambda i: (i, 0)),
            gain_spec(2),
            mod_spec(3),
            mod_spec(4),
            _resident((D_MODEL, 2 * D_FF), (0, 0)),
            _resident((None, 3, 2 * D_FF), (layer, 0, 0)),
            _resident((None, 1, 2 * D_FF), (layer, 0, 0)),
            _resident((D_FF, D_MODEL), (0, 0)),
            gain_spec(3),
            mod_spec(5),
        ],
        out_specs=pl.BlockSpec((tm, D_MODEL), lambda i: (i, 0)),
        out_shape=jax.ShapeDtypeStruct((n_tok, D_MODEL), F32),
        scratch_shapes=[
            pltpu.VMEM((tm, D_MODEL), BF16),
            pltpu.VMEM((tm, D_FF), BF16),
            pltpu.VMEM((8, SUBLANES + n_seg * (seg + SUBLANES), tf), F32),
        ],
        compiler_params=_params("parallel"),
        name="convffn",
    )(x, ng4, mod5, mod5, w_up, conv_w, conv_b, w_down, ng4, mod5)


def _rope_tables(seq_len):
    quarter = B_DK // 4
    inv = ROPE_BASE ** (-jnp.arange(quarter, dtype=F32) / quarter)
    t = jnp.arange(seq_len)
    rows = (t // GRID_W).astype(F32)[:, None] * inv
    cols = (t % GRID_W).astype(F32)[:, None] * inv
    cos = jnp.concatenate([jnp.cos(rows), jnp.cos(cols)] * 2, axis=-1)
    sin = jnp.concatenate([-jnp.sin(rows), -jnp.sin(cols), jnp.sin(rows), jnp.sin(cols)], axis=-1)
    return cos, sin


def _rope_qk_weights(w_in):
    n_l = w_in.shape[0]
    quarter = B_DK // 4
    w_qk = w_in[:, :, :B_QK].astype(BF16).reshape(n_l, D_MODEL, 2 * B_HEADS, 2, 2, quarter)
    return jnp.swapaxes(w_qk, 3, 4).reshape(n_l, D_MODEL, B_QK)


def _gate_weights(w_in, b_gate):
    n_l = w_in.shape[0]
    order = jnp.array((0, 2, 1, 3))
    wg = w_in[:, :, A_MAIN:].reshape(n_l, D_MODEL, 4, A_HEADS)[:, :, order, :]
    wg_t = jnp.transpose(wg, (0, 2, 3, 1)).reshape(n_l, 2 * NG, D_MODEL).astype(BF16)
    bg = b_gate[:, order, :].reshape(n_l, 2 * NG, 1)
    return wg_t, bg


def kernel(x_prompt, x_sample, state_mlstm_C, state_mlstm_n, state_mlstm_m, state_ret_S, c, c_ctx,
           norm_gain, ada_w, ada_b, ml_w_in, ml_b_gate, ml_norm, ml_w_out,
           ret_w_in, ret_decay, ret_norm, ret_w_out, ffn_w_up, ffn_conv, ffn_conv_b, ffn_w_down):
    bp, tp, _ = x_prompt.shape
    bs, ts, _ = x_sample.shape
    n_a = ml_w_in.shape[0]
    n_b = ret_w_in.shape[0]

    cond = jnp.concatenate([c_ctx[None, :], c, jnp.zeros((MOD_ROWS - 1 - bs, D_MODEL), F32)], axis=0)
    mod5 = _modulation(cond, ada_w, ada_b).reshape(DEPTH, MOD_ROWS, 6, 1, D_MODEL)
    ng4 = norm_gain.reshape(DEPTH, 4, 1, D_MODEL)
    rope = _rope_tables(ts)

    ml_gates = _gate_weights(ml_w_in, ml_b_gate)
    ret_w_qk_rope = _rope_qk_weights(ret_w_in)
    ffn_conv_b3 = ffn_conv_b.reshape(DEPTH, 1, 2 * D_FF)
    ml_gn4 = ml_norm.reshape(n_a, A_HEADS, 1, A_DV)
    ret_gn4 = ret_norm.reshape(n_b, B_HEADS, 1, B_DV)
    dec_rep = jnp.broadcast_to(jnp.swapaxes(ret_decay, 1, 2)[..., None], (n_b, B_HEADS, 2, B_DV))
    st_c = state_mlstm_C
    st_n = state_mlstm_n.reshape(bs, n_a, 2, A_HEADS, A_DK, 1)
    st_m = state_mlstm_m.reshape(bs, n_a, 2, A_HEADS, 1, 1)

    groups = [
        dict(x=x_prompt.reshape(bp * tp, D_MODEL), batch=bp, seq_len=tp, sample=False),
        dict(x=x_sample.reshape(bs * ts, D_MODEL), batch=bs, seq_len=ts, sample=True),
    ]
    ml_states = None
    ret_states = None
    for i in range(DEPTH):
        j = i // N_MIXERS
        for grp in groups:
            x = grp["x"]
            geo = dict(seq_len=grp["seq_len"], sample=grp["sample"])
            bt = dict(batch=grp["batch"], seq_len=grp["seq_len"])
            cast = () if grp["sample"] else ((ffn_w_up, i), (ffn_w_down, i))
            if i % N_MIXERS == 0:
                n_qk = A_HEADS * A_DK // TN_PROJ
                z, gates, *ffn_w = _inproj(x, ng4, mod5, i, [(ml_w_in, j, 0)], A_MAIN, n_q=n_qk, n_k=n_qk,
                                           k_scale=A_DK ** -0.5, gates=ml_gates + (j,), cast=cast, **geo)
                if grp["sample"]:
                    (h,) = _mlstm_scan(z, gates, ml_gn4, j, n_layers=n_a, state=(st_c, st_n, st_m), **bt)
                else:
                    h, *ml_states = _mlstm_scan(z, gates, ml_gn4, j, n_layers=n_a, prev=ml_states, **bt)
                x = _outproj(h, ml_w_out, j, x, ng4, mod5, i, **geo)
            else:
                n_qk = B_HEADS * B_DK // TN_PROJ
                w_qk = ret_w_qk_rope if grp["sample"] else ret_w_in
                w_parts = [(w_qk, j, 0), (ret_w_in, j, 1), (ret_w_in, j, 2)]
                z, *ffn_w = _inproj(x, ng4, mod5, i, w_parts, B_QK, n_q=n_qk, n_k=n_qk, k_scale=B_DK ** -0.5,
                                    rope=rope if grp["sample"] else None, cast=cast, **geo)
                if grp["sample"]:
                    (h,) = _ret_scan(z, dec_rep, ret_gn4, j, n_layers=n_b, state=state_ret_S, **bt)
                else:
                    h, ret_states = _ret_scan(z, dec_rep, ret_gn4, j, n_layers=n_b, prev=ret_states, **bt)
                x = _outproj(h, ret_w_out, j, x, ng4, mod5, i, **geo)
            if ffn_w:
                ffn_w_up_b, ffn_w_down_b = ffn_w
            grp["x"] = _ffn(x, ng4, mod5, i, ffn_w_up_b, ffn_conv, ffn_conv_b3, ffn_w_down_b, **geo)

    y_prompt = groups[0]["x"].reshape(bp, tp, D_MODEL)
    y_sample = groups[1]["x"].reshape(bs, ts, D_MODEL)
    new_c, new_n, new_m = ml_states
    return (y_prompt, y_sample, new_c, new_n.reshape(bp, n_a, 2, A_HEADS, A_DK),
            new_m.reshape(bp, n_a, 2, A_HEADS), ret_states)
```

```python
import functools
import math

import jax
import jax.numpy as jnp
from jax import lax
from jax.experimental import pallas as pl
from jax.experimental.pallas import tpu as pltpu

D_MODEL = 1024
DEPTH = 4
GRID_W = 64
CHUNK = 128
N_MIXERS = 2
A_HEADS = 4
A_DV = D_MODEL // A_HEADS
A_DK = A_DV // 2
A_MAIN = 2 * A_HEADS * A_DK + 2 * A_HEADS * A_DV
B_HEADS = 8
B_DK = D_MODEL // B_HEADS
B_DV = 2 * D_MODEL // B_HEADS
B_QK = 2 * B_HEADS * B_DK
ROPE_BASE = 10000.0
D_FF = ((8 * D_MODEL // 3 + 127) // 128) * 128
EPS = 1e-6
LN2 = math.log(2.0)

F32 = jnp.float32
BF16 = jnp.bfloat16

LANES = 128
SUBLANES = 8
MOD_ROWS = 16
VMEM_LIMIT = 48 * 1024 * 1024

TM_PROJ = 512
TN_PROJ = 512
TM_OUT = 1024
TR_OUT = 512
TM_FFN = 512
TF_FFN = 256
FFN_DOWN_BLOCKS = 2
GATE_UNROLL = 16
STATE_UNROLL = 16
OUT_UNROLL = 16
MLSTM_OUT_UNROLL = 16
SCAN_TOKENS = 2048


def _dot(a, b):
    return jnp.dot(a, b, preferred_element_type=F32)


def _dot_nt(a, b):
    return lax.dot_general(a, b, (((1,), (1,)), ((), ())), preferred_element_type=F32)


def _dot_tn(a, b):
    return lax.dot_general(a, b, (((0,), (0,)), ((), ())), preferred_element_type=F32)


def _rms(x):
    return x * lax.rsqrt(jnp.mean(x * x, axis=-1, keepdims=True) + EPS)


def _layer_norm(h):
    d = h - jnp.mean(h, axis=-1, keepdims=True)
    return d * lax.rsqrt(jnp.mean(d * d, axis=-1, keepdims=True) + EPS)


def _params(*sem):
    return pltpu.CompilerParams(dimension_semantics=sem, vmem_limit_bytes=VMEM_LIMIT)


def _resident(shape, index):
    return pl.BlockSpec(shape, lambda *_: index, pipeline_mode=pl.Buffered(1))


def _mod_kernel(cond_ref, w_ref, b_ref, o_ref):
    cnd = cond_ref[...]
    s = cnd * jax.nn.sigmoid(cnd)
    o_ref[...] = _dot(s.astype(BF16), w_ref[...].astype(BF16)) + b_ref[...]


def _modulation(cond, ada_w, ada_b):
    tn = 1024
    n_out = ada_w.shape[-1]
    return pl.pallas_call(
        _mod_kernel,
        grid=(DEPTH, n_out // tn),
        in_specs=[
            pl.BlockSpec((MOD_ROWS, D_MODEL), lambda l, j: (0, 0)),
            pl.BlockSpec((None, D_MODEL, tn), lambda l, j: (l, 0, j)),
            pl.BlockSpec((None, 1, tn), lambda l, j: (l, 0, j)),
        ],
        out_specs=pl.BlockSpec((None, MOD_ROWS, tn), lambda l, j: (l, 0, j)),
        out_shape=jax.ShapeDtypeStruct((DEPTH, MOD_ROWS, n_out), F32),
        compiler_params=_params("parallel", "parallel"),
        name="modulation",
    )(cond, ada_w, ada_b.reshape(DEPTH, 1, n_out))


def _mod_row(sample, seq_len, tm):
    if not sample:
        return lambda i: 0
    tiles_per_seq = seq_len // tm
    return lambda i: 1 + i // tiles_per_seq


def _rope_slab(x, cos, sin):
    return x * cos + pltpu.roll(x, 64, axis=1) * sin


def _inproj_kernel(*refs, n_w, n_q, n_k, k_scale, rope, gates, n_cast, tn):
    x_ref, g_ref, sh_ref, sc_ref = refs[:4]
    w_refs = refs[4:4 + n_w]
    pos = 4 + n_w
    if gates:
        wg_ref, bg_ref = refs[pos:pos + 2]
        pos += 2
    if rope:
        cos_ref, sin_ref = refs[pos:pos + 2]
        pos += 2
    cast_in = refs[pos:pos + n_cast]
    pos += n_cast
    z_ref = refs[pos]
    pos += 1
    if gates:
        gates_ref = refs[pos]
        pos += 1
    cast_out = refs[pos:pos + n_cast]
    pos += n_cast
    u_sc = refs[pos]

    for src, dst in zip(cast_in, cast_out):
        dst[...] = src[...].astype(BF16)

    u = _rms(x_ref[...]) * g_ref[...] * (1.0 + sc_ref[...]) + sh_ref[...]
    u_sc[...] = u.astype(BF16)
    if gates:
        gates_ref[...] = _dot_nt(wg_ref[...], u_sc[...]) + bg_ref[...]

    wb = w_refs[0].shape[1]
    for j in range(z_ref.shape[1] // tn):
        part, off = divmod(j * tn, wb)
        z = _dot(u_sc[...], w_refs[part][:, off:off + tn].astype(BF16))
        scale = k_scale if n_q <= j < n_q + n_k else 1.0
        if rope and j < n_q + n_k:
            for s in range(tn // LANES):
                r = _rope_slab(z[:, s * LANES:(s + 1) * LANES], cos_ref[...], sin_ref[...])
                if scale != 1.0:
                    r = r * scale
                z_ref[:, j * tn + s * LANES:j * tn + (s + 1) * LANES] = r.astype(BF16)
        elif scale != 1.0:
            z_ref[:, j * tn:(j + 1) * tn] = (z * scale).astype(BF16)
        else:
            z_ref[:, j * tn:(j + 1) * tn] = z.astype(BF16)


def _inproj(x, ng4, mod5, layer, w_parts, wb, *, seq_len, sample, n_q, n_k, k_scale, rope=None, gates=None,
            cast=()):
    n_tok = x.shape[0]
    tm, tn = TM_PROJ, TN_PROJ
    n_col = wb * len(w_parts)
    row = _mod_row(sample, seq_len, tm)
    in_specs = [
        pl.BlockSpec((tm, D_MODEL), lambda i: (i, 0)),
        pl.BlockSpec((None, None, 1, D_MODEL), lambda i: (layer, 0, 0, 0)),
        pl.BlockSpec((None, None, None, 1, D_MODEL), lambda i: (layer, row(i), 0, 0, 0)),
        pl.BlockSpec((None, None, None, 1, D_MODEL), lambda i: (layer, row(i), 1, 0, 0)),
    ]
    in_specs += [_resident((None, D_MODEL, wb), (jl, 0, blk)) for _, jl, blk in w_parts]
    args = [x, ng4, mod5, mod5] + [w for w, _, _ in w_parts]
    out_specs = [pl.BlockSpec((tm, n_col), lambda i: (i, 0))]
    out_shape = [jax.ShapeDtypeStruct((n_tok, n_col), BF16)]
    if gates is not None:
        wg_t, bg, jg = gates
        n_g = wg_t.shape[1]
        in_specs += [_resident((None, n_g, D_MODEL), (jg, 0, 0)), _resident((None, n_g, 1), (jg, 0, 0))]
        args += [wg_t, bg]
        out_specs += [pl.BlockSpec((n_g, tm), lambda i: (0, i))]
        out_shape += [jax.ShapeDtypeStruct((n_g, n_tok), F32)]
    if rope is not None:
        tiles_per_seq = seq_len // tm
        in_specs += [pl.BlockSpec((tm, LANES), lambda i: (i % tiles_per_seq, 0))] * 2
        args += list(rope)
    n_steps = n_tok // tm
    for w_src, jl in cast:
        _, rows, cols = w_src.shape
        slab = rows // n_steps
        in_specs += [pl.BlockSpec((None, slab, cols), lambda i, jl=jl: (jl, i, 0))]
        args += [w_src]
        out_specs += [pl.BlockSpec((slab, cols), lambda i: (i, 0))]
        out_shape += [jax.ShapeDtypeStruct((rows, cols), BF16)]
    kern = functools.partial(_inproj_kernel, n_w=len(w_parts), n_q=n_q, n_k=n_k, k_scale=k_scale,
                             rope=rope is not None, gates=gates is not None, n_cast=len(cast), tn=tn)
    return pl.pallas_call(
        kern,
        grid=(n_tok // tm,),
        in_specs=in_specs,
        out_specs=out_specs,
        out_shape=out_shape,
        scratch_shapes=[pltpu.VMEM((tm, D_MODEL), BF16)],
        compiler_params=_params("parallel"),
        name="inproj",
    )(*args)


def _tri_masks():
    li = lax.broadcasted_iota(jnp.int32, (CHUNK, CHUNK), 0)
    si = lax.broadcasted_iota(jnp.int32, (CHUNK, CHUNK), 1)
    return si <= li, si >= li


def _layer_slot(ref, fresh_slot):
    if fresh_slot is None:
        return ref
    for other in range(ref.shape[1]):
        if other != fresh_slot:
            ref[:, other] = jnp.zeros(ref.shape[:1] + ref.shape[2:], ref.dtype)
    return ref.at[:, fresh_slot]


def _seqs_per_step(batch, seq_len, carry_in):
    if carry_in:
        return 1
    nb = max(1, SCAN_TOKENS // seq_len)
    while batch % nb:
        nb -= 1
    return nb


NG = 2 * A_HEADS


def _split_dot(x, mask_b):
    hi = x.astype(BF16)
    r1 = x - hi.astype(F32)
    mid = r1.astype(BF16)
    lo = (r1 - mid.astype(F32)).astype(BF16)
    return _dot(hi, mask_b) + _dot(mid, mask_b) + _dot(lo, mask_b)


def _mlstm_kernel(*refs, n_chunks, nb, carry_in, carry_out, n_alias, fresh_slot):
    q_ref, k_ref, v_ref, o_ref, g_ref, gn_ref = refs[:6]
    pos = 6
    if carry_in:
        c0_ref, n0_ref, m0_ref = refs[pos:pos + 3]
        pos += 3
    pos += n_alias
    h_ref = refs[pos]
    pos += 1
    if carry_out:
        cout_ref, nout_ref, mout_ref = refs[pos:pos + 3]
        pos += 3
    ab_sc, bt_sc, g_sc, bm_sc, mpf_sc, mpb_sc, c_sc, call_sc = refs[pos:pos + 8]

    head = pl.program_id(1)
    masks = _tri_masks()
    ones_b = jnp.ones((CHUNK, LANES), BF16)
    sum_b = jnp.concatenate([masks[1].astype(BF16), ones_b], axis=1)
    lane = lax.broadcasted_iota(jnp.int32, (CHUNK, LANES), 1)
    grow = lax.broadcasted_iota(jnp.int32, (2 * NG, CHUNK), 0)
    gsub = lax.broadcasted_iota(jnp.int32, (NG, LANES), 0)
    zpad = jnp.zeros((LANES - 2 * NG, CHUNK), F32)
    cols = (head, head + A_HEADS)
    mp_sc = (mpf_sc, mpb_sc)
    t_seq = n_chunks * CHUNK

    def pick_col(x, col):
        return jnp.sum(jnp.where(lane == col, x, 0.0), axis=1, keepdims=True)

    def at(s, c):
        return pl.ds(pl.multiple_of(s * t_seq + c * CHUNK, CHUNK), CHUNK)

    def gate_body(c, carry):
        for s in range(nb):
            idx = s * n_chunks + c
            gates = g_ref[:, at(s, c)]
            lf = jnp.minimum(gates, 0.0) - jnp.log1p(jnp.exp(-jnp.abs(gates)))
            lf = jnp.where(grow >= NG, lf, 0.0)
            sums = _split_dot(lf, sum_b)
            a_f = sums[:, :CHUNK]
            tot = sums[:, CHUNK:]
            a_all = jnp.where(grow < NG + A_HEADS, a_f, tot - a_f + lf)[NG:, :]
            b_all = gates[:NG, :] - a_all
            bt_sc[idx] = b_all
            ab_sc[at(s, c), :] = jnp.concatenate([a_all, b_all, zpad], axis=0).T
            g_sc[idx] = tot[NG:, :]
            bm_sc[idx] = jnp.broadcast_to(jnp.max(b_all, axis=1, keepdims=True), (NG, LANES))
        return carry

    lax.fori_loop(0, n_chunks, gate_body, 0, unroll=min(n_chunks, GATE_UNROLL))

    if carry_in:
        m_init = (jnp.broadcast_to(m0_ref[0], (NG, LANES)), jnp.broadcast_to(m0_ref[1], (NG, LANES)))
    else:
        m_init = (jnp.zeros((NG, LANES), F32),) * (2 * nb)

    def m_body(i, carry):
        out = []
        for s in range(nb):
            m_f, m_b = carry[2 * s], carry[2 * s + 1]
            jf = s * n_chunks + i
            jb = s * n_chunks + n_chunks - 1 - i
            mpf_sc[jf] = m_f
            mpb_sc[jb] = m_b
            out.append(g_sc[jf] + jnp.maximum(m_f, bm_sc[jf]))
            out.append(g_sc[jb] + jnp.maximum(m_b, bm_sc[jb]))
        return tuple(out)

    m_last = lax.fori_loop(0, n_chunks, m_body, m_init, unroll=min(n_chunks, STATE_UNROLL))

    if carry_in:
        for dirn in range(2):
            c_sc[dirn, :, :A_DV] = c0_ref[dirn]
            c_sc[dirn, :, A_DV:] = jnp.broadcast_to(n0_ref[dirn], (A_DK, LANES))
    else:
        c_sc[...] = jnp.zeros_like(c_sc)

    def state_body(i, carry):
        for s in range(nb):
            for dirn in range(2):
                c = i if dirn == 0 else n_chunks - 1 - i
                idx = s * n_chunks + c
                mp_row = mp_sc[dirn][idx, pl.ds(cols[dirn], 1), :]
                m_top = jnp.maximum(mp_row, bm_sc[idx, pl.ds(cols[dirn], 1), :])
                ws = jnp.exp(pick_col(ab_sc[at(s, c), :], NG + cols[dirn]) - m_top)
                dec = jnp.exp(mp_row - m_top)
                dec = jnp.concatenate([dec] * (c_sc.shape[2] // LANES), axis=1)
                c_old = c_sc[2 * s + dirn]
                call_sc[2 * s + dirn, c] = c_old.astype(BF16)
                kw = (k_ref[at(s, c), :].astype(F32) * ws).astype(BF16)
                upd = jnp.concatenate([_dot_tn(kw, v_ref[at(s, c), :]), _dot_tn(kw, ones_b)], axis=1)
                c_sc[2 * s + dirn] = dec * c_old + upd
        return carry

    lax.fori_loop(0, n_chunks, state_body, 0, unroll=min(n_chunks, STATE_UNROLL))
    if carry_out:
        outs = [_layer_slot(r, fresh_slot) for r in (cout_ref, nout_ref, mout_ref)]
        for s in range(nb):
            for dirn in range(2):
                outs[0][s, dirn] = c_sc[2 * s + dirn, :, :A_DV]
                outs[1][s, dirn] = c_sc[2 * s + dirn, :, A_DV:].T[0:1, :]
                m_end = jnp.where(gsub == cols[dirn], m_last[2 * s + dirn], 0.0)
                outs[2][s, dirn] = jnp.sum(m_end, axis=0, keepdims=True)[:, 0:1]

    def out_body(c, carry):
        for s in range(nb):
            q = q_ref[at(s, c), :]
            qf = q.astype(F32)
            v_ext = jnp.concatenate([v_ref[at(s, c), :], ones_b], axis=1)
            s_raw = _dot_nt(q, k_ref[at(s, c), :])
            a_chunk = ab_sc[at(s, c), :]
            idx = s * n_chunks + c
            h = None
            for dirn in range(2):
                col = cols[dirn]
                m_prev = mp_sc[dirn][idx, pl.ds(col, 1), :]
                b_vis = jnp.where(masks[dirn], bt_sc[idx, pl.ds(col, 1), :], -jnp.inf)
                m_row = jnp.maximum(m_prev, jnp.max(b_vis, axis=1, keepdims=True))
                sw = (s_raw * jnp.exp(b_vis - m_row)).astype(BF16)
                w_inter = jnp.exp(m_prev - m_row)
                floor = jnp.exp(-(pick_col(a_chunk, col) + m_row))
                qw = (qf * w_inter).astype(BF16)
                nd = _dot(sw, v_ext) + _dot(qw, call_sc[2 * s + dirn, c])
                r = 1.0 / jnp.maximum(jnp.abs(nd[:, A_DV:]), floor)
                hd = nd[:, :A_DV] * jnp.concatenate([r, r], axis=1)
                h = hd if h is None else h + hd
            o = o_ref[at(s, c), :].astype(F32)
            h_ref[at(s, c), :] = (_layer_norm(h) * gn_ref[...] * jax.nn.sigmoid(o)).astype(BF16)
        return carry

    lax.fori_loop(0, n_chunks, out_body, 0, unroll=min(n_chunks, max(1, MLSTM_OUT_UNROLL // nb)))


def _mlstm_scan(z, gates, gn4, j, *, batch, seq_len, n_layers, state=None, prev=None):
    n_tok = z.shape[0]
    n_chunks = seq_len // CHUNK
    carry_in = state is not None
    carry_out = not carry_in
    nb = _seqs_per_step(batch, seq_len, carry_in)
    t = nb * seq_len
    in_specs = [
        pl.BlockSpec((t, A_DK), lambda b, h: (b, h)),
        pl.BlockSpec((t, A_DK), lambda b, h: (b, A_HEADS + h)),
        pl.BlockSpec((t, A_DV), lambda b, h: (b, A_HEADS + h)),
        pl.BlockSpec((t, A_DV), lambda b, h: (b, 2 * A_HEADS + h)),
        pl.BlockSpec((2 * NG, t), lambda b, h: (0, b)),
        pl.BlockSpec((None, None, 1, A_DV), lambda b, h: (j, h, 0, 0)),
    ]
    args = [z, z, z, z, gates, gn4]
    aliases = {}
    if carry_in:
        in_specs += [
            pl.BlockSpec((None, None, 2, None, A_DK, A_DV), lambda b, h: (b, j, 0, h, 0, 0)),
            pl.BlockSpec((None, None, 2, None, A_DK, 1), lambda b, h: (b, j, 0, h, 0, 0)),
            pl.BlockSpec((None, None, 2, None, 1, 1), lambda b, h: (b, j, 0, h, 0, 0)),
        ]
        args += list(state)
    out_specs = [pl.BlockSpec((t, A_DV), lambda b, h: (b, h))]
    out_shape = [jax.ShapeDtypeStruct((n_tok, A_HEADS * A_DV), BF16)]
    fresh_slot = j if carry_out and prev is None else None
    if carry_out:
        lay, jb = (n_layers, 0) if prev is None else (None, j)
        out_specs += [
            pl.BlockSpec((nb, lay, 2, None, A_DK, A_DV), lambda b, h: (b, jb, 0, h, 0, 0)),
            pl.BlockSpec((nb, lay, 2, None, 1, A_DK), lambda b, h: (b, jb, 0, h, 0, 0)),
            pl.BlockSpec((nb, lay, 2, None, 1, 1), lambda b, h: (b, jb, 0, h, 0, 0)),
        ]
        out_shape += [
            jax.ShapeDtypeStruct((batch, n_layers, 2, A_HEADS, A_DK, A_DV), F32),
            jax.ShapeDtypeStruct((batch, n_layers, 2, A_HEADS, 1, A_DK), F32),
            jax.ShapeDtypeStruct((batch, n_layers, 2, A_HEADS, 1, 1), F32),
        ]
        if prev is not None:
            aliases = {len(args) + k: 1 + k for k in range(3)}
            in_specs += [pl.BlockSpec(memory_space=pl.ANY)] * 3
            args += list(prev)
    kern = functools.partial(_mlstm_kernel, n_chunks=n_chunks, nb=nb, carry_in=carry_in,
                             carry_out=carry_out, n_alias=len(aliases), fresh_slot=fresh_slot)
    return pl.pallas_call(
        kern,
        grid=(batch // nb, A_HEADS),
        in_specs=in_specs,
        out_specs=out_specs,
        out_shape=out_shape,
        input_output_aliases=aliases,
        scratch_shapes=[
            pltpu.VMEM((t, LANES), F32),
            pltpu.VMEM((nb * n_chunks, NG, CHUNK), F32),
            pltpu.VMEM((nb * n_chunks, NG, LANES), F32),
            pltpu.VMEM((nb * n_chunks, NG, LANES), F32),
            pltpu.VMEM((nb * n_chunks, NG, LANES), F32),
            pltpu.VMEM((nb * n_chunks, NG, LANES), F32),
            pltpu.VMEM((2 * nb, A_DK, A_DV + LANES), F32),
            pltpu.VMEM((2 * nb, n_chunks, A_DK, A_DV + LANES), BF16),
        ],
        compiler_params=_params("parallel", "parallel"),
        name="mlstm_scan",
    )(*args)


def _ret_kernel(*refs, n_chunks, nb, carry_in, carry_out, n_alias, fresh_slot):
    q_ref, k_ref, v_ref, gate_ref, dec_ref, gn_ref = refs[:6]
    pos = 6
    if carry_in:
        s0_ref = refs[pos]
        pos += 1
    pos += n_alias
    h_ref = refs[pos]
    pos += 1
    if carry_out:
        sout_ref = refs[pos]
        pos += 1
    s_sc, sall_sc, dsum_sc, xi_sc, zeta_sc = refs[pos:pos + 5]
    t_seq = n_chunks * CHUNK

    def at(s, c):
        return pl.ds(pl.multiple_of(s * t_seq + c * CHUNK, CHUNK), CHUNK)

    lg = jnp.log1p(-jnp.exp(-dec_ref[...] * LN2))
    lg_f = lg[0:1, :]
    lg_b = lg[1:2, :]

    @pl.when(pl.program_id(1) == 0)
    def _():
        masks = _tri_masks()
        li = lax.broadcasted_iota(jnp.int32, (CHUNK, B_DV), 0).astype(F32)
        si = lax.broadcasted_iota(jnp.int32, (CHUNK, CHUNK), 1).astype(F32)
        lq = li[:, :CHUNK]
        dsum_sc[...] = (
            jnp.where(masks[0], jnp.exp(jnp.where(masks[0], lq - si, 0.0) * lg_f[:, :CHUNK]), 0.0)
            + jnp.where(masks[1], jnp.exp(jnp.where(masks[1], si - lq, 0.0) * lg_b[:, :CHUNK]), 0.0))
        xi_sc[0] = jnp.exp((lq + 1.0) * lg_f[:, :CHUNK])
        xi_sc[1] = jnp.exp((CHUNK - lq) * lg_b[:, :CHUNK])
        zeta_sc[0] = jnp.exp((CHUNK - 1.0 - lq) * lg_f[:, :CHUNK])
        zeta_sc[1] = jnp.exp(lq * lg_b[:, :CHUNK])

    cdec = (jnp.exp(CHUNK * lg_f), jnp.exp(CHUNK * lg_b))

    if carry_in:
        qr = B_DK // 4
        for dirn in range(2):
            for n, o in enumerate((0, 2, 1, 3)):
                s_sc[dirn, n * qr:(n + 1) * qr, :] = s0_ref[dirn, o * qr:(o + 1) * qr, :]
    else:
        s_sc[...] = jnp.zeros_like(s_sc)

    def state_body(i, carry):
        for s in range(nb):
            for dirn in range(2):
                c = i if dirn == 0 else n_chunks - 1 - i
                s_old = s_sc[2 * s + dirn]
                sall_sc[s, c, dirn * B_DK:(dirn + 1) * B_DK, :] = s_old.astype(BF16)
                kz = (k_ref[at(s, c), :].astype(F32) * zeta_sc[dirn]).astype(BF16)
                s_sc[2 * s + dirn] = cdec[dirn] * s_old + _dot_tn(kz, v_ref[at(s, c), :])
        return carry

    lax.fori_loop(0, n_chunks, state_body, 0, unroll=min(n_chunks, STATE_UNROLL))
    if carry_out:
        s_out = _layer_slot(sout_ref, fresh_slot)
        for s in range(nb):
            for dirn in range(2):
                s_out[s, dirn] = s_sc[2 * s + dirn]

    def out_body(c, carry):
        for s in range(nb):
            q = q_ref[at(s, c), :]
            v = v_ref[at(s, c), :]
            sw = _dot_nt(q, k_ref[at(s, c), :]) * dsum_sc[...]
            qf = q.astype(F32)
            qx = jnp.concatenate([qf * xi_sc[0], qf * xi_sc[1]], axis=1).astype(BF16)
            h = _dot(sw.astype(BF16), v) + _dot(qx, sall_sc[s, c])
            g = gate_ref[at(s, c), :].astype(F32)
            h_ref[at(s, c), :] = (_layer_norm(h) * gn_ref[...] * (g * jax.nn.sigmoid(g))).astype(BF16)
        return carry

    lax.fori_loop(0, n_chunks, out_body, 0, unroll=min(n_chunks, max(1, OUT_UNROLL // nb)))


def _ret_scan(z, dec_rep, gn4, j, *, batch, seq_len, n_layers, state=None, prev=None):
    n_tok = z.shape[0]
    n_chunks = seq_len // CHUNK
    carry_in = state is not None
    carry_out = not carry_in
    nb = _seqs_per_step(batch, seq_len, carry_in)
    t = nb * seq_len
    in_specs = [
        pl.BlockSpec((t, B_DK), lambda h, b: (b, h)),
        pl.BlockSpec((t, B_DK), lambda h, b: (b, B_HEADS + h)),
        pl.BlockSpec((t, B_DV), lambda h, b: (b, B_HEADS + h)),
        pl.BlockSpec((t, B_DV), lambda h, b: (b, 2 * B_HEADS + h)),
        pl.BlockSpec((None, None, 2, B_DV), lambda h, b: (j, h, 0, 0)),
        pl.BlockSpec((None, None, 1, B_DV), lambda h, b: (j, h, 0, 0)),
    ]
    args = [z, z, z, z, dec_rep, gn4]
    aliases = {}
    if carry_in:
        in_specs += [pl.BlockSpec((None, None, 2, None, B_DK, B_DV), lambda h, b: (b, j, 0, h, 0, 0))]
        args += [state]
    out_specs = [pl.BlockSpec((t, B_DV), lambda h, b: (b, h))]
    out_shape = [jax.ShapeDtypeStruct((n_tok, B_HEADS * B_DV), BF16)]
    fresh_slot = j if carry_out and prev is None else None
    if carry_out:
        lay, jb = (n_layers, 0) if prev is None else (None, j)
        out_specs += [pl.BlockSpec((nb, lay, 2, None, B_DK, B_DV), lambda h, b: (b, jb, 0, h, 0, 0))]
        out_shape += [jax.ShapeDtypeStruct((batch, n_layers, 2, B_HEADS, B_DK, B_DV), F32)]
        if prev is not None:
            aliases = {len(args): 1}
            in_specs += [pl.BlockSpec(memory_space=pl.ANY)]
            args += [prev]
    kern = functools.partial(_ret_kernel, n_chunks=n_chunks, nb=nb, carry_in=carry_in,
                             carry_out=carry_out, n_alias=len(aliases), fresh_slot=fresh_slot)
    return pl.pallas_call(
        kern,
        grid=(B_HEADS, batch // nb),
        in_specs=in_specs,
        out_specs=out_specs,
        out_shape=out_shape,
        input_output_aliases=aliases,
        scratch_shapes=[
            pltpu.VMEM((2 * nb, B_DK, B_DV), F32),
            pltpu.VMEM((nb, n_chunks, 2 * B_DK, B_DV), BF16),
            pltpu.VMEM((CHUNK, CHUNK), F32),
            pltpu.VMEM((2, CHUNK, B_DK), F32),
            pltpu.VMEM((2, CHUNK, B_DK), F32),
        ],
        compiler_params=_params("parallel", "arbitrary"),
        name="ret_scan",
    )(*args)


OUT_RING = 3


def _outproj_kernel(h_hbm, w_ref, x_hbm, g_ref, gate_ref, o_ref, hbuf, xbuf, sem, *, n_steps):
    step = pl.program_id(0)
    tm = o_ref.shape[0]

    def tile_copies(t, slot):
        rows = pl.ds(pl.multiple_of(t * tm, tm), tm)
        return (pltpu.make_async_copy(h_hbm.at[rows, :], hbuf.at[slot], sem.at[0, slot]),
                pltpu.make_async_copy(x_hbm.at[rows, :], xbuf.at[slot], sem.at[1, slot]))

    def start(t, slot):
        for cp in tile_copies(t, slot):
            cp.start()

    @pl.when(step == 0)
    def _():
        for t in range(min(OUT_RING - 1, n_steps)):
            start(t, t)

    ahead = step + (OUT_RING - 1)

    @pl.when(ahead < n_steps)
    def _():
        start(ahead, ahead % OUT_RING)

    slot = step % OUT_RING
    for cp in tile_copies(step, slot):
        cp.wait()

    w = w_ref[...].astype(BF16)
    for r in range(tm // TR_OUT):
        rows = slice(r * TR_OUT, (r + 1) * TR_OUT)
        y = _dot(hbuf[slot, rows, :], w)
        o_ref[rows, :] = xbuf[slot, rows, :] + gate_ref[...] * (_rms(y) * g_ref[...])


def _outproj(h, w, j, x, ng4, mod5, layer, *, seq_len, sample):
    n_tok, hv = h.shape
    tm = TM_OUT
    row = _mod_row(sample, seq_len, tm)
    n_steps = n_tok // tm
    return pl.pallas_call(
        functools.partial(_outproj_kernel, n_steps=n_steps),
        grid=(n_steps,),
        in_specs=[
            pl.BlockSpec(memory_space=pl.ANY),
            _resident((None, hv, D_MODEL), (j, 0, 0)),
            pl.BlockSpec(memory_space=pl.ANY),
            pl.BlockSpec((None, None, 1, D_MODEL), lambda i: (layer, 1, 0, 0)),
            pl.BlockSpec((None, None, None, 1, D_MODEL), lambda i: (layer, row(i), 2, 0, 0)),
        ],
        out_specs=pl.BlockSpec((tm, D_MODEL), lambda i: (i, 0)),
        out_shape=jax.ShapeDtypeStruct((n_tok, D_MODEL), F32),
        scratch_shapes=[
            pltpu.VMEM((OUT_RING, tm, hv), BF16),
            pltpu.VMEM((OUT_RING, tm, D_MODEL), F32),
            pltpu.SemaphoreType.DMA((2, OUT_RING)),
        ],
        compiler_params=_params("arbitrary"),
        name="outproj",
    )(h, w, x, ng4, mod5)


def _conv3(hs_ref, half, h, cw, cb, seg, n_seg):
    for s in range(n_seg):
        base = SUBLANES + s * (seg + SUBLANES)
        h_seg = h[s * seg:(s + 1) * seg, :]
        hs_ref[2 * half, base + 1:base + 1 + seg, :] = h_seg
        hs_ref[2 * half + 1, base - 1:base - 1 + seg, :] = h_seg
    parts = []
    for s in range(n_seg):
        base = SUBLANES + s * (seg + SUBLANES)
        h_prev = hs_ref[2 * half, base:base + seg, :]
        h_next = hs_ref[2 * half + 1, base:base + seg, :]
        h_mid = h[s * seg:(s + 1) * seg, :]
        parts.append(h_prev * cw[0:1, :] + h_mid * cw[1:2, :] + h_next * cw[2:3, :] + cb)
    return parts


def _ffn_kernel(x_ref, g2_ref, sh_ref, sc_ref, wup_ref, cw_ref, cb_ref, wd_ref, g3_ref, gate_ref,
                o_ref, u_sc, act_sc, hs_sc, *, seg, n_seg, tf):
    u = _rms(x_ref[...]) * g2_ref[...] * (1.0 + sc_ref[...]) + sh_ref[...]
    u_sc[...] = u.astype(BF16)
    zero_rows = jnp.zeros((SUBLANES, tf), F32)
    for s in range(n_seg):
        base = SUBLANES + s * (seg + SUBLANES)
        for half in range(hs_sc.shape[0] // 2):
            hs_sc[2 * half, base:base + SUBLANES, :] = zero_rows
            hs_sc[2 * half + 1, base + seg - SUBLANES:base + seg, :] = zero_rows

    for cidx in range(D_FF // tf):
        cg = slice(cidx * tf, (cidx + 1) * tf)
        cu = slice(D_FF + cidx * tf, D_FF + (cidx + 1) * tf)
        hb = 2 * (cidx % 2)
        hg = _conv3(hs_sc, hb, _dot(u_sc[...], wup_ref[:, cg]), cw_ref[:, cg], cb_ref[:, cg], seg, n_seg)
        hu = _conv3(hs_sc, hb + 1, _dot(u_sc[...], wup_ref[:, cu]), cw_ref[:, cu], cb_ref[:, cu], seg, n_seg)
        for s in range(n_seg):
            act = jax.nn.gelu(hg[s], approximate=True) * hu[s]
            act_sc[s * seg:(s + 1) * seg, cg] = act.astype(BF16)

    tr = o_ref.shape[0] // FFN_DOWN_BLOCKS
    for r in range(FFN_DOWN_BLOCKS):
        rows = slice(r * tr, (r + 1) * tr)
        f = _dot(act_sc[rows, :], wd_ref[...])
        o_ref[rows, :] = x_ref[rows, :] + gate_ref[...] * (_rms(f) * g3_ref[...])


def _ffn(x, ng4, mod5, layer, w_up, conv_w, conv_b, w_down, *, seq_len, sample):
    n_tok = x.shape[0]
    tm, tf = TM_FFN, TF_FFN
    row = _mod_row(sample, seq_len, tm)
    seg = GRID_W if sample else seq_len
    n_seg = tm // seg
    kern = functools.partial(_ffn_kernel, seg=seg, n_seg=n_seg, tf=tf)
    mod_spec = lambda k: pl.BlockSpec((None, None, None, 1, D_MODEL), lambda i: (layer, row(i), k, 0, 0))
    gain_spec = lambda k: pl.BlockSpec((None, None, 1, D_MODEL), lambda i: (layer, k, 0, 0))
    return pl.pallas_call(
        kern,
        grid=(n_tok // tm,),
        in_specs=[
            pl.BlockSpec((tm, D_MODEL), lambda i: (i, 0)),
            gain_spec(2),
            mod_spec(3),
            mod_spec(4),
            _resident((D_MODEL, 2 * D_FF), (0, 0)),
            _resident((None, 3, 2 * D_FF), (layer, 0, 0)),
            _resident((None, 1, 2 * D_FF), (layer, 0, 0)),
            _resident((D_FF, D_MODEL), (0, 0)),
            gain_spec(3),
            mod_spec(5),
        ],
        out_specs=pl.BlockSpec((tm, D_MODEL), lambda i: (i, 0)),
        out_shape=jax.ShapeDtypeStruct((n_tok, D_MODEL), F32),
        scratch_shapes=[
            pltpu.VMEM((tm, D_MODEL), BF16),
            pltpu.VMEM((tm, D_FF), BF16),
            pltpu.VMEM((8, SUBLANES + n_seg * (seg + SUBLANES), tf), F32),
        ],
        compiler_params=_params("parallel"),
        name="convffn",
    )(x, ng4, mod5, mod5, w_up, conv_w, conv_b, w_down, ng4, mod5)


def _rope_tables(seq_len):
    quarter = B_DK // 4
    inv = ROPE_BASE ** (-jnp.arange(quarter, dtype=F32) / quarter)
    t = jnp.arange(seq_len)
    rows = (t // GRID_W).astype(F32)[:, None] * inv
    cols = (t % GRID_W).astype(F32)[:, None] * inv
    cos = jnp.concatenate([jnp.cos(rows), jnp.cos(cols)] * 2, axis=-1)
    sin = jnp.concatenate([-jnp.sin(rows), -jnp.sin(cols), jnp.sin(rows), jnp.sin(cols)], axis=-1)
    return cos, sin


def _rope_qk_weights(w_in):
    n_l = w_in.shape[0]
    quarter = B_DK // 4
    w_qk = w_in[:, :, :B_QK].astype(BF16).reshape(n_l, D_MODEL, 2 * B_HEADS, 2, 2, quarter)
    return jnp.swapaxes(w_qk, 3, 4).reshape(n_l, D_MODEL, B_QK)


def _gate_weights(w_in, b_gate):
    n_l = w_in.shape[0]
    order = jnp.array((0, 2, 1, 3))
    wg = w_in[:, :, A_MAIN:].reshape(n_l, D_MODEL, 4, A_HEADS)[:, :, order, :]
    wg_t = jnp.transpose(wg, (0, 2, 3, 1)).reshape(n_l, 2 * NG, D_MODEL).astype(BF16)
    bg = b_gate[:, order, :].reshape(n_l, 2 * NG, 1)
    return wg_t, bg


def kernel(x_prompt, x_sample, state_mlstm_C, state_mlstm_n, state_mlstm_m, state_ret_S, c, c_ctx,
           norm_gain, ada_w, ada_b, ml_w_in, ml_b_gate, ml_norm, ml_w_out,
           ret_w_in, ret_decay, ret_norm, ret_w_out, ffn_w_up, ffn_conv, ffn_conv_b, ffn_w_down):
    bp, tp, _ = x_prompt.shape
    bs, ts, _ = x_sample.shape
    n_a = ml_w_in.shape[0]
    n_b = ret_w_in.shape[0]

    cond = jnp.concatenate([c_ctx[None, :], c, jnp.zeros((MOD_ROWS - 1 - bs, D_MODEL), F32)], axis=0)
    mod5 = _modulation(cond, ada_w, ada_b).reshape(DEPTH, MOD_ROWS, 6, 1, D_MODEL)
    ng4 = norm_gain.reshape(DEPTH, 4, 1, D_MODEL)
    rope = _rope_tables(ts)

    ml_gates = _gate_weights(ml_w_in, ml_b_gate)
    ret_w_qk_rope = _rope_qk_weights(ret_w_in)
    ffn_conv_b3 = ffn_conv_b.reshape(DEPTH, 1, 2 * D_FF)
    ml_gn4 = ml_norm.reshape(n_a, A_HEADS, 1, A_DV)
    ret_gn4 = ret_norm.reshape(n_b, B_HEADS, 1, B_DV)
    dec_rep = jnp.broadcast_to(jnp.swapaxes(ret_decay, 1, 2)[..., None], (n_b, B_HEADS, 2, B_DV))
    st_c = state_mlstm_C
    st_n = state_mlstm_n.reshape(bs, n_a, 2, A_HEADS, A_DK, 1)
    st_m = state_mlstm_m.reshape(bs, n_a, 2, A_HEADS, 1, 1)

    groups = [
        dict(x=x_prompt.reshape(bp * tp, D_MODEL), batch=bp, seq_len=tp, sample=False),
        dict(x=x_sample.reshape(bs * ts, D_MODEL), batch=bs, seq_len=ts, sample=True),
    ]
    ml_states = None
    ret_states = None
    for i in range(DEPTH):
        j = i // N_MIXERS
        for grp in groups:
            x = grp["x"]
            geo = dict(seq_len=grp["seq_len"], sample=grp["sample"])
            bt = dict(batch=grp["batch"], seq_len=grp["seq_len"])
            cast = () if grp["sample"] else ((ffn_w_up, i), (ffn_w_down, i))
            if i % N_MIXERS == 0:
                n_qk = A_HEADS * A_DK // TN_PROJ
                z, gates, *ffn_w = _inproj(x, ng4, mod5, i, [(ml_w_in, j, 0)], A_MAIN, n_q=n_qk, n_k=n_qk,
                                           k_scale=A_DK ** -0.5, gates=ml_gates + (j,), cast=cast, **geo)
                if grp["sample"]:
                    (h,) = _mlstm_scan(z, gates, ml_gn4, j, n_layers=n_a, state=(st_c, st_n, st_m), **bt)
                else:
                    h, *ml_states = _mlstm_scan(z, gates, ml_gn4, j, n_layers=n_a, prev=ml_states, **bt)
                x = _outproj(h, ml_w_out, j, x, ng4, mod5, i, **geo)
            else:
                n_qk = B_HEADS * B_DK // TN_PROJ
                w_qk = ret_w_qk_rope if grp["sample"] else ret_w_in
                w_parts = [(w_qk, j, 0), (ret_w_in, j, 1), (ret_w_in, j, 2)]
                z, *ffn_w = _inproj(x, ng4, mod5, i, w_parts, B_QK, n_q=n_qk, n_k=n_qk, k_scale=B_DK ** -0.5,
                                    rope=rope if grp["sample"] else None, cast=cast, **geo)
                if grp["sample"]:
                    (h,) = _ret_scan(z, dec_rep, ret_gn4, j, n_layers=n_b, state=state_ret_S, **bt)
                else:
                    h, ret_states = _ret_scan(z, dec_rep, ret_gn4, j, n_layers=n_b, prev=ret_states, **bt)
                x = _outproj(h, ret_w_out, j, x, ng4, mod5, i, **geo)
            if ffn_w:
                ffn_w_up_b, ffn_w_down_b = ffn_w
            grp["x"] = _ffn(x, ng4, mod5, i, ffn_w_up_b, ffn_conv, ffn_conv_b3, ffn_w_down_b, **geo)

    y_prompt = groups[0]["x"].reshape(bp, tp, D_MODEL)
    y_sample = groups[1]["x"].reshape(bs, ts, D_MODEL)
    new_c, new_n, new_m = ml_states
    return (y_prompt, y_sample, new_c, new_n.reshape(bp, n_a, 2, A_HEADS, A_DK),
            new_m.reshape(bp, n_a, 2, A_HEADS), ret_states)
```
